```python
import jax, jax.numpy as jnp
from jax import lax
import numpy as np

D_MODEL = 1024
BATCH = 8
SEQ = 2048
DEPTH = 2

P_DIM = 256
NORM_EPS = 1e-6

MLA_HEADS = 8
MLA_Q_RANK = 384
MLA_KV_RANK = 256
MLA_NOPE = 64
MLA_ROPE = 32
MLA_V = 64
ROPE_THETA = 10000.0
Q_BLOCK = 128

DIL_CONFIGS = ((128, 1), (512, 4), (2048, 16))
DIL_GROUPS = len(DIL_CONFIGS)
DIL_HEADS_PER_GROUP = 8
DIL_HEADS = DIL_GROUPS * DIL_HEADS_PER_GROUP
DIL_HEAD_DIM = 64
DIL_WIDTH = DIL_HEADS * DIL_HEAD_DIM

D_FF = 3584
N_EXPERTS = 8
TOP_K = 2
D_EXPERT = 3584
N_DENSE_LAYERS = (DEPTH + 1) // 2
N_MOE_LAYERS = DEPTH // 2

IN_WIDTHS = (MLA_Q_RANK, MLA_KV_RANK, MLA_ROPE, DIL_WIDTH, DIL_WIDTH, DIL_WIDTH, D_MODEL, D_MODEL)
D_IN = sum(IN_WIDTHS)

kernel_name = "hybrid_mla_dilated_moe_ple"


def rms_norm(x, g):
    xf = x.astype(jnp.float32)
    y = xf * lax.rsqrt(jnp.mean(xf * xf, axis=-1, keepdims=True) + NORM_EPS)
    return (y * g.astype(jnp.float32)).astype(x.dtype)


def rope(x, pos):
    half = x.shape[-1] // 2
    inv_freq = ROPE_THETA ** (-jnp.arange(half, dtype=jnp.float32) / half)
    ang = pos.astype(jnp.float32)[..., None] * inv_freq
    ang = ang.reshape(ang.shape[:2] + (1,) * (x.ndim - 3) + (half,))
    cos, sin = jnp.cos(ang), jnp.sin(ang)
    xf = x.astype(jnp.float32)
    x1, x2 = xf[..., :half], xf[..., half:]
    return jnp.concatenate([x1 * cos - x2 * sin, x2 * cos + x1 * sin], axis=-1).astype(x.dtype)


def alibi_slopes(n):
    def pow2(m):
        start = 2.0 ** (-8.0 / m)
        return [start ** (i + 1) for i in range(m)]
    if float(np.log2(n)).is_integer():
        s = pow2(n)
    else:
        c = 2 ** int(np.floor(np.log2(n)))
        s = pow2(c) + pow2(2 * c)[0::2][: n - c]
    return jnp.asarray(sorted(s, reverse=True), dtype=jnp.float32)


def mla_attention(q_nope, q_rope, k_nope, k_rope, v):
    B, S, H, _ = q_nope.shape
    nb = S // Q_BLOCK
    scale = (MLA_NOPE + MLA_ROPE) ** -0.5
    kpos = jnp.arange(S)

    def one_block(args):
        qn, qr, start = args
        s = (jnp.einsum('bqhd,bkhd->bhqk', qn, k_nope, preferred_element_type=jnp.float32)
             + jnp.einsum('bqhd,bkd->bhqk', qr, k_rope, preferred_element_type=jnp.float32)) * scale
        qpos = start + jnp.arange(Q_BLOCK)
        s = jnp.where(kpos[None, :] <= qpos[:, None], s, -jnp.inf)
        prob = jax.nn.softmax(s, axis=-1)
        return jnp.einsum('bhqk,bkhd->bqhd', prob.astype(v.dtype), v)

    to_blocks = lambda t: t.reshape((B, nb, Q_BLOCK) + t.shape[2:]).swapaxes(0, 1)
    out = lax.map(one_block, (to_blocks(q_nope), to_blocks(q_rope), jnp.arange(nb) * Q_BLOCK))
    return out.swapaxes(0, 1).reshape(B, S, H, MLA_V)


def dilated_group(q, k, v, slopes, window, dilation):
    B, S, H, D = q.shape
    span = window // dilation
    L = S // dilation
    nb = -(-L // span)
    Lp = nb * span
    z = B * dilation

    def to_phase(t):
        t = t.reshape(B, L, dilation, H, D).transpose(0, 2, 1, 3, 4).reshape(z, L, H, D)
        t = jnp.pad(t, ((0, 0), (0, Lp - L), (0, 0), (0, 0)))
        return t.reshape(z, nb, span, H, D)

    def with_prev(t):
        prev = jnp.pad(t[:, :-1], ((0, 0), (1, 0), (0, 0), (0, 0), (0, 0)))
        return jnp.concatenate([prev, t], axis=2)

    qb = to_phase(q)
    kw, vw = with_prev(to_phase(k)), with_prev(to_phase(v))
    s = jnp.einsum('znqhd,znkhd->znhqk', qb, kw, preferred_element_type=jnp.float32) * (D ** -0.5)
    qi = jnp.arange(span)[:, None]
    kj = jnp.arange(2 * span)[None, :]
    dist = qi + span - kj
    blk = jnp.arange(nb)[:, None, None]
    valid = (dist >= 0) & (dist <= span) & (blk * span - span + kj >= 0)
    bias = -slopes[:, None, None] * (dilation * dist).astype(jnp.float32)
    s = jnp.where(valid[None, :, None], s + bias, -jnp.inf)
    m = jnp.max(s, axis=-1, keepdims=True)
    e = jnp.exp(s - m)
    den = jnp.sum(e, axis=-1, keepdims=True)
    o = jnp.einsum('znhqk,znkhd->znqhd', (e / den).astype(vw.dtype), vw)
    lse = (m + jnp.log(den))[..., 0]
    o = o.reshape(B, dilation, Lp, H, D)[:, :, :L].transpose(0, 2, 1, 3, 4).reshape(B, S, H, D)
    lse = lse.transpose(0, 1, 3, 2).reshape(B, dilation, Lp, H)[:, :, :L]
    lse = lse.transpose(0, 2, 1, 3).reshape(B, S, H)
    return o, lse


def dilated_mixer(q, k, v):
    slopes = alibi_slopes(DIL_HEADS)
    outs, lses = [], []
    for g, (window, dilation) in enumerate(DIL_CONFIGS):
        sl = slice(g * DIL_HEADS_PER_GROUP, (g + 1) * DIL_HEADS_PER_GROUP)
        o, l = dilated_group(q[:, :, sl], k[:, :, sl], v[:, :, sl], slopes[sl], window, dilation)
        outs.append(o)
        lses.append(l)
    wts = jax.nn.softmax(jnp.stack(lses, axis=0), axis=0)
    comb = jnp.sum(wts[..., None] * jnp.stack(outs, 0).astype(jnp.float32), axis=0)
    return comb.astype(q.dtype)


def swiglu(h, w_gate, w_up, w_down):
    return (jax.nn.silu(h @ w_gate) * (h @ w_up)) @ w_down


def moe_swiglu(h, w_router, b_router, w_gate, w_up, w_down):
    logits = (h @ w_router).astype(jnp.float32) + b_router.astype(jnp.float32)
    top_val, top_idx = lax.top_k(logits, TOP_K)
    top_w = jax.nn.softmax(top_val, axis=-1)
    gates = jnp.sum(jax.nn.one_hot(top_idx, N_EXPERTS, dtype=jnp.float32) * top_w[..., None], axis=-2)
    out = jnp.zeros(h.shape, jnp.float32)
    for e in range(N_EXPERTS):
        out = out + gates[..., e:e + 1] * swiglu(h, w_gate[e], w_up[e], w_down[e]).astype(jnp.float32)
    return out.astype(h.dtype)


def setup_inputs(seed: int = 0) -> dict:
    key = jax.random.key(seed)
    keys = jax.random.split(key, 32)
    counter = iter(range(32))
    nk = lambda: keys[next(counter)]
    w = lambda shape, fan_in: jax.random.normal(nk(), shape, jnp.float32) * (fan_in ** -0.5)
    gain = lambda shape: 1.0 + 0.05 * jax.random.normal(nk(), shape, jnp.float32)
    start = jax.random.randint(nk(), (BATCH, 1), 0, 1024, dtype=jnp.int32)
    positions = start + jnp.arange(SEQ, dtype=jnp.int32)[None, :]
    return {
        "x": jax.random.normal(nk(), (BATCH, SEQ, D_MODEL), jnp.float32),
        "p": jax.random.normal(nk(), (DEPTH, BATCH, SEQ, P_DIM), jnp.float32),
        "positions": positions,
        "attn_norm": gain((DEPTH, D_MODEL)),
        "w_in": w((DEPTH, D_MODEL, D_IN), D_MODEL),
        "q_norm": gain((DEPTH, MLA_Q_RANK)),
        "w_uq": w((DEPTH, MLA_Q_RANK, MLA_HEADS * (MLA_NOPE + MLA_ROPE)), MLA_Q_RANK),
        "kv_norm": gain((DEPTH, MLA_KV_RANK)),
        "w_ukv": w((DEPTH, MLA_KV_RANK, MLA_HEADS * (MLA_NOPE + MLA_V)), MLA_KV_RANK),
        "w_br_mla": w((DEPTH, MLA_HEADS * MLA_V, D_MODEL), MLA_HEADS * MLA_V),
        "w_br_dil": w((DEPTH, DIL_HEADS_PER_GROUP * DIL_HEAD_DIM, D_MODEL), DIL_HEADS_PER_GROUP * DIL_HEAD_DIM),
        "w_out": w((DEPTH, D_MODEL, D_MODEL), D_MODEL),
        "ffn_norm": gain((DEPTH, D_MODEL)),
        "dense_w_gate": w((N_DENSE_LAYERS, D_MODEL, D_FF), D_MODEL),
        "dense_w_up": w((N_DENSE_LAYERS, D_MODEL, D_FF), D_MODEL),
        "dense_w_down": w((N_DENSE_LAYERS, D_FF, D_MODEL), D_FF),
        "router_w": w((N_MOE_LAYERS, D_MODEL, N_EXPERTS), D_MODEL),
        "router_b": 0.01 * jax.random.normal(nk(), (N_MOE_LAYERS, N_EXPERTS), jnp.float32),
        "moe_w_gate": w((N_MOE_LAYERS, N_EXPERTS, D_MODEL, D_EXPERT), D_MODEL),
        "moe_w_up": w((N_MOE_LAYERS, N_EXPERTS, D_MODEL, D_EXPERT), D_MODEL),
        "moe_w_down": w((N_MOE_LAYERS, N_EXPERTS, D_EXPERT, D_MODEL), D_EXPERT),
        "ple_norm": gain((DEPTH, D_MODEL)),
        "ple_w_gate": w((DEPTH, D_MODEL, D_MODEL), D_MODEL),
        "ple_w_proj": w((DEPTH, P_DIM, D_MODEL), P_DIM),
        "final_norm": gain((D_MODEL,)),
    }


def reference(x, p, positions, attn_norm, w_in, q_norm, w_uq, kv_norm, w_ukv, w_br_mla, w_br_dil,
              w_out, ffn_norm, dense_w_gate, dense_w_up, dense_w_down, router_w, router_b,
              moe_w_gate, moe_w_up, moe_w_down, ple_norm, ple_w_gate, ple_w_proj, final_norm):
    B, S, _ = x.shape
    split_at = tuple(int(o) for o in np.cumsum(IN_WIDTHS)[:-1])
    for i in range(DEPTH):
        h = rms_norm(x, attn_norm[i])
        c_q, c_kv, k_r, q_d, k_d, v_d, g_mla, g_dil = jnp.split(h @ w_in[i], split_at, axis=-1)

        q = (rms_norm(c_q, q_norm[i]) @ w_uq[i]).reshape(B, S, MLA_HEADS, MLA_NOPE + MLA_ROPE)
        q_nope, q_rope = q[..., :MLA_NOPE], rope(q[..., MLA_NOPE:], positions)
        kv = (rms_norm(c_kv, kv_norm[i]) @ w_ukv[i]).reshape(B, S, MLA_HEADS, MLA_NOPE + MLA_V)
        k_nope, v_mla = kv[..., :MLA_NOPE], kv[..., MLA_NOPE:]
        k_rope = rope(k_r, positions)
        o_mla = mla_attention(q_nope, q_rope, k_nope, k_rope, v_mla).reshape(B, S, -1)

        shp = (B, S, DIL_HEADS, DIL_HEAD_DIM)
        o_dil = dilated_mixer(q_d.reshape(shp), k_d.reshape(shp), v_d.reshape(shp)).reshape(B, S, -1)

        merged = (jax.nn.sigmoid(g_mla) * (o_mla @ w_br_mla[i])
                  + jax.nn.sigmoid(g_dil) * (o_dil @ w_br_dil[i]))
        x = x + merged @ w_out[i]

        h = rms_norm(x, ffn_norm[i])
        j = i // 2
        if i % 2 == 0:
            x = x + swiglu(h, dense_w_gate[j], dense_w_up[j], dense_w_down[j])
        else:
            x = x + moe_swiglu(h, router_w[j], router_b[j], moe_w_gate[j], moe_w_up[j], moe_w_down[j])

        h = rms_norm(x, ple_norm[i])
        x = x + jax.nn.sigmoid(h @ ple_w_gate[i]) * (p[i] @ ple_w_proj[i])
    return rms_norm(x, final_norm)
```

```python
import functools

import numpy as np
import jax
import jax.numpy as jnp
from jax import lax
from jax.experimental import pallas as pl
from jax.experimental.pallas import tpu as pltpu

F32 = jnp.float32
BF16 = jnp.bfloat16

D_MODEL = 1024
P_DIM = 256
NORM_EPS = 1e-6
DEPTH = 2

MLA_HEADS = 8
MLA_Q_RANK = 384
MLA_KV_RANK = 256
MLA_NOPE = 64
MLA_ROPE = 32
MLA_V = 64
ROPE_THETA = 10000.0
MLA_SLAB = 128

DIL_CONFIGS = ((128, 1), (512, 4), (2048, 16))
DIL_GROUPS = 3
DIL_HPG = 8
DIL_HEADS = 24
DIL_HD = 64
DIL_WIDTH = DIL_HEADS * DIL_HD
DIL_SPAN = 128

D_FF = 3584
N_EXPERTS = 8
TOP_K = 2

LANES = 128
NEG = -1e30
VMEM_LIMIT = 56 * 1024 * 1024

IN_OFF = tuple(int(o) for o in np.cumsum((0, MLA_Q_RANK, MLA_KV_RANK, MLA_ROPE, DIL_WIDTH, DIL_WIDTH,
                                          DIL_WIDTH, D_MODEL, D_MODEL)))


def _cparams(sem):
    return pltpu.CompilerParams(dimension_semantics=sem, vmem_limit_bytes=VMEM_LIMIT)


def _rms(x, g):
    return x * lax.rsqrt(jnp.mean(x * x, axis=-1, keepdims=True) + NORM_EPS) * g


def _sigmoid(x):
    return 1.0 / (1.0 + jnp.exp(-x))


def _alibi_slopes(n):
    def pow2(m):
        start = 2.0 ** (-8.0 / m)
        return [start ** (i + 1) for i in range(m)]
    if float(np.log2(n)).is_integer():
        s = pow2(n)
    else:
        c = 2 ** int(np.floor(np.log2(n)))
        s = pow2(c) + pow2(2 * c)[0::2][: n - c]
    return jnp.asarray(sorted(s, reverse=True), dtype=F32)


def _norm_kernel(x_ref, g_ref, o_ref):
    o_ref[...] = _rms(x_ref[...], g_ref[...]).astype(o_ref.dtype)


def _norm(x, g, tm=1024):
    t, d = x.shape
    return pl.pallas_call(
        _norm_kernel,
        grid=(t // tm,),
        in_specs=[pl.BlockSpec((tm, d), lambda i: (i, 0)), pl.BlockSpec((1, d), lambda i: (0, 0))],
        out_specs=pl.BlockSpec((tm, d), lambda i: (i, 0)),
        out_shape=jax.ShapeDtypeStruct((t, d), BF16),
        compiler_params=_cparams(("parallel",)),
        name="rms_norm",
    )(x, g.reshape(1, d))


def _rope_table_kernel(pos_ref, invf_ref, cos_ref, sin_ref):
    ang = pos_ref[...].astype(F32) * invf_ref[...]
    lane = lax.broadcasted_iota(jnp.int32, ang.shape, 1)
    rope_lane = (lane >= MLA_NOPE) & (lane < MLA_NOPE + MLA_ROPE)
    cos_ref[...] = jnp.where(lane < MLA_NOPE, 1.0, jnp.where(rope_lane, jnp.cos(ang), 0.0))
    sin_ref[...] = jnp.where(rope_lane, jnp.sin(ang), 0.0)


def _rope_tables(positions, tm=2048):
    t = positions.size
    half = MLA_ROPE // 2
    inv_freq = ROPE_THETA ** (-jnp.arange(half, dtype=F32) / half)
    invf = jnp.zeros((1, LANES), F32).at[0, MLA_NOPE:MLA_NOPE + MLA_ROPE].set(jnp.concatenate([inv_freq, inv_freq]))
    pos_b = jnp.broadcast_to(positions.reshape(t, 1), (t, LANES))
    spec = pl.BlockSpec((tm, LANES), lambda i: (i, 0))
    return pl.pallas_call(
        _rope_table_kernel,
        grid=(t // tm,),
        in_specs=[spec, pl.BlockSpec((1, LANES), lambda i: (0, 0))],
        out_specs=[spec, spec],
        out_shape=[jax.ShapeDtypeStruct((t, LANES), F32)] * 2,
        compiler_params=_cparams(("parallel",)),
        name="rope_tables",
    )(pos_b, invf)


def _mla_proj_kernel(h_ref, wlat_ref, wq_ref, wqr_ref, wk_ref, wv_ref, qn_ref, kvn_ref, cos_ref, sin_ref,
                     q_out, k_out, v_out):
    lat = jnp.dot(h_ref[...], wlat_ref[...], preferred_element_type=F32)
    cqn = _rms(lat[:, :MLA_Q_RANK], qn_ref[...]).astype(BF16)
    ckvn = _rms(lat[:, MLA_Q_RANK:MLA_Q_RANK + MLA_KV_RANK], kvn_ref[...]).astype(BF16)
    cos = cos_ref[...]
    sin = sin_ref[...]
    o = MLA_Q_RANK + MLA_KV_RANK
    k_rope = lat[:, o:o + LANES] * cos + lat[:, o + LANES:o + 2 * LANES] * sin
    qa = jnp.dot(cqn, wq_ref[...], preferred_element_type=F32)
    qb = jnp.dot(cqn, wqr_ref[...], preferred_element_type=F32)
    kk = jnp.dot(ckvn, wk_ref[...], preferred_element_type=F32)
    scale = (MLA_NOPE + MLA_ROPE) ** -0.5
    for hd in range(MLA_HEADS):
        sl = slice(hd * MLA_SLAB, (hd + 1) * MLA_SLAB)
        q_out[:, sl] = ((qa[:, sl] * cos + qb[:, sl] * sin) * scale).astype(BF16)
        k_out[:, sl] = (kk[:, sl] + k_rope).astype(BF16)
    v_out[...] = jnp.dot(ckvn, wv_ref[...], preferred_element_type=F32).astype(BF16)


def _mla_proj(h, wlat, wq, wqr, wk, wv, qn, kvn, cos_t, sin_t, tm=512):
    t = h.shape[0]
    row = lambda w: pl.BlockSpec((tm, w), lambda i: (i, 0))
    full = lambda a: pl.BlockSpec(a.shape, lambda i: (0, 0))
    hs = MLA_HEADS * MLA_SLAB
    return pl.pallas_call(
        _mla_proj_kernel,
        grid=(t // tm,),
        in_specs=[row(D_MODEL), full(wlat), full(wq), full(wqr), full(wk), full(wv), full(qn), full(kvn),
                  row(LANES), row(LANES)],
        out_specs=[row(hs), row(hs), row(MLA_HEADS * MLA_V)],
        out_shape=[jax.ShapeDtypeStruct((t, hs), BF16), jax.ShapeDtypeStruct((t, hs), BF16),
                   jax.ShapeDtypeStruct((t, MLA_HEADS * MLA_V), BF16)],
        compiler_params=_cparams(("parallel",)),
        name="mla_proj",
    )(h, wlat, wq, wqr, wk, wv, qn, kvn, cos_t, sin_t)


def _mla_attn_kernel(q_ref, k_ref, v_ref, o_ref, *, tq):
    qi = pl.program_id(2)
    lane = lax.broadcasted_iota(jnp.int32, (tq, LANES), 1)
    rq = lax.broadcasted_iota(jnp.int32, (tq, tq), 0)
    ck = lax.broadcasted_iota(jnp.int32, (tq, tq), 1)
    causal = ck <= rq
    outs = []
    for hh in range(2):
        q = q_ref[:, hh * MLA_SLAB:(hh + 1) * MLA_SLAB]

        def block(j, carry, masked):
            m, l, acc = carry
            r0 = pl.multiple_of(j * tq, tq)
            ks = k_ref[pl.ds(r0, tq), hh * MLA_SLAB:(hh + 1) * MLA_SLAB]
            vs = v_ref[pl.ds(r0, tq), :]
            s = lax.dot_general(q, ks, (((1,), (1,)), ((), ())), preferred_element_type=F32)
            if masked:
                s = jnp.where(causal, s, NEG)
            m_new = jnp.maximum(m, jnp.max(s, axis=-1, keepdims=True))
            alpha = jnp.exp(m - m_new)
            p = jnp.exp(s - m_new)
            l = alpha * l + jnp.sum(p, axis=-1, keepdims=True)
            acc = alpha * acc + jnp.dot(p.astype(BF16), vs, preferred_element_type=F32)
            return m_new, l, acc

        init = (jnp.full((tq, 1), NEG, F32), jnp.zeros((tq, 1), F32), jnp.zeros((tq, LANES), F32))
        carry = lax.fori_loop(0, qi, functools.partial(block, masked=False), init)
        m, l, acc = block(qi, carry, True)
        outs.append(acc / l)
    o_ref[...] = jnp.where(lane < MLA_V, outs[0], outs[1]).astype(o_ref.dtype)


def _mla_attn(q, k, v, batch, seq, tq=256):
    t = q.shape[0]
    nq = seq // tq
    pairs = MLA_HEADS // 2
    return pl.pallas_call(
        functools.partial(_mla_attn_kernel, tq=tq),
        grid=(batch, pairs, nq),
        in_specs=[pl.BlockSpec((tq, 2 * MLA_SLAB), lambda b, p, i: (b * nq + i, p)),
                  pl.BlockSpec((seq, 2 * MLA_SLAB), lambda b, p, i: (b, p)),
                  pl.BlockSpec((seq, 2 * MLA_V), lambda b, p, i: (b, p))],
        out_specs=pl.BlockSpec((tq, 2 * MLA_V), lambda b, p, i: (b * nq + i, p)),
        out_shape=jax.ShapeDtypeStruct((t, MLA_HEADS * MLA_V), BF16),
        compiler_params=_cparams(("parallel", "parallel", "arbitrary")),
        name="mla_attn",
    )(q, k, v)


def _dil_proj_kernel(h_ref, w_ref, o_ref, acc_ref, *, seq):
    res = jnp.dot(h_ref[...], w_ref[...], preferred_element_type=F32)
    g = pl.program_id(0) % DIL_GROUPS
    nl = o_ref.shape[1] // LANES

    @pl.when(g == 0)
    def _():
        o_ref[...] = res.astype(o_ref.dtype)

    @pl.when(g > 0)
    def _():
        for j in range(nl):
            acc_ref[j] = res[:, j * LANES:(j + 1) * LANES]

    for gi, (_, r) in enumerate(DIL_CONFIGS):
        if r == 1:
            continue

        @pl.when(g == gi)
        def _(r=r):
            ln = seq // r
            for c in range(r):
                for j in range(nl):
                    o_ref[c * ln:(c + 1) * ln, j * LANES:(j + 1) * LANES] = (
                        acc_ref[j, pl.ds(c, ln, stride=r), :].astype(o_ref.dtype))


def _dil_proj(h, w, seq, tn=512):
    t = h.shape[0]
    n = w.shape[1]
    return pl.pallas_call(
        functools.partial(_dil_proj_kernel, seq=seq),
        grid=(n // tn, t // seq),
        in_specs=[pl.BlockSpec((seq, D_MODEL), lambda j, b: (b, 0)),
                  pl.BlockSpec((D_MODEL, tn), lambda j, b: (0, j))],
        out_specs=pl.BlockSpec((seq, tn), lambda j, b: (b, j)),
        out_shape=jax.ShapeDtypeStruct((t, n), BF16),
        scratch_shapes=[pltpu.VMEM((tn // LANES, seq, LANES), F32)],
        compiler_params=_cparams(("parallel", "parallel")),
        name="dil_proj",
    )(h, w)


def _dil_attn_kernel(slopes_ref, q0, q1, q2, k0, k1, k2, v0, v1, v2, o_ref, acc_s, m_s, l_s, *, seq):
    pair = pl.program_id(1)
    sp = DIL_SPAN
    head0 = lax.broadcasted_iota(jnp.int32, (sp, LANES), 1) < DIL_HD
    qi = lax.broadcasted_iota(jnp.int32, (sp, 2 * sp), 0)
    kj = lax.broadcasted_iota(jnp.int32, (sp, 2 * sp), 1)
    dist_w = qi + sp - kj
    valid_w = (dist_w >= 0) & (dist_w <= sp)
    dist_1 = dist_w[:, sp:]
    valid_1 = dist_1 >= 0
    qs, ks, vs = (q0, q1, q2), (k0, k1, k2), (v0, v1, v2)
    for g, (_, r) in enumerate(DIL_CONFIGS):
        ln = seq // r
        nb = ln // sp
        sl = [slopes_ref[g * DIL_HPG + 2 * pair + hh] * float(r) for hh in range(2)]
        bias_w = jnp.concatenate([jnp.where(valid_w, -s * dist_w.astype(F32), NEG) for s in sl], axis=0)
        bias_1 = jnp.concatenate([jnp.where(valid_1, -s * dist_1.astype(F32), NEG) for s in sl], axis=0)
        for c in range(r):
            for i in range(nb):
                row0 = c * ln + i * sp
                qb = qs[g][row0:row0 + sp, :]
                zero = jnp.zeros_like(qb)
                q2h = jnp.concatenate([jnp.where(head0, qb, zero), jnp.where(head0, zero, qb)], axis=0)
                lo = row0 if i == 0 else row0 - sp
                kw = ks[g][lo:row0 + sp, :]
                vw = vs[g][lo:row0 + sp, :]
                s = lax.dot_general(q2h, kw, (((1,), (1,)), ((), ())), preferred_element_type=F32)
                s = s + (bias_1 if i == 0 else bias_w)
                m = jnp.max(s, axis=-1, keepdims=True)
                p = jnp.exp(s - m)
                l = jnp.sum(p, axis=-1, keepdims=True)
                o2 = jnp.dot(p.astype(BF16), vw, preferred_element_type=F32)
                dst = pl.ds(i * sp * r + c, sp, stride=r) if r > 1 else pl.ds(i * sp, sp)
                acc_s[g, dst, :] = jnp.where(head0, o2[:sp], o2[sp:])
                m_s[g, dst, :] = jnp.where(head0, m[:sp], m[sp:])
                l_s[g, dst, :] = jnp.where(head0, l[:sp], l[sp:])

    ch = 256

    def combine(i, _):
        rows = pl.ds(pl.multiple_of(i * ch, ch), ch)
        ms = [m_s[g, rows, :] for g in range(DIL_GROUPS)]
        mx = jnp.maximum(jnp.maximum(ms[0], ms[1]), ms[2])
        ws = [jnp.exp(m - mx) for m in ms]
        num = sum(ws[g] * acc_s[g, rows, :] for g in range(DIL_GROUPS))
        den = sum(ws[g] * l_s[g, rows, :] for g in range(DIL_GROUPS))
        o_ref[rows, :] = (num / den).astype(o_ref.dtype)
        return 0

    lax.fori_loop(0, seq // ch, combine, 0)


def _dil_attn(qkv, slopes, batch, seq):
    t = qkv.shape[0]
    pairs = DIL_HPG // 2
    cpg = DIL_HPG * DIL_HD // LANES
    cpt = DIL_WIDTH // LANES

    def spec(tt, g):
        return pl.BlockSpec((seq, LANES), lambda b, p: (b, tt * cpt + g * cpg + p))

    in_specs = [pl.BlockSpec(memory_space=pltpu.SMEM)] + [spec(tt, g) for tt in range(3) for g in range(DIL_GROUPS)]
    return pl.pallas_call(
        functools.partial(_dil_attn_kernel, seq=seq),
        grid=(batch, pairs),
        in_specs=in_specs,
        out_specs=pl.BlockSpec((seq, LANES), lambda b, p: (b, p)),
        out_shape=jax.ShapeDtypeStruct((t, DIL_HPG * DIL_HD), BF16),
        scratch_shapes=[pltpu.VMEM((DIL_GROUPS, seq, LANES), F32)] * 3,
        compiler_params=_cparams(("parallel", "parallel")),
        name="dil_attn",
    )(slopes, *([qkv] * 9))


def _merge_kernel(x_ref, h_ref, om_ref, od_ref, wg_ref, wbm_ref, wbd_ref, wo_ref, g_ref, x_out, h_out):
    gates = jnp.dot(h_ref[...], wg_ref[...], preferred_element_type=F32)
    bm = jnp.dot(om_ref[...], wbm_ref[...], preferred_element_type=F32)
    bd = jnp.dot(od_ref[...], wbd_ref[...], preferred_element_type=F32)
    merged = _sigmoid(gates[:, :D_MODEL]) * bm + _sigmoid(gates[:, D_MODEL:]) * bd
    x1 = x_ref[...] + jnp.dot(merged.astype(BF16), wo_ref[...], preferred_element_type=F32)
    x_out[...] = x1
    h_out[...] = _rms(x1, g_ref[...]).astype(h_out.dtype)


def _merge(x, h, o_mla, o_dil, wg, wbm, wbd, wo, g_next, h_dtype, tm=512):
    t = x.shape[0]
    row = lambda w: pl.BlockSpec((tm, w), lambda i: (i, 0))
    full = lambda a: pl.BlockSpec(a.shape, lambda i: (0, 0))
    return pl.pallas_call(
        _merge_kernel,
        grid=(t // tm,),
        in_specs=[row(D_MODEL), row(D_MODEL), row(o_mla.shape[1]), row(o_dil.shape[1]),
                  full(wg), full(wbm), full(wbd), full(wo), full(g_next)],
        out_specs=[row(D_MODEL), row(D_MODEL)],
        out_shape=[jax.ShapeDtypeStruct((t, D_MODEL), F32), jax.ShapeDtypeStruct((t, D_MODEL), h_dtype)],
        compiler_params=_cparams(("parallel",)),
        name="merge",
    )(x, h, o_mla, o_dil, wg, wbm, wbd, wo, g_next)


def _ffn_kernel(x_ref, h_ref, wg_ref, wu_ref, wd_ref, o_ref, acc_ref):
    f = pl.program_id(1)
    h = h_ref[...]
    a = jnp.dot(h, wg_ref[...], preferred_element_type=F32)
    u = jnp.dot(h, wu_ref[...], preferred_element_type=F32)
    y = jnp.dot((a * _sigmoid(a) * u).astype(BF16), wd_ref[...], preferred_element_type=F32)

    @pl.when(f == 0)
    def _():
        acc_ref[...] = x_ref[...] + y

    @pl.when(f > 0)
    def _():
        acc_ref[...] += y

    @pl.when(f == pl.num_programs(1) - 1)
    def _():
        o_ref[...] = acc_ref[...]


def _ffn(x, h, wg, wu, wd, tm=1024, tf=512):
    t = x.shape[0]
    nf = wg.shape[1] // tf
    return pl.pallas_call(
        _ffn_kernel,
        grid=(t // tm, nf),
        in_specs=[pl.BlockSpec((tm, D_MODEL), lambda i, f: (i, 0)),
                  pl.BlockSpec((tm, D_MODEL), lambda i, f: (i, 0)),
                  pl.BlockSpec((D_MODEL, tf), lambda i, f: (0, f)),
                  pl.BlockSpec((D_MODEL, tf), lambda i, f: (0, f)),
                  pl.BlockSpec((tf, D_MODEL), lambda i, f: (f, 0))],
        out_specs=pl.BlockSpec((tm, D_MODEL), lambda i, f: (i, 0)),
        out_shape=jax.ShapeDtypeStruct((t, D_MODEL), F32),
        scratch_shapes=[pltpu.VMEM((tm, D_MODEL), F32)],
        compiler_params=_cparams(("parallel", "arbitrary")),
        name="dense_ffn",
    )(x, h, wg, wu, wd)


def _router_kernel(h_ref, w_ref, b_ref, idx_ref, wt_ref):
    logits = jnp.dot(h_ref[...], w_ref[...], preferred_element_type=F32,
                     precision=lax.Precision.HIGHEST) + b_ref[...]
    lane = lax.broadcasted_iota(jnp.int32, logits.shape, 1)
    m1 = jnp.max(logits, axis=-1, keepdims=True)
    i1 = jnp.min(jnp.where(logits == m1, lane, LANES), axis=-1, keepdims=True)
    rest = jnp.where(lane == i1, NEG, logits)
    m2 = jnp.max(rest, axis=-1, keepdims=True)
    i2 = jnp.min(jnp.where(rest == m2, lane, LANES), axis=-1, keepdims=True)
    e = jnp.exp(m2 - m1)
    w1 = 1.0 / (1.0 + e)
    idx_ref[...] = jnp.where(lane == 0, i1, jnp.where(lane == 1, i2, 0))
    wt_ref[...] = jnp.where(lane == 0, w1, jnp.where(lane == 1, e * w1, 0.0))


def _router(h, w_pad, b_pad, tm=1024):
    t = h.shape[0]
    row = pl.BlockSpec((tm, LANES), lambda i: (i, 0))
    return pl.pallas_call(
        _router_kernel,
        grid=(t // tm,),
        in_specs=[pl.BlockSpec((tm, D_MODEL), lambda i: (i, 0)),
                  pl.BlockSpec(w_pad.shape, lambda i: (0, 0)), pl.BlockSpec(b_pad.shape, lambda i: (0, 0))],
        out_specs=[row, row],
        out_shape=[jax.ShapeDtypeStruct((t, LANES), jnp.int32), jax.ShapeDtypeStruct((t, LANES), F32)],
        compiler_params=_cparams(("parallel",)),
        name="moe_router",
    )(h, w_pad, b_pad)


def _gather_row_copy(src_hbm, out_ref, sem, src_row, dst_row):
    return pltpu.make_async_copy(src_hbm.at[pl.ds(src_row, 1), :], out_ref.at[pl.ds(dst_row, 1), :], sem)


def _gather_kernel(idx_ref, src_hbm, out_ref, sem, *, tm):
    base = pl.program_id(0) * tm

    def issue(r, _):
        _gather_row_copy(src_hbm, out_ref, sem, idx_ref[base + r], r).start()
        return 0

    def wait(r, _):
        _gather_row_copy(src_hbm, out_ref, sem, 0, r).wait()
        return 0

    lax.fori_loop(0, tm, issue, 0)
    lax.fori_loop(0, tm, wait, 0)


def _gather_rows(src, idx, tm=512):
    n = idx.shape[0]
    d = src.shape[1]
    return pl.pallas_call(
        functools.partial(_gather_kernel, tm=tm),
        grid_spec=pltpu.PrefetchScalarGridSpec(
            num_scalar_prefetch=1,
            grid=(n // tm,),
            in_specs=[pl.BlockSpec(memory_space=pl.ANY)],
            out_specs=pl.BlockSpec((tm, d), lambda i, idx: (i, 0)),
            scratch_shapes=[pltpu.SemaphoreType.DMA(())]),
        out_shape=jax.ShapeDtypeStruct((n, d), src.dtype),
        compiler_params=_cparams(("arbitrary",)),
        name="row_gather",
    )(idx, src)


def _moe_ffn_kernel(te_ref, nv_ref, x_ref, wg_ref, wu_ref, wd_ref, o_ref, acc_ref, xb_ref):
    i = pl.program_id(0)
    f = pl.program_id(1)
    valid = i < nv_ref[0]

    @pl.when(valid & (f == 0))
    def _():
        xb_ref[...] = x_ref[...].astype(BF16)

    @pl.when(valid)
    def _():
        h = xb_ref[...]
        a = jnp.dot(h, wg_ref[0], preferred_element_type=F32)
        u = jnp.dot(h, wu_ref[0], preferred_element_type=F32)
        y = jnp.dot((a * _sigmoid(a) * u).astype(BF16), wd_ref[0], preferred_element_type=F32)

        @pl.when(f == 0)
        def _():
            acc_ref[...] = y

        @pl.when(f > 0)
        def _():
            acc_ref[...] += y

    last = f == pl.num_programs(1) - 1

    @pl.when(valid & last)
    def _():
        o_ref[...] = acc_ref[...]

    @pl.when(jnp.logical_not(valid) & last)
    def _():
        o_ref[...] = jnp.zeros_like(o_ref)


def _moe_ffn(xs, tile_expert, n_valid, wg, wu, wd, tm, tf=512):
    n = xs.shape[0]
    nf = wg.shape[2] // tf
    return pl.pallas_call(
        _moe_ffn_kernel,
        grid_spec=pltpu.PrefetchScalarGridSpec(
            num_scalar_prefetch=2,
            grid=(n // tm, nf),
            in_specs=[pl.BlockSpec((tm, D_MODEL), lambda i, f, te, nv: (i, 0)),
                      pl.BlockSpec((1, D_MODEL, tf), lambda i, f, te, nv: (te[i], 0, f)),
                      pl.BlockSpec((1, D_MODEL, tf), lambda i, f, te, nv: (te[i], 0, f)),
                      pl.BlockSpec((1, tf, D_MODEL), lambda i, f, te, nv: (te[i], f, 0))],
            out_specs=pl.BlockSpec((tm, D_MODEL), lambda i, f, te, nv: (i, 0)),
            scratch_shapes=[pltpu.VMEM((tm, D_MODEL), F32), pltpu.VMEM((tm, D_MODEL), BF16)]),
        out_shape=jax.ShapeDtypeStruct((n, D_MODEL), F32),
        compiler_params=_cparams(("arbitrary", "arbitrary")),
        name="moe_ffn",
    )(tile_expert, n_valid, xs, wg, wu, wd)


def _combine_kernel(x_ref, y0_ref, y1_ref, wt_ref, o_ref):
    wt = wt_ref[...]
    o_ref[...] = x_ref[...] + wt[:, 0:1] * y0_ref[...] + wt[:, 1:2] * y1_ref[...]


def _combine(x, y0, y1, wt, tm=512):
    t = x.shape[0]
    row = lambda w: pl.BlockSpec((tm, w), lambda i: (i, 0))
    return pl.pallas_call(
        _combine_kernel,
        grid=(t // tm,),
        in_specs=[row(D_MODEL), row(D_MODEL), row(D_MODEL), row(LANES)],
        out_specs=row(D_MODEL),
        out_shape=jax.ShapeDtypeStruct((t, D_MODEL), F32),
        compiler_params=_cparams(("parallel",)),
        name="moe_combine",
    )(x, y0, y1, wt)


def _moe(x, h, w_router, b_router, wg, wu, wd, tm=512):
    t = x.shape[0]
    w_pad = jnp.zeros((D_MODEL, LANES), F32).at[:, :N_EXPERTS].set(w_router)
    b_pad = jnp.full((1, LANES), NEG, F32).at[0, :N_EXPERTS].set(b_router)
    idx_l, wt_l = _router(h, w_pad, b_pad)
    expert = idx_l[:, :TOP_K].reshape(-1)
    onehot = (expert[:, None] == jnp.arange(N_EXPERTS)[None, :]).astype(jnp.int32)
    csum = jnp.cumsum(onehot, axis=0)
    counts = csum[-1]
    rank = jnp.sum((csum - onehot) * onehot, axis=1)
    tiles = (counts + tm - 1) // tm
    tile_end = jnp.cumsum(tiles)
    pad_off = (tile_end - tiles) * tm
    pos = pad_off[expert] + rank
    n_rows = TOP_K * t + N_EXPERTS * tm
    n_tiles = n_rows // tm
    src_tok = jnp.zeros((n_rows,), jnp.int32).at[pos].set(jnp.arange(TOP_K * t, dtype=jnp.int32) // TOP_K)
    tile_expert = jnp.minimum(jnp.sum(tile_end[None, :] <= jnp.arange(n_tiles)[:, None], axis=1),
                              N_EXPERTS - 1).astype(jnp.int32)
    n_valid = tile_end[-1:].astype(jnp.int32)
    xs = _gather_rows(h, src_tok, tm)
    ys = _moe_ffn(xs, tile_expert, n_valid, wg, wu, wd, tm)
    pos2 = pos.reshape(t, TOP_K)
    y0 = _gather_rows(ys, pos2[:, 0], tm)
    y1 = _gather_rows(ys, pos2[:, 1], tm)
    return _combine(x, y0, y1, wt_l)


def _ple_kernel(x_ref, p_ref, wg_ref, wp_ref, g_ref, gn_ref, x_out, h_out):
    x = x_ref[...]
    gate = _sigmoid(jnp.dot(_rms(x, g_ref[...]).astype(BF16), wg_ref[...], preferred_element_type=F32))
    x2 = x + gate * jnp.dot(p_ref[...].astype(BF16), wp_ref[...], preferred_element_type=F32)
    x_out[...] = x2
    h_out[...] = _rms(x2, gn_ref[...]).astype(h_out.dtype)


def _ple(x, p, wg, wp, g, g_next, h_dtype, tm=512):
    t = x.shape[0]
    row = lambda w: pl.BlockSpec((tm, w), lambda i: (i, 0))
    full = lambda a: pl.BlockSpec(a.shape, lambda i: (0, 0))
    return pl.pallas_call(
        _ple_kernel,
        grid=(t // tm,),
        in_specs=[row(D_MODEL), row(P_DIM), full(wg), full(wp), full(g), full(g_next)],
        out_specs=[row(D_MODEL), row(D_MODEL)],
        out_shape=[jax.ShapeDtypeStruct((t, D_MODEL), F32), jax.ShapeDtypeStruct((t, D_MODEL), h_dtype)],
        compiler_params=_cparams(("parallel",)),
        name="ple",
    )(x, p, wg, wp, g, g_next)


def _rot_cols(w):
    half = w.shape[-1] // 2
    return jnp.concatenate([-w[:, half:], w[:, :half]], axis=-1)


def _layer_weights(i, w_in, w_uq, w_ukv):
    wi = w_in[i]
    z = lambda n: jnp.zeros((wi.shape[0], n), F32)
    w_kr = wi[:, IN_OFF[2]:IN_OFF[3]]
    pad = MLA_SLAB - MLA_NOPE - MLA_ROPE
    wlat = jnp.concatenate([wi[:, :IN_OFF[2]], z(MLA_NOPE), w_kr, z(pad), z(MLA_NOPE), _rot_cols(w_kr), z(pad)],
                           axis=1).astype(BF16)
    qscale = jnp.concatenate([jnp.full((DIL_WIDTH,), DIL_HD ** -0.5, F32), jnp.ones((2 * DIL_WIDTH,), F32)])
    wdil = (wi[:, IN_OFF[3]:IN_OFF[6]] * qscale).astype(BF16)
    wgate = wi[:, IN_OFF[6]:].astype(BF16)
    uq = w_uq[i].reshape(MLA_Q_RANK, MLA_HEADS, MLA_NOPE + MLA_ROPE)
    zq = lambda n: jnp.zeros((MLA_Q_RANK, MLA_HEADS, n), F32)
    rope_rot = jnp.concatenate([-uq[..., MLA_NOPE + MLA_ROPE // 2:], uq[..., MLA_NOPE:MLA_NOPE + MLA_ROPE // 2]], -1)
    wq = jnp.concatenate([uq, zq(pad)], axis=-1).reshape(MLA_Q_RANK, -1).astype(BF16)
    wqr = jnp.concatenate([zq(MLA_NOPE), rope_rot, zq(pad)], axis=-1).reshape(MLA_Q_RANK, -1).astype(BF16)
    ukv = w_ukv[i].reshape(MLA_KV_RANK, MLA_HEADS, MLA_NOPE + MLA_V)
    wk = jnp.concatenate([ukv[..., :MLA_NOPE], jnp.zeros((MLA_KV_RANK, MLA_HEADS, MLA_SLAB - MLA_NOPE), F32)],
                         axis=-1).reshape(MLA_KV_RANK, -1).astype(BF16)
    wv = ukv[..., MLA_NOPE:].reshape(MLA_KV_RANK, -1).astype(BF16)
    return wlat, wdil, wgate, wq, wqr, wk, wv


def kernel(x, p, positions, attn_norm, w_in, q_norm, w_uq, kv_norm, w_ukv, w_br_mla, w_br_dil, w_out, ffn_norm, dense_w_gate, dense_w_up, dense_w_down, router_w, router_b, moe_w_gate, moe_w_up, moe_w_down, ple_norm, ple_w_gate, ple_w_proj, final_norm):
    batch, seq, d = x.shape
    t = batch * seq
    depth = w_in.shape[0]
    xf = x.reshape(t, d)
    cos_t, sin_t = _rope_tables(positions)
    slopes = _alibi_slopes(DIL_HEADS)
    row = lambda v: v.reshape(1, -1)
    h = _norm(xf, attn_norm[0])
    for i in range(depth):
        wlat, wdil, wgate, wq, wqr, wk, wv = _layer_weights(i, w_in, w_uq, w_ukv)
        q, k, v = _mla_proj(h, wlat, wq, wqr, wk, wv, row(q_norm[i]), row(kv_norm[i]), cos_t, sin_t)
        o_mla = _mla_attn(q, k, v, batch, seq)
        qkv_d = _dil_proj(h, wdil, seq)
        o_dil = _dil_attn(qkv_d, slopes, batch, seq)
        moe_layer = i % 2 == 1
        xf, h2 = _merge(xf, h, o_mla, o_dil, wgate, w_br_mla[i].astype(BF16), w_br_dil[i].astype(BF16),
                        w_out[i].astype(BF16), row(ffn_norm[i]), F32 if moe_layer else BF16)
        j = i // 2
        if moe_layer:
            xf = _moe(xf, h2, router_w[j], router_b[j], moe_w_gate[j].astype(BF16), moe_w_up[j].astype(BF16),
                      moe_w_down[j].astype(BF16))
        else:
            xf = _ffn(xf, h2, dense_w_gate[j].astype(BF16), dense_w_up[j].astype(BF16),
                      dense_w_down[j].astype(BF16))
        last = i == depth - 1
        g_next = final_norm if last else attn_norm[i + 1]
        xf, h = _ple(xf, p[i].reshape(t, -1), ple_w_gate[i].astype(BF16), ple_w_proj[i].astype(BF16),
                     row(ple_norm[i]), row(g_next), F32 if last else BF16)
    return h.reshape(batch, seq, d)
```

```python
import functools

import numpy as np
import jax
import jax.numpy as jnp
from jax import lax
from jax.experimental import pallas as pl
from jax.experimental.pallas import tpu as pltpu

F32 = jnp.float32
BF16 = jnp.bfloat16

D_MODEL = 1024
P_DIM = 256
NORM_EPS = 1e-6
DEPTH = 2

MLA_HEADS = 8
MLA_Q_RANK = 384
MLA_KV_RANK = 256
MLA_NOPE = 64
MLA_ROPE = 32
MLA_V = 64
ROPE_THETA = 10000.0
MLA_SLAB = 128

DIL_CONFIGS = ((128, 1), (512, 4), (2048, 16))
DIL_GROUPS = 3
DIL_HPG = 8
DIL_HEADS = 24
DIL_HD = 64
DIL_WIDTH = DIL_HEADS * DIL_HD
DIL_SPAN = 128

D_FF = 3584
N_EXPERTS = 8
TOP_K = 2

LANES = 128
NEG = -1e30
LOG2E = 1.4426950408889634
VMEM_LIMIT = 56 * 1024 * 1024

IN_OFF = tuple(int(o) for o in np.cumsum((0, MLA_Q_RANK, MLA_KV_RANK, MLA_ROPE, DIL_WIDTH, DIL_WIDTH,
                                          DIL_WIDTH, D_MODEL, D_MODEL)))


def _cparams(sem):
    return pltpu.CompilerParams(dimension_semantics=sem, vmem_limit_bytes=VMEM_LIMIT)


def _rms(x, g):
    return x * lax.rsqrt(jnp.mean(x * x, axis=-1, keepdims=True) + NORM_EPS) * g


def _sigmoid(x):
    return 1.0 / (1.0 + jnp.exp(-x))


def _alibi_slopes(n):
    def pow2(m):
        start = 2.0 ** (-8.0 / m)
        return [start ** (i + 1) for i in range(m)]
    if float(np.log2(n)).is_integer():
        s = pow2(n)
    else:
        c = 2 ** int(np.floor(np.log2(n)))
        s = pow2(c) + pow2(2 * c)[0::2][: n - c]
    return jnp.asarray(sorted(s, reverse=True), dtype=F32)


def _norm_kernel(x_ref, g_ref, o_ref):
    o_ref[...] = _rms(x_ref[...], g_ref[...]).astype(o_ref.dtype)


def _norm(x, g, tm=1024):
    t, d = x.shape
    return pl.pallas_call(
        _norm_kernel,
        grid=(t // tm,),
        in_specs=[pl.BlockSpec((tm, d), lambda i: (i, 0)), pl.BlockSpec((1, d), lambda i: (0, 0))],
        out_specs=pl.BlockSpec((tm, d), lambda i: (i, 0)),
        out_shape=jax.ShapeDtypeStruct((t, d), BF16),
        compiler_params=_cparams(("parallel",)),
        name="rms_norm",
    )(x, g.reshape(1, d))


def _rope_table_kernel(pos_ref, invf_ref, cos_ref, sin_ref):
    ang = pos_ref[...].astype(F32) * invf_ref[...]
    lane = lax.broadcasted_iota(jnp.int32, ang.shape, 1)
    rope_lane = (lane >= MLA_NOPE) & (lane < MLA_NOPE + MLA_ROPE)
    cos_ref[...] = jnp.where(lane < MLA_NOPE, 1.0, jnp.where(rope_lane, jnp.cos(ang), 0.0))
    sin_ref[...] = jnp.where(rope_lane, jnp.sin(ang), 0.0)


def _rope_tables(positions, tm=2048):
    t = positions.size
    half = MLA_ROPE // 2
    inv_freq = ROPE_THETA ** (-jnp.arange(half, dtype=F32) / half)
    invf = jnp.zeros((1, LANES), F32).at[0, MLA_NOPE:MLA_NOPE + MLA_ROPE].set(jnp.concatenate([inv_freq, inv_freq]))
    pos_b = jnp.broadcast_to(positions.reshape(t, 1), (t, LANES))
    spec = pl.BlockSpec((tm, LANES), lambda i: (i, 0))
    return pl.pallas_call(
        _rope_table_kernel,
        grid=(t // tm,),
        in_specs=[spec, pl.BlockSpec((1, LANES), lambda i: (0, 0))],
        out_specs=[spec, spec],
        out_shape=[jax.ShapeDtypeStruct((t, LANES), F32)] * 2,
        compiler_params=_cparams(("parallel",)),
        name="rope_tables",
    )(pos_b, invf)


def _mla_proj_kernel(h_ref, wlat_ref, wq_ref, wqr_ref, wk_ref, wv_ref, qn_ref, kvn_ref, cos_ref, sin_ref,
                     q_out, k_out, v_out):
    lat = jnp.dot(h_ref[...], wlat_ref[...], preferred_element_type=F32)
    cqn = _rms(lat[:, :MLA_Q_RANK], qn_ref[...]).astype(BF16)
    ckvn = _rms(lat[:, MLA_Q_RANK:MLA_Q_RANK + MLA_KV_RANK], kvn_ref[...]).astype(BF16)
    cos = cos_ref[...]
    sin = sin_ref[...]
    o = MLA_Q_RANK + MLA_KV_RANK
    k_rope = lat[:, o:o + LANES] * cos + lat[:, o + LANES:o + 2 * LANES] * sin
    qa = jnp.dot(cqn, wq_ref[...], preferred_element_type=F32)
    qb = jnp.dot(cqn, wqr_ref[...], preferred_element_type=F32)
    kk = jnp.dot(ckvn, wk_ref[...], preferred_element_type=F32)
    scale = (MLA_NOPE + MLA_ROPE) ** -0.5 * LOG2E
    vv = jnp.dot(ckvn, wv_ref[...], preferred_element_type=F32)
    ones_lane = lax.broadcasted_iota(jnp.int32, cos.shape, 1) >= MLA_V
    for hd in range(MLA_HEADS):
        sl = slice(hd * MLA_SLAB, (hd + 1) * MLA_SLAB)
        q_out[:, sl] = ((qa[:, sl] * cos + qb[:, sl] * sin) * scale).astype(BF16)
        k_out[:, sl] = (kk[:, sl] + k_rope).astype(BF16)
        v_out[:, sl] = jnp.where(ones_lane, 1.0, vv[:, sl]).astype(BF16)


def _mla_proj(h, wlat, wq, wqr, wk, wv, qn, kvn, cos_t, sin_t, tm=512):
    t = h.shape[0]
    row = lambda w: pl.BlockSpec((tm, w), lambda i: (i, 0))
    full = lambda a: pl.BlockSpec(a.shape, lambda i: (0, 0))
    hs = MLA_HEADS * MLA_SLAB
    return pl.pallas_call(
        _mla_proj_kernel,
        grid=(t // tm,),
        in_specs=[row(D_MODEL), full(wlat), full(wq), full(wqr), full(wk), full(wv), full(qn), full(kvn),
                  row(LANES), row(LANES)],
        out_specs=[row(hs), row(hs), row(hs)],
        out_shape=[jax.ShapeDtypeStruct((t, hs), BF16)] * 3,
        compiler_params=_cparams(("parallel",)),
        name="mla_proj",
    )(h, wlat, wq, wqr, wk, wv, qn, kvn, cos_t, sin_t)


def _mla_attn_kernel(q_ref, k_ref, v_ref, o_ref, s_scr, *, tq):
    qi = pl.program_id(2)
    rq = lax.broadcasted_iota(jnp.int32, (tq, tq), 0)
    ck = lax.broadcasted_iota(jnp.int32, (tq, tq), 1)
    causal = ck <= rq
    nl = tq // LANES

    def tile(n):
        outs = []
        for hh in range(2):
            q = q_ref[:, hh * MLA_SLAB:(hh + 1) * MLA_SLAB]
            mx = None
            for c in range(n + 1):
                ks = k_ref[c * tq:(c + 1) * tq, hh * MLA_SLAB:(hh + 1) * MLA_SLAB]
                s = lax.dot_general(q, ks, (((1,), (1,)), ((), ())), preferred_element_type=F32)
                if c == n:
                    s = jnp.where(causal, s, NEG)
                s_scr[hh, c] = s
                parts = [s[:, i * LANES:(i + 1) * LANES] for i in range(nl)]
                mx = functools.reduce(jnp.maximum, parts if mx is None else [mx] + parts)
            m = jnp.max(mx, axis=-1, keepdims=True)
            acc = None
            for c in range(n + 1):
                vs = v_ref[c * tq:(c + 1) * tq, hh * MLA_SLAB:(hh + 1) * MLA_SLAB]
                p = jnp.exp2((s_scr[hh, c] - m).astype(BF16))
                pv = jnp.dot(p, vs, preferred_element_type=F32)
                acc = pv if acc is None else acc + pv
            outs.append(acc / acc[:, MLA_V:MLA_V + 1])
        o_ref[...] = jnp.concatenate([outs[0][:, :MLA_V], outs[1][:, :MLA_V]], axis=-1).astype(o_ref.dtype)

    for n in range(s_scr.shape[1]):
        pl.when(qi == n)(functools.partial(tile, n))


def _mla_attn(q, k, v, batch, seq, tq=512):
    t = q.shape[0]
    nq = seq // tq
    pairs = MLA_HEADS // 2
    return pl.pallas_call(
        functools.partial(_mla_attn_kernel, tq=tq),
        grid=(batch, pairs, nq),
        in_specs=[pl.BlockSpec((tq, 2 * MLA_SLAB), lambda b, p, i: (b * nq + i, p)),
                  pl.BlockSpec((seq, 2 * MLA_SLAB), lambda b, p, i: (b, p)),
                  pl.BlockSpec((seq, 2 * MLA_SLAB), lambda b, p, i: (b, p))],
        out_specs=pl.BlockSpec((tq, 2 * MLA_V), lambda b, p, i: (b * nq + i, p)),
        out_shape=jax.ShapeDtypeStruct((t, MLA_HEADS * MLA_V), BF16),
        scratch_shapes=[pltpu.VMEM((2, nq, tq, tq), F32)],
        compiler_params=_cparams(("parallel", "parallel", "arbitrary")),
        name="mla_attn",
    )(q, k, v)


def _dil_proj_kernel(h_ref, w_ref, o_ref, acc_ref, *, seq):
    res = jnp.dot(h_ref[...], w_ref[...], preferred_element_type=F32)
    g = pl.program_id(0) % DIL_GROUPS
    nl = o_ref.shape[1] // LANES

    @pl.when(g == 0)
    def _():
        o_ref[...] = res.astype(o_ref.dtype)

    @pl.when(g > 0)
    def _():
        for j in range(nl):
            acc_ref[j] = res[:, j * LANES:(j + 1) * LANES]

    for gi, (_, r) in enumerate(DIL_CONFIGS):
        if r == 1:
            continue

        @pl.when(g == gi)
        def _(r=r):
            ln = seq // r
            for c in range(r):
                for j in range(nl):
                    o_ref[c * ln:(c + 1) * ln, j * LANES:(j + 1) * LANES] = (
                        acc_ref[j, pl.ds(c, ln, stride=r), :].astype(o_ref.dtype))


def _dil_proj(h, w, seq, tn=512):
    t = h.shape[0]
    n = w.shape[1]
    return pl.pallas_call(
        functools.partial(_dil_proj_kernel, seq=seq),
        grid=(n // tn, t // seq),
        in_specs=[pl.BlockSpec((seq, D_MODEL), lambda j, b: (b, 0)),
                  pl.BlockSpec((D_MODEL, tn), lambda j, b: (0, j))],
        out_specs=pl.BlockSpec((seq, tn), lambda j, b: (b, j)),
        out_shape=jax.ShapeDtypeStruct((t, n), BF16),
        scratch_shapes=[pltpu.VMEM((tn // LANES, seq, LANES), F32)],
        compiler_params=_cparams(("parallel", "parallel")),
        name="dil_proj",
    )(h, w)


def _dil_attn_kernel(slopes_ref, q0, q1, q2, k0, k1, k2, v0, v1, v2, o_ref, acc_s, m_s, l_s, *, seq):
    pair = pl.program_id(1)
    sp = DIL_SPAN
    head0 = lax.broadcasted_iota(jnp.int32, (sp, LANES), 1) < DIL_HD
    qi = lax.broadcasted_iota(jnp.int32, (sp, 2 * sp), 0)
    kj = lax.broadcasted_iota(jnp.int32, (sp, 2 * sp), 1)
    dist_w = qi + sp - kj
    valid_w = (dist_w >= 0) & (dist_w <= sp)
    dist_1 = dist_w[:, sp:]
    valid_1 = dist_1 >= 0
    qs, ks, vs = (q0, q1, q2), (k0, k1, k2), (v0, v1, v2)
    for g, (_, r) in enumerate(DIL_CONFIGS):
        ln = seq // r
        nb = ln // sp
        sl = [slopes_ref[g * DIL_HPG + 2 * pair + hh] * float(r) for hh in range(2)]
        bias_w = jnp.concatenate([jnp.where(valid_w, -s * dist_w.astype(F32), NEG) for s in sl], axis=0)
        bias_1 = jnp.concatenate([jnp.where(valid_1, -s * dist_1.astype(F32), NEG) for s in sl], axis=0)
        for c in range(r):
            for i in range(nb):
                row0 = c * ln + i * sp
                qb = qs[g][row0:row0 + sp, :]
                zero = jnp.zeros_like(qb)
                q2h = jnp.concatenate([jnp.where(head0, qb, zero), jnp.where(head0, zero, qb)], axis=0)
                lo = row0 if i == 0 else row0 - sp
                kw = ks[g][lo:row0 + sp, :]
                vw = vs[g][lo:row0 + sp, :]
                s = lax.dot_general(q2h, kw, (((1,), (1,)), ((), ())), preferred_element_type=F32)
                s = s + (bias_1 if i == 0 else bias_w)
                m = jnp.max(s, axis=-1, keepdims=True)
                p = jnp.exp(s - m)
                l = jnp.sum(p, axis=-1, keepdims=True)
                o2 = jnp.dot(p.astype(BF16), vw, preferred_element_type=F32)
                dst = pl.ds(i * sp * r + c, sp, stride=r) if r > 1 else pl.ds(i * sp, sp)
                acc_s[g, dst, :] = jnp.where(head0, o2[:sp], o2[sp:])
                m_s[g, dst, :] = jnp.where(head0, m[:sp], m[sp:])
                l_s[g, dst, :] = jnp.where(head0, l[:sp], l[sp:])

    ch = 256

    def combine(i, _):
        rows = pl.ds(pl.multiple_of(i * ch, ch), ch)
        ms = [m_s[g, rows, :] for g in range(DIL_GROUPS)]
        mx = jnp.maximum(jnp.maximum(ms[0], ms[1]), ms[2])
        ws = [jnp.exp(m - mx) for m in ms]
        num = sum(ws[g] * acc_s[g, rows, :] for g in range(DIL_GROUPS))
        den = sum(ws[g] * l_s[g, rows, :] for g in range(DIL_GROUPS))
        o_ref[rows, :] = (num / den).astype(o_ref.dtype)
        return 0

    lax.fori_loop(0, seq // ch, combine, 0)


def _dil_attn(qkv, slopes, batch, seq):
    t = qkv.shape[0]
    pairs = DIL_HPG // 2
    cpg = DIL_HPG * DIL_HD // LANES
    cpt = DIL_WIDTH // LANES

    def spec(tt, g):
        return pl.BlockSpec((seq, LANES), lambda b, p: (b, tt * cpt + g * cpg + p))

    in_specs = [pl.BlockSpec(memory_space=pltpu.SMEM)] + [spec(tt, g) for tt in range(3) for g in range(DIL_GROUPS)]
    return pl.pallas_call(
        functools.partial(_dil_attn_kernel, seq=seq),
        grid=(batch, pairs),
        in_specs=in_specs,
        out_specs=pl.BlockSpec((seq, LANES), lambda b, p: (b, p)),
        out_shape=jax.ShapeDtypeStruct((t, DIL_HPG * DIL_HD), BF16),
        scratch_shapes=[pltpu.VMEM((DIL_GROUPS, seq, LANES), F32)] * 3,
        compiler_params=_cparams(("parallel", "parallel")),
        name="dil_attn",
    )(slopes, *([qkv] * 9))


def _merge_kernel(x_ref, h_ref, om_ref, od_ref, wg_ref, wbm_ref, wbd_ref, wo_ref, g_ref, x_out, h_out):
    gates = jnp.dot(h_ref[...], wg_ref[...], preferred_element_type=F32)
    bm = jnp.dot(om_ref[...], wbm_ref[...], preferred_element_type=F32)
    bd = jnp.dot(od_ref[...], wbd_ref[...], preferred_element_type=F32)
    merged = _sigmoid(gates[:, :D_MODEL]) * bm + _sigmoid(gates[:, D_MODEL:]) * bd
    x1 = x_ref[...] + jnp.dot(merged.astype(BF16), wo_ref[...], preferred_element_type=F32)
    x_out[...] = x1
    h_out[...] = _rms(x1, g_ref[...]).astype(h_out.dtype)


def _merge(x, h, o_mla, o_dil, wg, wbm, wbd, wo, g_next, h_dtype, tm=512):
    t = x.shape[0]
    row = lambda w: pl.BlockSpec((tm, w), lambda i: (i, 0))
    full = lambda a: pl.BlockSpec(a.shape, lambda i: (0, 0))
    return pl.pallas_call(
        _merge_kernel,
        grid=(t // tm,),
        in_specs=[row(D_MODEL), row(D_MODEL), row(o_mla.shape[1]), row(o_dil.shape[1]),
                  full(wg), full(wbm), full(wbd), full(wo), full(g_next)],
        out_specs=[row(D_MODEL), row(D_MODEL)],
        out_shape=[jax.ShapeDtypeStruct((t, D_MODEL), F32), jax.ShapeDtypeStruct((t, D_MODEL), h_dtype)],
        compiler_params=_cparams(("parallel",)),
        name="merge",
    )(x, h, o_mla, o_dil, wg, wbm, wbd, wo, g_next)


def _ffn_kernel(x_ref, h_ref, wg_ref, wu_ref, wd_ref, o_ref, acc_ref):
    f = pl.program_id(1)
    h = h_ref[...]
    a = jnp.dot(h, wg_ref[...], preferred_element_type=F32)
    u = jnp.dot(h, wu_ref[...], preferred_element_type=F32)
    y = jnp.dot((a * _sigmoid(a) * u).astype(BF16), wd_ref[...], preferred_element_type=F32)

    @pl.when(f == 0)
    def _():
        acc_ref[...] = x_ref[...] + y

    @pl.when(f > 0)
    def _():
        acc_ref[...] += y

    @pl.when(f == pl.num_programs(1) - 1)
    def _():
        o_ref[...] = acc_ref[...]


def _ffn(x, h, wg, wu, wd, tm=1024, tf=512):
    t = x.shape[0]
    nf = wg.shape[1] // tf
    return pl.pallas_call(
        _ffn_kernel,
        grid=(t // tm, nf),
        in_specs=[pl.BlockSpec((tm, D_MODEL), lambda i, f: (i, 0)),
                  pl.BlockSpec((tm, D_MODEL), lambda i, f: (i, 0)),
                  pl.BlockSpec((D_MODEL, tf), lambda i, f: (0, f)),
                  pl.BlockSpec((D_MODEL, tf), lambda i, f: (0, f)),
                  pl.BlockSpec((tf, D_MODEL), lambda i, f: (f, 0))],
        out_specs=pl.BlockSpec((tm, D_MODEL), lambda i, f: (i, 0)),
        out_shape=jax.ShapeDtypeStruct((t, D_MODEL), F32),
        scratch_shapes=[pltpu.VMEM((tm, D_MODEL), F32)],
        compiler_params=_cparams(("parallel", "arbitrary")),
        name="dense_ffn",
    )(x, h, wg, wu, wd)


def _router_kernel(h_ref, w_ref, b_ref, idx_ref, wt_ref):
    logits = jnp.dot(h_ref[...], w_ref[...], preferred_element_type=F32,
                     precision=lax.Precision.HIGHEST) + b_ref[...]
    lane = lax.broadcasted_iota(jnp.int32, logits.shape, 1)
    m1 = jnp.max(logits, axis=-1, keepdims=True)
    i1 = jnp.min(jnp.where(logits == m1, lane, LANES), axis=-1, keepdims=True)
    rest = jnp.where(lane == i1, NEG, logits)
    m2 = jnp.max(rest, axis=-1, keepdims=True)
    i2 = jnp.min(jnp.where(rest == m2, lane, LANES), axis=-1, keepdims=True)
    e = jnp.exp(m2 - m1)
    w1 = 1.0 / (1.0 + e)
    idx_ref[...] = jnp.where(lane == 0, i1, jnp.where(lane == 1, i2, 0))
    wt_ref[...] = jnp.where(lane == 0, w1, jnp.where(lane == 1, e * w1, 0.0))


def _router(h, w_pad, b_pad, tm=1024):
    t = h.shape[0]
    row = pl.BlockSpec((tm, LANES), lambda i: (i, 0))
    return pl.pallas_call(
        _router_kernel,
        grid=(t // tm,),
        in_specs=[pl.BlockSpec((tm, D_MODEL), lambda i: (i, 0)),
                  pl.BlockSpec(w_pad.shape, lambda i: (0, 0)), pl.BlockSpec(b_pad.shape, lambda i: (0, 0))],
        out_specs=[row, row],
        out_shape=[jax.ShapeDtypeStruct((t, LANES), jnp.int32), jax.ShapeDtypeStruct((t, LANES), F32)],
        compiler_params=_cparams(("parallel",)),
        name="moe_router",
    )(h, w_pad, b_pad)


def _gather_row_copy(src_hbm, out_ref, sem, src_row, dst_row):
    return pltpu.make_async_copy(src_hbm.at[pl.ds(src_row, 1), :], out_ref.at[pl.ds(dst_row, 1), :], sem)


def _gather_kernel(idx_ref, src_hbm, out_ref, sem, *, tm):
    base = pl.program_id(0) * tm

    def issue(r, _):
        _gather_row_copy(src_hbm, out_ref, sem, idx_ref[base + r], r).start()
        return 0

    def wait(r, _):
        _gather_row_copy(src_hbm, out_ref, sem, 0, r).wait()
        return 0

    lax.fori_loop(0, tm, issue, 0)
    lax.fori_loop(0, tm, wait, 0)


def _gather_rows(src, idx, tm=512):
    n = idx.shape[0]
    d = src.shape[1]
    return pl.pallas_call(
        functools.partial(_gather_kernel, tm=tm),
        grid_spec=pltpu.PrefetchScalarGridSpec(
            num_scalar_prefetch=1,
            grid=(n // tm,),
            in_specs=[pl.BlockSpec(memory_space=pl.ANY)],
            out_specs=pl.BlockSpec((tm, d), lambda i, idx: (i, 0)),
            scratch_shapes=[pltpu.SemaphoreType.DMA(())]),
        out_shape=jax.ShapeDtypeStruct((n, d), src.dtype),
        compiler_params=_cparams(("arbitrary",)),
        name="row_gather",
    )(idx, src)


def _moe_ffn_kernel(te_ref, nv_ref, x_ref, wg_ref, wu_ref, wd_ref, o_ref, acc_ref, xb_ref):
    i = pl.program_id(0)
    f = pl.program_id(1)
    valid = i < nv_ref[0]

    @pl.when(valid & (f == 0))
    def _():
        xb_ref[...] = x_ref[...].astype(BF16)

    @pl.when(valid)
    def _():
        h = xb_ref[...]
        a = jnp.dot(h, wg_ref[0], preferred_element_type=F32)
        u = jnp.dot(h, wu_ref[0], preferred_element_type=F32)
        y = jnp.dot((a * _sigmoid(a) * u).astype(BF16), wd_ref[0], preferred_element_type=F32)

        @pl.when(f == 0)
        def _():
            acc_ref[...] = y

        @pl.when(f > 0)
        def _():
            acc_ref[...] += y

    last = f == pl.num_programs(1) - 1

    @pl.when(valid & last)
    def _():
        o_ref[...] = acc_ref[...]

    @pl.when(jnp.logical_not(valid) & last)
    def _():
        o_ref[...] = jnp.zeros_like(o_ref)


def _moe_ffn(xs, tile_expert, n_valid, wg, wu, wd, tm, tf=512):
    n = xs.shape[0]
    nf = wg.shape[2] // tf
    return pl.pallas_call(
        _moe_ffn_kernel,
        grid_spec=pltpu.PrefetchScalarGridSpec(
            num_scalar_prefetch=2,
            grid=(n // tm, nf),
            in_specs=[pl.BlockSpec((tm, D_MODEL), lambda i, f, te, nv: (i, 0)),
                      pl.BlockSpec((1, D_MODEL, tf), lambda i, f, te, nv: (te[i], 0, f)),
                      pl.BlockSpec((1, D_MODEL, tf), lambda i, f, te, nv: (te[i], 0, f)),
                      pl.BlockSpec((1, tf, D_MODEL), lambda i, f, te, nv: (te[i], f, 0))],
            out_specs=pl.BlockSpec((tm, D_MODEL), lambda i, f, te, nv: (i, 0)),
            scratch_shapes=[pltpu.VMEM((tm, D_MODEL), F32), pltpu.VMEM((tm, D_MODEL), BF16)]),
        out_shape=jax.ShapeDtypeStruct((n, D_MODEL), F32),
        compiler_params=_cparams(("arbitrary", "arbitrary")),
        name="moe_ffn",
    )(tile_expert, n_valid, xs, wg, wu, wd)


def _combine_kernel(x_ref, y0_ref, y1_ref, wt_ref, o_ref):
    wt = wt_ref[...]
    o_ref[...] = x_ref[...] + wt[:, 0:1] * y0_ref[...] + wt[:, 1:2] * y1_ref[...]


def _combine(x, y0, y1, wt, tm=512):
    t = x.shape[0]
    row = lambda w: pl.BlockSpec((tm, w), lambda i: (i, 0))
    return pl.pallas_call(
        _combine_kernel,
        grid=(t // tm,),
        in_specs=[row(D_MODEL), row(D_MODEL), row(D_MODEL), row(LANES)],
        out_specs=row(D_MODEL),
        out_shape=jax.ShapeDtypeStruct((t, D_MODEL), F32),
        compiler_params=_cparams(("parallel",)),
        name="moe_combine",
    )(x, y0, y1, wt)


def _moe(x, h, w_router, b_router, wg, wu, wd, tm=512):
    t = x.shape[0]
    w_pad = jnp.zeros((D_MODEL, LANES), F32).at[:, :N_EXPERTS].set(w_router)
    b_pad = jnp.full((1, LANES), NEG, F32).at[0, :N_EXPERTS].set(b_router)
    idx_l, wt_l = _router(h, w_pad, b_pad)
    expert = idx_l[:, :TOP_K].reshape(-1)
    onehot = (expert[:, None] == jnp.arange(N_EXPERTS)[None, :]).astype(jnp.int32)
    csum = jnp.cumsum(onehot, axis=0)
    counts = csum[-1]
    rank = jnp.sum((csum - onehot) * onehot, axis=1)
    tiles = (counts + tm - 1) // tm
    tile_end = jnp.cumsum(tiles)
    pad_off = (tile_end - tiles) * tm
    pos = pad_off[expert] + rank
    n_rows = TOP_K * t + N_EXPERTS * tm
    n_tiles = n_rows // tm
    src_tok = jnp.zeros((n_rows,), jnp.int32).at[pos].set(jnp.arange(TOP_K * t, dtype=jnp.int32) // TOP_K)
    tile_expert = jnp.minimum(jnp.sum(tile_end[None, :] <= jnp.arange(n_tiles)[:, None], axis=1),
                              N_EXPERTS - 1).astype(jnp.int32)
    n_valid = tile_end[-1:].astype(jnp.int32)
    xs = _gather_rows(h, src_tok, tm)
    ys = _moe_ffn(xs, tile_expert, n_valid, wg, wu, wd, tm)
    pos2 = pos.reshape(t, TOP_K)
    y0 = _gather_rows(ys, pos2[:, 0], tm)
    y1 = _gather_rows(ys, pos2[:, 1], tm)
    return _combine(x, y0, y1, wt_l)


def _ple_kernel(x_ref, p_ref, wg_ref, wp_ref, g_ref, gn_ref, x_out, h_out):
    x = x_ref[...]
    gate = _sigmoid(jnp.dot(_rms(x, g_ref[...]).astype(BF16), wg_ref[...], preferred_element_type=F32))
    x2 = x + gate * jnp.dot(p_ref[...].astype(BF16), wp_ref[...], preferred_element_type=F32)
    x_out[...] = x2
    h_out[...] = _rms(x2, gn_ref[...]).astype(h_out.dtype)


def _ple(x, p, wg, wp, g, g_next, h_dtype, tm=512):
    t = x.shape[0]
    row = lambda w: pl.BlockSpec((tm, w), lambda i: (i, 0))
    full = lambda a: pl.BlockSpec(a.shape, lambda i: (0, 0))
    return pl.pallas_call(
        _ple_kernel,
        grid=(t // tm,),
        in_specs=[row(D_MODEL), row(P_DIM), full(wg), full(wp), full(g), full(g_next)],
        out_specs=[row(D_MODEL), row(D_MODEL)],
        out_shape=[jax.ShapeDtypeStruct((t, D_MODEL), F32), jax.ShapeDtypeStruct((t, D_MODEL), h_dtype)],
        compiler_params=_cparams(("parallel",)),
        name="ple",
    )(x, p, wg, wp, g, g_next)


def _rot_cols(w):
    half = w.shape[-1] // 2
    return jnp.concatenate([-w[:, half:], w[:, :half]], axis=-1)


def _layer_weights(i, w_in, w_uq, w_ukv):
    wi = w_in[i]
    z = lambda n: jnp.zeros((wi.shape[0], n), F32)
    w_kr = wi[:, IN_OFF[2]:IN_OFF[3]]
    pad = MLA_SLAB - MLA_NOPE - MLA_ROPE
    wlat = jnp.concatenate([wi[:, :IN_OFF[2]], z(MLA_NOPE), w_kr, z(pad), z(MLA_NOPE), _rot_cols(w_kr), z(pad)],
                           axis=1).astype(BF16)
    qscale = jnp.concatenate([jnp.full((DIL_WIDTH,), DIL_HD ** -0.5, F32), jnp.ones((2 * DIL_WIDTH,), F32)])
    wdil = (wi[:, IN_OFF[3]:IN_OFF[6]] * qscale).astype(BF16)
    wgate = wi[:, IN_OFF[6]:].astype(BF16)
    uq = w_uq[i].reshape(MLA_Q_RANK, MLA_HEADS, MLA_NOPE + MLA_ROPE)
    zq = lambda n: jnp.zeros((MLA_Q_RANK, MLA_HEADS, n), F32)
    rope_rot = jnp.concatenate([-uq[..., MLA_NOPE + MLA_ROPE // 2:], uq[..., MLA_NOPE:MLA_NOPE + MLA_ROPE // 2]], -1)
    wq = jnp.concatenate([uq, zq(pad)], axis=-1).reshape(MLA_Q_RANK, -1).astype(BF16)
    wqr = jnp.concatenate([zq(MLA_NOPE), rope_rot, zq(pad)], axis=-1).reshape(MLA_Q_RANK, -1).astype(BF16)
    ukv = w_ukv[i].reshape(MLA_KV_RANK, MLA_HEADS, MLA_NOPE + MLA_V)
    zkv = jnp.zeros((MLA_KV_RANK, MLA_HEADS, MLA_SLAB - MLA_NOPE), F32)
    wk = jnp.concatenate([ukv[..., :MLA_NOPE], zkv], axis=-1).reshape(MLA_KV_RANK, -1).astype(BF16)
    wv = jnp.concatenate([ukv[..., MLA_NOPE:], zkv], axis=-1).reshape(MLA_KV_RANK, -1).astype(BF16)
    return wlat, wdil, wgate, wq, wqr, wk, wv


def kernel(x, p, positions, attn_norm, w_in, q_norm, w_uq, kv_norm, w_ukv, w_br_mla, w_br_dil, w_out, ffn_norm, dense_w_gate, dense_w_up, dense_w_down, router_w, router_b, moe_w_gate, moe_w_up, moe_w_down, ple_norm, ple_w_gate, ple_w_proj, final_norm):
    batch, seq, d = x.shape
    t = batch * seq
    depth = w_in.shape[0]
    xf = x.reshape(t, d)
    cos_t, sin_t = _rope_tables(positions)
    slopes = _alibi_slopes(DIL_HEADS)
    row = lambda v: v.reshape(1, -1)
    h = _norm(xf, attn_norm[0])
    for i in range(depth):
        wlat, wdil, wgate, wq, wqr, wk, wv = _layer_weights(i, w_in, w_uq, w_ukv)
        q, k, v = _mla_proj(h, wlat, wq, wqr, wk, wv, row(q_norm[i]), row(kv_norm[i]), cos_t, sin_t)
        o_mla = _mla_attn(q, k, v, batch, seq)
        qkv_d = _dil_proj(h, wdil, seq)
        o_dil = _dil_attn(qkv_d, slopes, batch, seq)
        moe_layer = i % 2 == 1
        xf, h2 = _merge(xf, h, o_mla, o_dil, wgate, w_br_mla[i].astype(BF16), w_br_dil[i].astype(BF16),
                        w_out[i].astype(BF16), row(ffn_norm[i]), F32 if moe_layer else BF16)
        j = i // 2
        if moe_layer:
            xf = _moe(xf, h2, router_w[j], router_b[j], moe_w_gate[j].astype(BF16), moe_w_up[j].astype(BF16),
                      moe_w_down[j].astype(BF16))
        else:
            xf = _ffn(xf, h2, dense_w_gate[j].astype(BF16), dense_w_up[j].astype(BF16),
                      dense_w_down[j].astype(BF16))
        last = i == depth - 1
        g_next = final_norm if last else attn_norm[i + 1]
        xf, h = _ple(xf, p[i].reshape(t, -1), ple_w_gate[i].astype(BF16), ple_w_proj[i].astype(BF16),
                     row(ple_norm[i]), row(g_next), F32 if last else BF16)
    return h.reshape(batch, seq, d)
```

```python
import functools

import numpy as np
import jax
import jax.numpy as jnp
from jax import lax
from jax.experimental import pallas as pl
from jax.experimental.pallas import tpu as pltpu

F32 = jnp.float32
BF16 = jnp.bfloat16

D_MODEL = 1024
P_DIM = 256
NORM_EPS = 1e-6
DEPTH = 2

MLA_HEADS = 8
MLA_Q_RANK = 384
MLA_KV_RANK = 256
MLA_NOPE = 64
MLA_ROPE = 32
MLA_V = 64
ROPE_THETA = 10000.0
MLA_SLAB = 128

DIL_CONFIGS = ((128, 1), (512, 4), (2048, 16))
DIL_GROUPS = 3
DIL_HPG = 8
DIL_HEADS = 24
DIL_HD = 64
DIL_WIDTH = DIL_HEADS * DIL_HD
DIL_SPAN = 128

D_FF = 3584
N_EXPERTS = 8
TOP_K = 2

LANES = 128
NEG = -1e30
LOG2E = 1.4426950408889634
VMEM_LIMIT = 56 * 1024 * 1024

IN_OFF = tuple(int(o) for o in np.cumsum((0, MLA_Q_RANK, MLA_KV_RANK, MLA_ROPE, DIL_WIDTH, DIL_WIDTH,
                                          DIL_WIDTH, D_MODEL, D_MODEL)))


def _cparams(sem):
    return pltpu.CompilerParams(dimension_semantics=sem, vmem_limit_bytes=VMEM_LIMIT)


def _rms(x, g):
    return x * lax.rsqrt(jnp.mean(x * x, axis=-1, keepdims=True) + NORM_EPS) * g


def _sigmoid(x):
    return 1.0 / (1.0 + jnp.exp(-x))


def _alibi_slopes(n):
    def pow2(m):
        start = 2.0 ** (-8.0 / m)
        return [start ** (i + 1) for i in range(m)]
    if float(np.log2(n)).is_integer():
        s = pow2(n)
    else:
        c = 2 ** int(np.floor(np.log2(n)))
        s = pow2(c) + pow2(2 * c)[0::2][: n - c]
    return jnp.asarray(sorted(s, reverse=True), dtype=F32)


def _norm_kernel(x_ref, g_ref, o_ref):
    o_ref[...] = _rms(x_ref[...], g_ref[...]).astype(o_ref.dtype)


def _norm(x, g, tm=1024):
    t, d = x.shape
    return pl.pallas_call(
        _norm_kernel,
        grid=(t // tm,),
        in_specs=[pl.BlockSpec((tm, d), lambda i: (i, 0)), pl.BlockSpec((1, d), lambda i: (0, 0))],
        out_specs=pl.BlockSpec((tm, d), lambda i: (i, 0)),
        out_shape=jax.ShapeDtypeStruct((t, d), BF16),
        compiler_params=_cparams(("parallel",)),
        name="rms_norm",
    )(x, g.reshape(1, d))


def _rope_table_kernel(pos_ref, invf_ref, cos_ref, sin_ref):
    ang = pos_ref[...].astype(F32) * invf_ref[...]
    lane = lax.broadcasted_iota(jnp.int32, ang.shape, 1)
    rope_lane = (lane >= MLA_NOPE) & (lane < MLA_NOPE + MLA_ROPE)
    cos_ref[...] = jnp.where(lane < MLA_NOPE, 1.0, jnp.where(rope_lane, jnp.cos(ang), 0.0))
    sin_ref[...] = jnp.where(rope_lane, jnp.sin(ang), 0.0)


def _rope_tables(positions, tm=2048):
    t = positions.size
    half = MLA_ROPE // 2
    inv_freq = ROPE_THETA ** (-jnp.arange(half, dtype=F32) / half)
    invf = jnp.zeros((1, LANES), F32).at[0, MLA_NOPE:MLA_NOPE + MLA_ROPE].set(jnp.concatenate([inv_freq, inv_freq]))
    pos_b = jnp.broadcast_to(positions.reshape(t, 1), (t, LANES))
    spec = pl.BlockSpec((tm, LANES), lambda i: (i, 0))
    return pl.pallas_call(
        _rope_table_kernel,
        grid=(t // tm,),
        in_specs=[spec, pl.BlockSpec((1, LANES), lambda i: (0, 0))],
        out_specs=[spec, spec],
        out_shape=[jax.ShapeDtypeStruct((t, LANES), F32)] * 2,
        compiler_params=_cparams(("parallel",)),
        name="rope_tables",
    )(pos_b, invf)


def _mla_proj_kernel(h_ref, wlat_ref, wq_ref, wqr_ref, wk_ref, wv_ref, qn_ref, kvn_ref, cos_ref, sin_ref,
                     q_out, k_out, v_out):
    lat = jnp.dot(h_ref[...], wlat_ref[...], preferred_element_type=F32)
    cqn = _rms(lat[:, :MLA_Q_RANK], qn_ref[...]).astype(BF16)
    ckvn = _rms(lat[:, MLA_Q_RANK:MLA_Q_RANK + MLA_KV_RANK], kvn_ref[...]).astype(BF16)
    cos = cos_ref[...]
    sin = sin_ref[...]
    o = MLA_Q_RANK + MLA_KV_RANK
    k_rope = lat[:, o:o + LANES] * cos + lat[:, o + LANES:o + 2 * LANES] * sin
    qa = jnp.dot(cqn, wq_ref[...], preferred_element_type=F32)
    qb = jnp.dot(cqn, wqr_ref[...], preferred_element_type=F32)
    kk = jnp.dot(ckvn, wk_ref[...], preferred_element_type=F32)
    scale = (MLA_NOPE + MLA_ROPE) ** -0.5 * LOG2E
    vv = jnp.dot(ckvn, wv_ref[...], preferred_element_type=F32)
    ones_lane = lax.broadcasted_iota(jnp.int32, cos.shape, 1) >= MLA_V
    for hd in range(MLA_HEADS):
        sl = slice(hd * MLA_SLAB, (hd + 1) * MLA_SLAB)
        q_out[:, sl] = ((qa[:, sl] * cos + qb[:, sl] * sin) * scale).astype(BF16)
        k_out[:, sl] = (kk[:, sl] + k_rope).astype(BF16)
        v_out[:, sl] = jnp.where(ones_lane, 1.0, vv[:, sl]).astype(BF16)


def _mla_proj(h, wlat, wq, wqr, wk, wv, qn, kvn, cos_t, sin_t, tm=512):
    t = h.shape[0]
    row = lambda w: pl.BlockSpec((tm, w), lambda i: (i, 0))
    full = lambda a: pl.BlockSpec(a.shape, lambda i: (0, 0))
    hs = MLA_HEADS * MLA_SLAB
    return pl.pallas_call(
        _mla_proj_kernel,
        grid=(t // tm,),
        in_specs=[row(D_MODEL), full(wlat), full(wq), full(wqr), full(wk), full(wv), full(qn), full(kvn),
                  row(LANES), row(LANES)],
        out_specs=[row(hs), row(hs), row(hs)],
        out_shape=[jax.ShapeDtypeStruct((t, hs), BF16)] * 3,
        compiler_params=_cparams(("parallel",)),
        name="mla_proj",
    )(h, wlat, wq, wqr, wk, wv, qn, kvn, cos_t, sin_t)


def _mla_attn_kernel(q_ref, k_ref, v_ref, o_ref, s_scr, *, tq):
    qi = pl.program_id(2)
    rq = lax.broadcasted_iota(jnp.int32, (tq, tq), 0)
    ck = lax.broadcasted_iota(jnp.int32, (tq, tq), 1)
    causal = ck <= rq
    nl = tq // LANES

    def tile(n):
        outs = []
        for hh in range(2):
            q = q_ref[:, hh * MLA_SLAB:(hh + 1) * MLA_SLAB]
            mx = None
            for c in range(n + 1):
                ks = k_ref[c * tq:(c + 1) * tq, hh * MLA_SLAB:(hh + 1) * MLA_SLAB]
                s = lax.dot_general(q, ks, (((1,), (1,)), ((), ())), preferred_element_type=F32)
                if c == n:
                    s = jnp.where(causal, s, NEG)
                s_scr[hh, c] = s
                parts = [s[:, i * LANES:(i + 1) * LANES] for i in range(nl)]
                mx = functools.reduce(jnp.maximum, parts if mx is None else [mx] + parts)
            m = jnp.max(mx, axis=-1, keepdims=True)
            acc = None
            for c in range(n + 1):
                vs = v_ref[c * tq:(c + 1) * tq, hh * MLA_SLAB:(hh + 1) * MLA_SLAB]
                p = jnp.exp2((s_scr[hh, c] - m).astype(BF16))
                pv = jnp.dot(p, vs, preferred_element_type=F32)
                acc = pv if acc is None else acc + pv
            outs.append(acc / acc[:, MLA_V:MLA_V + 1])
        o_ref[...] = jnp.concatenate([outs[0][:, :MLA_V], outs[1][:, :MLA_V]], axis=-1).astype(o_ref.dtype)

    for n in range(s_scr.shape[1]):
        pl.when(qi == n)(functools.partial(tile, n))


def _mla_attn(q, k, v, batch, seq, tq=512):
    t = q.shape[0]
    nq = seq // tq
    pairs = MLA_HEADS // 2
    return pl.pallas_call(
        functools.partial(_mla_attn_kernel, tq=tq),
        grid=(batch, pairs, nq),
        in_specs=[pl.BlockSpec((tq, 2 * MLA_SLAB), lambda b, p, i: (b * nq + i, p)),
                  pl.BlockSpec((seq, 2 * MLA_SLAB), lambda b, p, i: (b, p)),
                  pl.BlockSpec((seq, 2 * MLA_SLAB), lambda b, p, i: (b, p))],
        out_specs=pl.BlockSpec((tq, 2 * MLA_V), lambda b, p, i: (b * nq + i, p)),
        out_shape=jax.ShapeDtypeStruct((t, MLA_HEADS * MLA_V), BF16),
        scratch_shapes=[pltpu.VMEM((2, nq, tq, tq), F32)],
        compiler_params=_cparams(("parallel", "parallel", "arbitrary")),
        name="mla_attn",
    )(q, k, v)


def _dil_proj_kernel(h_ref, w_ref, o_ref, acc_ref, *, seq):
    res = jnp.dot(h_ref[...], w_ref[...], preferred_element_type=F32)
    g = pl.program_id(0) % DIL_GROUPS
    nl = o_ref.shape[1] // LANES

    @pl.when(g == 0)
    def _():
        o_ref[...] = res.astype(o_ref.dtype)

    @pl.when(g > 0)
    def _():
        for j in range(nl):
            acc_ref[j] = res[:, j * LANES:(j + 1) * LANES]

    for gi, (_, r) in enumerate(DIL_CONFIGS):
        if r == 1:
            continue

        @pl.when(g == gi)
        def _(r=r):
            ln = seq // r
            for c in range(r):
                for j in range(nl):
                    o_ref[c * ln:(c + 1) * ln, j * LANES:(j + 1) * LANES] = (
                        acc_ref[j, pl.ds(c, ln, stride=r), :].astype(o_ref.dtype))


def _dil_proj(h, w, seq, tn=512):
    t = h.shape[0]
    n = w.shape[1]
    return pl.pallas_call(
        functools.partial(_dil_proj_kernel, seq=seq),
        grid=(n // tn, t // seq),
        in_specs=[pl.BlockSpec((seq, D_MODEL), lambda j, b: (b, 0)),
                  pl.BlockSpec((D_MODEL, tn), lambda j, b: (0, j))],
        out_specs=pl.BlockSpec((seq, tn), lambda j, b: (b, j)),
        out_shape=jax.ShapeDtypeStruct((t, n), BF16),
        scratch_shapes=[pltpu.VMEM((tn // LANES, seq, LANES), F32)],
        compiler_params=_cparams(("parallel", "parallel")),
        name="dil_proj",
    )(h, w)


def _dil_attn_kernel(slopes_ref, q0, q1, q2, k0, k1, k2, v0, v1, v2, o_ref, acc_s, m_s, l_s, *, seq):
    pair = pl.program_id(1)
    sp = DIL_SPAN
    head0 = lax.broadcasted_iota(jnp.int32, (sp, LANES), 1) < DIL_HD
    qi = lax.broadcasted_iota(jnp.int32, (sp, 2 * sp), 0)
    kj = lax.broadcasted_iota(jnp.int32, (sp, 2 * sp), 1)
    dist_w = qi + sp - kj
    valid_w = (dist_w >= 0) & (dist_w <= sp)
    dist_1 = dist_w[:, sp:]
    valid_1 = dist_1 >= 0
    qs, ks, vs = (q0, q1, q2), (k0, k1, k2), (v0, v1, v2)
    for g, (_, r) in enumerate(DIL_CONFIGS):
        ln = seq // r
        nb = ln // sp
        sl = [slopes_ref[g * DIL_HPG + 2 * pair + hh] * float(r) for hh in range(2)]
        bias_w = jnp.concatenate([jnp.where(valid_w, -s * dist_w.astype(F32), NEG) for s in sl], axis=0)
        bias_1 = jnp.concatenate([jnp.where(valid_1, -s * dist_1.astype(F32), NEG) for s in sl], axis=0)
        for c in range(r):
            for i in range(nb):
                row0 = c * ln + i * sp
                qb = qs[g][row0:row0 + sp, :]
                zero = jnp.zeros_like(qb)
                q2h = jnp.concatenate([jnp.where(head0, qb, zero), jnp.where(head0, zero, qb)], axis=0)
                lo = row0 if i == 0 else row0 - sp
                kw = ks[g][lo:row0 + sp, :]
                vw = vs[g][lo:row0 + sp, :]
                s = lax.dot_general(q2h, kw, (((1,), (1,)), ((), ())), preferred_element_type=F32)
                s = s + (bias_1 if i == 0 else bias_w)
                m = jnp.max(s, axis=-1, keepdims=True)
                p = jnp.exp(s - m)
                l = jnp.sum(p, axis=-1, keepdims=True)
                o2 = jnp.dot(p.astype(BF16), vw, preferred_element_type=F32)
                dst = pl.ds(i * sp * r + c, sp, stride=r) if r > 1 else pl.ds(i * sp, sp)
                acc_s[g, dst, :] = jnp.where(head0, o2[:sp], o2[sp:])
                m_s[g, dst, :] = jnp.where(head0, m[:sp], m[sp:])
                l_s[g, dst, :] = jnp.where(head0, l[:sp], l[sp:])

    ch = 256

    def combine(i, _):
        rows = pl.ds(pl.multiple_of(i * ch, ch), ch)
        ms = [m_s[g, rows, :] for g in range(DIL_GROUPS)]
        mx = jnp.maximum(jnp.maximum(ms[0], ms[1]), ms[2])
        ws = [jnp.exp(m - mx) for m in ms]
        num = sum(ws[g] * acc_s[g, rows, :] for g in range(DIL_GROUPS))
        den = sum(ws[g] * l_s[g, rows, :] for g in range(DIL_GROUPS))
        o_ref[rows, :] = (num / den).astype(o_ref.dtype)
        return 0

    lax.fori_loop(0, seq // ch, combine, 0)


def _dil_attn(qkv, slopes, batch, seq):
    t = qkv.shape[0]
    pairs = DIL_HPG // 2
    cpg = DIL_HPG * DIL_HD // LANES
    cpt = DIL_WIDTH // LANES

    def spec(tt, g):
        return pl.BlockSpec((seq, LANES), lambda b, p: (b, tt * cpt + g * cpg + p))

    in_specs = [pl.BlockSpec(memory_space=pltpu.SMEM)] + [spec(tt, g) for tt in range(3) for g in range(DIL_GROUPS)]
    return pl.pallas_call(
        functools.partial(_dil_attn_kernel, seq=seq),
        grid=(batch, pairs),
        in_specs=in_specs,
        out_specs=pl.BlockSpec((seq, LANES), lambda b, p: (b, p)),
        out_shape=jax.ShapeDtypeStruct((t, DIL_HPG * DIL_HD), BF16),
        scratch_shapes=[pltpu.VMEM((DIL_GROUPS, seq, LANES), F32)] * 3,
        compiler_params=_cparams(("parallel", "parallel")),
        name="dil_attn",
    )(slopes, *([qkv] * 9))


def _merge_kernel(x_ref, h_ref, om_ref, od_ref, wg_ref, wbm_ref, wbd_ref, wo_ref, g_ref, x_out, h_out):
    gates = jnp.dot(h_ref[...], wg_ref[...], preferred_element_type=F32)
    bm = jnp.dot(om_ref[...], wbm_ref[...], preferred_element_type=F32)
    bd = jnp.dot(od_ref[...], wbd_ref[...], preferred_element_type=F32)
    merged = _sigmoid(gates[:, :D_MODEL]) * bm + _sigmoid(gates[:, D_MODEL:]) * bd
    x1 = x_ref[...] + jnp.dot(merged.astype(BF16), wo_ref[...], preferred_element_type=F32)
    x_out[...] = x1
    h_out[...] = _rms(x1, g_ref[...]).astype(h_out.dtype)


def _merge(x, h, o_mla, o_dil, wg, wbm, wbd, wo, g_next, h_dtype, tm=512):
    t = x.shape[0]
    row = lambda w: pl.BlockSpec((tm, w), lambda i: (i, 0))
    full = lambda a: pl.BlockSpec(a.shape, lambda i: (0, 0))
    return pl.pallas_call(
        _merge_kernel,
        grid=(t // tm,),
        in_specs=[row(D_MODEL), row(D_MODEL), row(o_mla.shape[1]), row(o_dil.shape[1]),
                  full(wg), full(wbm), full(wbd), full(wo), full(g_next)],
        out_specs=[row(D_MODEL), row(D_MODEL)],
        out_shape=[jax.ShapeDtypeStruct((t, D_MODEL), F32), jax.ShapeDtypeStruct((t, D_MODEL), h_dtype)],
        compiler_params=_cparams(("parallel",)),
        name="merge",
    )(x, h, o_mla, o_dil, wg, wbm, wbd, wo, g_next)


def _ffn_kernel(x_ref, h_ref, wg_ref, wu_ref, wd_ref, o_ref):
    @pl.when(pl.program_id(1) == 0)
    def _():
        o_ref[...] = x_ref[...]

    h = h_ref[...]
    a = jnp.dot(h, wg_ref[...], preferred_element_type=F32)
    u = jnp.dot(h, wu_ref[...], preferred_element_type=F32)
    o_ref[...] += jnp.dot((a * _sigmoid(a) * u).astype(BF16), wd_ref[...], preferred_element_type=F32)


def _ffn(x, h, wg, wu, wd, tm=1024, tf=512):
    t = x.shape[0]
    nf = wg.shape[1] // tf
    return pl.pallas_call(
        _ffn_kernel,
        grid=(t // tm, nf),
        in_specs=[pl.BlockSpec((tm, D_MODEL), lambda i, f: (i, 0)),
                  pl.BlockSpec((tm, D_MODEL), lambda i, f: (i, 0)),
                  pl.BlockSpec((D_MODEL, tf), lambda i, f: (0, f)),
                  pl.BlockSpec((D_MODEL, tf), lambda i, f: (0, f)),
                  pl.BlockSpec((tf, D_MODEL), lambda i, f: (f, 0))],
        out_specs=pl.BlockSpec((tm, D_MODEL), lambda i, f: (i, 0)),
        out_shape=jax.ShapeDtypeStruct((t, D_MODEL), F32),
        compiler_params=_cparams(("parallel", "arbitrary")),
        name="dense_ffn",
    )(x, h, wg, wu, wd)


def _router_kernel(h_ref, w_ref, b_ref, idx_ref, wt_ref):
    logits = jnp.dot(h_ref[...], w_ref[...], preferred_element_type=F32,
                     precision=lax.Precision.HIGHEST) + b_ref[...]
    lane = lax.broadcasted_iota(jnp.int32, logits.shape, 1)
    m1 = jnp.max(logits, axis=-1, keepdims=True)
    i1 = jnp.min(jnp.where(logits == m1, lane, LANES), axis=-1, keepdims=True)
    rest = jnp.where(lane == i1, NEG, logits)
    m2 = jnp.max(rest, axis=-1, keepdims=True)
    i2 = jnp.min(jnp.where(rest == m2, lane, LANES), axis=-1, keepdims=True)
    e = jnp.exp(m2 - m1)
    w1 = 1.0 / (1.0 + e)
    idx_ref[...] = jnp.where(lane == 0, i1, jnp.where(lane == 1, i2, 0))
    wt_ref[...] = jnp.where(lane == 0, w1, jnp.where(lane == 1, e * w1, 0.0))


def _router(h, w_pad, b_pad, tm=1024):
    t = h.shape[0]
    row = pl.BlockSpec((tm, LANES), lambda i: (i, 0))
    return pl.pallas_call(
        _router_kernel,
        grid=(t // tm,),
        in_specs=[pl.BlockSpec((tm, D_MODEL), lambda i: (i, 0)),
                  pl.BlockSpec(w_pad.shape, lambda i: (0, 0)), pl.BlockSpec(b_pad.shape, lambda i: (0, 0))],
        out_specs=[row, row],
        out_shape=[jax.ShapeDtypeStruct((t, LANES), jnp.int32), jax.ShapeDtypeStruct((t, LANES), F32)],
        compiler_params=_cparams(("parallel",)),
        name="moe_router",
    )(h, w_pad, b_pad)


def _row_in(h_hbm, xbuf, sem_in, slot, tok, r):
    return pltpu.make_async_copy(h_hbm.at[pl.ds(tok, 1), :], xbuf.at[slot, pl.ds(r, 1), :], sem_in.at[slot])


def _row_out(obuf, y_hbm, sem_out, slot, r, dst):
    return pltpu.make_async_copy(obuf.at[slot, pl.ds(r, 1), :], y_hbm.at[pl.ds(dst, 1), :], sem_out.at[slot])


def _pad_fill(obuf, y_hbm, sem_fill, e, tm):
    rows = y_hbm.shape[0] - (N_EXPERTS + 1 - e) * tm
    return pltpu.make_async_copy(obuf.at[1], y_hbm.at[pl.ds(rows, tm), :], sem_fill)


def _moe_ffn_kernel(te_ref, nv_ref, src_ref, dst_ref, h_hbm, wg_ref, wu_ref, wd_ref, y_hbm,
                    xbuf, obuf, xb_ref, sem_in, sem_out, sem_fill, *, tm, rps):
    i = pl.program_id(0)
    f = pl.program_id(1)
    nf = pl.num_programs(1)
    nv = nv_ref[0]
    slot = i % 2
    other = 1 - slot
    sink = y_hbm.shape[0] - tm

    def wait_rows(buf, sem, s):
        pltpu.make_async_copy(buf.at[s], buf.at[s], sem.at[s]).wait()

    def gather_chunk(tile, buf_slot, chunk):
        for j in range(rps):
            r = chunk * rps + j
            _row_in(h_hbm, xbuf, sem_in, buf_slot, src_ref[tile * tm + r], r).start()

    def scatter_chunk(dst_of_row, chunk):
        for j in range(rps):
            r = chunk * rps + j
            _row_out(obuf, y_hbm, sem_out, other, r, dst_of_row(r)).start()

    @pl.when((i == 0) & (f == 0))
    def _():
        def body(c, _):
            gather_chunk(0, 0, c)
            return 0
        lax.fori_loop(0, nf, body, 0)
        obuf[1] = jnp.zeros((tm, D_MODEL), F32)
        for e in range(N_EXPERTS):
            _pad_fill(obuf, y_hbm, sem_fill, e, tm).start()

    @pl.when((i == 0) & (f == nf - 1))
    def _():
        for e in range(N_EXPERTS):
            _pad_fill(obuf, y_hbm, sem_fill, e, tm).wait()

    @pl.when((i <= nv) & (f == 0))
    def _():
        wait_rows(xbuf, sem_in, slot)

    @pl.when((i < nv) & (f == 0))
    def _():
        xb_ref[...] = xbuf[slot].astype(BF16)
        obuf[slot] = jnp.zeros((tm, D_MODEL), F32)

    @pl.when(i < nv)
    def _():
        gather_chunk(i + 1, other, f)
        prev = jnp.maximum(i - 1, 0) * tm
        scatter_chunk(lambda r: jnp.where(i == 0, sink + r, dst_ref[prev + r]), f)
        h = xb_ref[...]
        a = jnp.dot(h, wg_ref[0], preferred_element_type=F32)
        u = jnp.dot(h, wu_ref[0], preferred_element_type=F32)
        obuf[slot] += jnp.dot((a * _sigmoid(a) * u).astype(BF16), wd_ref[0], preferred_element_type=F32)

    @pl.when((i < nv) & (f == nf - 1))
    def _():
        wait_rows(obuf, sem_out, other)

    @pl.when((i == nv) & (f == 0))
    def _():
        def body(c, _):
            scatter_chunk(lambda r: dst_ref[(nv - 1) * tm + r], c)
            return 0
        lax.fori_loop(0, nf, body, 0)
        wait_rows(obuf, sem_out, other)


def _moe_ffn(h, tile_expert, n_valid, src_tok, dst_row, wg, wu, wd, n_out_rows, tm, tf=512):
    n_tiles = tile_expert.shape[0] - 1
    nf = wg.shape[2] // tf
    rps = tm // nf

    def wmap(i, f, te, nv, src, dst):
        return (te[i], jnp.where(i < nv[0], f, nf - 1))

    return pl.pallas_call(
        functools.partial(_moe_ffn_kernel, tm=tm, rps=rps),
        grid_spec=pltpu.PrefetchScalarGridSpec(
            num_scalar_prefetch=4,
            grid=(n_tiles + 1, nf),
            in_specs=[pl.BlockSpec(memory_space=pl.ANY),
                      pl.BlockSpec((1, D_MODEL, tf), lambda *a: (wmap(*a)[0], 0, wmap(*a)[1])),
                      pl.BlockSpec((1, D_MODEL, tf), lambda *a: (wmap(*a)[0], 0, wmap(*a)[1])),
                      pl.BlockSpec((1, tf, D_MODEL), lambda *a: (wmap(*a)[0], wmap(*a)[1], 0))],
            out_specs=pl.BlockSpec(memory_space=pl.ANY),
            scratch_shapes=[pltpu.VMEM((2, tm, D_MODEL), F32), pltpu.VMEM((2, tm, D_MODEL), F32),
                            pltpu.VMEM((tm, D_MODEL), BF16),
                            pltpu.SemaphoreType.DMA((2,)), pltpu.SemaphoreType.DMA((2,)),
                            pltpu.SemaphoreType.DMA(())]),
        out_shape=jax.ShapeDtypeStruct((n_out_rows, D_MODEL), F32),
        compiler_params=_cparams(("arbitrary", "arbitrary")),
        name="moe_ffn",
    )(tile_expert, n_valid, src_tok, dst_row, h, wg, wu, wd)


def _moe(h, w_router, b_router, wg, wu, wd, tm=448):
    t = h.shape[0]
    w_pad = jnp.zeros((D_MODEL, LANES), F32).at[:, :N_EXPERTS].set(w_router)
    b_pad = jnp.full((1, LANES), NEG, F32).at[0, :N_EXPERTS].set(b_router)
    idx_l, wt_l = _router(h, w_pad, b_pad)
    expert = idx_l[:, :TOP_K].reshape(-1)
    onehot = (expert[:, None] == jnp.arange(N_EXPERTS)[None, :]).astype(jnp.int32)
    csum = jnp.cumsum(onehot, axis=0)
    counts = csum[-1]
    rank = jnp.sum((csum - onehot) * onehot, axis=1)
    tiles = (counts + tm - 1) // tm
    tile_end = jnp.cumsum(tiles)
    pad_off = (tile_end - tiles) * tm
    pos = pad_off[expert] + rank
    n_tiles = (TOP_K * t) // tm + N_EXPERTS
    n_rows = (n_tiles + 1) * tm
    a_ids = jnp.arange(TOP_K * t, dtype=jnp.int32)
    src_tok = jnp.zeros((n_rows,), jnp.int32).at[pos].set(a_ids // TOP_K)
    n_valid = tile_end[-1:].astype(jnp.int32)
    tile_ids = jnp.minimum(jnp.arange(n_tiles + 1), n_valid[0] - 1)
    tile_expert = jnp.sum(tile_end[None, :] <= tile_ids[:, None], axis=1).astype(jnp.int32)
    pad_dst = TOP_K * t + jnp.repeat(tile_expert, tm) * tm + jnp.arange(n_rows, dtype=jnp.int32) % tm
    dst_row = pad_dst.at[pos].set((a_ids % TOP_K) * t + a_ids // TOP_K)
    y = _moe_ffn(h, tile_expert, n_valid, src_tok, dst_row, wg, wu, wd, TOP_K * t + (N_EXPERTS + 1) * tm, tm)
    return y, wt_l


def _ple_kernel(x_ref, p_ref, wg_ref, wp_ref, g_ref, gn_ref, *rest, moe):
    x = x_ref[...]
    if moe:
        y0_ref, y1_ref, wt_ref = rest[:3]
        wt = wt_ref[...]
        x = x + wt[:, 0:1] * y0_ref[...] + wt[:, 1:2] * y1_ref[...]
    x_out, h_out = rest[-2:]
    gate = _sigmoid(jnp.dot(_rms(x, g_ref[...]).astype(BF16), wg_ref[...], preferred_element_type=F32))
    x2 = x + gate * jnp.dot(p_ref[...].astype(BF16), wp_ref[...], preferred_element_type=F32)
    x_out[...] = x2
    h_out[...] = _rms(x2, gn_ref[...]).astype(h_out.dtype)


def _ple(x, p, wg, wp, g, g_next, h_dtype, moe=None, tm=512):
    t = x.shape[0]
    row = lambda w: pl.BlockSpec((tm, w), lambda i: (i, 0))
    full = lambda a: pl.BlockSpec(a.shape, lambda i: (0, 0))
    in_specs = [row(D_MODEL), row(P_DIM), full(wg), full(wp), full(g), full(g_next)]
    args = [x, p, wg, wp, g, g_next]
    if moe is not None:
        y, wt = moe
        in_specs += [row(D_MODEL), pl.BlockSpec((tm, D_MODEL), lambda i: (t // tm + i, 0)), row(LANES)]
        args += [y, y, wt]
    return pl.pallas_call(
        functools.partial(_ple_kernel, moe=moe is not None),
        grid=(t // tm,),
        in_specs=in_specs,
        out_specs=[row(D_MODEL), row(D_MODEL)],
        out_shape=[jax.ShapeDtypeStruct((t, D_MODEL), F32), jax.ShapeDtypeStruct((t, D_MODEL), h_dtype)],
        compiler_params=_cparams(("parallel",)),
        name="ple",
    )(*args)


def _rot_cols(w):
    half = w.shape[-1] // 2
    return jnp.concatenate([-w[:, half:], w[:, :half]], axis=-1)


def _layer_weights(i, w_in, w_uq, w_ukv):
    wi = w_in[i]
    z = lambda n: jnp.zeros((wi.shape[0], n), F32)
    w_kr = wi[:, IN_OFF[2]:IN_OFF[3]]
    pad = MLA_SLAB - MLA_NOPE - MLA_ROPE
    wlat = jnp.concatenate([wi[:, :IN_OFF[2]], z(MLA_NOPE), w_kr, z(pad), z(MLA_NOPE), _rot_cols(w_kr), z(pad)],
                           axis=1).astype(BF16)
    qscale = jnp.concatenate([jnp.full((DIL_WIDTH,), DIL_HD ** -0.5, F32), jnp.ones((2 * DIL_WIDTH,), F32)])
    wdil = (wi[:, IN_OFF[3]:IN_OFF[6]] * qscale).astype(BF16)
    wgate = wi[:, IN_OFF[6]:].astype(BF16)
    uq = w_uq[i].reshape(MLA_Q_RANK, MLA_HEADS, MLA_NOPE + MLA_ROPE)
    zq = lambda n: jnp.zeros((MLA_Q_RANK, MLA_HEADS, n), F32)
    rope_rot = jnp.concatenate([-uq[..., MLA_NOPE + MLA_ROPE // 2:], uq[..., MLA_NOPE:MLA_NOPE + MLA_ROPE // 2]], -1)
    wq = jnp.concatenate([uq, zq(pad)], axis=-1).reshape(MLA_Q_RANK, -1).astype(BF16)
    wqr = jnp.concatenate([zq(MLA_NOPE), rope_rot, zq(pad)], axis=-1).reshape(MLA_Q_RANK, -1).astype(BF16)
    ukv = w_ukv[i].reshape(MLA_KV_RANK, MLA_HEADS, MLA_NOPE + MLA_V)
    zkv = jnp.zeros((MLA_KV_RANK, MLA_HEADS, MLA_SLAB - MLA_NOPE), F32)
    wk = jnp.concatenate([ukv[..., :MLA_NOPE], zkv], axis=-1).reshape(MLA_KV_RANK, -1).astype(BF16)
    wv = jnp.concatenate([ukv[..., MLA_NOPE:], zkv], axis=-1).reshape(MLA_KV_RANK, -1).astype(BF16)
    return wlat, wdil, wgate, wq, wqr, wk, wv


def kernel(x, p, positions, attn_norm, w_in, q_norm, w_uq, kv_norm, w_ukv, w_br_mla, w_br_dil, w_out, ffn_norm, dense_w_gate, dense_w_up, dense_w_down, router_w, router_b, moe_w_gate, moe_w_up, moe_w_down, ple_norm, ple_w_gate, ple_w_proj, final_norm):
    batch, seq, d = x.shape
    t = batch * seq
    depth = w_in.shape[0]
    xf = x.reshape(t, d)
    cos_t, sin_t = _rope_tables(positions)
    slopes = _alibi_slopes(DIL_HEADS)
    row = lambda v: v.reshape(1, -1)
    h = _norm(xf, attn_norm[0])
    for i in range(depth):
        wlat, wdil, wgate, wq, wqr, wk, wv = _layer_weights(i, w_in, w_uq, w_ukv)
        q, k, v = _mla_proj(h, wlat, wq, wqr, wk, wv, row(q_norm[i]), row(kv_norm[i]), cos_t, sin_t)
        o_mla = _mla_attn(q, k, v, batch, seq)
        qkv_d = _dil_proj(h, wdil, seq)
        o_dil = _dil_attn(qkv_d, slopes, batch, seq)
        moe_layer = i % 2 == 1
        xf, h2 = _merge(xf, h, o_mla, o_dil, wgate, w_br_mla[i].astype(BF16), w_br_dil[i].astype(BF16),
                        w_out[i].astype(BF16), row(ffn_norm[i]), F32 if moe_layer else BF16)
        j = i // 2
        moe = None
        if moe_layer:
            moe = _moe(h2, router_w[j], router_b[j], moe_w_gate[j].astype(BF16), moe_w_up[j].astype(BF16),
                       moe_w_down[j].astype(BF16))
        else:
            xf = _ffn(xf, h2, dense_w_gate[j].astype(BF16), dense_w_up[j].astype(BF16),
                      dense_w_down[j].astype(BF16))
        last = i == depth - 1
        g_next = final_norm if last else attn_norm[i + 1]
        xf, h = _ple(xf, p[i].reshape(t, -1), ple_w_gate[i].astype(BF16), ple_w_proj[i].astype(BF16),
                     row(ple_norm[i]), row(g_next), F32 if last else BF16, moe)
    return h.reshape(batch, seq, d)
```

```python
import functools

import numpy as np
import jax
import jax.numpy as jnp
from jax import lax
from jax.experimental import pallas as pl
from jax.experimental.pallas import tpu as pltpu

F32 = jnp.float32
BF16 = jnp.bfloat16

D_MODEL = 1024
P_DIM = 256
NORM_EPS = 1e-6
DEPTH = 2

MLA_HEADS = 8
MLA_Q_RANK = 384
MLA_KV_RANK = 256
MLA_NOPE = 64
MLA_ROPE = 32
MLA_V = 64
ROPE_THETA = 10000.0
MLA_SLAB = 128

DIL_CONFIGS = ((128, 1), (512, 4), (2048, 16))
DIL_GROUPS = 3
DIL_HPG = 8
DIL_HEADS = 24
DIL_HD = 64
DIL_WIDTH = DIL_HEADS * DIL_HD
DIL_SPAN = 128

D_FF = 3584
N_EXPERTS = 8
TOP_K = 2

LANES = 128
NEG = -1e30
LOG2E = 1.4426950408889634
VMEM_LIMIT = 56 * 1024 * 1024

IN_OFF = tuple(int(o) for o in np.cumsum((0, MLA_Q_RANK, MLA_KV_RANK, MLA_ROPE, DIL_WIDTH, DIL_WIDTH,
                                          DIL_WIDTH, D_MODEL, D_MODEL)))


def _cparams(sem):
    return pltpu.CompilerParams(dimension_semantics=sem, vmem_limit_bytes=VMEM_LIMIT)


def _rms(x, g):
    return x * lax.rsqrt(jnp.mean(x * x, axis=-1, keepdims=True) + NORM_EPS) * g


def _sigmoid(x):
    return 1.0 / (1.0 + jnp.exp(-x))


def _alibi_slopes(n):
    def pow2(m):
        start = 2.0 ** (-8.0 / m)
        return [start ** (i + 1) for i in range(m)]
    if float(np.log2(n)).is_integer():
        s = pow2(n)
    else:
        c = 2 ** int(np.floor(np.log2(n)))
        s = pow2(c) + pow2(2 * c)[0::2][: n - c]
    return jnp.asarray(sorted(s, reverse=True), dtype=F32)


def _norm_kernel(x_ref, g_ref, o_ref):
    o_ref[...] = _rms(x_ref[...], g_ref[...]).astype(o_ref.dtype)


def _norm(x, g, tm=1024):
    t, d = x.shape
    return pl.pallas_call(
        _norm_kernel,
        grid=(t // tm,),
        in_specs=[pl.BlockSpec((tm, d), lambda i: (i, 0)), pl.BlockSpec((1, d), lambda i: (0, 0))],
        out_specs=pl.BlockSpec((tm, d), lambda i: (i, 0)),
        out_shape=jax.ShapeDtypeStruct((t, d), BF16),
        compiler_params=_cparams(("parallel",)),
        name="rms_norm",
    )(x, g.reshape(1, d))


def _rope_table_kernel(pos_ref, invf_ref, cos_ref, sin_ref):
    ang = pos_ref[...].astype(F32) * invf_ref[...]
    lane = lax.broadcasted_iota(jnp.int32, ang.shape, 1)
    rope_lane = (lane >= MLA_NOPE) & (lane < MLA_NOPE + MLA_ROPE)
    cos_ref[...] = jnp.where(lane < MLA_NOPE, 1.0, jnp.where(rope_lane, jnp.cos(ang), 0.0))
    sin_ref[...] = jnp.where(rope_lane, jnp.sin(ang), 0.0)


def _rope_tables(positions, tm=2048):
    t = positions.size
    half = MLA_ROPE // 2
    inv_freq = ROPE_THETA ** (-jnp.arange(half, dtype=F32) / half)
    invf = jnp.zeros((1, LANES), F32).at[0, MLA_NOPE:MLA_NOPE + MLA_ROPE].set(jnp.concatenate([inv_freq, inv_freq]))
    pos_b = jnp.broadcast_to(positions.reshape(t, 1), (t, LANES))
    spec = pl.BlockSpec((tm, LANES), lambda i: (i, 0))
    return pl.pallas_call(
        _rope_table_kernel,
        grid=(t // tm,),
        in_specs=[spec, pl.BlockSpec((1, LANES), lambda i: (0, 0))],
        out_specs=[spec, spec],
        out_shape=[jax.ShapeDtypeStruct((t, LANES), F32)] * 2,
        compiler_params=_cparams(("parallel",)),
        name="rope_tables",
    )(pos_b, invf)


def _mla_proj_kernel(h_ref, wlat_ref, wq_ref, wqr_ref, wk_ref, wv_ref, qn_ref, kvn_ref, cos_ref, sin_ref,
                     q_out, k_out, v_out):
    lat = jnp.dot(h_ref[...], wlat_ref[...], preferred_element_type=F32)
    cqn = _rms(lat[:, :MLA_Q_RANK], qn_ref[...]).astype(BF16)
    ckvn = _rms(lat[:, MLA_Q_RANK:MLA_Q_RANK + MLA_KV_RANK], kvn_ref[...]).astype(BF16)
    cos = cos_ref[...]
    sin = sin_ref[...]
    o = MLA_Q_RANK + MLA_KV_RANK
    k_rope = lat[:, o:o + LANES] * cos + lat[:, o + LANES:o + 2 * LANES] * sin
    qa = jnp.dot(cqn, wq_ref[...], preferred_element_type=F32)
    qb = jnp.dot(cqn, wqr_ref[...], preferred_element_type=F32)
    kk = jnp.dot(ckvn, wk_ref[...], preferred_element_type=F32)
    scale = (MLA_NOPE + MLA_ROPE) ** -0.5 * LOG2E
    vv = jnp.dot(ckvn, wv_ref[...], preferred_element_type=F32)
    ones_lane = lax.broadcasted_iota(jnp.int32, cos.shape, 1) >= MLA_V
    for hd in range(MLA_HEADS):
        sl = slice(hd * MLA_SLAB, (hd + 1) * MLA_SLAB)
        q_out[:, sl] = ((qa[:, sl] * cos + qb[:, sl] * sin) * scale).astype(BF16)
        k_out[:, sl] = (kk[:, sl] + k_rope).astype(BF16)
        v_out[:, sl] = jnp.where(ones_lane, 1.0, vv[:, sl]).astype(BF16)


def _mla_proj(h, wlat, wq, wqr, wk, wv, qn, kvn, cos_t, sin_t, tm=512):
    t = h.shape[0]
    row = lambda w: pl.BlockSpec((tm, w), lambda i: (i, 0))
    full = lambda a: pl.BlockSpec(a.shape, lambda i: (0, 0))
    hs = MLA_HEADS * MLA_SLAB
    return pl.pallas_call(
        _mla_proj_kernel,
        grid=(t // tm,),
        in_specs=[row(D_MODEL), full(wlat), full(wq), full(wqr), full(wk), full(wv), full(qn), full(kvn),
                  row(LANES), row(LANES)],
        out_specs=[row(hs), row(hs), row(hs)],
        out_shape=[jax.ShapeDtypeStruct((t, hs), BF16)] * 3,
        compiler_params=_cparams(("parallel",)),
        name="mla_proj",
    )(h, wlat, wq, wqr, wk, wv, qn, kvn, cos_t, sin_t)


def _mla_attn_kernel(q_ref, k_ref, v_ref, o_ref, s_scr, *, tq):
    qi = pl.program_id(2)
    rq = lax.broadcasted_iota(jnp.int32, (tq, tq), 0)
    ck = lax.broadcasted_iota(jnp.int32, (tq, tq), 1)
    causal = ck <= rq
    nl = tq // LANES

    def tile(n):
        outs = []
        for hh in range(2):
            q = q_ref[:, hh * MLA_SLAB:(hh + 1) * MLA_SLAB]
            mx = None
            for c in range(n + 1):
                ks = k_ref[c * tq:(c + 1) * tq, hh * MLA_SLAB:(hh + 1) * MLA_SLAB]
                s = lax.dot_general(q, ks, (((1,), (1,)), ((), ())), preferred_element_type=F32)
                if c == n:
                    s = jnp.where(causal, s, NEG)
                s_scr[hh, c] = s
                parts = [s[:, i * LANES:(i + 1) * LANES] for i in range(nl)]
                mx = functools.reduce(jnp.maximum, parts if mx is None else [mx] + parts)
            m = jnp.max(mx, axis=-1, keepdims=True)
            acc = None
            for c in range(n + 1):
                vs = v_ref[c * tq:(c + 1) * tq, hh * MLA_SLAB:(hh + 1) * MLA_SLAB]
                p = jnp.exp2((s_scr[hh, c] - m).astype(BF16))
                pv = jnp.dot(p, vs, preferred_element_type=F32)
                acc = pv if acc is None else acc + pv
            outs.append(acc / acc[:, MLA_V:MLA_V + 1])
        o_ref[...] = jnp.concatenate([outs[0][:, :MLA_V], outs[1][:, :MLA_V]], axis=-1).astype(o_ref.dtype)

    for n in range(s_scr.shape[1]):
        pl.when(qi == n)(functools.partial(tile, n))


def _mla_attn(q, k, v, batch, seq, tq=512):
    t = q.shape[0]
    nq = seq // tq
    pairs = MLA_HEADS // 2
    return pl.pallas_call(
        functools.partial(_mla_attn_kernel, tq=tq),
        grid=(batch, pairs, nq),
        in_specs=[pl.BlockSpec((tq, 2 * MLA_SLAB), lambda b, p, i: (b * nq + i, p)),
                  pl.BlockSpec((seq, 2 * MLA_SLAB), lambda b, p, i: (b, p)),
                  pl.BlockSpec((seq, 2 * MLA_SLAB), lambda b, p, i: (b, p))],
        out_specs=pl.BlockSpec((tq, 2 * MLA_V), lambda b, p, i: (b * nq + i, p)),
        out_shape=jax.ShapeDtypeStruct((t, MLA_HEADS * MLA_V), BF16),
        scratch_shapes=[pltpu.VMEM((2, nq, tq, tq), F32)],
        compiler_params=_cparams(("parallel", "parallel", "arbitrary")),
        name="mla_attn",
    )(q, k, v)


PHASE_STRIDE = 4


def _phase_of_slot(r):
    if r <= PHASE_STRIDE:
        return list(range(r))
    f2 = r // PHASE_STRIDE
    return [c1 + PHASE_STRIDE * c2 for c1 in range(PHASE_STRIDE) for c2 in range(f2)]


def _dil_proj_kernel(h_ref, w_ref, o_ref, acc_ref, tmp_ref, *, seq):
    res = jnp.dot(h_ref[...], w_ref[...], preferred_element_type=F32)
    g = pl.program_id(0) % DIL_GROUPS
    nl = o_ref.shape[0]

    @pl.when(g == 0)
    def _():
        for j in range(nl):
            o_ref[j] = res[:, j * LANES:(j + 1) * LANES].astype(o_ref.dtype)

    @pl.when(g > 0)
    def _():
        for j in range(nl):
            acc_ref[j] = res[:, j * LANES:(j + 1) * LANES]

    for gi, (_, r) in enumerate(DIL_CONFIGS):
        if r == 1:
            continue

        @pl.when(g == gi)
        def _(r=r):
            f1 = min(r, PHASE_STRIDE)
            l1 = seq // f1
            dst = o_ref if r == f1 else tmp_ref
            for c1 in range(f1):
                for j in range(nl):
                    dst[j, c1 * l1:(c1 + 1) * l1, :] = acc_ref[j, pl.ds(c1, l1, stride=f1), :].astype(dst.dtype)
            if r > f1:
                f2 = r // f1
                l2 = l1 // f2
                for s in range(r):
                    c1, c2 = divmod(s, f2)
                    for j in range(nl):
                        o_ref[j, s * l2:(s + 1) * l2, :] = (
                            tmp_ref[j, pl.ds(c1 * l1 + c2, l2, stride=f2), :].astype(o_ref.dtype))


def _dil_proj(h, w, seq, tn=512):
    t = h.shape[0]
    n = w.shape[1]
    nl = tn // LANES
    return pl.pallas_call(
        functools.partial(_dil_proj_kernel, seq=seq),
        grid=(n // tn, t // seq),
        in_specs=[pl.BlockSpec((seq, D_MODEL), lambda j, b: (b, 0)),
                  pl.BlockSpec((D_MODEL, tn), lambda j, b: (0, j))],
        out_specs=pl.BlockSpec((nl, seq, LANES), lambda j, b: (j, b, 0)),
        out_shape=jax.ShapeDtypeStruct((n // LANES, t, LANES), BF16),
        scratch_shapes=[pltpu.VMEM((nl, seq, LANES), F32)] * 2,
        compiler_params=_cparams(("parallel", "parallel")),
        name="dil_proj",
    )(h, w)


def _dil_attn_kernel(slopes_ref, q0, q1, q2, k0, k1, k2, v0, v1, v2, o_ref, acc_s, m_s, l_s, *, seq):
    pair = pl.program_id(1)
    sp = DIL_SPAN
    head0 = lax.broadcasted_iota(jnp.int32, (sp, LANES), 1) < DIL_HD
    qi = lax.broadcasted_iota(jnp.int32, (sp, 2 * sp), 0)
    kj = lax.broadcasted_iota(jnp.int32, (sp, 2 * sp), 1)
    dist_w = qi + sp - kj
    valid_w = (dist_w >= 0) & (dist_w <= sp)
    dist_1 = dist_w[:, sp:]
    valid_1 = dist_1 >= 0
    qs, ks, vs = (q0, q1, q2), (k0, k1, k2), (v0, v1, v2)
    for g, (_, r) in enumerate(DIL_CONFIGS):
        ln = seq // r
        nb = ln // sp
        sl = [slopes_ref[g * DIL_HPG + 2 * pair + hh] * float(r) for hh in range(2)]
        bias_w = jnp.concatenate([jnp.where(valid_w, -s * dist_w.astype(F32), NEG) for s in sl], axis=0)
        bias_1 = jnp.concatenate([jnp.where(valid_1, -s * dist_1.astype(F32), NEG) for s in sl], axis=0)
        for slot_i, c in enumerate(_phase_of_slot(r)):
            for i in range(nb):
                row0 = slot_i * ln + i * sp
                qb = qs[g][0, row0:row0 + sp, :]
                zero = jnp.zeros_like(qb)
                q2h = jnp.concatenate([jnp.where(head0, qb, zero), jnp.where(head0, zero, qb)], axis=0)
                lo = row0 if i == 0 else row0 - sp
                kw = ks[g][0, lo:row0 + sp, :]
                vw = vs[g][0, lo:row0 + sp, :]
                s = lax.dot_general(q2h, kw, (((1,), (1,)), ((), ())), preferred_element_type=F32)
                s = s + (bias_1 if i == 0 else bias_w)
                m = jnp.max(s, axis=-1, keepdims=True)
                p = jnp.exp(s - m)
                l = jnp.sum(p, axis=-1, keepdims=True)
                o2 = jnp.dot(p.astype(BF16), vw, preferred_element_type=F32)
                dst = pl.ds(i * sp * r + c, sp, stride=r) if r > 1 else pl.ds(i * sp, sp)
                acc_s[g, dst, :] = jnp.where(head0, o2[:sp], o2[sp:])
                m_s[g, dst, :] = jnp.where(head0, m[:sp], m[sp:])
                l_s[g, dst, :] = jnp.where(head0, l[:sp], l[sp:])

    ch = 256

    def combine(i, _):
        rows = pl.ds(pl.multiple_of(i * ch, ch), ch)
        ms = [m_s[g, rows, :] for g in range(DIL_GROUPS)]
        mx = jnp.maximum(jnp.maximum(ms[0], ms[1]), ms[2])
        ws = [jnp.exp(m - mx) for m in ms]
        num = sum(ws[g] * acc_s[g, rows, :] for g in range(DIL_GROUPS))
        den = sum(ws[g] * l_s[g, rows, :] for g in range(DIL_GROUPS))
        o_ref[rows, :] = (num / den).astype(o_ref.dtype)
        return 0

    lax.fori_loop(0, seq // ch, combine, 0)


def _dil_attn(qkv, slopes, batch, seq):
    t = qkv.shape[1]
    pairs = DIL_HPG // 2
    cpg = DIL_HPG * DIL_HD // LANES
    cpt = DIL_WIDTH // LANES

    def spec(tt, g):
        return pl.BlockSpec((1, seq, LANES), lambda b, p: (tt * cpt + g * cpg + p, b, 0))

    in_specs = [pl.BlockSpec(memory_space=pltpu.SMEM)] + [spec(tt, g) for tt in range(3) for g in range(DIL_GROUPS)]
    return pl.pallas_call(
        functools.partial(_dil_attn_kernel, seq=seq),
        grid=(batch, pairs),
        in_specs=in_specs,
        out_specs=pl.BlockSpec((seq, LANES), lambda b, p: (b, p)),
        out_shape=jax.ShapeDtypeStruct((t, DIL_HPG * DIL_HD), BF16),
        scratch_shapes=[pltpu.VMEM((DIL_GROUPS, seq, LANES), F32)] * 3,
        compiler_params=_cparams(("parallel", "parallel")),
        name="dil_attn",
    )(slopes, *([qkv] * 9))


def _merge_kernel(x_ref, h_ref, om_ref, od_ref, wg_ref, wbm_ref, wbd_ref, wo_ref, g_ref, x_out, h_out):
    gates = jnp.dot(h_ref[...], wg_ref[...], preferred_element_type=F32)
    bm = jnp.dot(om_ref[...], wbm_ref[...], preferred_element_type=F32)
    bd = jnp.dot(od_ref[...], wbd_ref[...], preferred_element_type=F32)
    merged = _sigmoid(gates[:, :D_MODEL]) * bm + _sigmoid(gates[:, D_MODEL:]) * bd
    x1 = x_ref[...] + jnp.dot(merged.astype(BF16), wo_ref[...], preferred_element_type=F32)
    x_out[...] = x1
    h_out[...] = _rms(x1, g_ref[...]).astype(h_out.dtype)


def _merge(x, h, o_mla, o_dil, wg, wbm, wbd, wo, g_next, h_dtype, tm=512):
    t = x.shape[0]
    row = lambda w: pl.BlockSpec((tm, w), lambda i: (i, 0))
    full = lambda a: pl.BlockSpec(a.shape, lambda i: (0, 0))
    return pl.pallas_call(
        _merge_kernel,
        grid=(t // tm,),
        in_specs=[row(D_MODEL), row(D_MODEL), row(o_mla.shape[1]), row(o_dil.shape[1]),
                  full(wg), full(wbm), full(wbd), full(wo), full(g_next)],
        out_specs=[row(D_MODEL), row(D_MODEL)],
        out_shape=[jax.ShapeDtypeStruct((t, D_MODEL), F32), jax.ShapeDtypeStruct((t, D_MODEL), h_dtype)],
        compiler_params=_cparams(("parallel",)),
        name="merge",
    )(x, h, o_mla, o_dil, wg, wbm, wbd, wo, g_next)


def _ffn_kernel(x_ref, h_ref, wg_ref, wu_ref, wd_ref, o_ref):
    @pl.when(pl.program_id(1) == 0)
    def _():
        o_ref[...] = x_ref[...]

    h = h_ref[...]
    a = jnp.dot(h, wg_ref[...], preferred_element_type=F32)
    u = jnp.dot(h, wu_ref[...], preferred_element_type=F32)
    o_ref[...] += jnp.dot((a * _sigmoid(a) * u).astype(BF16), wd_ref[...], preferred_element_type=F32)


def _ffn(x, h, wg, wu, wd, tm=1024, tf=512):
    t = x.shape[0]
    nf = wg.shape[1] // tf
    return pl.pallas_call(
        _ffn_kernel,
        grid=(t // tm, nf),
        in_specs=[pl.BlockSpec((tm, D_MODEL), lambda i, f: (i, 0)),
                  pl.BlockSpec((tm, D_MODEL), lambda i, f: (i, 0)),
                  pl.BlockSpec((D_MODEL, tf), lambda i, f: (0, f)),
                  pl.BlockSpec((D_MODEL, tf), lambda i, f: (0, f)),
                  pl.BlockSpec((tf, D_MODEL), lambda i, f: (f, 0))],
        out_specs=pl.BlockSpec((tm, D_MODEL), lambda i, f: (i, 0)),
        out_shape=jax.ShapeDtypeStruct((t, D_MODEL), F32),
        compiler_params=_cparams(("parallel", "arbitrary")),
        name="dense_ffn",
    )(x, h, wg, wu, wd)


def _router_kernel(h_ref, w_ref, b_ref, idx_ref, wt_ref):
    logits = jnp.dot(h_ref[...], w_ref[...], preferred_element_type=F32,
                     precision=lax.Precision.HIGHEST) + b_ref[...]
    lane = lax.broadcasted_iota(jnp.int32, logits.shape, 1)
    m1 = jnp.max(logits, axis=-1, keepdims=True)
    i1 = jnp.min(jnp.where(logits == m1, lane, LANES), axis=-1, keepdims=True)
    rest = jnp.where(lane == i1, NEG, logits)
    m2 = jnp.max(rest, axis=-1, keepdims=True)
    i2 = jnp.min(jnp.where(rest == m2, lane, LANES), axis=-1, keepdims=True)
    e = jnp.exp(m2 - m1)
    w1 = 1.0 / (1.0 + e)
    idx_ref[...] = jnp.where(lane == 0, i1, jnp.where(lane == 1, i2, 0))
    wt_ref[...] = jnp.where(lane == 0, w1, jnp.where(lane == 1, e * w1, 0.0))


def _router(h, w_pad, b_pad, tm=1024):
    t = h.shape[0]
    row = pl.BlockSpec((tm, LANES), lambda i: (i, 0))
    return pl.pallas_call(
        _router_kernel,
        grid=(t // tm,),
        in_specs=[pl.BlockSpec((tm, D_MODEL), lambda i: (i, 0)),
                  pl.BlockSpec(w_pad.shape, lambda i: (0, 0)), pl.BlockSpec(b_pad.shape, lambda i: (0, 0))],
        out_specs=[row, row],
        out_shape=[jax.ShapeDtypeStruct((t, LANES), jnp.int32), jax.ShapeDtypeStruct((t, LANES), F32)],
        compiler_params=_cparams(("parallel",)),
        name="moe_router",
    )(h, w_pad, b_pad)


def _row_in(h_hbm, xbuf, sem_in, slot, tok, r):
    return pltpu.make_async_copy(h_hbm.at[pl.ds(tok, 1), :], xbuf.at[slot, pl.ds(r, 1), :], sem_in.at[slot])


def _row_out(obuf, y_hbm, sem_out, slot, r, dst):
    return pltpu.make_async_copy(obuf.at[slot, pl.ds(r, 1), :], y_hbm.at[pl.ds(dst, 1), :], sem_out.at[slot])


def _pad_fill(obuf, y_hbm, sem_fill, e, tm):
    rows = y_hbm.shape[0] - (N_EXPERTS + 1 - e) * tm
    return pltpu.make_async_copy(obuf.at[1], y_hbm.at[pl.ds(rows, tm), :], sem_fill)


def _moe_ffn_kernel(te_ref, nv_ref, src_ref, dst_ref, h_hbm, wg_ref, wu_ref, wd_ref, y_hbm,
                    xbuf, obuf, xb_ref, sem_in, sem_out, sem_fill, *, tm, rps):
    i = pl.program_id(0)
    f = pl.program_id(1)
    nf = pl.num_programs(1)
    nv = nv_ref[0]
    slot = i % 2
    other = 1 - slot
    sink = y_hbm.shape[0] - tm

    def wait_rows(buf, sem, s):
        pltpu.make_async_copy(buf.at[s], buf.at[s], sem.at[s]).wait()

    def gather_chunk(tile, buf_slot, chunk):
        for j in range(rps):
            r = chunk * rps + j
            _row_in(h_hbm, xbuf, sem_in, buf_slot, src_ref[tile * tm + r], r).start()

    def scatter_chunk(dst_of_row, chunk):
        for j in range(rps):
            r = chunk * rps + j
            _row_out(obuf, y_hbm, sem_out, other, r, dst_of_row(r)).start()

    @pl.when((i == 0) & (f == 0))
    def _():
        def body(c, _):
            gather_chunk(0, 0, c)
            return 0
        lax.fori_loop(0, nf, body, 0)
        obuf[1] = jnp.zeros((tm, D_MODEL), F32)
        for e in range(N_EXPERTS):
            _pad_fill(obuf, y_hbm, sem_fill, e, tm).start()

    @pl.when((i == 0) & (f == nf - 1))
    def _():
        for e in range(N_EXPERTS):
            _pad_fill(obuf, y_hbm, sem_fill, e, tm).wait()

    @pl.when((i <= nv) & (f == 0))
    def _():
        wait_rows(xbuf, sem_in, slot)

    @pl.when((i < nv) & (f == 0))
    def _():
        xb_ref[...] = xbuf[slot].astype(BF16)
        obuf[slot] = jnp.zeros((tm, D_MODEL), F32)

    @pl.when(i < nv)
    def _():
        gather_chunk(i + 1, other, f)
        prev = jnp.maximum(i - 1, 0) * tm
        scatter_chunk(lambda r: jnp.where(i == 0, sink + r, dst_ref[prev + r]), f)
        h = xb_ref[...]
        a = jnp.dot(h, wg_ref[0], preferred_element_type=F32)
        u = jnp.dot(h, wu_ref[0], preferred_element_type=F32)
        obuf[slot] += jnp.dot((a * _sigmoid(a) * u).astype(BF16), wd_ref[0], preferred_element_type=F32)

    @pl.when((i < nv) & (f == nf - 1))
    def _():
        wait_rows(obuf, sem_out, other)

    @pl.when((i == nv) & (f == 0))
    def _():
        def body(c, _):
            scatter_chunk(lambda r: dst_ref[(nv - 1) * tm + r], c)
            return 0
        lax.fori_loop(0, nf, body, 0)
        wait_rows(obuf, sem_out, other)


def _moe_ffn(h, tile_expert, n_valid, src_tok, dst_row, wg, wu, wd, n_out_rows, tm, tf=1792):
    n_tiles = tile_expert.shape[0] - 1
    nf = wg.shape[2] // tf
    rps = tm // nf

    def wmap(i, f, te, nv, src, dst):
        return (te[i], jnp.where(i < nv[0], f, nf - 1))

    return pl.pallas_call(
        functools.partial(_moe_ffn_kernel, tm=tm, rps=rps),
        grid_spec=pltpu.PrefetchScalarGridSpec(
            num_scalar_prefetch=4,
            grid=(n_tiles + 1, nf),
            in_specs=[pl.BlockSpec(memory_space=pl.ANY),
                      pl.BlockSpec((1, D_MODEL, tf), lambda *a: (wmap(*a)[0], 0, wmap(*a)[1])),
                      pl.BlockSpec((1, D_MODEL, tf), lambda *a: (wmap(*a)[0], 0, wmap(*a)[1])),
                      pl.BlockSpec((1, tf, D_MODEL), lambda *a: (wmap(*a)[0], wmap(*a)[1], 0))],
            out_specs=pl.BlockSpec(memory_space=pl.ANY),
            scratch_shapes=[pltpu.VMEM((2, tm, D_MODEL), F32), pltpu.VMEM((2, tm, D_MODEL), F32),
                            pltpu.VMEM((tm, D_MODEL), BF16),
                            pltpu.SemaphoreType.DMA((2,)), pltpu.SemaphoreType.DMA((2,)),
                            pltpu.SemaphoreType.DMA(())]),
        out_shape=jax.ShapeDtypeStruct((n_out_rows, D_MODEL), F32),
        compiler_params=_cparams(("arbitrary", "arbitrary")),
        name="moe_ffn",
    )(tile_expert, n_valid, src_tok, dst_row, h, wg, wu, wd)


def _moe(h, w_router, b_router, wg, wu, wd, tm=448):
    t = h.shape[0]
    w_pad = jnp.zeros((D_MODEL, LANES), F32).at[:, :N_EXPERTS].set(w_router)
    b_pad = jnp.full((1, LANES), NEG, F32).at[0, :N_EXPERTS].set(b_router)
    idx_l, wt_l = _router(h, w_pad, b_pad)
    expert = idx_l[:, :TOP_K].reshape(-1)
    onehot = (expert[:, None] == jnp.arange(N_EXPERTS)[None, :]).astype(jnp.int32)
    csum = jnp.cumsum(onehot, axis=0)
    counts = csum[-1]
    rank = jnp.sum((csum - onehot) * onehot, axis=1)
    tiles = (counts + tm - 1) // tm
    tile_end = jnp.cumsum(tiles)
    pad_off = (tile_end - tiles) * tm
    pos = pad_off[expert] + rank
    n_tiles = (TOP_K * t) // tm + N_EXPERTS
    n_rows = (n_tiles + 1) * tm
    inv = jnp.full((n_rows,), -1, jnp.int32).at[pos].set(jnp.arange(TOP_K * t, dtype=jnp.int32))
    real = inv >= 0
    src_tok = jnp.where(real, inv // TOP_K, 0)
    n_valid = tile_end[-1:].astype(jnp.int32)
    tile_ids = jnp.minimum(jnp.arange(n_tiles + 1), n_valid[0] - 1)
    tile_expert = jnp.sum(tile_end[None, :] <= tile_ids[:, None], axis=1).astype(jnp.int32)
    pad_dst = TOP_K * t + jnp.repeat(tile_expert, tm) * tm + jnp.arange(n_rows, dtype=jnp.int32) % tm
    dst_row = jnp.where(real, (inv % TOP_K) * t + inv // TOP_K, pad_dst)
    y = _moe_ffn(h, tile_expert, n_valid, src_tok, dst_row, wg, wu, wd, TOP_K * t + (N_EXPERTS + 1) * tm, tm)
    return y, wt_l


def _ple_kernel(x_ref, p_ref, wg_ref, wp_ref, g_ref, gn_ref, *rest, moe):
    x = x_ref[...]
    if moe:
        y0_ref, y1_ref, wt_ref = rest[:3]
        wt = wt_ref[...]
        x = x + wt[:, 0:1] * y0_ref[...] + wt[:, 1:2] * y1_ref[...]
    x_out, h_out = rest[-2:]
    gate = _sigmoid(jnp.dot(_rms(x, g_ref[...]).astype(BF16), wg_ref[...], preferred_element_type=F32))
    x2 = x + gate * jnp.dot(p_ref[...].astype(BF16), wp_ref[...], preferred_element_type=F32)
    x_out[...] = x2
    h_out[...] = _rms(x2, gn_ref[...]).astype(h_out.dtype)


def _ple(x, p, wg, wp, g, g_next, h_dtype, moe=None, tm=512):
    t = x.shape[0]
    row = lambda w: pl.BlockSpec((tm, w), lambda i: (i, 0))
    full = lambda a: pl.BlockSpec(a.shape, lambda i: (0, 0))
    in_specs = [row(D_MODEL), row(P_DIM), full(wg), full(wp), full(g), full(g_next)]
    args = [x, p, wg, wp, g, g_next]
    if moe is not None:
        y, wt = moe
        in_specs += [row(D_MODEL), pl.BlockSpec((tm, D_MODEL), lambda i: (t // tm + i, 0)), row(LANES)]
        args += [y, y, wt]
    return pl.pallas_call(
        functools.partial(_ple_kernel, moe=moe is not None),
        grid=(t // tm,),
        in_specs=in_specs,
        out_specs=[row(D_MODEL), row(D_MODEL)],
        out_shape=[jax.ShapeDtypeStruct((t, D_MODEL), F32), jax.ShapeDtypeStruct((t, D_MODEL), h_dtype)],
        compiler_params=_cparams(("parallel",)),
        name="ple",
    )(*args)


def _rot_cols(w):
    half = w.shape[-1] // 2
    return jnp.concatenate([-w[:, half:], w[:, :half]], axis=-1)


def _layer_weights(i, w_in, w_uq, w_ukv):
    wi = w_in[i]
    z = lambda n: jnp.zeros((wi.shape[0], n), F32)
    w_kr = wi[:, IN_OFF[2]:IN_OFF[3]]
    pad = MLA_SLAB - MLA_NOPE - MLA_ROPE
    wlat = jnp.concatenate([wi[:, :IN_OFF[2]], z(MLA_NOPE), w_kr, z(pad), z(MLA_NOPE), _rot_cols(w_kr), z(pad)],
                           axis=1).astype(BF16)
    qscale = jnp.concatenate([jnp.full((DIL_WIDTH,), DIL_HD ** -0.5, F32), jnp.ones((2 * DIL_WIDTH,), F32)])
    wdil = (wi[:, IN_OFF[3]:IN_OFF[6]] * qscale).astype(BF16)
    wgate = wi[:, IN_OFF[6]:].astype(BF16)
    uq = w_uq[i].reshape(MLA_Q_RANK, MLA_HEADS, MLA_NOPE + MLA_ROPE)
    zq = lambda n: jnp.zeros((MLA_Q_RANK, MLA_HEADS, n), F32)
    rope_rot = jnp.concatenate([-uq[..., MLA_NOPE + MLA_ROPE // 2:], uq[..., MLA_NOPE:MLA_NOPE + MLA_ROPE // 2]], -1)
    wq = jnp.concatenate([uq, zq(pad)], axis=-1).reshape(MLA_Q_RANK, -1).astype(BF16)
    wqr = jnp.concatenate([zq(MLA_NOPE), rope_rot, zq(pad)], axis=-1).reshape(MLA_Q_RANK, -1).astype(BF16)
    ukv = w_ukv[i].reshape(MLA_KV_RANK, MLA_HEADS, MLA_NOPE + MLA_V)
    zkv = jnp.zeros((MLA_KV_RANK, MLA_HEADS, MLA_SLAB - MLA_NOPE), F32)
    wk = jnp.concatenate([ukv[..., :MLA_NOPE], zkv], axis=-1).reshape(MLA_KV_RANK, -1).astype(BF16)
    wv = jnp.concatenate([ukv[..., MLA_NOPE:], zkv], axis=-1).reshape(MLA_KV_RANK, -1).astype(BF16)
    return wlat, wdil, wgate, wq, wqr, wk, wv


def kernel(x, p, positions, attn_norm, w_in, q_norm, w_uq, kv_norm, w_ukv, w_br_mla, w_br_dil, w_out, ffn_norm, dense_w_gate, dense_w_up, dense_w_down, router_w, router_b, moe_w_gate, moe_w_up, moe_w_down, ple_norm, ple_w_gate, ple_w_proj, final_norm):
    batch, seq, d = x.shape
    t = batch * seq
    depth = w_in.shape[0]
    xf = x.reshape(t, d)
    cos_t, sin_t = _rope_tables(positions)
    slopes = _alibi_slopes(DIL_HEADS)
    row = lambda v: v.reshape(1, -1)
    h = _norm(xf, attn_norm[0])
    for i in range(depth):
        wlat, wdil, wgate, wq, wqr, wk, wv = _layer_weights(i, w_in, w_uq, w_ukv)
        q, k, v = _mla_proj(h, wlat, wq, wqr, wk, wv, row(q_norm[i]), row(kv_norm[i]), cos_t, sin_t)
        o_mla = _mla_attn(q, k, v, batch, seq)
        qkv_d = _dil_proj(h, wdil, seq)
        o_dil = _dil_attn(qkv_d, slopes, batch, seq)
        moe_layer = i % 2 == 1
        xf, h2 = _merge(xf, h, o_mla, o_dil, wgate, w_br_mla[i].astype(BF16), w_br_dil[i].astype(BF16),
                        w_out[i].astype(BF16), row(ffn_norm[i]), F32 if moe_layer else BF16)
        j = i // 2
        moe = None
        if moe_layer:
            moe = _moe(h2, router_w[j], router_b[j], moe_w_gate[j].astype(BF16), moe_w_up[j].astype(BF16),
                       moe_w_down[j].astype(BF16))
        else:
            xf = _ffn(xf, h2, dense_w_gate[j].astype(BF16), dense_w_up[j].astype(BF16),
                      dense_w_down[j].astype(BF16))
        last = i == depth - 1
        g_next = final_norm if last else attn_norm[i + 1]
        xf, h = _ple(xf, p[i].reshape(t, -1), ple_w_gate[i].astype(BF16), ple_w_proj[i].astype(BF16),
                     row(ple_norm[i]), row(g_next), F32 if last else BF16, moe)
    return h.reshape(batch, seq, d)
```

```python
import functools

import numpy as np
import jax
import jax.numpy as jnp
from jax import lax
from jax.experimental import pallas as pl
from jax.experimental.pallas import tpu as pltpu

F32 = jnp.float32
BF16 = jnp.bfloat16

D_MODEL = 1024
P_DIM = 256
NORM_EPS = 1e-6
DEPTH = 2

MLA_HEADS = 8
MLA_Q_RANK = 384
MLA_KV_RANK = 256
MLA_NOPE = 64
MLA_ROPE = 32
MLA_V = 64
ROPE_THETA = 10000.0
MLA_SLAB = 128

DIL_CONFIGS = ((128, 1), (512, 4), (2048, 16))
DIL_GROUPS = 3
DIL_HPG = 8
DIL_HEADS = 24
DIL_HD = 64
DIL_WIDTH = DIL_HEADS * DIL_HD
DIL_SPAN = 128

D_FF = 3584
N_EXPERTS = 8
TOP_K = 2

LANES = 128
ROW_TILES = D_MODEL // LANES
NEG = -1e30
LOG2E = 1.4426950408889634
VMEM_LIMIT = 56 * 1024 * 1024

IN_OFF = tuple(int(o) for o in np.cumsum((0, MLA_Q_RANK, MLA_KV_RANK, MLA_ROPE, DIL_WIDTH, DIL_WIDTH,
                                          DIL_WIDTH, D_MODEL, D_MODEL)))


def _cparams(sem):
    return pltpu.CompilerParams(dimension_semantics=sem, vmem_limit_bytes=VMEM_LIMIT)


def _rms(x, g):
    return x * lax.rsqrt(jnp.mean(x * x, axis=-1, keepdims=True) + NORM_EPS) * g


def _sigmoid(x):
    return 1.0 / (1.0 + jnp.exp(-x))


def _alibi_slopes(n):
    def pow2(m):
        start = 2.0 ** (-8.0 / m)
        return [start ** (i + 1) for i in range(m)]
    if float(np.log2(n)).is_integer():
        s = pow2(n)
    else:
        c = 2 ** int(np.floor(np.log2(n)))
        s = pow2(c) + pow2(2 * c)[0::2][: n - c]
    return jnp.asarray(sorted(s, reverse=True), dtype=F32)


def _norm_kernel(x_ref, g_ref, o_ref):
    o_ref[...] = _rms(x_ref[...], g_ref[...]).astype(o_ref.dtype)


def _norm(x, g, tm=1024):
    t, d = x.shape
    return pl.pallas_call(
        _norm_kernel,
        grid=(t // tm,),
        in_specs=[pl.BlockSpec((tm, d), lambda i: (i, 0)), pl.BlockSpec((1, d), lambda i: (0, 0))],
        out_specs=pl.BlockSpec((tm, d), lambda i: (i, 0)),
        out_shape=jax.ShapeDtypeStruct((t, d), BF16),
        compiler_params=_cparams(("parallel",)),
        name="rms_norm",
    )(x, g.reshape(1, d))


def _rope_table_kernel(pos_ref, invf_ref, cos_ref, sin_ref):
    ang = pos_ref[...].astype(F32) * invf_ref[...]
    lane = lax.broadcasted_iota(jnp.int32, ang.shape, 1)
    rope_lane = (lane >= MLA_NOPE) & (lane < MLA_NOPE + MLA_ROPE)
    cos_ref[...] = jnp.where(lane < MLA_NOPE, 1.0, jnp.where(rope_lane, jnp.cos(ang), 0.0))
    sin_ref[...] = jnp.where(rope_lane, jnp.sin(ang), 0.0)


def _rope_tables(positions, tm=2048):
    t = positions.size
    half = MLA_ROPE // 2
    inv_freq = ROPE_THETA ** (-jnp.arange(half, dtype=F32) / half)
    invf = jnp.zeros((1, LANES), F32).at[0, MLA_NOPE:MLA_NOPE + MLA_ROPE].set(jnp.concatenate([inv_freq, inv_freq]))
    pos_b = jnp.broadcast_to(positions.reshape(t, 1), (t, LANES))
    spec = pl.BlockSpec((tm, LANES), lambda i: (i, 0))
    return pl.pallas_call(
        _rope_table_kernel,
        grid=(t // tm,),
        in_specs=[spec, pl.BlockSpec((1, LANES), lambda i: (0, 0))],
        out_specs=[spec, spec],
        out_shape=[jax.ShapeDtypeStruct((t, LANES), F32)] * 2,
        compiler_params=_cparams(("parallel",)),
        name="rope_tables",
    )(pos_b, invf)


def _mla_proj_kernel(h_ref, wlat_ref, wq_ref, wqr_ref, wk_ref, wv_ref, qn_ref, kvn_ref, cos_ref, sin_ref,
                     q_out, k_out, v_out):
    lat = jnp.dot(h_ref[...], wlat_ref[...], preferred_element_type=F32)
    cqn = _rms(lat[:, :MLA_Q_RANK], qn_ref[...]).astype(BF16)
    ckvn = _rms(lat[:, MLA_Q_RANK:MLA_Q_RANK + MLA_KV_RANK], kvn_ref[...]).astype(BF16)
    cos = cos_ref[...]
    sin = sin_ref[...]
    o = MLA_Q_RANK + MLA_KV_RANK
    k_rope = lat[:, o:o + LANES] * cos + lat[:, o + LANES:o + 2 * LANES] * sin
    qa = jnp.dot(cqn, wq_ref[...], preferred_element_type=F32)
    qb = jnp.dot(cqn, wqr_ref[...], preferred_element_type=F32)
    kk = jnp.dot(ckvn, wk_ref[...], preferred_element_type=F32)
    scale = (MLA_NOPE + MLA_ROPE) ** -0.5 * LOG2E
    vv = jnp.dot(ckvn, wv_ref[...], preferred_element_type=F32)
    ones_lane = lax.broadcasted_iota(jnp.int32, cos.shape, 1) >= MLA_V
    for hd in range(MLA_HEADS):
        sl = slice(hd * MLA_SLAB, (hd + 1) * MLA_SLAB)
        q_out[:, sl] = ((qa[:, sl] * cos + qb[:, sl] * sin) * scale).astype(BF16)
        k_out[:, sl] = (kk[:, sl] + k_rope).astype(BF16)
        v_out[:, sl] = jnp.where(ones_lane, 1.0, vv[:, sl]).astype(BF16)


def _mla_proj(h, wlat, wq, wqr, wk, wv, qn, kvn, cos_t, sin_t, tm=512):
    t = h.shape[0]
    row = lambda w: pl.BlockSpec((tm, w), lambda i: (i, 0))
    full = lambda a: pl.BlockSpec(a.shape, lambda i: (0, 0))
    hs = MLA_HEADS * MLA_SLAB
    return pl.pallas_call(
        _mla_proj_kernel,
        grid=(t // tm,),
        in_specs=[row(D_MODEL), full(wlat), full(wq), full(wqr), full(wk), full(wv), full(qn), full(kvn),
                  row(LANES), row(LANES)],
        out_specs=[row(hs), row(hs), row(hs)],
        out_shape=[jax.ShapeDtypeStruct((t, hs), BF16)] * 3,
        compiler_params=_cparams(("parallel",)),
        name="mla_proj",
    )(h, wlat, wq, wqr, wk, wv, qn, kvn, cos_t, sin_t)


def _mla_attn_kernel(q_ref, k_ref, v_ref, o_ref, s_scr, *, tq):
    qi = pl.program_id(2)
    rq = lax.broadcasted_iota(jnp.int32, (tq, tq), 0)
    ck = lax.broadcasted_iota(jnp.int32, (tq, tq), 1)
    causal = ck <= rq
    nl = tq // LANES

    def tile(n):
        outs = []
        for hh in range(2):
            q = q_ref[:, hh * MLA_SLAB:(hh + 1) * MLA_SLAB]
            mx = None
            for c in range(n + 1):
                ks = k_ref[c * tq:(c + 1) * tq, hh * MLA_SLAB:(hh + 1) * MLA_SLAB]
                s = lax.dot_general(q, ks, (((1,), (1,)), ((), ())), preferred_element_type=F32)
                if c == n:
                    s = jnp.where(causal, s, NEG)
                s_scr[hh, c] = s
                parts = [s[:, i * LANES:(i + 1) * LANES] for i in range(nl)]
                mx = functools.reduce(jnp.maximum, parts if mx is None else [mx] + parts)
            m = jnp.max(mx, axis=-1, keepdims=True)
            acc = None
            for c in range(n + 1):
                vs = v_ref[c * tq:(c + 1) * tq, hh * MLA_SLAB:(hh + 1) * MLA_SLAB]
                p = jnp.exp2((s_scr[hh, c] - m).astype(BF16))
                pv = jnp.dot(p, vs, preferred_element_type=F32)
                acc = pv if acc is None else acc + pv
            outs.append(acc / acc[:, MLA_V:MLA_V + 1])
        o_ref[...] = jnp.concatenate([outs[0][:, :MLA_V], outs[1][:, :MLA_V]], axis=-1).astype(o_ref.dtype)

    for n in range(s_scr.shape[1]):
        pl.when(qi == n)(functools.partial(tile, n))


def _mla_attn(q, k, v, batch, seq, tq=512):
    t = q.shape[0]
    nq = seq // tq
    pairs = MLA_HEADS // 2
    return pl.pallas_call(
        functools.partial(_mla_attn_kernel, tq=tq),
        grid=(batch, pairs, nq),
        in_specs=[pl.BlockSpec((tq, 2 * MLA_SLAB), lambda b, p, i: (b * nq + i, p)),
                  pl.BlockSpec((seq, 2 * MLA_SLAB), lambda b, p, i: (b, p)),
                  pl.BlockSpec((seq, 2 * MLA_SLAB), lambda b, p, i: (b, p))],
        out_specs=pl.BlockSpec((tq, 2 * MLA_V), lambda b, p, i: (b * nq + i, p)),
        out_shape=jax.ShapeDtypeStruct((t, MLA_HEADS * MLA_V), BF16),
        scratch_shapes=[pltpu.VMEM((2, nq, tq, tq), F32)],
        compiler_params=_cparams(("parallel", "parallel", "arbitrary")),
        name="mla_attn",
    )(q, k, v)


PHASE_STRIDE = 4


def _phase_of_slot(r):
    if r <= PHASE_STRIDE:
        return list(range(r))
    f2 = r // PHASE_STRIDE
    return [c1 + PHASE_STRIDE * c2 for c1 in range(PHASE_STRIDE) for c2 in range(f2)]


def _dil_proj_kernel(h_ref, w_ref, o_ref, acc_ref, tmp_ref, *, seq):
    res = jnp.dot(h_ref[...], w_ref[...], preferred_element_type=F32)
    g = pl.program_id(0) % DIL_GROUPS
    nl = o_ref.shape[0]

    @pl.when(g == 0)
    def _():
        for j in range(nl):
            o_ref[j] = res[:, j * LANES:(j + 1) * LANES].astype(o_ref.dtype)

    @pl.when(g > 0)
    def _():
        for j in range(nl):
            acc_ref[j] = res[:, j * LANES:(j + 1) * LANES]

    for gi, (_, r) in enumerate(DIL_CONFIGS):
        if r == 1:
            continue

        @pl.when(g == gi)
        def _(r=r):
            f1 = min(r, PHASE_STRIDE)
            l1 = seq // f1
            dst = o_ref if r == f1 else tmp_ref
            for c1 in range(f1):
                for j in range(nl):
                    dst[j, c1 * l1:(c1 + 1) * l1, :] = acc_ref[j, pl.ds(c1, l1, stride=f1), :].astype(dst.dtype)
            if r > f1:
                f2 = r // f1
                l2 = l1 // f2
                for s in range(r):
                    c1, c2 = divmod(s, f2)
                    for j in range(nl):
                        o_ref[j, s * l2:(s + 1) * l2, :] = (
                            tmp_ref[j, pl.ds(c1 * l1 + c2, l2, stride=f2), :].astype(o_ref.dtype))


def _dil_proj(h, w, seq, tn=512):
    t = h.shape[0]
    n = w.shape[1]
    nl = tn // LANES
    return pl.pallas_call(
        functools.partial(_dil_proj_kernel, seq=seq),
        grid=(n // tn, t // seq),
        in_specs=[pl.BlockSpec((seq, D_MODEL), lambda j, b: (b, 0)),
                  pl.BlockSpec((D_MODEL, tn), lambda j, b: (0, j))],
        out_specs=pl.BlockSpec((nl, seq, LANES), lambda j, b: (j, b, 0)),
        out_shape=jax.ShapeDtypeStruct((n // LANES, t, LANES), BF16),
        scratch_shapes=[pltpu.VMEM((nl, seq, LANES), F32)] * 2,
        compiler_params=_cparams(("parallel", "parallel")),
        name="dil_proj",
    )(h, w)


def _dil_attn_kernel(slopes_ref, q0, q1, q2, k0, k1, k2, v0, v1, v2, o_ref, acc_s, m_s, l_s, *, seq):
    pair = pl.program_id(1)
    sp = DIL_SPAN
    head0 = lax.broadcasted_iota(jnp.int32, (sp, LANES), 1) < DIL_HD
    qi = lax.broadcasted_iota(jnp.int32, (sp, 2 * sp), 0)
    kj = lax.broadcasted_iota(jnp.int32, (sp, 2 * sp), 1)
    dist_w = qi + sp - kj
    valid_w = (dist_w >= 0) & (dist_w <= sp)
    dist_1 = dist_w[:, sp:]
    valid_1 = dist_1 >= 0
    qs, ks, vs = (q0, q1, q2), (k0, k1, k2), (v0, v1, v2)
    for g, (_, r) in enumerate(DIL_CONFIGS):
        ln = seq // r
        nb = ln // sp
        sl = [slopes_ref[g * DIL_HPG + 2 * pair + hh] * float(r) for hh in range(2)]
        bias_w = jnp.concatenate([jnp.where(valid_w, -s * dist_w.astype(F32), NEG) for s in sl], axis=0)
        bias_1 = jnp.concatenate([jnp.where(valid_1, -s * dist_1.astype(F32), NEG) for s in sl], axis=0)
        for slot_i, c in enumerate(_phase_of_slot(r)):
            for i in range(nb):
                row0 = slot_i * ln + i * sp
                qb = qs[g][0, row0:row0 + sp, :]
                zero = jnp.zeros_like(qb)
                q2h = jnp.concatenate([jnp.where(head0, qb, zero), jnp.where(head0, zero, qb)], axis=0)
                lo = row0 if i == 0 else row0 - sp
                kw = ks[g][0, lo:row0 + sp, :]
                vw = vs[g][0, lo:row0 + sp, :]
                s = lax.dot_general(q2h, kw, (((1,), (1,)), ((), ())), preferred_element_type=F32)
                s = s + (bias_1 if i == 0 else bias_w)
                m = jnp.max(s, axis=-1, keepdims=True)
                p = jnp.exp(s - m)
                l = jnp.sum(p, axis=-1, keepdims=True)
                o2 = jnp.dot(p.astype(BF16), vw, preferred_element_type=F32)
                dst = pl.ds(i * sp * r + c, sp, stride=r) if r > 1 else pl.ds(i * sp, sp)
                acc_s[g, dst, :] = jnp.where(head0, o2[:sp], o2[sp:])
                m_s[g, dst, :] = jnp.where(head0, m[:sp], m[sp:])
                l_s[g, dst, :] = jnp.where(head0, l[:sp], l[sp:])

    ch = 256

    def combine(i, _):
        rows = pl.ds(pl.multiple_of(i * ch, ch), ch)
        ms = [m_s[g, rows, :] for g in range(DIL_GROUPS)]
        mx = jnp.maximum(jnp.maximum(ms[0], ms[1]), ms[2])
        ws = [jnp.exp(m - mx) for m in ms]
        num = sum(ws[g] * acc_s[g, rows, :] for g in range(DIL_GROUPS))
        den = sum(ws[g] * l_s[g, rows, :] for g in range(DIL_GROUPS))
        o_ref[rows, :] = (num / den).astype(o_ref.dtype)
        return 0

    lax.fori_loop(0, seq // ch, combine, 0)


def _dil_attn(qkv, slopes, batch, seq):
    t = qkv.shape[1]
    pairs = DIL_HPG // 2
    cpg = DIL_HPG * DIL_HD // LANES
    cpt = DIL_WIDTH // LANES

    def spec(tt, g):
        return pl.BlockSpec((1, seq, LANES), lambda b, p: (tt * cpt + g * cpg + p, b, 0))

    in_specs = [pl.BlockSpec(memory_space=pltpu.SMEM)] + [spec(tt, g) for tt in range(3) for g in range(DIL_GROUPS)]
    return pl.pallas_call(
        functools.partial(_dil_attn_kernel, seq=seq),
        grid=(batch, pairs),
        in_specs=in_specs,
        out_specs=pl.BlockSpec((seq, LANES), lambda b, p: (b, p)),
        out_shape=jax.ShapeDtypeStruct((t, DIL_HPG * DIL_HD), BF16),
        scratch_shapes=[pltpu.VMEM((DIL_GROUPS, seq, LANES), F32)] * 3,
        compiler_params=_cparams(("parallel", "parallel")),
        name="dil_attn",
    )(slopes, *([qkv] * 9))


def _to_token_tiles(dst_ref, val):
    n = val.shape[0]
    for s in range(ROW_TILES):
        dst_ref[pl.ds(s, n, stride=ROW_TILES), :] = val[:, s * LANES:(s + 1) * LANES]


def _from_token_tiles(src_ref, n):
    return jnp.concatenate([src_ref[pl.ds(s, n, stride=ROW_TILES), :] for s in range(ROW_TILES)], axis=1)


def _merge_kernel(x_ref, h_ref, om_ref, od_ref, wg_ref, wbm_ref, wbd_ref, wo_ref, g_ref, x_out, h_out, *, tiles):
    gates = jnp.dot(h_ref[...], wg_ref[...], preferred_element_type=F32)
    bm = jnp.dot(om_ref[...], wbm_ref[...], preferred_element_type=F32)
    bd = jnp.dot(od_ref[...], wbd_ref[...], preferred_element_type=F32)
    merged = _sigmoid(gates[:, :D_MODEL]) * bm + _sigmoid(gates[:, D_MODEL:]) * bd
    x1 = x_ref[...] + jnp.dot(merged.astype(BF16), wo_ref[...], preferred_element_type=F32)
    x_out[...] = x1
    hn = _rms(x1, g_ref[...])
    if tiles:
        _to_token_tiles(h_out, hn)
    else:
        h_out[...] = hn.astype(h_out.dtype)


def _merge(x, h, o_mla, o_dil, wg, wbm, wbd, wo, g_next, tiles, tm=512):
    t = x.shape[0]
    row = lambda w: pl.BlockSpec((tm, w), lambda i: (i, 0))
    full = lambda a: pl.BlockSpec(a.shape, lambda i: (0, 0))
    if tiles:
        h_spec = pl.BlockSpec((tm * ROW_TILES, LANES), lambda i: (i, 0))
        h_shape = jax.ShapeDtypeStruct((t * ROW_TILES, LANES), F32)
    else:
        h_spec, h_shape = row(D_MODEL), jax.ShapeDtypeStruct((t, D_MODEL), BF16)
    return pl.pallas_call(
        functools.partial(_merge_kernel, tiles=tiles),
        grid=(t // tm,),
        in_specs=[row(D_MODEL), row(D_MODEL), row(o_mla.shape[1]), row(o_dil.shape[1]),
                  full(wg), full(wbm), full(wbd), full(wo), full(g_next)],
        out_specs=[row(D_MODEL), h_spec],
        out_shape=[jax.ShapeDtypeStruct((t, D_MODEL), F32), h_shape],
        compiler_params=_cparams(("parallel",)),
        name="merge",
    )(x, h, o_mla, o_dil, wg, wbm, wbd, wo, g_next)


def _ffn_kernel(x_ref, h_ref, wg_ref, wu_ref, wd_ref, o_ref):
    @pl.when(pl.program_id(1) == 0)
    def _():
        o_ref[...] = x_ref[...]

    h = h_ref[...]
    a = jnp.dot(h, wg_ref[...], preferred_element_type=F32)
    u = jnp.dot(h, wu_ref[...], preferred_element_type=F32)
    o_ref[...] += jnp.dot((a * _sigmoid(a) * u).astype(BF16), wd_ref[...], preferred_element_type=F32)


def _ffn(x, h, wg, wu, wd, tm=1024, tf=512):
    t = x.shape[0]
    nf = wg.shape[1] // tf
    return pl.pallas_call(
        _ffn_kernel,
        grid=(t // tm, nf),
        in_specs=[pl.BlockSpec((tm, D_MODEL), lambda i, f: (i, 0)),
                  pl.BlockSpec((tm, D_MODEL), lambda i, f: (i, 0)),
                  pl.BlockSpec((D_MODEL, tf), lambda i, f: (0, f)),
                  pl.BlockSpec((D_MODEL, tf), lambda i, f: (0, f)),
                  pl.BlockSpec((tf, D_MODEL), lambda i, f: (f, 0))],
        out_specs=pl.BlockSpec((tm, D_MODEL), lambda i, f: (i, 0)),
        out_shape=jax.ShapeDtypeStruct((t, D_MODEL), F32),
        compiler_params=_cparams(("parallel", "arbitrary")),
        name="dense_ffn",
    )(x, h, wg, wu, wd)


def _router_kernel(h_ref, w_ref, b_ref, idx_ref, wt_ref):
    h = _from_token_tiles(h_ref, idx_ref.shape[0])
    logits = jnp.dot(h, w_ref[...], preferred_element_type=F32,
                     precision=lax.Precision.HIGHEST) + b_ref[...]
    lane = lax.broadcasted_iota(jnp.int32, logits.shape, 1)
    m1 = jnp.max(logits, axis=-1, keepdims=True)
    i1 = jnp.min(jnp.where(logits == m1, lane, LANES), axis=-1, keepdims=True)
    rest = jnp.where(lane == i1, NEG, logits)
    m2 = jnp.max(rest, axis=-1, keepdims=True)
    i2 = jnp.min(jnp.where(rest == m2, lane, LANES), axis=-1, keepdims=True)
    e = jnp.exp(m2 - m1)
    w1 = 1.0 / (1.0 + e)
    idx_ref[...] = jnp.where(lane == 0, i1, jnp.where(lane == 1, i2, 0))
    wt_ref[...] = jnp.where(lane == 0, w1, jnp.where(lane == 1, e * w1, 0.0))


def _router(h, w_pad, b_pad, tm=1024):
    t = h.shape[0] // ROW_TILES
    row = pl.BlockSpec((tm, LANES), lambda i: (i, 0))
    return pl.pallas_call(
        _router_kernel,
        grid=(t // tm,),
        in_specs=[pl.BlockSpec((tm * ROW_TILES, LANES), lambda i: (i, 0)),
                  pl.BlockSpec(w_pad.shape, lambda i: (0, 0)), pl.BlockSpec(b_pad.shape, lambda i: (0, 0))],
        out_specs=[row, row],
        out_shape=[jax.ShapeDtypeStruct((t, LANES), jnp.int32), jax.ShapeDtypeStruct((t, LANES), F32)],
        compiler_params=_cparams(("parallel",)),
        name="moe_router",
    )(h, w_pad, b_pad)


def _tile_at(ref, row8):
    return ref.at[pl.ds(pl.multiple_of(row8, ROW_TILES), ROW_TILES), :]


def _row_in(h_hbm, xbuf, sem_in, slot, tok8, r):
    return pltpu.make_async_copy(_tile_at(h_hbm, tok8), _tile_at(xbuf.at[slot], r * ROW_TILES), sem_in.at[slot])


def _row_out(ybuf, y_hbm, sem_out, slot, r, dst8):
    return pltpu.make_async_copy(_tile_at(ybuf.at[slot], r * ROW_TILES), _tile_at(y_hbm, dst8), sem_out.at[slot])


def _pad_fill(ybuf, y_hbm, sem_fill, e, tm):
    n = tm * ROW_TILES
    rows = y_hbm.shape[0] - (N_EXPERTS + 1 - e) * n
    return pltpu.make_async_copy(ybuf.at[1], y_hbm.at[pl.ds(rows, n), :], sem_fill)


def _moe_ffn_kernel(te_ref, nv_ref, src_ref, dst_ref, h_hbm, wg_ref, wu_ref, wd_ref, y_hbm,
                    xbuf, ybuf, acc_ref, xb_ref, sem_in, sem_out, sem_fill, *, tm, rps):
    i = pl.program_id(0)
    f = pl.program_id(1)
    nf = pl.num_programs(1)
    nv = nv_ref[0]
    slot = i % 2
    other = 1 - slot
    sink8 = y_hbm.shape[0] - tm * ROW_TILES

    def wait_rows(buf, sem, s):
        pltpu.make_async_copy(buf.at[s], buf.at[s], sem.at[s]).wait()

    def gather_chunk(tile, buf_slot, chunk):
        for j in range(rps):
            r = chunk * rps + j
            _row_in(h_hbm, xbuf, sem_in, buf_slot, src_ref[tile * tm + r], r).start()

    def scatter_chunk(dst_of_row, chunk):
        for j in range(rps):
            r = chunk * rps + j
            _row_out(ybuf, y_hbm, sem_out, other, r, dst_of_row(r)).start()

    @pl.when((i == 0) & (f == 0))
    def _():
        def body(c, _):
            gather_chunk(0, 0, c)
            return 0
        lax.fori_loop(0, nf, body, 0)
        ybuf[1] = jnp.zeros(ybuf.shape[1:], F32)
        for e in range(N_EXPERTS):
            _pad_fill(ybuf, y_hbm, sem_fill, e, tm).start()

    @pl.when((i == 0) & (f == nf - 1))
    def _():
        for e in range(N_EXPERTS):
            _pad_fill(ybuf, y_hbm, sem_fill, e, tm).wait()

    @pl.when((i <= nv) & (f == 0))
    def _():
        wait_rows(xbuf, sem_in, slot)

    @pl.when((i < nv) & (f == 0))
    def _():
        xb_ref[...] = _from_token_tiles(xbuf.at[slot], tm).astype(BF16)
        acc_ref[...] = jnp.zeros_like(acc_ref)

    @pl.when(i < nv)
    def _():
        gather_chunk(i + 1, other, f)
        prev = jnp.maximum(i - 1, 0) * tm
        scatter_chunk(lambda r: jnp.where(i == 0, sink8 + r * ROW_TILES, dst_ref[prev + r]), f)
        h = xb_ref[...]
        a = jnp.dot(h, wg_ref[0], preferred_element_type=F32)
        u = jnp.dot(h, wu_ref[0], preferred_element_type=F32)
        acc_ref[...] += jnp.dot((a * _sigmoid(a) * u).astype(BF16), wd_ref[0], preferred_element_type=F32)

    @pl.when((i < nv) & (f == nf - 1))
    def _():
        wait_rows(ybuf, sem_out, other)
        _to_token_tiles(ybuf.at[slot], acc_ref[...])

    @pl.when((i == nv) & (f == 0))
    def _():
        def body(c, _):
            scatter_chunk(lambda r: dst_ref[(nv - 1) * tm + r], c)
            return 0
        lax.fori_loop(0, nf, body, 0)
        wait_rows(ybuf, sem_out, other)


def _moe_ffn(h, tile_expert, n_valid, src_tok, dst_row, wg, wu, wd, n_out_rows, tm, tf=1792):
    n_tiles = tile_expert.shape[0] - 1
    nf = wg.shape[2] // tf
    rps = tm // nf

    def wmap(i, f, te, nv, src, dst):
        return (te[i], jnp.where(i < nv[0], f, nf - 1))

    return pl.pallas_call(
        functools.partial(_moe_ffn_kernel, tm=tm, rps=rps),
        grid_spec=pltpu.PrefetchScalarGridSpec(
            num_scalar_prefetch=4,
            grid=(n_tiles + 1, nf),
            in_specs=[pl.BlockSpec(memory_space=pl.ANY),
                      pl.BlockSpec((1, D_MODEL, tf), lambda *a: (wmap(*a)[0], 0, wmap(*a)[1])),
                      pl.BlockSpec((1, D_MODEL, tf), lambda *a: (wmap(*a)[0], 0, wmap(*a)[1])),
                      pl.BlockSpec((1, tf, D_MODEL), lambda *a: (wmap(*a)[0], wmap(*a)[1], 0))],
            out_specs=pl.BlockSpec(memory_space=pl.ANY),
            scratch_shapes=[pltpu.VMEM((2, tm * ROW_TILES, LANES), F32), pltpu.VMEM((2, tm * ROW_TILES, LANES), F32),
                            pltpu.VMEM((tm, D_MODEL), F32), pltpu.VMEM((tm, D_MODEL), BF16),
                            pltpu.SemaphoreType.DMA((2,)), pltpu.SemaphoreType.DMA((2,)),
                            pltpu.SemaphoreType.DMA(())]),
        out_shape=jax.ShapeDtypeStruct((n_out_rows * ROW_TILES, LANES), F32),
        compiler_params=_cparams(("arbitrary", "arbitrary")),
        name="moe_ffn",
    )(tile_expert, n_valid, src_tok, dst_row, h, wg, wu, wd)


def _moe(h, w_router, b_router, wg, wu, wd, tm=448):
    t = h.shape[0] // ROW_TILES
    w_pad = jnp.zeros((D_MODEL, LANES), F32).at[:, :N_EXPERTS].set(w_router)
    b_pad = jnp.full((1, LANES), NEG, F32).at[0, :N_EXPERTS].set(b_router)
    idx_l, wt_l = _router(h, w_pad, b_pad)
    expert = idx_l[:, :TOP_K].reshape(-1)
    onehot = (expert[:, None] == jnp.arange(N_EXPERTS)[None, :]).astype(jnp.int32)
    csum = jnp.cumsum(onehot, axis=0)
    counts = csum[-1]
    rank = jnp.sum((csum - onehot) * onehot, axis=1)
    tiles = (counts + tm - 1) // tm
    tile_end = jnp.cumsum(tiles)
    pad_off = (tile_end - tiles) * tm
    pos = pad_off[expert] + rank
    n_tiles = (TOP_K * t) // tm + N_EXPERTS
    n_rows = (n_tiles + 1) * tm
    inv = jnp.full((n_rows,), -1, jnp.int32).at[pos].set(jnp.arange(TOP_K * t, dtype=jnp.int32))
    real = inv >= 0
    src_tok = jnp.where(real, inv // TOP_K, 0)
    n_valid = tile_end[-1:].astype(jnp.int32)
    tile_ids = jnp.minimum(jnp.arange(n_tiles + 1), n_valid[0] - 1)
    tile_expert = jnp.sum(tile_end[None, :] <= tile_ids[:, None], axis=1).astype(jnp.int32)
    pad_dst = TOP_K * t + jnp.repeat(tile_expert, tm) * tm + jnp.arange(n_rows, dtype=jnp.int32) % tm
    dst_row = jnp.where(real, (inv % TOP_K) * t + inv // TOP_K, pad_dst)
    y = _moe_ffn(h, tile_expert, n_valid, src_tok * ROW_TILES, dst_row * ROW_TILES, wg, wu, wd,
                 TOP_K * t + (N_EXPERTS + 1) * tm, tm)
    return y, wt_l


def _ple_kernel(x_ref, p_ref, wg_ref, wp_ref, g_ref, gn_ref, *rest, moe):
    x = x_ref[...]
    if moe:
        y0_ref, y1_ref, wt_ref = rest[:3]
        wt = wt_ref[...]
        n = x.shape[0]
        x = x + wt[:, 0:1] * _from_token_tiles(y0_ref, n) + wt[:, 1:2] * _from_token_tiles(y1_ref, n)
    x_out, h_out = rest[-2:]
    gate = _sigmoid(jnp.dot(_rms(x, g_ref[...]).astype(BF16), wg_ref[...], preferred_element_type=F32))
    x2 = x + gate * jnp.dot(p_ref[...].astype(BF16), wp_ref[...], preferred_element_type=F32)
    x_out[...] = x2
    h_out[...] = _rms(x2, gn_ref[...]).astype(h_out.dtype)


def _ple(x, p, wg, wp, g, g_next, h_dtype, moe=None, tm=512):
    t = x.shape[0]
    row = lambda w: pl.BlockSpec((tm, w), lambda i: (i, 0))
    full = lambda a: pl.BlockSpec(a.shape, lambda i: (0, 0))
    in_specs = [row(D_MODEL), row(P_DIM), full(wg), full(wp), full(g), full(g_next)]
    args = [x, p, wg, wp, g, g_next]
    if moe is not None:
        y, wt = moe
        in_specs += [pl.BlockSpec((tm * ROW_TILES, LANES), lambda i: (i, 0)),
                     pl.BlockSpec((tm * ROW_TILES, LANES), lambda i: (t // tm + i, 0)), row(LANES)]
        args += [y, y, wt]
    return pl.pallas_call(
        functools.partial(_ple_kernel, moe=moe is not None),
        grid=(t // tm,),
        in_specs=in_specs,
        out_specs=[row(D_MODEL), row(D_MODEL)],
        out_shape=[jax.ShapeDtypeStruct((t, D_MODEL), F32), jax.ShapeDtypeStruct((t, D_MODEL), h_dtype)],
        compiler_params=_cparams(("parallel",)),
        name="ple",
    )(*args)


def _rot_cols(w):
    half = w.shape[-1] // 2
    return jnp.concatenate([-w[:, half:], w[:, :half]], axis=-1)


def _layer_weights(i, w_in, w_uq, w_ukv):
    wi = w_in[i]
    z = lambda n: jnp.zeros((wi.shape[0], n), F32)
    w_kr = wi[:, IN_OFF[2]:IN_OFF[3]]
    pad = MLA_SLAB - MLA_NOPE - MLA_ROPE
    wlat = jnp.concatenate([wi[:, :IN_OFF[2]], z(MLA_NOPE), w_kr, z(pad), z(MLA_NOPE), _rot_cols(w_kr), z(pad)],
                           axis=1).astype(BF16)
    qscale = jnp.concatenate([jnp.full((DIL_WIDTH,), DIL_HD ** -0.5, F32), jnp.ones((2 * DIL_WIDTH,), F32)])
    wdil = (wi[:, IN_OFF[3]:IN_OFF[6]] * qscale).astype(BF16)
    wgate = wi[:, IN_OFF[6]:].astype(BF16)
    uq = w_uq[i].reshape(MLA_Q_RANK, MLA_HEADS, MLA_NOPE + MLA_ROPE)
    zq = lambda n: jnp.zeros((MLA_Q_RANK, MLA_HEADS, n), F32)
    rope_rot = jnp.concatenate([-uq[..., MLA_NOPE + MLA_ROPE // 2:], uq[..., MLA_NOPE:MLA_NOPE + MLA_ROPE // 2]], -1)
    wq = jnp.concatenate([uq, zq(pad)], axis=-1).reshape(MLA_Q_RANK, -1).astype(BF16)
    wqr = jnp.concatenate([zq(MLA_NOPE), rope_rot, zq(pad)], axis=-1).reshape(MLA_Q_RANK, -1).astype(BF16)
    ukv = w_ukv[i].reshape(MLA_KV_RANK, MLA_HEADS, MLA_NOPE + MLA_V)
    zkv = jnp.zeros((MLA_KV_RANK, MLA_HEADS, MLA_SLAB - MLA_NOPE), F32)
    wk = jnp.concatenate([ukv[..., :MLA_NOPE], zkv], axis=-1).reshape(MLA_KV_RANK, -1).astype(BF16)
    wv = jnp.concatenate([ukv[..., MLA_NOPE:], zkv], axis=-1).reshape(MLA_KV_RANK, -1).astype(BF16)
    return wlat, wdil, wgate, wq, wqr, wk, wv


def kernel(x, p, positions, attn_norm, w_in, q_norm, w_uq, kv_norm, w_ukv, w_br_mla, w_br_dil, w_out, ffn_norm, dense_w_gate, dense_w_up, dense_w_down, router_w, router_b, moe_w_gate, moe_w_up, moe_w_down, ple_norm, ple_w_gate, ple_w_proj, final_norm):
    batch, seq, d = x.shape
    t = batch * seq
    depth = w_in.shape[0]
    xf = x.reshape(t, d)
    cos_t, sin_t = _rope_tables(positions)
    slopes = _alibi_slopes(DIL_HEADS)
    row = lambda v: v.reshape(1, -1)
    h = _norm(xf, attn_norm[0])
    for i in range(depth):
        wlat, wdil, wgate, wq, wqr, wk, wv = _layer_weights(i, w_in, w_uq, w_ukv)
        q, k, v = _mla_proj(h, wlat, wq, wqr, wk, wv, row(q_norm[i]), row(kv_norm[i]), cos_t, sin_t)
        o_mla = _mla_attn(q, k, v, batch, seq)
        qkv_d = _dil_proj(h, wdil, seq)
        o_dil = _dil_attn(qkv_d, slopes, batch, seq)
        moe_layer = i % 2 == 1
        xf, h2 = _merge(xf, h, o_mla, o_dil, wgate, w_br_mla[i].astype(BF16), w_br_dil[i].astype(BF16),
                        w_out[i].astype(BF16), row(ffn_norm[i]), tiles=moe_layer)
        j = i // 2
        moe = None
        if moe_layer:
            moe = _moe(h2, router_w[j], router_b[j], moe_w_gate[j].astype(BF16), moe_w_up[j].astype(BF16),
                       moe_w_down[j].astype(BF16))
        else:
            xf = _ffn(xf, h2, dense_w_gate[j].astype(BF16), dense_w_up[j].astype(BF16),
                      dense_w_down[j].astype(BF16))
        last = i == depth - 1
        g_next = final_norm if last else attn_norm[i + 1]
        xf, h = _ple(xf, p[i].reshape(t, -1), ple_w_gate[i].astype(BF16), ple_w_proj[i].astype(BF16),
                     row(ple_norm[i]), row(g_next), F32 if last else BF16, moe)
    return h.reshape(batch, seq, d)
```

```python
import functools

import numpy as np
import jax
import jax.numpy as jnp
from jax import lax
from jax.experimental import pallas as pl
from jax.experimental.pallas import tpu as pltpu

F32 = jnp.float32
BF16 = jnp.bfloat16

D_MODEL = 1024
P_DIM = 256
NORM_EPS = 1e-6
DEPTH = 2

MLA_HEADS = 8
MLA_Q_RANK = 384
MLA_KV_RANK = 256
MLA_NOPE = 64
MLA_ROPE = 32
MLA_V = 64
ROPE_THETA = 10000.0
MLA_SLAB = 128

DIL_CONFIGS = ((128, 1), (512, 4), (2048, 16))
DIL_GROUPS = 3
DIL_HPG = 8
DIL_HEADS = 24
DIL_HD = 64
DIL_WIDTH = DIL_HEADS * DIL_HD
DIL_SPAN = 128

D_FF = 3584
N_EXPERTS = 8
TOP_K = 2

LANES = 128
ROW_TILES = D_MODEL // LANES
NEG = -1e30
LOG2E = 1.4426950408889634
VMEM_LIMIT = 56 * 1024 * 1024

IN_OFF = tuple(int(o) for o in np.cumsum((0, MLA_Q_RANK, MLA_KV_RANK, MLA_ROPE, DIL_WIDTH, DIL_WIDTH,
                                          DIL_WIDTH, D_MODEL, D_MODEL)))


def _cparams(sem):
    return pltpu.CompilerParams(dimension_semantics=sem, vmem_limit_bytes=VMEM_LIMIT)


def _rms(x, g):
    return x * lax.rsqrt(jnp.mean(x * x, axis=-1, keepdims=True) + NORM_EPS) * g


def _sigmoid(x):
    return 1.0 / (1.0 + jnp.exp(-x))


def _alibi_slopes(n):
    def pow2(m):
        start = 2.0 ** (-8.0 / m)
        return [start ** (i + 1) for i in range(m)]
    if float(np.log2(n)).is_integer():
        s = pow2(n)
    else:
        c = 2 ** int(np.floor(np.log2(n)))
        s = pow2(c) + pow2(2 * c)[0::2][: n - c]
    return jnp.asarray(sorted(s, reverse=True), dtype=F32)


def _norm_kernel(x_ref, g_ref, o_ref):
    o_ref[...] = _rms(x_ref[...], g_ref[...]).astype(o_ref.dtype)


def _norm(x, g, tm=1024):
    t, d = x.shape
    return pl.pallas_call(
        _norm_kernel,
        grid=(t // tm,),
        in_specs=[pl.BlockSpec((tm, d), lambda i: (i, 0)), pl.BlockSpec((1, d), lambda i: (0, 0))],
        out_specs=pl.BlockSpec((tm, d), lambda i: (i, 0)),
        out_shape=jax.ShapeDtypeStruct((t, d), BF16),
        compiler_params=_cparams(("parallel",)),
        name="rms_norm",
    )(x, g.reshape(1, d))


def _rope_table_kernel(pos_ref, invf_ref, cos_ref, sin_ref):
    ang = pos_ref[...].astype(F32) * invf_ref[...]
    lane = lax.broadcasted_iota(jnp.int32, ang.shape, 1)
    rope_lane = (lane >= MLA_NOPE) & (lane < MLA_NOPE + MLA_ROPE)
    cos_ref[...] = jnp.where(lane < MLA_NOPE, 1.0, jnp.where(rope_lane, jnp.cos(ang), 0.0))
    sin_ref[...] = jnp.where(rope_lane, jnp.sin(ang), 0.0)


def _rope_tables(positions, tm=2048):
    t = positions.size
    half = MLA_ROPE // 2
    inv_freq = ROPE_THETA ** (-jnp.arange(half, dtype=F32) / half)
    invf = jnp.zeros((1, LANES), F32).at[0, MLA_NOPE:MLA_NOPE + MLA_ROPE].set(jnp.concatenate([inv_freq, inv_freq]))
    pos_b = jnp.broadcast_to(positions.reshape(t, 1), (t, LANES))
    spec = pl.BlockSpec((tm, LANES), lambda i: (i, 0))
    return pl.pallas_call(
        _rope_table_kernel,
        grid=(t // tm,),
        in_specs=[spec, pl.BlockSpec((1, LANES), lambda i: (0, 0))],
        out_specs=[spec, spec],
        out_shape=[jax.ShapeDtypeStruct((t, LANES), F32)] * 2,
        compiler_params=_cparams(("parallel",)),
        name="rope_tables",
    )(pos_b, invf)


def _mla_proj_kernel(h_ref, wlat_ref, wq_ref, wqr_ref, wk_ref, wv_ref, qn_ref, kvn_ref, cos_ref, sin_ref,
                     q_out, k_out, v_out):
    lat = jnp.dot(h_ref[...], wlat_ref[...], preferred_element_type=F32)
    cqn = _rms(lat[:, :MLA_Q_RANK], qn_ref[...]).astype(BF16)
    ckvn = _rms(lat[:, MLA_Q_RANK:MLA_Q_RANK + MLA_KV_RANK], kvn_ref[...]).astype(BF16)
    cos = cos_ref[...]
    sin = sin_ref[...]
    o = MLA_Q_RANK + MLA_KV_RANK
    k_rope = lat[:, o:o + LANES] * cos + lat[:, o + LANES:o + 2 * LANES] * sin
    qa = jnp.dot(cqn, wq_ref[...], preferred_element_type=F32)
    qb = jnp.dot(cqn, wqr_ref[...], preferred_element_type=F32)
    kk = jnp.dot(ckvn, wk_ref[...], preferred_element_type=F32)
    scale = (MLA_NOPE + MLA_ROPE) ** -0.5 * LOG2E
    vv = jnp.dot(ckvn, wv_ref[...], preferred_element_type=F32)
    ones_lane = lax.broadcasted_iota(jnp.int32, cos.shape, 1) >= MLA_V
    for hd in range(MLA_HEADS):
        sl = slice(hd * MLA_SLAB, (hd + 1) * MLA_SLAB)
        q_out[:, sl] = ((qa[:, sl] * cos + qb[:, sl] * sin) * scale).astype(BF16)
        k_out[:, sl] = (kk[:, sl] + k_rope).astype(BF16)
        v_out[:, sl] = jnp.where(ones_lane, 1.0, vv[:, sl]).astype(BF16)


def _mla_proj(h, wlat, wq, wqr, wk, wv, qn, kvn, cos_t, sin_t, tm=512):
    t = h.shape[0]
    row = lambda w: pl.BlockSpec((tm, w), lambda i: (i, 0))
    full = lambda a: pl.BlockSpec(a.shape, lambda i: (0, 0))
    hs = MLA_HEADS * MLA_SLAB
    return pl.pallas_call(
        _mla_proj_kernel,
        grid=(t // tm,),
        in_specs=[row(D_MODEL), full(wlat), full(wq), full(wqr), full(wk), full(wv), full(qn), full(kvn),
                  row(LANES), row(LANES)],
        out_specs=[row(hs), row(hs), row(hs)],
        out_shape=[jax.ShapeDtypeStruct((t, hs), BF16)] * 3,
        compiler_params=_cparams(("parallel",)),
        name="mla_proj",
    )(h, wlat, wq, wqr, wk, wv, qn, kvn, cos_t, sin_t)


def _mla_attn_kernel(q_ref, k_ref, v_ref, o_ref, s_scr, *, tq):
    qi = pl.program_id(2)
    rq = lax.broadcasted_iota(jnp.int32, (tq, tq), 0)
    ck = lax.broadcasted_iota(jnp.int32, (tq, tq), 1)
    causal = ck <= rq
    nl = tq // LANES

    def tile(n):
        outs = []
        for hh in range(2):
            q = q_ref[:, hh * MLA_SLAB:(hh + 1) * MLA_SLAB]
            mx = None
            for c in range(n + 1):
                ks = k_ref[c * tq:(c + 1) * tq, hh * MLA_SLAB:(hh + 1) * MLA_SLAB]
                s = lax.dot_general(q, ks, (((1,), (1,)), ((), ())), preferred_element_type=F32)
                if c == n:
                    s = jnp.where(causal, s, NEG)
                s_scr[hh, c] = s
                parts = [s[:, i * LANES:(i + 1) * LANES] for i in range(nl)]
                mx = functools.reduce(jnp.maximum, parts if mx is None else [mx] + parts)
            m = jnp.max(mx, axis=-1, keepdims=True)
            acc = None
            for c in range(n + 1):
                vs = v_ref[c * tq:(c + 1) * tq, hh * MLA_SLAB:(hh + 1) * MLA_SLAB]
                p = jnp.exp2((s_scr[hh, c] - m).astype(BF16))
                pv = jnp.dot(p, vs, preferred_element_type=F32)
                acc = pv if acc is None else acc + pv
            outs.append(acc / acc[:, MLA_V:MLA_V + 1])
        o_ref[...] = jnp.concatenate([outs[0][:, :MLA_V], outs[1][:, :MLA_V]], axis=-1).astype(o_ref.dtype)

    for n in range(s_scr.shape[1]):
        pl.when(qi == n)(functools.partial(tile, n))


def _mla_attn(q, k, v, batch, seq, tq=512):
    t = q.shape[0]
    nq = seq // tq
    pairs = MLA_HEADS // 2
    return pl.pallas_call(
        functools.partial(_mla_attn_kernel, tq=tq),
        grid=(batch, pairs, nq),
        in_specs=[pl.BlockSpec((tq, 2 * MLA_SLAB), lambda b, p, i: (b * nq + i, p)),
                  pl.BlockSpec((seq, 2 * MLA_SLAB), lambda b, p, i: (b, p)),
                  pl.BlockSpec((seq, 2 * MLA_SLAB), lambda b, p, i: (b, p))],
        out_specs=pl.BlockSpec((tq, 2 * MLA_V), lambda b, p, i: (b * nq + i, p)),
        out_shape=jax.ShapeDtypeStruct((t, MLA_HEADS * MLA_V), BF16),
        scratch_shapes=[pltpu.VMEM((2, nq, tq, tq), F32)],
        compiler_params=_cparams(("parallel", "parallel", "arbitrary")),
        name="mla_attn",
    )(q, k, v)


PHASE_STRIDE = 4


def _phase_of_slot(r):
    if r <= PHASE_STRIDE:
        return list(range(r))
    f2 = r // PHASE_STRIDE
    return [c1 + PHASE_STRIDE * c2 for c1 in range(PHASE_STRIDE) for c2 in range(f2)]


def _dil_proj_kernel(h_ref, w_ref, o_ref, acc_ref, tmp_ref, *, seq):
    res = jnp.dot(h_ref[...], w_ref[...], preferred_element_type=F32)
    g = pl.program_id(0) % DIL_GROUPS
    nl = o_ref.shape[0]

    @pl.when(g == 0)
    def _():
        for j in range(nl):
            o_ref[j] = res[:, j * LANES:(j + 1) * LANES].astype(o_ref.dtype)

    @pl.when(g > 0)
    def _():
        for j in range(nl):
            acc_ref[j] = res[:, j * LANES:(j + 1) * LANES]

    for gi, (_, r) in enumerate(DIL_CONFIGS):
        if r == 1:
            continue

        @pl.when(g == gi)
        def _(r=r):
            f1 = min(r, PHASE_STRIDE)
            l1 = seq // f1
            dst = o_ref if r == f1 else tmp_ref
            for c1 in range(f1):
                for j in range(nl):
                    dst[j, c1 * l1:(c1 + 1) * l1, :] = acc_ref[j, pl.ds(c1, l1, stride=f1), :].astype(dst.dtype)
            if r > f1:
                f2 = r // f1
                l2 = l1 // f2
                for s in range(r):
                    c1, c2 = divmod(s, f2)
                    for j in range(nl):
                        o_ref[j, s * l2:(s + 1) * l2, :] = (
                            tmp_ref[j, pl.ds(c1 * l1 + c2, l2, stride=f2), :].astype(o_ref.dtype))


def _dil_proj(h, w, seq, tn=512):
    t = h.shape[0]
    n = w.shape[1]
    nl = tn // LANES
    return pl.pallas_call(
        functools.partial(_dil_proj_kernel, seq=seq),
        grid=(n // tn, t // seq),
        in_specs=[pl.BlockSpec((seq, D_MODEL), lambda j, b: (b, 0)),
                  pl.BlockSpec((D_MODEL, tn), lambda j, b: (0, j))],
        out_specs=pl.BlockSpec((nl, seq, LANES), lambda j, b: (j, b, 0)),
        out_shape=jax.ShapeDtypeStruct((n // LANES, t, LANES), BF16),
        scratch_shapes=[pltpu.VMEM((nl, seq, LANES), F32)] * 2,
        compiler_params=_cparams(("parallel", "parallel")),
        name="dil_proj",
    )(h, w)


def _dil_attn_kernel(slopes_ref, q0, q1, q2, k0, k1, k2, v0, v1, v2, o_ref, acc_s, m_s, l_s, *, seq):
    pair = pl.program_id(1)
    sp = DIL_SPAN
    head0 = lax.broadcasted_iota(jnp.int32, (sp, LANES), 1) < DIL_HD
    qi = lax.broadcasted_iota(jnp.int32, (sp, 2 * sp), 0)
    kj = lax.broadcasted_iota(jnp.int32, (sp, 2 * sp), 1)
    dist_w = qi + sp - kj
    valid_w = (dist_w >= 0) & (dist_w <= sp)
    dist_1 = dist_w[:, sp:]
    valid_1 = dist_1 >= 0
    qs, ks, vs = (q0, q1, q2), (k0, k1, k2), (v0, v1, v2)
    for g, (_, r) in enumerate(DIL_CONFIGS):
        ln = seq // r
        nb = ln // sp
        sl = [slopes_ref[g * DIL_HPG + 2 * pair + hh] * float(r) for hh in range(2)]
        bias_w = jnp.concatenate([jnp.where(valid_w, -s * dist_w.astype(F32), NEG) for s in sl], axis=0)
        bias_1 = jnp.concatenate([jnp.where(valid_1, -s * dist_1.astype(F32), NEG) for s in sl], axis=0)
        for slot_i, c in enumerate(_phase_of_slot(r)):
            for i in range(nb):
                row0 = slot_i * ln + i * sp
                qb = qs[g][0, row0:row0 + sp, :]
                zero = jnp.zeros_like(qb)
                q2h = jnp.concatenate([jnp.where(head0, qb, zero), jnp.where(head0, zero, qb)], axis=0)
                lo = row0 if i == 0 else row0 - sp
                kw = ks[g][0, lo:row0 + sp, :]
                vw = vs[g][0, lo:row0 + sp, :]
                s = lax.dot_general(q2h, kw, (((1,), (1,)), ((), ())), preferred_element_type=F32)
                s = s + (bias_1 if i == 0 else bias_w)
                m = jnp.max(s, axis=-1, keepdims=True)
                p = jnp.exp(s - m)
                l = jnp.sum(p, axis=-1, keepdims=True)
                o2 = jnp.dot(p.astype(BF16), vw, preferred_element_type=F32)
                dst = pl.ds(i * sp * r + c, sp, stride=r) if r > 1 else pl.ds(i * sp, sp)
                acc_s[g, dst, :] = jnp.where(head0, o2[:sp], o2[sp:])
                m_s[g, dst, :] = jnp.where(head0, m[:sp], m[sp:])
                l_s[g, dst, :] = jnp.where(head0, l[:sp], l[sp:])

    ch = 256

    def combine(i, _):
        rows = pl.ds(pl.multiple_of(i * ch, ch), ch)
        ms = [m_s[g, rows, :] for g in range(DIL_GROUPS)]
        mx = jnp.maximum(jnp.maximum(ms[0], ms[1]), ms[2])
        ws = [jnp.exp(m - mx) for m in ms]
        num = sum(ws[g] * acc_s[g, rows, :] for g in range(DIL_GROUPS))
        den = sum(ws[g] * l_s[g, rows, :] for g in range(DIL_GROUPS))
        o_ref[rows, :] = (num / den).astype(o_ref.dtype)
        return 0

    lax.fori_loop(0, seq // ch, combine, 0)


def _dil_attn(qkv, slopes, batch, seq):
    t = qkv.shape[1]
    pairs = DIL_HPG // 2
    cpg = DIL_HPG * DIL_HD // LANES
    cpt = DIL_WIDTH // LANES

    def spec(tt, g):
        return pl.BlockSpec((1, seq, LANES), lambda b, p: (tt * cpt + g * cpg + p, b, 0))

    in_specs = [pl.BlockSpec(memory_space=pltpu.SMEM)] + [spec(tt, g) for tt in range(3) for g in range(DIL_GROUPS)]
    return pl.pallas_call(
        functools.partial(_dil_attn_kernel, seq=seq),
        grid=(batch, pairs),
        in_specs=in_specs,
        out_specs=pl.BlockSpec((seq, LANES), lambda b, p: (b, p)),
        out_shape=jax.ShapeDtypeStruct((t, DIL_HPG * DIL_HD), BF16),
        scratch_shapes=[pltpu.VMEM((DIL_GROUPS, seq, LANES), F32)] * 3,
        compiler_params=_cparams(("parallel", "parallel")),
        name="dil_attn",
    )(slopes, *([qkv] * 9))


def _to_token_tiles(dst_ref, val):
    n = val.shape[0]
    for s in range(ROW_TILES):
        dst_ref[pl.ds(s, n, stride=ROW_TILES), :] = val[:, s * LANES:(s + 1) * LANES]


def _from_token_tiles(src_ref, n):
    return jnp.concatenate([src_ref[pl.ds(s, n, stride=ROW_TILES), :] for s in range(ROW_TILES)], axis=1)


def _merge_kernel(x_ref, h_ref, om_ref, od_ref, wg_ref, wbm_ref, wbd_ref, wo_ref, g_ref, x_out, h_out, *, tiles):
    gates = jnp.dot(h_ref[...], wg_ref[...], preferred_element_type=F32)
    bm = jnp.dot(om_ref[...], wbm_ref[...], preferred_element_type=F32)
    bd = jnp.dot(od_ref[...], wbd_ref[...], preferred_element_type=F32)
    merged = _sigmoid(gates[:, :D_MODEL]) * bm + _sigmoid(gates[:, D_MODEL:]) * bd
    x1 = x_ref[...] + jnp.dot(merged.astype(BF16), wo_ref[...], preferred_element_type=F32)
    x_out[...] = x1
    hn = _rms(x1, g_ref[...])
    if tiles:
        _to_token_tiles(h_out, hn)
    else:
        h_out[...] = hn.astype(h_out.dtype)


def _merge(x, h, o_mla, o_dil, wg, wbm, wbd, wo, g_next, tiles, tm=512):
    t = x.shape[0]
    row = lambda w: pl.BlockSpec((tm, w), lambda i: (i, 0))
    full = lambda a: pl.BlockSpec(a.shape, lambda i: (0, 0))
    if tiles:
        h_spec = pl.BlockSpec((tm * ROW_TILES, LANES), lambda i: (i, 0))
        h_shape = jax.ShapeDtypeStruct((t * ROW_TILES, LANES), F32)
    else:
        h_spec, h_shape = row(D_MODEL), jax.ShapeDtypeStruct((t, D_MODEL), BF16)
    return pl.pallas_call(
        functools.partial(_merge_kernel, tiles=tiles),
        grid=(t // tm,),
        in_specs=[row(D_MODEL), row(D_MODEL), row(o_mla.shape[1]), row(o_dil.shape[1]),
                  full(wg), full(wbm), full(wbd), full(wo), full(g_next)],
        out_specs=[row(D_MODEL), h_spec],
        out_shape=[jax.ShapeDtypeStruct((t, D_MODEL), F32), h_shape],
        compiler_params=_cparams(("parallel",)),
        name="merge",
    )(x, h, o_mla, o_dil, wg, wbm, wbd, wo, g_next)


def _ffn_kernel(x_ref, h_ref, wg_ref, wu_ref, wd_ref, o_ref):
    @pl.when(pl.program_id(1) == 0)
    def _():
        o_ref[...] = x_ref[...]

    h = h_ref[...]
    a = jnp.dot(h, wg_ref[...], preferred_element_type=F32)
    u = jnp.dot(h, wu_ref[...], preferred_element_type=F32)
    o_ref[...] += jnp.dot((a * _sigmoid(a) * u).astype(BF16), wd_ref[...], preferred_element_type=F32)


def _ffn(x, h, wg, wu, wd, tm=1024, tf=512):
    t = x.shape[0]
    nf = wg.shape[1] // tf
    return pl.pallas_call(
        _ffn_kernel,
        grid=(t // tm, nf),
        in_specs=[pl.BlockSpec((tm, D_MODEL), lambda i, f: (i, 0)),
                  pl.BlockSpec((tm, D_MODEL), lambda i, f: (i, 0)),
                  pl.BlockSpec((D_MODEL, tf), lambda i, f: (0, f)),
                  pl.BlockSpec((D_MODEL, tf), lambda i, f: (0, f)),
                  pl.BlockSpec((tf, D_MODEL), lambda i, f: (f, 0))],
        out_specs=pl.BlockSpec((tm, D_MODEL), lambda i, f: (i, 0)),
        out_shape=jax.ShapeDtypeStruct((t, D_MODEL), F32),
        compiler_params=_cparams(("parallel", "arbitrary")),
        name="dense_ffn",
    )(x, h, wg, wu, wd)


def _router_kernel(h_ref, w_ref, b_ref, idx_ref, wt_ref):
    h = _from_token_tiles(h_ref, idx_ref.shape[0])
    logits = jnp.dot(h, w_ref[...], preferred_element_type=F32,
                     precision=lax.Precision.HIGHEST) + b_ref[...]
    lane = lax.broadcasted_iota(jnp.int32, logits.shape, 1)
    m1 = jnp.max(logits, axis=-1, keepdims=True)
    i1 = jnp.min(jnp.where(logits == m1, lane, LANES), axis=-1, keepdims=True)
    rest = jnp.where(lane == i1, NEG, logits)
    m2 = jnp.max(rest, axis=-1, keepdims=True)
    i2 = jnp.min(jnp.where(rest == m2, lane, LANES), axis=-1, keepdims=True)
    e = jnp.exp(m2 - m1)
    w1 = 1.0 / (1.0 + e)
    idx_ref[...] = jnp.where(lane == 0, i1, jnp.where(lane == 1, i2, 0))
    wt_ref[...] = jnp.where(lane == 0, w1, jnp.where(lane == 1, e * w1, 0.0))


def _router(h, w_pad, b_pad, tm=1024):
    t = h.shape[0] // ROW_TILES
    row = pl.BlockSpec((tm, LANES), lambda i: (i, 0))
    return pl.pallas_call(
        _router_kernel,
        grid=(t // tm,),
        in_specs=[pl.BlockSpec((tm * ROW_TILES, LANES), lambda i: (i, 0)),
                  pl.BlockSpec(w_pad.shape, lambda i: (0, 0)), pl.BlockSpec(b_pad.shape, lambda i: (0, 0))],
        out_specs=[row, row],
        out_shape=[jax.ShapeDtypeStruct((t, LANES), jnp.int32), jax.ShapeDtypeStruct((t, LANES), F32)],
        compiler_params=_cparams(("parallel",)),
        name="moe_router",
    )(h, w_pad, b_pad)


def _tile_at(ref, row8):
    return ref.at[pl.ds(pl.multiple_of(row8, ROW_TILES), ROW_TILES), :]


def _dispatch_kernel(pos_ref, fill_ref, h_ref, xs_hbm, zero_ref, sem, sem_fill, *, t, tmd, tm):
    i = pl.program_id(0)

    @pl.when(i == 0)
    def _():
        zero_ref[...] = jnp.zeros_like(zero_ref)
        for e in range(fill_ref.shape[0]):
            @pl.when(fill_ref[e] >= 0)
            def _():
                cp = pltpu.make_async_copy(
                    zero_ref, xs_hbm.at[pl.ds(pl.multiple_of(fill_ref[e], ROW_TILES), tm * ROW_TILES), :], sem_fill)
                cp.start()
                cp.wait()

    for k in range(TOP_K):
        for r in range(tmd):
            pltpu.make_async_copy(_tile_at(h_ref, r * ROW_TILES),
                                  _tile_at(xs_hbm, pos_ref[k * t + i * tmd + r]), sem.at[k]).start()
    for k in range(TOP_K):
        pltpu.make_async_copy(h_ref, h_ref, sem.at[k]).wait()


def _dispatch(h, pos8, fill8, n_rows, tm, tmd=256):
    t = h.shape[0] // ROW_TILES
    return pl.pallas_call(
        functools.partial(_dispatch_kernel, t=t, tmd=tmd, tm=tm),
        grid_spec=pltpu.PrefetchScalarGridSpec(
            num_scalar_prefetch=2,
            grid=(t // tmd,),
            in_specs=[pl.BlockSpec((tmd * ROW_TILES, LANES), lambda i, pos, fill: (i, 0))],
            out_specs=pl.BlockSpec(memory_space=pl.ANY),
            scratch_shapes=[pltpu.VMEM((tm * ROW_TILES, LANES), F32),
                            pltpu.SemaphoreType.DMA((TOP_K,)), pltpu.SemaphoreType.DMA(())]),
        out_shape=jax.ShapeDtypeStruct((n_rows * ROW_TILES, LANES), F32),
        compiler_params=_cparams(("arbitrary",)),
        name="moe_dispatch",
    )(pos8, fill8, h)


def _moe_ffn_kernel(te_ref, nv_ref, x_ref, wg_ref, wu_ref, wd_ref, o_ref, acc_ref, xb_ref, *, tm):
    i = pl.program_id(0)
    f = pl.program_id(1)
    valid = i < nv_ref[0]
    last = f == pl.num_programs(1) - 1

    @pl.when(valid & (f == 0))
    def _():
        xb_ref[...] = _from_token_tiles(x_ref, tm).astype(BF16)
        acc_ref[...] = jnp.zeros_like(acc_ref)

    @pl.when(valid)
    def _():
        h = xb_ref[...]
        a = jnp.dot(h, wg_ref[0], preferred_element_type=F32)
        u = jnp.dot(h, wu_ref[0], preferred_element_type=F32)
        acc_ref[...] += jnp.dot((a * _sigmoid(a) * u).astype(BF16), wd_ref[0], preferred_element_type=F32)

    @pl.when(valid & last)
    def _():
        _to_token_tiles(o_ref, acc_ref[...])

    @pl.when(jnp.logical_not(valid) & last)
    def _():
        o_ref[...] = jnp.zeros_like(o_ref)


def _moe_ffn(xs, tile_expert, n_valid, wg, wu, wd, tm, tf=1792):
    n_tiles = xs.shape[0] // (tm * ROW_TILES)
    nf = wg.shape[2] // tf

    def live(i, f, te, nv):
        ok = i < nv[0]
        return jnp.where(ok, i, nv[0] - 1), jnp.where(ok, f, nf - 1)

    return pl.pallas_call(
        functools.partial(_moe_ffn_kernel, tm=tm),
        grid_spec=pltpu.PrefetchScalarGridSpec(
            num_scalar_prefetch=2,
            grid=(n_tiles, nf),
            in_specs=[pl.BlockSpec((tm * ROW_TILES, LANES), lambda *a: (live(*a)[0], 0)),
                      pl.BlockSpec((1, D_MODEL, tf), lambda *a: (a[2][live(*a)[0]], 0, live(*a)[1])),
                      pl.BlockSpec((1, D_MODEL, tf), lambda *a: (a[2][live(*a)[0]], 0, live(*a)[1])),
                      pl.BlockSpec((1, tf, D_MODEL), lambda *a: (a[2][live(*a)[0]], live(*a)[1], 0))],
            out_specs=pl.BlockSpec((tm * ROW_TILES, LANES), lambda i, f, te, nv: (i, 0)),
            scratch_shapes=[pltpu.VMEM((tm, D_MODEL), F32), pltpu.VMEM((tm, D_MODEL), BF16)]),
        out_shape=jax.ShapeDtypeStruct(xs.shape, F32),
        compiler_params=_cparams(("arbitrary", "arbitrary")),
        name="moe_ffn",
    )(tile_expert, n_valid, xs, wg, wu, wd)


def _moe(h, w_router, b_router, wg, wu, wd, tm=448):
    t = h.shape[0] // ROW_TILES
    w_pad = jnp.zeros((D_MODEL, LANES), F32).at[:, :N_EXPERTS].set(w_router)
    b_pad = jnp.full((1, LANES), NEG, F32).at[0, :N_EXPERTS].set(b_router)
    idx_l, wt_l = _router(h, w_pad, b_pad)
    expert = idx_l[:, :TOP_K].T.reshape(-1)
    onehot = (expert[:, None] == jnp.arange(N_EXPERTS)[None, :]).astype(jnp.int32)
    csum = jnp.cumsum(onehot, axis=0)
    counts = csum[-1]
    rank = jnp.sum((csum - onehot) * onehot, axis=1)
    tiles = (counts + tm - 1) // tm
    tile_end = jnp.cumsum(tiles)
    pad_off = (tile_end - tiles) * tm
    pos8 = (pad_off[expert] + rank) * ROW_TILES
    n_tiles = (TOP_K * t) // tm + N_EXPERTS
    n_valid = tile_end[-1:].astype(jnp.int32)
    tile_ids = jnp.minimum(jnp.arange(n_tiles), n_valid[0] - 1)
    tile_expert = jnp.sum(tile_end[None, :] <= tile_ids[:, None], axis=1).astype(jnp.int32)
    spare = n_valid[0] + jnp.arange(N_EXPERTS)
    fill8 = jnp.concatenate([jnp.where(tiles > 0, tile_end - 1, -1), jnp.where(spare < n_tiles, spare, -1)])
    fill8 = jnp.where(fill8 >= 0, fill8 * (tm * ROW_TILES), -1).astype(jnp.int32)
    xs = _dispatch(h, pos8, fill8, n_tiles * tm, tm)
    ys = _moe_ffn(xs, tile_expert, n_valid, wg, wu, wd, tm)
    return ys, pos8, wt_l


def _ple_body(x, p_ref, wg_ref, wp_ref, g_ref, gn_ref, x_out, h_out):
    gate = _sigmoid(jnp.dot(_rms(x, g_ref[...]).astype(BF16), wg_ref[...], preferred_element_type=F32))
    x2 = x + gate * jnp.dot(p_ref[...].astype(BF16), wp_ref[...], preferred_element_type=F32)
    x_out[...] = x2
    h_out[...] = _rms(x2, gn_ref[...]).astype(h_out.dtype)


def _ple_kernel(x_ref, p_ref, wg_ref, wp_ref, g_ref, gn_ref, x_out, h_out):
    _ple_body(x_ref[...], p_ref, wg_ref, wp_ref, g_ref, gn_ref, x_out, h_out)


def _ple_moe_kernel(pos_ref, x_ref, p_ref, wt_ref, wg_ref, wp_ref, g_ref, gn_ref, y_hbm, x_out, h_out,
                    ybuf, sem, *, t, tm):
    i = pl.program_id(0)
    slot = i % 2

    def gather(block, s):
        for k in range(TOP_K):
            for r in range(tm):
                pltpu.make_async_copy(_tile_at(y_hbm, pos_ref[k * t + block * tm + r]),
                                      _tile_at(ybuf.at[s, k], r * ROW_TILES), sem.at[s]).start()

    @pl.when(i == 0)
    def _():
        gather(0, 0)

    @pl.when(i + 1 < pl.num_programs(0))
    def _():
        gather(i + 1, 1 - slot)

    pltpu.make_async_copy(ybuf.at[slot], ybuf.at[slot], sem.at[slot]).wait()
    wt = wt_ref[...]
    x = (x_ref[...] + wt[:, 0:1] * _from_token_tiles(ybuf.at[slot, 0], tm)
         + wt[:, 1:2] * _from_token_tiles(ybuf.at[slot, 1], tm))
    _ple_body(x, p_ref, wg_ref, wp_ref, g_ref, gn_ref, x_out, h_out)


def _ple(x, p, wg, wp, g, g_next, h_dtype, moe=None, tm=512):
    t = x.shape[0]
    out_shape = [jax.ShapeDtypeStruct((t, D_MODEL), F32), jax.ShapeDtypeStruct((t, D_MODEL), h_dtype)]
    if moe is None:
        row = lambda w: pl.BlockSpec((tm, w), lambda i: (i, 0))
        full = lambda a: pl.BlockSpec(a.shape, lambda i: (0, 0))
        return pl.pallas_call(
            _ple_kernel,
            grid=(t // tm,),
            in_specs=[row(D_MODEL), row(P_DIM), full(wg), full(wp), full(g), full(g_next)],
            out_specs=[row(D_MODEL), row(D_MODEL)],
            out_shape=out_shape,
            compiler_params=_cparams(("parallel",)),
            name="ple",
        )(x, p, wg, wp, g, g_next)
    ys, pos8, wt = moe
    tm = 256
    row = lambda w: pl.BlockSpec((tm, w), lambda i, pos: (i, 0))
    full = lambda a: pl.BlockSpec(a.shape, lambda i, pos: (0, 0))
    return pl.pallas_call(
        functools.partial(_ple_moe_kernel, t=t, tm=tm),
        grid_spec=pltpu.PrefetchScalarGridSpec(
            num_scalar_prefetch=1,
            grid=(t // tm,),
            in_specs=[row(D_MODEL), row(P_DIM), row(LANES), full(wg), full(wp), full(g), full(g_next),
                      pl.BlockSpec(memory_space=pl.ANY)],
            out_specs=[row(D_MODEL), row(D_MODEL)],
            scratch_shapes=[pltpu.VMEM((2, TOP_K, tm * ROW_TILES, LANES), F32), pltpu.SemaphoreType.DMA((2,))]),
        out_shape=out_shape,
        compiler_params=_cparams(("arbitrary",)),
        name="ple_moe",
    )(pos8, x, p, wt, wg, wp, g, g_next, ys)


def _rot_cols(w):
    half = w.shape[-1] // 2
    return jnp.concatenate([-w[:, half:], w[:, :half]], axis=-1)


def _layer_weights(i, w_in, w_uq, w_ukv):
    wi = w_in[i]
    z = lambda n: jnp.zeros((wi.shape[0], n), F32)
    w_kr = wi[:, IN_OFF[2]:IN_OFF[3]]
    pad = MLA_SLAB - MLA_NOPE - MLA_ROPE
    wlat = jnp.concatenate([wi[:, :IN_OFF[2]], z(MLA_NOPE), w_kr, z(pad), z(MLA_NOPE), _rot_cols(w_kr), z(pad)],
                           axis=1).astype(BF16)
    qscale = jnp.concatenate([jnp.full((DIL_WIDTH,), DIL_HD ** -0.5, F32), jnp.ones((2 * DIL_WIDTH,), F32)])
    wdil = (wi[:, IN_OFF[3]:IN_OFF[6]] * qscale).astype(BF16)
    wgate = wi[:, IN_OFF[6]:].astype(BF16)
    uq = w_uq[i].reshape(MLA_Q_RANK, MLA_HEADS, MLA_NOPE + MLA_ROPE)
    zq = lambda n: jnp.zeros((MLA_Q_RANK, MLA_HEADS, n), F32)
    rope_rot = jnp.concatenate([-uq[..., MLA_NOPE + MLA_ROPE // 2:], uq[..., MLA_NOPE:MLA_NOPE + MLA_ROPE // 2]], -1)
    wq = jnp.concatenate([uq, zq(pad)], axis=-1).reshape(MLA_Q_RANK, -1).astype(BF16)
    wqr = jnp.concatenate([zq(MLA_NOPE), rope_rot, zq(pad)], axis=-1).reshape(MLA_Q_RANK, -1).astype(BF16)
    ukv = w_ukv[i].reshape(MLA_KV_RANK, MLA_HEADS, MLA_NOPE + MLA_V)
    zkv = jnp.zeros((MLA_KV_RANK, MLA_HEADS, MLA_SLAB - MLA_NOPE), F32)
    wk = jnp.concatenate([ukv[..., :MLA_NOPE], zkv], axis=-1).reshape(MLA_KV_RANK, -1).astype(BF16)
    wv = jnp.concatenate([ukv[..., MLA_NOPE:], zkv], axis=-1).reshape(MLA_KV_RANK, -1).astype(BF16)
    return wlat, wdil, wgate, wq, wqr, wk, wv


def kernel(x, p, positions, attn_norm, w_in, q_norm, w_uq, kv_norm, w_ukv, w_br_mla, w_br_dil, w_out, ffn_norm, dense_w_gate, dense_w_up, dense_w_down, router_w, router_b, moe_w_gate, moe_w_up, moe_w_down, ple_norm, ple_w_gate, ple_w_proj, final_norm):
    batch, seq, d = x.shape
    t = batch * seq
    depth = w_in.shape[0]
    xf = x.reshape(t, d)
    cos_t, sin_t = _rope_tables(positions)
    slopes = _alibi_slopes(DIL_HEADS)
    row = lambda v: v.reshape(1, -1)
    h = _norm(xf, attn_norm[0])
    for i in range(depth):
        wlat, wdil, wgate, wq, wqr, wk, wv = _layer_weights(i, w_in, w_uq, w_ukv)
        q, k, v = _mla_proj(h, wlat, wq, wqr, wk, wv, row(q_norm[i]), row(kv_norm[i]), cos_t, sin_t)
        o_mla = _mla_attn(q, k, v, batch, seq)
        qkv_d = _dil_proj(h, wdil, seq)
        o_dil = _dil_attn(qkv_d, slopes, batch, seq)
        moe_layer = i % 2 == 1
        xf, h2 = _merge(xf, h, o_mla, o_dil, wgate, w_br_mla[i].astype(BF16), w_br_dil[i].astype(BF16),
                        w_out[i].astype(BF16), row(ffn_norm[i]), tiles=moe_layer)
        j = i // 2
        moe = None
        if moe_layer:
            moe = _moe(h2, router_w[j], router_b[j], moe_w_gate[j].astype(BF16), moe_w_up[j].astype(BF16),
                       moe_w_down[j].astype(BF16))
        else:
            xf = _ffn(xf, h2, dense_w_gate[j].astype(BF16), dense_w_up[j].astype(BF16),
                      dense_w_down[j].astype(BF16))
        last = i == depth - 1
        g_next = final_norm if last else attn_norm[i + 1]
        xf, h = _ple(xf, p[i].reshape(t, -1), ple_w_gate[i].astype(BF16), ple_w_proj[i].astype(BF16),
                     row(ple_norm[i]), row(g_next), F32 if last else BF16, moe)
    return h.reshape(batch, seq, d)
```

```python
import functools

import numpy as np
import jax
import jax.numpy as jnp
from jax import lax
from jax.experimental import pallas as pl
from jax.experimental.pallas import tpu as pltpu

F32 = jnp.float32
BF16 = jnp.bfloat16

D_MODEL = 1024
P_DIM = 256
NORM_EPS = 1e-6
DEPTH = 2

MLA_HEADS = 8
MLA_Q_RANK = 384
MLA_KV_RANK = 256
MLA_NOPE = 64
MLA_ROPE = 32
MLA_V = 64
ROPE_THETA = 10000.0
MLA_SLAB = 128

DIL_CONFIGS = ((128, 1), (512, 4), (2048, 16))
DIL_GROUPS = 3
DIL_HPG = 8
DIL_HEADS = 24
DIL_HD = 64
DIL_WIDTH = DIL_HEADS * DIL_HD
DIL_SPAN = 128

D_FF = 3584
N_EXPERTS = 8
TOP_K = 2

LANES = 128
ROW_TILES = D_MODEL // LANES
NEG = -1e30
LOG2E = 1.4426950408889634
VMEM_LIMIT = 56 * 1024 * 1024

IN_OFF = tuple(int(o) for o in np.cumsum((0, MLA_Q_RANK, MLA_KV_RANK, MLA_ROPE, DIL_WIDTH, DIL_WIDTH,
                                          DIL_WIDTH, D_MODEL, D_MODEL)))


def _cparams(sem):
    return pltpu.CompilerParams(dimension_semantics=sem, vmem_limit_bytes=VMEM_LIMIT)


def _rms(x, g):
    return x * lax.rsqrt(jnp.mean(x * x, axis=-1, keepdims=True) + NORM_EPS) * g


def _sigmoid(x):
    return 1.0 / (1.0 + jnp.exp(-x))


def _alibi_slopes(n):
    def pow2(m):
        start = 2.0 ** (-8.0 / m)
        return [start ** (i + 1) for i in range(m)]
    if float(np.log2(n)).is_integer():
        s = pow2(n)
    else:
        c = 2 ** int(np.floor(np.log2(n)))
        s = pow2(c) + pow2(2 * c)[0::2][: n - c]
    return jnp.asarray(sorted(s, reverse=True), dtype=F32)


def _norm_kernel(x_ref, g_ref, o_ref):
    o_ref[...] = _rms(x_ref[...], g_ref[...]).astype(o_ref.dtype)


def _norm(x, g, tm=1024):
    t, d = x.shape
    return pl.pallas_call(
        _norm_kernel,
        grid=(t // tm,),
        in_specs=[pl.BlockSpec((tm, d), lambda i: (i, 0)), pl.BlockSpec((1, d), lambda i: (0, 0))],
        out_specs=pl.BlockSpec((tm, d), lambda i: (i, 0)),
        out_shape=jax.ShapeDtypeStruct((t, d), BF16),
        compiler_params=_cparams(("parallel",)),
        name="rms_norm",
    )(x, g.reshape(1, d))


def _rope_table_kernel(pos_ref, invf_ref, cos_ref, sin_ref):
    ang = pos_ref[...].astype(F32) * invf_ref[...]
    lane = lax.broadcasted_iota(jnp.int32, ang.shape, 1)
    rope_lane = (lane >= MLA_NOPE) & (lane < MLA_NOPE + MLA_ROPE)
    cos_ref[...] = jnp.where(lane < MLA_NOPE, 1.0, jnp.where(rope_lane, jnp.cos(ang), 0.0))
    sin_ref[...] = jnp.where(rope_lane, jnp.sin(ang), 0.0)


def _rope_tables(positions, tm=2048):
    t = positions.size
    half = MLA_ROPE // 2
    inv_freq = ROPE_THETA ** (-jnp.arange(half, dtype=F32) / half)
    invf = jnp.zeros((1, LANES), F32).at[0, MLA_NOPE:MLA_NOPE + MLA_ROPE].set(jnp.concatenate([inv_freq, inv_freq]))
    pos_b = jnp.broadcast_to(positions.reshape(t, 1), (t, LANES))
    spec = pl.BlockSpec((tm, LANES), lambda i: (i, 0))
    return pl.pallas_call(
        _rope_table_kernel,
        grid=(t // tm,),
        in_specs=[spec, pl.BlockSpec((1, LANES), lambda i: (0, 0))],
        out_specs=[spec, spec],
        out_shape=[jax.ShapeDtypeStruct((t, LANES), F32)] * 2,
        compiler_params=_cparams(("parallel",)),
        name="rope_tables",
    )(pos_b, invf)


def _mla_proj_kernel(h_ref, wlat_ref, wq_ref, wqr_ref, wk_ref, wv_ref, qn_ref, kvn_ref, cos_ref, sin_ref,
                     q_out, k_out, v_out):
    lat = jnp.dot(h_ref[...], wlat_ref[...], preferred_element_type=F32)
    cqn = _rms(lat[:, :MLA_Q_RANK], qn_ref[...]).astype(BF16)
    ckvn = _rms(lat[:, MLA_Q_RANK:MLA_Q_RANK + MLA_KV_RANK], kvn_ref[...]).astype(BF16)
    cos = cos_ref[...]
    sin = sin_ref[...]
    o = MLA_Q_RANK + MLA_KV_RANK
    k_rope = lat[:, o:o + LANES] * cos + lat[:, o + LANES:o + 2 * LANES] * sin
    qa = jnp.dot(cqn, wq_ref[...], preferred_element_type=F32)
    qb = jnp.dot(cqn, wqr_ref[...], preferred_element_type=F32)
    kk = jnp.dot(ckvn, wk_ref[...], preferred_element_type=F32)
    scale = (MLA_NOPE + MLA_ROPE) ** -0.5 * LOG2E
    vv = jnp.dot(ckvn, wv_ref[...], preferred_element_type=F32)
    ones_lane = lax.broadcasted_iota(jnp.int32, cos.shape, 1) >= MLA_V
    for hd in range(MLA_HEADS):
        sl = slice(hd * MLA_SLAB, (hd + 1) * MLA_SLAB)
        q_out[:, sl] = ((qa[:, sl] * cos + qb[:, sl] * sin) * scale).astype(BF16)
        k_out[:, sl] = (kk[:, sl] + k_rope).astype(BF16)
        v_out[:, sl] = jnp.where(ones_lane, 1.0, vv[:, sl]).astype(BF16)


def _mla_proj(h, wlat, wq, wqr, wk, wv, qn, kvn, cos_t, sin_t, tm=512):
    t = h.shape[0]
    row = lambda w: pl.BlockSpec((tm, w), lambda i: (i, 0))
    full = lambda a: pl.BlockSpec(a.shape, lambda i: (0, 0))
    hs = MLA_HEADS * MLA_SLAB
    return pl.pallas_call(
        _mla_proj_kernel,
        grid=(t // tm,),
        in_specs=[row(D_MODEL), full(wlat), full(wq), full(wqr), full(wk), full(wv), full(qn), full(kvn),
                  row(LANES), row(LANES)],
        out_specs=[row(hs), row(hs), row(hs)],
        out_shape=[jax.ShapeDtypeStruct((t, hs), BF16)] * 3,
        compiler_params=_cparams(("parallel",)),
        name="mla_proj",
    )(h, wlat, wq, wqr, wk, wv, qn, kvn, cos_t, sin_t)


def _mla_attn_kernel(q_ref, k_ref, v_ref, o_ref, s_scr, *, tq):
    qi = pl.program_id(2)
    rq = lax.broadcasted_iota(jnp.int32, (tq, tq), 0)
    ck = lax.broadcasted_iota(jnp.int32, (tq, tq), 1)
    causal = ck <= rq
    nl = tq // LANES

    def tile(n):
        outs = []
        for hh in range(2):
            q = q_ref[:, hh * MLA_SLAB:(hh + 1) * MLA_SLAB]
            mx = None
            for c in range(n + 1):
                ks = k_ref[c * tq:(c + 1) * tq, hh * MLA_SLAB:(hh + 1) * MLA_SLAB]
                s = lax.dot_general(q, ks, (((1,), (1,)), ((), ())), preferred_element_type=F32)
                if c == n:
                    s = jnp.where(causal, s, NEG)
                s_scr[hh, c] = s
                parts = [s[:, i * LANES:(i + 1) * LANES] for i in range(nl)]
                mx = functools.reduce(jnp.maximum, parts if mx is None else [mx] + parts)
            m = jnp.max(mx, axis=-1, keepdims=True)
            acc = None
            for c in range(n + 1):
                vs = v_ref[c * tq:(c + 1) * tq, hh * MLA_SLAB:(hh + 1) * MLA_SLAB]
                p = jnp.exp2((s_scr[hh, c] - m).astype(BF16))
                pv = jnp.dot(p, vs, preferred_element_type=F32)
                acc = pv if acc is None else acc + pv
            outs.append(acc / acc[:, MLA_V:MLA_V + 1])
        o_ref[...] = jnp.concatenate([outs[0][:, :MLA_V], outs[1][:, :MLA_V]], axis=-1).astype(o_ref.dtype)

    for n in range(s_scr.shape[1]):
        pl.when(qi == n)(functools.partial(tile, n))


def _mla_attn(q, k, v, batch, seq, tq=512):
    t = q.shape[0]
    nq = seq // tq
    pairs = MLA_HEADS // 2
    return pl.pallas_call(
        functools.partial(_mla_attn_kernel, tq=tq),
        grid=(batch, pairs, nq),
        in_specs=[pl.BlockSpec((tq, 2 * MLA_SLAB), lambda b, p, i: (b * nq + i, p)),
                  pl.BlockSpec((seq, 2 * MLA_SLAB), lambda b, p, i: (b, p)),
                  pl.BlockSpec((seq, 2 * MLA_SLAB), lambda b, p, i: (b, p))],
        out_specs=pl.BlockSpec((tq, 2 * MLA_V), lambda b, p, i: (b * nq + i, p)),
        out_shape=jax.ShapeDtypeStruct((t, MLA_HEADS * MLA_V), BF16),
        scratch_shapes=[pltpu.VMEM((2, nq, tq, tq), F32)],
        compiler_params=_cparams(("parallel", "parallel", "arbitrary")),
        name="mla_attn",
    )(q, k, v)


PHASE_STRIDE = 4


def _phase_of_slot(r):
    if r <= PHASE_STRIDE:
        return list(range(r))
    f2 = r // PHASE_STRIDE
    return [c1 + PHASE_STRIDE * c2 for c1 in range(PHASE_STRIDE) for c2 in range(f2)]


def _dil_proj_kernel(h_ref, w_ref, o_ref, acc_ref, tmp_ref, *, seq):
    res = jnp.dot(h_ref[...], w_ref[...], preferred_element_type=F32)
    g = pl.program_id(0) % DIL_GROUPS
    nl = o_ref.shape[0]

    @pl.when(g == 0)
    def _():
        for j in range(nl):
            o_ref[j] = res[:, j * LANES:(j + 1) * LANES].astype(o_ref.dtype)

    @pl.when(g > 0)
    def _():
        for j in range(nl):
            acc_ref[j] = res[:, j * LANES:(j + 1) * LANES]

    for gi, (_, r) in enumerate(DIL_CONFIGS):
        if r == 1:
            continue

        @pl.when(g == gi)
        def _(r=r):
            f1 = min(r, PHASE_STRIDE)
            l1 = seq // f1
            dst = o_ref if r == f1 else tmp_ref
            for c1 in range(f1):
                for j in range(nl):
                    dst[j, c1 * l1:(c1 + 1) * l1, :] = acc_ref[j, pl.ds(c1, l1, stride=f1), :].astype(dst.dtype)
            if r > f1:
                f2 = r // f1
                l2 = l1 // f2
                for s in range(r):
                    c1, c2 = divmod(s, f2)
                    for j in range(nl):
                        o_ref[j, s * l2:(s + 1) * l2, :] = (
                            tmp_ref[j, pl.ds(c1 * l1 + c2, l2, stride=f2), :].astype(o_ref.dtype))


def _dil_proj(h, w, seq, tn=512):
    t = h.shape[0]
    n = w.shape[1]
    nl = tn // LANES
    return pl.pallas_call(
        functools.partial(_dil_proj_kernel, seq=seq),
        grid=(n // tn, t // seq),
        in_specs=[pl.BlockSpec((seq, D_MODEL), lambda j, b: (b, 0)),
                  pl.BlockSpec((D_MODEL, tn), lambda j, b: (0, j))],
        out_specs=pl.BlockSpec((nl, seq, LANES), lambda j, b: (j, b, 0)),
        out_shape=jax.ShapeDtypeStruct((n // LANES, t, LANES), BF16),
        scratch_shapes=[pltpu.VMEM((nl, seq, LANES), F32)] * 2,
        compiler_params=_cparams(("parallel", "parallel")),
        name="dil_proj",
    )(h, w)


def _dil_attn_kernel(slopes_ref, q0, q1, q2, k0, k1, k2, v0, v1, v2, o_ref, acc_s, m_s, l_s, *, seq):
    pair = pl.program_id(1)
    sp = DIL_SPAN
    head0 = lax.broadcasted_iota(jnp.int32, (sp, LANES), 1) < DIL_HD
    qi = lax.broadcasted_iota(jnp.int32, (sp, 2 * sp), 0)
    kj = lax.broadcasted_iota(jnp.int32, (sp, 2 * sp), 1)
    dist_w = qi + sp - kj
    valid_w = (dist_w >= 0) & (dist_w <= sp)
    dist_1 = dist_w[:, sp:]
    valid_1 = dist_1 >= 0
    qs, ks, vs = (q0, q1, q2), (k0, k1, k2), (v0, v1, v2)
    for g, (_, r) in enumerate(DIL_CONFIGS):
        ln = seq // r
        nb = ln // sp
        sl = [slopes_ref[g * DIL_HPG + 2 * pair + hh] * float(r) for hh in range(2)]
        bias_w = jnp.concatenate([jnp.where(valid_w, -s * dist_w.astype(F32), NEG) for s in sl], axis=0)
        bias_1 = jnp.concatenate([jnp.where(valid_1, -s * dist_1.astype(F32), NEG) for s in sl], axis=0)
        for slot_i, c in enumerate(_phase_of_slot(r)):
            for i in range(nb):
                row0 = slot_i * ln + i * sp
                qb = qs[g][0, row0:row0 + sp, :]
                zero = jnp.zeros_like(qb)
                q2h = jnp.concatenate([jnp.where(head0, qb, zero), jnp.where(head0, zero, qb)], axis=0)
                lo = row0 if i == 0 else row0 - sp
                kw = ks[g][0, lo:row0 + sp, :]
                vw = vs[g][0, lo:row0 + sp, :]
                s = lax.dot_general(q2h, kw, (((1,), (1,)), ((), ())), preferred_element_type=F32)
                s = s + (bias_1 if i == 0 else bias_w)
                m = jnp.max(s, axis=-1, keepdims=True)
                p = jnp.exp(s - m)
                l = jnp.sum(p, axis=-1, keepdims=True)
                o2 = jnp.dot(p.astype(BF16), vw, preferred_element_type=F32)
                dst = pl.ds(i * sp * r + c, sp, stride=r) if r > 1 else pl.ds(i * sp, sp)
                acc_s[g, dst, :] = jnp.where(head0, o2[:sp], o2[sp:])
                m_s[g, dst, :] = jnp.where(head0, m[:sp], m[sp:])
                l_s[g, dst, :] = jnp.where(head0, l[:sp], l[sp:])

    ch = 256

    def combine(i, _):
        rows = pl.ds(pl.multiple_of(i * ch, ch), ch)
        ms = [m_s[g, rows, :] for g in range(DIL_GROUPS)]
        mx = jnp.maximum(jnp.maximum(ms[0], ms[1]), ms[2])
        ws = [jnp.exp(m - mx) for m in ms]
        num = sum(ws[g] * acc_s[g, rows, :] for g in range(DIL_GROUPS))
        den = sum(ws[g] * l_s[g, rows, :] for g in range(DIL_GROUPS))
        o_ref[rows, :] = (num / den).astype(o_ref.dtype)
        return 0

    lax.fori_loop(0, seq // ch, combine, 0)


def _dil_attn(qkv, slopes, batch, seq):
    t = qkv.shape[1]
    pairs = DIL_HPG // 2
    cpg = DIL_HPG * DIL_HD // LANES
    cpt = DIL_WIDTH // LANES

    def spec(tt, g):
        return pl.BlockSpec((1, seq, LANES), lambda b, p: (tt * cpt + g * cpg + p, b, 0))

    in_specs = [pl.BlockSpec(memory_space=pltpu.SMEM)] + [spec(tt, g) for tt in range(3) for g in range(DIL_GROUPS)]
    return pl.pallas_call(
        functools.partial(_dil_attn_kernel, seq=seq),
        grid=(batch, pairs),
        in_specs=in_specs,
        out_specs=pl.BlockSpec((seq, LANES), lambda b, p: (b, p)),
        out_shape=jax.ShapeDtypeStruct((t, DIL_HPG * DIL_HD), BF16),
        scratch_shapes=[pltpu.VMEM((DIL_GROUPS, seq, LANES), F32)] * 3,
        compiler_params=_cparams(("parallel", "parallel")),
        name="dil_attn",
    )(slopes, *([qkv] * 9))


def _to_token_tiles(dst_ref, val):
    n = val.shape[0]
    for s in range(ROW_TILES):
        dst_ref[pl.ds(s, n, stride=ROW_TILES), :] = val[:, s * LANES:(s + 1) * LANES]


def _from_token_tiles(src_ref, n):
    return jnp.concatenate([src_ref[pl.ds(s, n, stride=ROW_TILES), :] for s in range(ROW_TILES)], axis=1)


def _merge_kernel(x_ref, h_ref, om_ref, od_ref, wg_ref, wbm_ref, wbd_ref, wo_ref, g_ref, x_out, h_out, *, tiles):
    gates = jnp.dot(h_ref[...], wg_ref[...], preferred_element_type=F32)
    bm = jnp.dot(om_ref[...], wbm_ref[...], preferred_element_type=F32)
    bd = jnp.dot(od_ref[...], wbd_ref[...], preferred_element_type=F32)
    merged = _sigmoid(gates[:, :D_MODEL]) * bm + _sigmoid(gates[:, D_MODEL:]) * bd
    x1 = x_ref[...] + jnp.dot(merged.astype(BF16), wo_ref[...], preferred_element_type=F32)
    x_out[...] = x1
    hn = _rms(x1, g_ref[...])
    if tiles:
        _to_token_tiles(h_out, hn)
    else:
        h_out[...] = hn.astype(h_out.dtype)


def _merge(x, h, o_mla, o_dil, wg, wbm, wbd, wo, g_next, tiles, tm=512):
    t = x.shape[0]
    row = lambda w: pl.BlockSpec((tm, w), lambda i: (i, 0))
    full = lambda a: pl.BlockSpec(a.shape, lambda i: (0, 0))
    if tiles:
        h_spec = pl.BlockSpec((tm * ROW_TILES, LANES), lambda i: (i, 0))
        h_shape = jax.ShapeDtypeStruct((t * ROW_TILES, LANES), F32)
    else:
        h_spec, h_shape = row(D_MODEL), jax.ShapeDtypeStruct((t, D_MODEL), BF16)
    return pl.pallas_call(
        functools.partial(_merge_kernel, tiles=tiles),
        grid=(t // tm,),
        in_specs=[row(D_MODEL), row(D_MODEL), row(o_mla.shape[1]), row(o_dil.shape[1]),
                  full(wg), full(wbm), full(wbd), full(wo), full(g_next)],
        out_specs=[row(D_MODEL), h_spec],
        out_shape=[jax.ShapeDtypeStruct((t, D_MODEL), F32), h_shape],
        compiler_params=_cparams(("parallel",)),
        name="merge",
    )(x, h, o_mla, o_dil, wg, wbm, wbd, wo, g_next)


def _ffn_kernel(x_ref, h_ref, wg_ref, wu_ref, wd_ref, o_ref):
    @pl.when(pl.program_id(1) == 0)
    def _():
        o_ref[...] = x_ref[...]

    h = h_ref[...]
    a = jnp.dot(h, wg_ref[...], preferred_element_type=F32)
    u = jnp.dot(h, wu_ref[...], preferred_element_type=F32)
    o_ref[...] += jnp.dot((a * _sigmoid(a) * u).astype(BF16), wd_ref[...], preferred_element_type=F32)


def _ffn(x, h, wg, wu, wd, tm=512, tf=1792):
    t = x.shape[0]
    nf = wg.shape[1] // tf
    return pl.pallas_call(
        _ffn_kernel,
        grid=(t // tm, nf),
        in_specs=[pl.BlockSpec((tm, D_MODEL), lambda i, f: (i, 0)),
                  pl.BlockSpec((tm, D_MODEL), lambda i, f: (i, 0)),
                  pl.BlockSpec((D_MODEL, tf), lambda i, f: (0, f)),
                  pl.BlockSpec((D_MODEL, tf), lambda i, f: (0, f)),
                  pl.BlockSpec((tf, D_MODEL), lambda i, f: (f, 0))],
        out_specs=pl.BlockSpec((tm, D_MODEL), lambda i, f: (i, 0)),
        out_shape=jax.ShapeDtypeStruct((t, D_MODEL), F32),
        compiler_params=_cparams(("parallel", "arbitrary")),
        name="dense_ffn",
    )(x, h, wg, wu, wd)


def _router_kernel(h_ref, w_ref, b_ref, idx_ref, wt_ref):
    h = _from_token_tiles(h_ref, idx_ref.shape[0])
    logits = jnp.dot(h, w_ref[...], preferred_element_type=F32,
                     precision=lax.Precision.HIGHEST) + b_ref[...]
    lane = lax.broadcasted_iota(jnp.int32, logits.shape, 1)
    m1 = jnp.max(logits, axis=-1, keepdims=True)
    i1 = jnp.min(jnp.where(logits == m1, lane, LANES), axis=-1, keepdims=True)
    rest = jnp.where(lane == i1, NEG, logits)
    m2 = jnp.max(rest, axis=-1, keepdims=True)
    i2 = jnp.min(jnp.where(rest == m2, lane, LANES), axis=-1, keepdims=True)
    e = jnp.exp(m2 - m1)
    w1 = 1.0 / (1.0 + e)
    idx_ref[...] = jnp.where(lane == 0, i1, jnp.where(lane == 1, i2, 0))
    wt_ref[...] = jnp.where(lane == 0, w1, jnp.where(lane == 1, e * w1, 0.0))


def _router(h, w_pad, b_pad, tm=1024):
    t = h.shape[0] // ROW_TILES
    row = pl.BlockSpec((tm, LANES), lambda i: (i, 0))
    return pl.pallas_call(
        _router_kernel,
        grid=(t // tm,),
        in_specs=[pl.BlockSpec((tm * ROW_TILES, LANES), lambda i: (i, 0)),
                  pl.BlockSpec(w_pad.shape, lambda i: (0, 0)), pl.BlockSpec(b_pad.shape, lambda i: (0, 0))],
        out_specs=[row, row],
        out_shape=[jax.ShapeDtypeStruct((t, LANES), jnp.int32), jax.ShapeDtypeStruct((t, LANES), F32)],
        compiler_params=_cparams(("parallel",)),
        name="moe_router",
    )(h, w_pad, b_pad)


def _tile_at(ref, row8):
    return ref.at[pl.ds(pl.multiple_of(row8, ROW_TILES), ROW_TILES), :]


def _dispatch_kernel(pos_ref, fill_ref, h_ref, xs_hbm, zero_ref, sem, sem_fill, *, t, tmd, tm):
    i = pl.program_id(0)

    @pl.when(i == 0)
    def _():
        zero_ref[...] = jnp.zeros_like(zero_ref)
        for e in range(fill_ref.shape[0]):
            @pl.when(fill_ref[e] >= 0)
            def _():
                cp = pltpu.make_async_copy(
                    zero_ref, xs_hbm.at[pl.ds(pl.multiple_of(fill_ref[e], ROW_TILES), tm * ROW_TILES), :], sem_fill)
                cp.start()
                cp.wait()

    for k in range(TOP_K):
        for r in range(tmd):
            pltpu.make_async_copy(_tile_at(h_ref, r * ROW_TILES),
                                  _tile_at(xs_hbm, pos_ref[k * t + i * tmd + r]), sem.at[k]).start(priority=r % 2)
    for k in range(TOP_K):
        pltpu.make_async_copy(h_ref, h_ref, sem.at[k]).wait()


def _dispatch(h, pos8, fill8, n_rows, tm, tmd=256):
    t = h.shape[0] // ROW_TILES
    return pl.pallas_call(
        functools.partial(_dispatch_kernel, t=t, tmd=tmd, tm=tm),
        grid_spec=pltpu.PrefetchScalarGridSpec(
            num_scalar_prefetch=2,
            grid=(t // tmd,),
            in_specs=[pl.BlockSpec((tmd * ROW_TILES, LANES), lambda i, pos, fill: (i, 0))],
            out_specs=pl.BlockSpec(memory_space=pl.ANY),
            scratch_shapes=[pltpu.VMEM((tm * ROW_TILES, LANES), F32),
                            pltpu.SemaphoreType.DMA((TOP_K,)), pltpu.SemaphoreType.DMA(())]),
        out_shape=jax.ShapeDtypeStruct((n_rows * ROW_TILES, LANES), F32),
        compiler_params=_cparams(("arbitrary",)),
        name="moe_dispatch",
    )(pos8, fill8, h)


def _moe_ffn_kernel(te_ref, nv_ref, x_ref, wg_ref, wu_ref, wd_ref, o_ref, acc_ref, xb_ref, *, tm):
    i = pl.program_id(0)
    f = pl.program_id(1)
    valid = i < nv_ref[0]
    last = f == pl.num_programs(1) - 1

    @pl.when(valid & (f == 0))
    def _():
        xb_ref[...] = _from_token_tiles(x_ref, tm).astype(BF16)
        acc_ref[...] = jnp.zeros_like(acc_ref)

    @pl.when(valid)
    def _():
        h = xb_ref[...]
        a = jnp.dot(h, wg_ref[0], preferred_element_type=F32)
        u = jnp.dot(h, wu_ref[0], preferred_element_type=F32)
        acc_ref[...] += jnp.dot((a * _sigmoid(a) * u).astype(BF16), wd_ref[0], preferred_element_type=F32)

    @pl.when(valid & last)
    def _():
        _to_token_tiles(o_ref, acc_ref[...])

    @pl.when(jnp.logical_not(valid) & last)
    def _():
        o_ref[...] = jnp.zeros_like(o_ref)


def _moe_ffn(xs, tile_expert, n_valid, wg, wu, wd, tm, tf=1792):
    n_tiles = xs.shape[0] // (tm * ROW_TILES)
    nf = wg.shape[2] // tf

    def live(i, f, te, nv):
        ok = i < nv[0]
        return jnp.where(ok, i, nv[0] - 1), jnp.where(ok, f, nf - 1)

    return pl.pallas_call(
        functools.partial(_moe_ffn_kernel, tm=tm),
        grid_spec=pltpu.PrefetchScalarGridSpec(
            num_scalar_prefetch=2,
            grid=(n_tiles, nf),
            in_specs=[pl.BlockSpec((tm * ROW_TILES, LANES), lambda *a: (live(*a)[0], 0)),
                      pl.BlockSpec((1, D_MODEL, tf), lambda *a: (a[2][live(*a)[0]], 0, live(*a)[1])),
                      pl.BlockSpec((1, D_MODEL, tf), lambda *a: (a[2][live(*a)[0]], 0, live(*a)[1])),
                      pl.BlockSpec((1, tf, D_MODEL), lambda *a: (a[2][live(*a)[0]], live(*a)[1], 0))],
            out_specs=pl.BlockSpec((tm * ROW_TILES, LANES), lambda i, f, te, nv: (i, 0)),
            scratch_shapes=[pltpu.VMEM((tm, D_MODEL), F32), pltpu.VMEM((tm, D_MODEL), BF16)]),
        out_shape=jax.ShapeDtypeStruct(xs.shape, F32),
        compiler_params=_cparams(("arbitrary", "arbitrary")),
        name="moe_ffn",
    )(tile_expert, n_valid, xs, wg, wu, wd)


def _moe(h, w_router, b_router, wg, wu, wd, tm=448):
    t = h.shape[0] // ROW_TILES
    w_pad = jnp.zeros((D_MODEL, LANES), F32).at[:, :N_EXPERTS].set(w_router)
    b_pad = jnp.full((1, LANES), NEG, F32).at[0, :N_EXPERTS].set(b_router)
    idx_l, wt_l = _router(h, w_pad, b_pad)
    expert = idx_l[:, :TOP_K].T.reshape(-1)
    onehot = (expert[:, None] == jnp.arange(N_EXPERTS)[None, :]).astype(jnp.int32)
    csum = jnp.cumsum(onehot, axis=0)
    counts = csum[-1]
    rank = jnp.sum((csum - onehot) * onehot, axis=1)
    tiles = (counts + tm - 1) // tm
    tile_end = jnp.cumsum(tiles)
    pad_off = (tile_end - tiles) * tm
    pos8 = (pad_off[expert] + rank) * ROW_TILES
    n_tiles = (TOP_K * t) // tm + N_EXPERTS
    n_valid = tile_end[-1:].astype(jnp.int32)
    tile_ids = jnp.minimum(jnp.arange(n_tiles), n_valid[0] - 1)
    tile_expert = jnp.sum(tile_end[None, :] <= tile_ids[:, None], axis=1).astype(jnp.int32)
    spare = n_valid[0] + jnp.arange(N_EXPERTS)
    fill8 = jnp.concatenate([jnp.where(tiles > 0, tile_end - 1, -1), jnp.where(spare < n_tiles, spare, -1)])
    fill8 = jnp.where(fill8 >= 0, fill8 * (tm * ROW_TILES), -1).astype(jnp.int32)
    xs = _dispatch(h, pos8, fill8, n_tiles * tm, tm)
    ys = _moe_ffn(xs, tile_expert, n_valid, wg, wu, wd, tm)
    return ys, pos8, wt_l


def _ple_body(x, p_ref, wg_ref, wp_ref, g_ref, gn_ref, x_out, h_out):
    gate = _sigmoid(jnp.dot(_rms(x, g_ref[...]).astype(BF16), wg_ref[...], preferred_element_type=F32))
    x2 = x + gate * jnp.dot(p_ref[...].astype(BF16), wp_ref[...], preferred_element_type=F32)
    x_out[...] = x2
    h_out[...] = _rms(x2, gn_ref[...]).astype(h_out.dtype)


def _ple_kernel(x_ref, p_ref, wg_ref, wp_ref, g_ref, gn_ref, x_out, h_out):
    _ple_body(x_ref[...], p_ref, wg_ref, wp_ref, g_ref, gn_ref, x_out, h_out)


def _ple_moe_kernel(pos_ref, x_ref, p_ref, wt_ref, wg_ref, wp_ref, g_ref, gn_ref, y_hbm, x_out, h_out,
                    ybuf, sem, *, t, tm):
    i = pl.program_id(0)
    slot = i % 2

    def gather(block, s):
        for k in range(TOP_K):
            for r in range(tm):
                pltpu.make_async_copy(_tile_at(y_hbm, pos_ref[k * t + block * tm + r]),
                                      _tile_at(ybuf.at[s, k], r * ROW_TILES), sem.at[s]).start(priority=r % 2)

    @pl.when(i == 0)
    def _():
        gather(0, 0)

    @pl.when(i + 1 < pl.num_programs(0))
    def _():
        gather(i + 1, 1 - slot)

    pltpu.make_async_copy(ybuf.at[slot], ybuf.at[slot], sem.at[slot]).wait()
    wt = wt_ref[...]
    x = (x_ref[...] + wt[:, 0:1] * _from_token_tiles(ybuf.at[slot, 0], tm)
         + wt[:, 1:2] * _from_token_tiles(ybuf.at[slot, 1], tm))
    _ple_body(x, p_ref, wg_ref, wp_ref, g_ref, gn_ref, x_out, h_out)


def _ple(x, p, wg, wp, g, g_next, h_dtype, moe=None, tm=512):
    t = x.shape[0]
    out_shape = [jax.ShapeDtypeStruct((t, D_MODEL), F32), jax.ShapeDtypeStruct((t, D_MODEL), h_dtype)]
    if moe is None:
        row = lambda w: pl.BlockSpec((tm, w), lambda i: (i, 0))
        full = lambda a: pl.BlockSpec(a.shape, lambda i: (0, 0))
        return pl.pallas_call(
            _ple_kernel,
            grid=(t // tm,),
            in_specs=[row(D_MODEL), row(P_DIM), full(wg), full(wp), full(g), full(g_next)],
            out_specs=[row(D_MODEL), row(D_MODEL)],
            out_shape=out_shape,
            compiler_params=_cparams(("parallel",)),
            name="ple",
        )(x, p, wg, wp, g, g_next)
    ys, pos8, wt = moe
    tm = 256
    row = lambda w: pl.BlockSpec((tm, w), lambda i, pos: (i, 0))
    full = lambda a: pl.BlockSpec(a.shape, lambda i, pos: (0, 0))
    return pl.pallas_call(
        functools.partial(_ple_moe_kernel, t=t, tm=tm),
        grid_spec=pltpu.PrefetchScalarGridSpec(
            num_scalar_prefetch=1,
            grid=(t // tm,),
            in_specs=[row(D_MODEL), row(P_DIM), row(LANES), full(wg), full(wp), full(g), full(g_next),
                      pl.BlockSpec(memory_space=pl.ANY)],
            out_specs=[row(D_MODEL), row(D_MODEL)],
            scratch_shapes=[pltpu.VMEM((2, TOP_K, tm * ROW_TILES, LANES), F32), pltpu.SemaphoreType.DMA((2,))]),
        out_shape=out_shape,
        compiler_params=_cparams(("arbitrary",)),
        name="ple_moe",
    )(pos8, x, p, wt, wg, wp, g, g_next, ys)


def _rot_cols(w):
    half = w.shape[-1] // 2
    return jnp.concatenate([-w[:, half:], w[:, :half]], axis=-1)


def _layer_weights(i, w_in, w_uq, w_ukv):
    wi = w_in[i]
    z = lambda n: jnp.zeros((wi.shape[0], n), F32)
    w_kr = wi[:, IN_OFF[2]:IN_OFF[3]]
    pad = MLA_SLAB - MLA_NOPE - MLA_ROPE
    wlat = jnp.concatenate([wi[:, :IN_OFF[2]], z(MLA_NOPE), w_kr, z(pad), z(MLA_NOPE), _rot_cols(w_kr), z(pad)],
                           axis=1).astype(BF16)
    qscale = jnp.concatenate([jnp.full((DIL_WIDTH,), DIL_HD ** -0.5, F32), jnp.ones((2 * DIL_WIDTH,), F32)])
    wdil = (wi[:, IN_OFF[3]:IN_OFF[6]] * qscale).astype(BF16)
    wgate = wi[:, IN_OFF[6]:].astype(BF16)
    uq = w_uq[i].reshape(MLA_Q_RANK, MLA_HEADS, MLA_NOPE + MLA_ROPE)
    zq = lambda n: jnp.zeros((MLA_Q_RANK, MLA_HEADS, n), F32)
    rope_rot = jnp.concatenate([-uq[..., MLA_NOPE + MLA_ROPE // 2:], uq[..., MLA_NOPE:MLA_NOPE + MLA_ROPE // 2]], -1)
    wq = jnp.concatenate([uq, zq(pad)], axis=-1).reshape(MLA_Q_RANK, -1).astype(BF16)
    wqr = jnp.concatenate([zq(MLA_NOPE), rope_rot, zq(pad)], axis=-1).reshape(MLA_Q_RANK, -1).astype(BF16)
    ukv = w_ukv[i].reshape(MLA_KV_RANK, MLA_HEADS, MLA_NOPE + MLA_V)
    zkv = jnp.zeros((MLA_KV_RANK, MLA_HEADS, MLA_SLAB - MLA_NOPE), F32)
    wk = jnp.concatenate([ukv[..., :MLA_NOPE], zkv], axis=-1).reshape(MLA_KV_RANK, -1).astype(BF16)
    wv = jnp.concatenate([ukv[..., MLA_NOPE:], zkv], axis=-1).reshape(MLA_KV_RANK, -1).astype(BF16)
    return wlat, wdil, wgate, wq, wqr, wk, wv


def kernel(x, p, positions, attn_norm, w_in, q_norm, w_uq, kv_norm, w_ukv, w_br_mla, w_br_dil, w_out, ffn_norm, dense_w_gate, dense_w_up, dense_w_down, router_w, router_b, moe_w_gate, moe_w_up, moe_w_down, ple_norm, ple_w_gate, ple_w_proj, final_norm):
    batch, seq, d = x.shape
    t = batch * seq
    depth = w_in.shape[0]
    xf = x.reshape(t, d)
    cos_t, sin_t = _rope_tables(positions)
    slopes = _alibi_slopes(DIL_HEADS)
    row = lambda v: v.reshape(1, -1)
    h = _norm(xf, attn_norm[0])
    for i in range(depth):
        wlat, wdil, wgate, wq, wqr, wk, wv = _layer_weights(i, w_in, w_uq, w_ukv)
        q, k, v = _mla_proj(h, wlat, wq, wqr, wk, wv, row(q_norm[i]), row(kv_norm[i]), cos_t, sin_t)
        o_mla = _mla_attn(q, k, v, batch, seq)
        qkv_d = _dil_proj(h, wdil, seq)
        o_dil = _dil_attn(qkv_d, slopes, batch, seq)
        moe_layer = i % 2 == 1
        xf, h2 = _merge(xf, h, o_mla, o_dil, wgate, w_br_mla[i].astype(BF16), w_br_dil[i].astype(BF16),
                        w_out[i].astype(BF16), row(ffn_norm[i]), tiles=moe_layer)
        j = i // 2
        moe = None
        if moe_layer:
            moe = _moe(h2, router_w[j], router_b[j], moe_w_gate[j].astype(BF16), moe_w_up[j].astype(BF16),
                       moe_w_down[j].astype(BF16))
        else:
            xf = _ffn(xf, h2, dense_w_gate[j].astype(BF16), dense_w_up[j].astype(BF16),
                      dense_w_down[j].astype(BF16))
        last = i == depth - 1
        g_next = final_norm if last else attn_norm[i + 1]
        xf, h = _ple(xf, p[i].reshape(t, -1), ple_w_gate[i].astype(BF16), ple_w_proj[i].astype(BF16),
                     row(ple_norm[i]), row(g_next), F32 if last else BF16, moe)
    return h.reshape(batch, seq, d)
```

```python
import functools

import numpy as np
import jax
import jax.numpy as jnp
from jax import lax
from jax.experimental import pallas as pl
from jax.experimental.pallas import tpu as pltpu

F32 = jnp.float32
BF16 = jnp.bfloat16

D_MODEL = 1024
P_DIM = 256
NORM_EPS = 1e-6
DEPTH = 2

MLA_HEADS = 8
MLA_Q_RANK = 384
MLA_KV_RANK = 256
MLA_NOPE = 64
MLA_ROPE = 32
MLA_V = 64
ROPE_THETA = 10000.0
MLA_SLAB = 128

DIL_CONFIGS = ((128, 1), (512, 4), (2048, 16))
DIL_GROUPS = 3
DIL_HPG = 8
DIL_HEADS = 24
DIL_HD = 64
DIL_WIDTH = DIL_HEADS * DIL_HD
DIL_SPAN = 128

D_FF = 3584
N_EXPERTS = 8
TOP_K = 2

LANES = 128
ROW_TILES = D_MODEL // LANES
NEG = -1e30
LOG2E = 1.4426950408889634
VMEM_LIMIT = 56 * 1024 * 1024

IN_OFF = tuple(int(o) for o in np.cumsum((0, MLA_Q_RANK, MLA_KV_RANK, MLA_ROPE, DIL_WIDTH, DIL_WIDTH,
                                          DIL_WIDTH, D_MODEL, D_MODEL)))


def _cparams(sem):
    return pltpu.CompilerParams(dimension_semantics=sem, vmem_limit_bytes=VMEM_LIMIT)


def _rms(x, g):
    return x * lax.rsqrt(jnp.mean(x * x, axis=-1, keepdims=True) + NORM_EPS) * g


def _sigmoid(x):
    return 1.0 / (1.0 + jnp.exp(-x))


def _alibi_slopes(n):
    def pow2(m):
        start = 2.0 ** (-8.0 / m)
        return [start ** (i + 1) for i in range(m)]
    if float(np.log2(n)).is_integer():
        s = pow2(n)
    else:
        c = 2 ** int(np.floor(np.log2(n)))
        s = pow2(c) + pow2(2 * c)[0::2][: n - c]
    return jnp.asarray(sorted(s, reverse=True), dtype=F32)


def _norm_kernel(x_ref, g_ref, o_ref):
    o_ref[...] = _rms(x_ref[...], g_ref[...]).astype(o_ref.dtype)


def _norm(x, g, tm=1024):
    t, d = x.shape
    return pl.pallas_call(
        _norm_kernel,
        grid=(t // tm,),
        in_specs=[pl.BlockSpec((tm, d), lambda i: (i, 0)), pl.BlockSpec((1, d), lambda i: (0, 0))],
        out_specs=pl.BlockSpec((tm, d), lambda i: (i, 0)),
        out_shape=jax.ShapeDtypeStruct((t, d), BF16),
        compiler_params=_cparams(("parallel",)),
        name="rms_norm",
    )(x, g.reshape(1, d))


def _rope_table_kernel(pos_ref, invf_ref, cos_ref, sin_ref):
    ang = pos_ref[...].astype(F32) * invf_ref[...]
    lane = lax.broadcasted_iota(jnp.int32, ang.shape, 1)
    rope_lane = (lane >= MLA_NOPE) & (lane < MLA_NOPE + MLA_ROPE)
    cos_ref[...] = jnp.where(lane < MLA_NOPE, 1.0, jnp.where(rope_lane, jnp.cos(ang), 0.0))
    sin_ref[...] = jnp.where(rope_lane, jnp.sin(ang), 0.0)


def _rope_tables(positions, tm=2048):
    t = positions.size
    half = MLA_ROPE // 2
    inv_freq = ROPE_THETA ** (-jnp.arange(half, dtype=F32) / half)
    invf = jnp.zeros((1, LANES), F32).at[0, MLA_NOPE:MLA_NOPE + MLA_ROPE].set(jnp.concatenate([inv_freq, inv_freq]))
    pos_b = jnp.broadcast_to(positions.reshape(t, 1), (t, LANES))
    spec = pl.BlockSpec((tm, LANES), lambda i: (i, 0))
    return pl.pallas_call(
        _rope_table_kernel,
        grid=(t // tm,),
        in_specs=[spec, pl.BlockSpec((1, LANES), lambda i: (0, 0))],
        out_specs=[spec, spec],
        out_shape=[jax.ShapeDtypeStruct((t, LANES), F32)] * 2,
        compiler_params=_cparams(("parallel",)),
        name="rope_tables",
    )(pos_b, invf)


def _mla_proj_kernel(h_ref, wlat_ref, wq_ref, wqr_ref, wk_ref, wv_ref, qn_ref, kvn_ref, cos_ref, sin_ref,
                     q_out, k_out, v_out):
    lat = jnp.dot(h_ref[...], wlat_ref[...], preferred_element_type=F32)
    cqn = _rms(lat[:, :MLA_Q_RANK], qn_ref[...]).astype(BF16)
    ckvn = _rms(lat[:, MLA_Q_RANK:MLA_Q_RANK + MLA_KV_RANK], kvn_ref[...]).astype(BF16)
    cos = cos_ref[...]
    sin = sin_ref[...]
    o = MLA_Q_RANK + MLA_KV_RANK
    k_rope = lat[:, o:o + LANES] * cos + lat[:, o + LANES:o + 2 * LANES] * sin
    qa = jnp.dot(cqn, wq_ref[...], preferred_element_type=F32)
    qb = jnp.dot(cqn, wqr_ref[...], preferred_element_type=F32)
    kk = jnp.dot(ckvn, wk_ref[...], preferred_element_type=F32)
    scale = (MLA_NOPE + MLA_ROPE) ** -0.5 * LOG2E
    vv = jnp.dot(ckvn, wv_ref[...], preferred_element_type=F32)
    ones_lane = lax.broadcasted_iota(jnp.int32, cos.shape, 1) >= MLA_V
    for hd in range(MLA_HEADS):
        sl = slice(hd * MLA_SLAB, (hd + 1) * MLA_SLAB)
        q_out[:, sl] = ((qa[:, sl] * cos + qb[:, sl] * sin) * scale).astype(BF16)
        k_out[:, sl] = (kk[:, sl] + k_rope).astype(BF16)
        v_out[:, sl] = jnp.where(ones_lane, 1.0, vv[:, sl]).astype(BF16)


def _mla_proj(h, wlat, wq, wqr, wk, wv, qn, kvn, cos_t, sin_t, tm=512):
    t = h.shape[0]
    row = lambda w: pl.BlockSpec((tm, w), lambda i: (i, 0))
    full = lambda a: pl.BlockSpec(a.shape, lambda i: (0, 0))
    hs = MLA_HEADS * MLA_SLAB
    return pl.pallas_call(
        _mla_proj_kernel,
        grid=(t // tm,),
        in_specs=[row(D_MODEL), full(wlat), full(wq), full(wqr), full(wk), full(wv), full(qn), full(kvn),
                  row(LANES), row(LANES)],
        out_specs=[row(hs), row(hs), row(hs)],
        out_shape=[jax.ShapeDtypeStruct((t, hs), BF16)] * 3,
        compiler_params=_cparams(("parallel",)),
        name="mla_proj",
    )(h, wlat, wq, wqr, wk, wv, qn, kvn, cos_t, sin_t)


MLA_ROW_SPLIT = 2


def _mla_attn_kernel(q_ref, k_ref, v_ref, o_ref, s_scr, *, tq):
    qi = pl.program_id(2)
    nh = q_ref.shape[1] // MLA_SLAB
    tr = tq // MLA_ROW_SPLIT
    rq = lax.broadcasted_iota(jnp.int32, (tr, tq), 0)
    ck = lax.broadcasted_iota(jnp.int32, (tr, tq), 1)
    nl = tq // LANES

    def tile(n):
        def pass1(hh, rg):
            rows = slice(rg * tr, (rg + 1) * tr)
            q = q_ref[rows, hh * MLA_SLAB:(hh + 1) * MLA_SLAB]
            mx = None
            for c in range(n + 1):
                ks = k_ref[c * tq:(c + 1) * tq, hh * MLA_SLAB:(hh + 1) * MLA_SLAB]
                s = lax.dot_general(q, ks, (((1,), (1,)), ((), ())), preferred_element_type=F32)
                if c == n:
                    s = jnp.where(ck <= rq + rg * tr, s, NEG)
                s_scr[hh, c, rows, :] = s
                parts = [s[:, i * LANES:(i + 1) * LANES] for i in range(nl)]
                mx = functools.reduce(jnp.maximum, parts if mx is None else [mx] + parts)
            return jnp.max(mx, axis=-1, keepdims=True)

        def pass2(hh, rg, m):
            rows = slice(rg * tr, (rg + 1) * tr)
            acc = None
            for c in range(n + 1):
                vs = v_ref[c * tq:(c + 1) * tq, hh * MLA_SLAB:(hh + 1) * MLA_SLAB]
                p = jnp.exp2((s_scr[hh, c, rows, :] - m).astype(BF16))
                pv = jnp.dot(p, vs, preferred_element_type=F32)
                acc = pv if acc is None else acc + pv
            o = acc / acc[:, MLA_V:MLA_V + 1]
            o_ref[rows, hh * MLA_V:(hh + 1) * MLA_V] = o[:, :MLA_V].astype(o_ref.dtype)

        chains = [(hh, rg) for hh in range(nh) for rg in range(MLA_ROW_SPLIT)]
        ms = [pass1(*chains[0])]
        for i, chain in enumerate(chains):
            if i + 1 < len(chains):
                ms.append(pass1(*chains[i + 1]))
            pass2(*chain, ms[i])

    for n in range(s_scr.shape[1]):
        pl.when(qi == n)(functools.partial(tile, n))


def _mla_attn(q, k, v, batch, seq, tq=512, nh=4):
    t = q.shape[0]
    nq = seq // tq
    return pl.pallas_call(
        functools.partial(_mla_attn_kernel, tq=tq),
        grid=(batch, MLA_HEADS // nh, nq),
        in_specs=[pl.BlockSpec((tq, nh * MLA_SLAB), lambda b, p, i: (b * nq + i, p)),
                  pl.BlockSpec((seq, nh * MLA_SLAB), lambda b, p, i: (b, p)),
                  pl.BlockSpec((seq, nh * MLA_SLAB), lambda b, p, i: (b, p))],
        out_specs=pl.BlockSpec((tq, nh * MLA_V), lambda b, p, i: (b * nq + i, p)),
        out_shape=jax.ShapeDtypeStruct((t, MLA_HEADS * MLA_V), BF16),
        scratch_shapes=[pltpu.VMEM((nh, nq, tq, tq), F32)],
        compiler_params=_cparams(("parallel", "parallel", "arbitrary")),
        name="mla_attn",
    )(q, k, v)


PHASE_STRIDE = 4


def _phase_of_slot(r):
    if r <= PHASE_STRIDE:
        return list(range(r))
    f2 = r // PHASE_STRIDE
    return [c1 + PHASE_STRIDE * c2 for c1 in range(PHASE_STRIDE) for c2 in range(f2)]


def _dil_proj_kernel(h_ref, w_ref, o_ref, acc_ref, tmp_ref, *, seq):
    res = jnp.dot(h_ref[...], w_ref[...], preferred_element_type=F32)
    g = pl.program_id(0) % DIL_GROUPS
    nl = o_ref.shape[0]

    @pl.when(g == 0)
    def _():
        for j in range(nl):
            o_ref[j] = res[:, j * LANES:(j + 1) * LANES].astype(o_ref.dtype)

    @pl.when(g > 0)
    def _():
        for j in range(nl):
            acc_ref[j] = res[:, j * LANES:(j + 1) * LANES]

    for gi, (_, r) in enumerate(DIL_CONFIGS):
        if r == 1:
            continue

        @pl.when(g == gi)
        def _(r=r):
            f1 = min(r, PHASE_STRIDE)
            l1 = seq // f1
            dst = o_ref if r == f1 else tmp_ref
            for c1 in range(f1):
                for j in range(nl):
                    dst[j, c1 * l1:(c1 + 1) * l1, :] = acc_ref[j, pl.ds(c1, l1, stride=f1), :].astype(dst.dtype)
            if r > f1:
                f2 = r // f1
                l2 = l1 // f2
                for s in range(r):
                    c1, c2 = divmod(s, f2)
                    for j in range(nl):
                        o_ref[j, s * l2:(s + 1) * l2, :] = (
                            tmp_ref[j, pl.ds(c1 * l1 + c2, l2, stride=f2), :].astype(o_ref.dtype))


def _dil_proj(h, w, seq, tn=512):
    t = h.shape[0]
    n = w.shape[1]
    nl = tn // LANES
    return pl.pallas_call(
        functools.partial(_dil_proj_kernel, seq=seq),
        grid=(n // tn, t // seq),
        in_specs=[pl.BlockSpec((seq, D_MODEL), lambda j, b: (b, 0)),
                  pl.BlockSpec((D_MODEL, tn), lambda j, b: (0, j))],
        out_specs=pl.BlockSpec((nl, seq, LANES), lambda j, b: (j, b, 0)),
        out_shape=jax.ShapeDtypeStruct((n // LANES, t, LANES), BF16),
        scratch_shapes=[pltpu.VMEM((nl, seq, LANES), F32)] * 2,
        compiler_params=_cparams(("parallel", "parallel")),
        name="dil_proj",
    )(h, w)


def _dil_attn_kernel(slopes_ref, q0, q1, q2, k0, k1, k2, v0, v1, v2, o_ref, acc_s, m_s, l_s, *, seq):
    pair = pl.program_id(1)
    sp = DIL_SPAN
    head0 = lax.broadcasted_iota(jnp.int32, (sp, LANES), 1) < DIL_HD
    qi = lax.broadcasted_iota(jnp.int32, (sp, 2 * sp), 0)
    kj = lax.broadcasted_iota(jnp.int32, (sp, 2 * sp), 1)
    dist_w = qi + sp - kj
    valid_w = (dist_w >= 0) & (dist_w <= sp)
    dist_1 = dist_w[:, sp:]
    valid_1 = dist_1 >= 0
    qs, ks, vs = (q0, q1, q2), (k0, k1, k2), (v0, v1, v2)
    for g, (_, r) in enumerate(DIL_CONFIGS):
        ln = seq // r
        nb = ln // sp
        sl = [slopes_ref[g * DIL_HPG + 2 * pair + hh] * float(r) for hh in range(2)]
        bias_w = jnp.concatenate([jnp.where(valid_w, -s * dist_w.astype(F32), NEG) for s in sl], axis=0)
        bias_1 = jnp.concatenate([jnp.where(valid_1, -s * dist_1.astype(F32), NEG) for s in sl], axis=0)
        for slot_i, c in enumerate(_phase_of_slot(r)):
            for i in range(nb):
                row0 = slot_i * ln + i * sp
                qb = qs[g][0, row0:row0 + sp, :]
                zero = jnp.zeros_like(qb)
                q2h = jnp.concatenate([jnp.where(head0, qb, zero), jnp.where(head0, zero, qb)], axis=0)
                lo = row0 if i == 0 else row0 - sp
                kw = ks[g][0, lo:row0 + sp, :]
                vw = vs[g][0, lo:row0 + sp, :]
                s = lax.dot_general(q2h, kw, (((1,), (1,)), ((), ())), preferred_element_type=F32)
                s = s + (bias_1 if i == 0 else bias_w)
                m = jnp.max(s, axis=-1, keepdims=True)
                p = jnp.exp(s - m)
                l = jnp.sum(p, axis=-1, keepdims=True)
                o2 = jnp.dot(p.astype(BF16), vw, preferred_element_type=F32)
                dst = pl.ds(i * sp * r + c, sp, stride=r) if r > 1 else pl.ds(i * sp, sp)
                acc_s[g, dst, :] = jnp.where(head0, o2[:sp], o2[sp:])
                m_s[g, dst, :] = jnp.where(head0, m[:sp], m[sp:])
                l_s[g, dst, :] = jnp.where(head0, l[:sp], l[sp:])

    ch = 256

    def combine(i, _):
        rows = pl.ds(pl.multiple_of(i * ch, ch), ch)
        ms = [m_s[g, rows, :] for g in range(DIL_GROUPS)]
        mx = jnp.maximum(jnp.maximum(ms[0], ms[1]), ms[2])
        ws = [jnp.exp(m - mx) for m in ms]
        num = sum(ws[g] * acc_s[g, rows, :] for g in range(DIL_GROUPS))
        den = sum(ws[g] * l_s[g, rows, :] for g in range(DIL_GROUPS))
        o_ref[rows, :] = (num / den).astype(o_ref.dtype)
        return 0

    lax.fori_loop(0, seq // ch, combine, 0)


def _dil_attn(qkv, slopes, batch, seq):
    t = qkv.shape[1]
    pairs = DIL_HPG // 2
    cpg = DIL_HPG * DIL_HD // LANES
    cpt = DIL_WIDTH // LANES

    def spec(tt, g):
        return pl.BlockSpec((1, seq, LANES), lambda b, p: (tt * cpt + g * cpg + p, b, 0))

    in_specs = [pl.BlockSpec(memory_space=pltpu.SMEM)] + [spec(tt, g) for tt in range(3) for g in range(DIL_GROUPS)]
    return pl.pallas_call(
        functools.partial(_dil_attn_kernel, seq=seq),
        grid=(batch, pairs),
        in_specs=in_specs,
        out_specs=pl.BlockSpec((seq, LANES), lambda b, p: (b, p)),
        out_shape=jax.ShapeDtypeStruct((t, DIL_HPG * DIL_HD), BF16),
        scratch_shapes=[pltpu.VMEM((DIL_GROUPS, seq, LANES), F32)] * 3,
        compiler_params=_cparams(("parallel", "parallel")),
        name="dil_attn",
    )(slopes, *([qkv] * 9))


def _to_token_tiles(dst_ref, val):
    n = val.shape[0]
    for s in range(ROW_TILES):
        dst_ref[pl.ds(s, n, stride=ROW_TILES), :] = val[:, s * LANES:(s + 1) * LANES]


def _from_token_tiles(src_ref, n):
    return jnp.concatenate([src_ref[pl.ds(s, n, stride=ROW_TILES), :] for s in range(ROW_TILES)], axis=1)


def _top2(logits):
    lane = lax.broadcasted_iota(jnp.int32, logits.shape, 1)
    m1 = jnp.max(logits, axis=-1, keepdims=True)
    i1 = jnp.min(jnp.where(logits == m1, lane, LANES), axis=-1, keepdims=True)
    rest = jnp.where(lane == i1, NEG, logits)
    m2 = jnp.max(rest, axis=-1, keepdims=True)
    i2 = jnp.min(jnp.where(rest == m2, lane, LANES), axis=-1, keepdims=True)
    e = jnp.exp(m2 - m1)
    w1 = 1.0 / (1.0 + e)
    return (jnp.where(lane == 0, i1, jnp.where(lane == 1, i2, 0)),
            jnp.where(lane == 0, w1, jnp.where(lane == 1, e * w1, 0.0)))


def _merge_kernel(x_ref, h_ref, om_ref, od_ref, wg_ref, wbm_ref, wbd_ref, wo_ref, g_ref, *rest, route):
    gates = jnp.dot(h_ref[...], wg_ref[...], preferred_element_type=F32)
    bm = jnp.dot(om_ref[...], wbm_ref[...], preferred_element_type=F32)
    bd = jnp.dot(od_ref[...], wbd_ref[...], preferred_element_type=F32)
    merged = _sigmoid(gates[:, :D_MODEL]) * bm + _sigmoid(gates[:, D_MODEL:]) * bd
    x1 = x_ref[...] + jnp.dot(merged.astype(BF16), wo_ref[...], preferred_element_type=F32)
    hn = _rms(x1, g_ref[...])
    if route:
        wr_ref, br_ref, x_out, h_out, idx_out, wt_out = rest
        _to_token_tiles(h_out, hn)
        hi = hn.astype(BF16)
        lo = (hn - hi.astype(F32)).astype(BF16)
        both = jnp.dot(hi, wr_ref[...], preferred_element_type=F32)
        logits = (both[:, :LANES] + both[:, LANES:]
                  + jnp.dot(lo, wr_ref[:, :LANES], preferred_element_type=F32) + br_ref[...])
        idx_out[...], wt_out[...] = _top2(logits)
    else:
        x_out, h_out = rest
        h_out[...] = hn.astype(h_out.dtype)
    x_out[...] = x1


def _merge(x, h, o_mla, o_dil, wg, wbm, wbd, wo, g_next, router=None, tm=512):
    t = x.shape[0]
    row = lambda w: pl.BlockSpec((tm, w), lambda i: (i, 0))
    full = lambda a: pl.BlockSpec(a.shape, lambda i: (0, 0))
    in_specs = [row(D_MODEL), row(D_MODEL), row(o_mla.shape[1]), row(o_dil.shape[1]),
                full(wg), full(wbm), full(wbd), full(wo), full(g_next)]
    args = [x, h, o_mla, o_dil, wg, wbm, wbd, wo, g_next]
    out_specs = [row(D_MODEL)]
    out_shape = [jax.ShapeDtypeStruct((t, D_MODEL), F32)]
    if router is None:
        out_specs += [row(D_MODEL)]
        out_shape += [jax.ShapeDtypeStruct((t, D_MODEL), BF16)]
    else:
        in_specs += [full(router[0]), full(router[1])]
        args += list(router)
        out_specs += [pl.BlockSpec((tm * ROW_TILES, LANES), lambda i: (i, 0)), row(LANES), row(LANES)]
        out_shape += [jax.ShapeDtypeStruct((t * ROW_TILES, LANES), F32),
                      jax.ShapeDtypeStruct((t, LANES), jnp.int32), jax.ShapeDtypeStruct((t, LANES), F32)]
    return pl.pallas_call(
        functools.partial(_merge_kernel, route=router is not None),
        grid=(t // tm,),
        in_specs=in_specs,
        out_specs=out_specs,
        out_shape=out_shape,
        compiler_params=_cparams(("parallel",)),
        name="merge",
    )(*args)


def _ffn_kernel(x_ref, h_ref, wg_ref, wu_ref, wd_ref, o_ref):
    @pl.when(pl.program_id(1) == 0)
    def _():
        o_ref[...] = x_ref[...]

    h = h_ref[...]
    a = jnp.dot(h, wg_ref[...], preferred_element_type=F32)
    u = jnp.dot(h, wu_ref[...], preferred_element_type=F32)
    o_ref[...] += jnp.dot((a * _sigmoid(a) * u).astype(BF16), wd_ref[...], preferred_element_type=F32)


def _ffn(x, h, wg, wu, wd, tm=512, tf=1792):
    t = x.shape[0]
    nf = wg.shape[1] // tf
    return pl.pallas_call(
        _ffn_kernel,
        grid=(t // tm, nf),
        in_specs=[pl.BlockSpec((tm, D_MODEL), lambda i, f: (i, 0)),
                  pl.BlockSpec((tm, D_MODEL), lambda i, f: (i, 0)),
                  pl.BlockSpec((D_MODEL, tf), lambda i, f: (0, f)),
                  pl.BlockSpec((D_MODEL, tf), lambda i, f: (0, f)),
                  pl.BlockSpec((tf, D_MODEL), lambda i, f: (f, 0))],
        out_specs=pl.BlockSpec((tm, D_MODEL), lambda i, f: (i, 0)),
        out_shape=jax.ShapeDtypeStruct((t, D_MODEL), F32),
        compiler_params=_cparams(("parallel", "arbitrary")),
        name="dense_ffn",
    )(x, h, wg, wu, wd)


def _tile_at(ref, row8):
    return ref.at[pl.ds(pl.multiple_of(row8, ROW_TILES), ROW_TILES), :]


def _dispatch_kernel(pos_ref, fill_ref, h_ref, xs_hbm, zero_ref, sem, sem_fill, *, t, tmd, tm):
    i = pl.program_id(0)

    @pl.when(i == 0)
    def _():
        zero_ref[...] = jnp.zeros_like(zero_ref)
        for e in range(fill_ref.shape[0]):
            @pl.when(fill_ref[e] >= 0)
            def _():
                cp = pltpu.make_async_copy(
                    zero_ref, xs_hbm.at[pl.ds(pl.multiple_of(fill_ref[e], ROW_TILES), tm * ROW_TILES), :], sem_fill)
                cp.start()
                cp.wait()

    for k in range(TOP_K):
        for r in range(tmd):
            pltpu.make_async_copy(_tile_at(h_ref, r * ROW_TILES),
                                  _tile_at(xs_hbm, pos_ref[k * t + i * tmd + r]), sem.at[k]).start(priority=r % 2)
    for k in range(TOP_K):
        pltpu.make_async_copy(h_ref, h_ref, sem.at[k]).wait()


def _dispatch(h, pos8, fill8, n_rows, tm, tmd=256):
    t = h.shape[0] // ROW_TILES
    return pl.pallas_call(
        functools.partial(_dispatch_kernel, t=t, tmd=tmd, tm=tm),
        grid_spec=pltpu.PrefetchScalarGridSpec(
            num_scalar_prefetch=2,
            grid=(t // tmd,),
            in_specs=[pl.BlockSpec((tmd * ROW_TILES, LANES), lambda i, pos, fill: (i, 0))],
            out_specs=pl.BlockSpec(memory_space=pl.ANY),
            scratch_shapes=[pltpu.VMEM((tm * ROW_TILES, LANES), F32),
                            pltpu.SemaphoreType.DMA((TOP_K,)), pltpu.SemaphoreType.DMA(())]),
        out_shape=jax.ShapeDtypeStruct((n_rows * ROW_TILES, LANES), F32),
        compiler_params=_cparams(("arbitrary",)),
        name="moe_dispatch",
    )(pos8, fill8, h)


def _moe_ffn_kernel(te_ref, nv_ref, x_ref, wg_ref, wu_ref, wd_ref, o_ref, acc_ref, xb_ref, *, tm):
    i = pl.program_id(0)
    f = pl.program_id(1)
    valid = i < nv_ref[0]
    last = f == pl.num_programs(1) - 1

    @pl.when(valid & (f == 0))
    def _():
        xb_ref[...] = _from_token_tiles(x_ref, tm).astype(BF16)
        acc_ref[...] = jnp.zeros_like(acc_ref)

    @pl.when(valid)
    def _():
        h = xb_ref[...]
        a = jnp.dot(h, wg_ref[0], preferred_element_type=F32)
        u = jnp.dot(h, wu_ref[0], preferred_element_type=F32)
        acc_ref[...] += jnp.dot((a * _sigmoid(a) * u).astype(BF16), wd_ref[0], preferred_element_type=F32)

    @pl.when(valid & last)
    def _():
        _to_token_tiles(o_ref, acc_ref[...])

    @pl.when(jnp.logical_not(valid) & last)
    def _():
        o_ref[...] = jnp.zeros_like(o_ref)


def _moe_ffn(xs, tile_expert, n_valid, wg, wu, wd, tm, tf=1792):
    n_tiles = xs.shape[0] // (tm * ROW_TILES)
    nf = wg.shape[2] // tf

    def live(i, f, te, nv):
        ok = i < nv[0]
        return jnp.where(ok, i, nv[0] - 1), jnp.where(ok, f, nf - 1)

    return pl.pallas_call(
        functools.partial(_moe_ffn_kernel, tm=tm),
        grid_spec=pltpu.PrefetchScalarGridSpec(
            num_scalar_prefetch=2,
            grid=(n_tiles, nf),
            in_specs=[pl.BlockSpec((tm * ROW_TILES, LANES), lambda *a: (live(*a)[0], 0)),
                      pl.BlockSpec((1, D_MODEL, tf), lambda *a: (a[2][live(*a)[0]], 0, live(*a)[1])),
                      pl.BlockSpec((1, D_MODEL, tf), lambda *a: (a[2][live(*a)[0]], 0, live(*a)[1])),
                      pl.BlockSpec((1, tf, D_MODEL), lambda *a: (a[2][live(*a)[0]], live(*a)[1], 0))],
            out_specs=pl.BlockSpec((tm * ROW_TILES, LANES), lambda i, f, te, nv: (i, 0)),
            scratch_shapes=[pltpu.VMEM((tm, D_MODEL), F32), pltpu.VMEM((tm, D_MODEL), BF16)]),
        out_shape=jax.ShapeDtypeStruct(xs.shape, F32),
        compiler_params=_cparams(("arbitrary", "arbitrary")),
        name="moe_ffn",
    )(tile_expert, n_valid, xs, wg, wu, wd)


def _router_params(w_router, b_router):
    w_pad = jnp.zeros((D_MODEL, LANES), F32).at[:, :N_EXPERTS].set(w_router)
    w_hi = lax.reduce_precision(w_pad, exponent_bits=8, mantissa_bits=7)
    w_lo = w_pad - w_hi
    b_pad = jnp.full((1, LANES), NEG, F32).at[0, :N_EXPERTS].set(b_router)
    return jnp.concatenate([w_hi, w_lo], axis=1).astype(BF16), b_pad


def _moe(h, idx_l, wg, wu, wd, tm=448):
    t = h.shape[0] // ROW_TILES
    expert = idx_l[:, :TOP_K].T.reshape(-1)
    onehot = (expert[:, None] == jnp.arange(N_EXPERTS)[None, :]).astype(jnp.int32)
    csum = jnp.cumsum(onehot, axis=0)
    counts = csum[-1]
    rank = jnp.sum((csum - onehot) * onehot, axis=1)
    tiles = (counts + tm - 1) // tm
    tile_end = jnp.cumsum(tiles)
    pad_off = (tile_end - tiles) * tm
    pos8 = (pad_off[expert] + rank) * ROW_TILES
    n_tiles = (TOP_K * t) // tm + N_EXPERTS
    n_valid = tile_end[-1:].astype(jnp.int32)
    tile_ids = jnp.minimum(jnp.arange(n_tiles), n_valid[0] - 1)
    tile_expert = jnp.sum(tile_end[None, :] <= tile_ids[:, None], axis=1).astype(jnp.int32)
    spare = n_valid[0] + jnp.arange(N_EXPERTS)
    fill8 = jnp.concatenate([jnp.where(tiles > 0, tile_end - 1, -1), jnp.where(spare < n_tiles, spare, -1)])
    fill8 = jnp.where(fill8 >= 0, fill8 * (tm * ROW_TILES), -1).astype(jnp.int32)
    xs = _dispatch(h, pos8, fill8, n_tiles * tm, tm)
    ys = _moe_ffn(xs, tile_expert, n_valid, wg, wu, wd, tm)
    return ys, pos8


def _ple_body(x, p_ref, wg_ref, wp_ref, g_ref, gn_ref, x_out, h_out):
    gate = _sigmoid(jnp.dot(_rms(x, g_ref[...]).astype(BF16), wg_ref[...], preferred_element_type=F32))
    x2 = x + gate * jnp.dot(p_ref[...].astype(BF16), wp_ref[...], preferred_element_type=F32)
    x_out[...] = x2
    h_out[...] = _rms(x2, gn_ref[...]).astype(h_out.dtype)


def _ple_kernel(x_ref, p_ref, wg_ref, wp_ref, g_ref, gn_ref, x_out, h_out):
    _ple_body(x_ref[...], p_ref, wg_ref, wp_ref, g_ref, gn_ref, x_out, h_out)


def _ple_moe_kernel(pos_ref, x_ref, p_ref, wt_ref, wg_ref, wp_ref, g_ref, gn_ref, y_hbm, x_out, h_out,
                    ybuf, sem, *, t, tm):
    i = pl.program_id(0)
    slot = i % 2

    def gather(block, s):
        for k in range(TOP_K):
            for r in range(tm):
                pltpu.make_async_copy(_tile_at(y_hbm, pos_ref[k * t + block * tm + r]),
                                      _tile_at(ybuf.at[s, k], r * ROW_TILES), sem.at[s]).start(priority=r % 2)

    @pl.when(i == 0)
    def _():
        gather(0, 0)

    @pl.when(i + 1 < pl.num_programs(0))
    def _():
        gather(i + 1, 1 - slot)

    pltpu.make_async_copy(ybuf.at[slot], ybuf.at[slot], sem.at[slot]).wait()
    wt = wt_ref[...]
    x = (x_ref[...] + wt[:, 0:1] * _from_token_tiles(ybuf.at[slot, 0], tm)
         + wt[:, 1:2] * _from_token_tiles(ybuf.at[slot, 1], tm))
    _ple_body(x, p_ref, wg_ref, wp_ref, g_ref, gn_ref, x_out, h_out)


def _ple(x, p, wg, wp, g, g_next, h_dtype, moe=None, tm=512):
    t = x.shape[0]
    out_shape = [jax.ShapeDtypeStruct((t, D_MODEL), F32), jax.ShapeDtypeStruct((t, D_MODEL), h_dtype)]
    if moe is None:
        row = lambda w: pl.BlockSpec((tm, w), lambda i: (i, 0))
        full = lambda a: pl.BlockSpec(a.shape, lambda i: (0, 0))
        return pl.pallas_call(
            _ple_kernel,
            grid=(t // tm,),
            in_specs=[row(D_MODEL), row(P_DIM), full(wg), full(wp), full(g), full(g_next)],
            out_specs=[row(D_MODEL), row(D_MODEL)],
            out_shape=out_shape,
            compiler_params=_cparams(("parallel",)),
            name="ple",
        )(x, p, wg, wp, g, g_next)
    ys, pos8, wt = moe
    tm = 256
    row = lambda w: pl.BlockSpec((tm, w), lambda i, pos: (i, 0))
    full = lambda a: pl.BlockSpec(a.shape, lambda i, pos: (0, 0))
    return pl.pallas_call(
        functools.partial(_ple_moe_kernel, t=t, tm=tm),
        grid_spec=pltpu.PrefetchScalarGridSpec(
            num_scalar_prefetch=1,
            grid=(t // tm,),
            in_specs=[row(D_MODEL), row(P_DIM), row(LANES), full(wg), full(wp), full(g), full(g_next),
                      pl.BlockSpec(memory_space=pl.ANY)],
            out_specs=[row(D_MODEL), row(D_MODEL)],
            scratch_shapes=[pltpu.VMEM((2, TOP_K, tm * ROW_TILES, LANES), F32), pltpu.SemaphoreType.DMA((2,))]),
        out_shape=out_shape,
        compiler_params=_cparams(("arbitrary",)),
        name="ple_moe",
    )(pos8, x, p, wt, wg, wp, g, g_next, ys)


def _rot_cols(w):
    half = w.shape[-1] // 2
    return jnp.concatenate([-w[:, half:], w[:, :half]], axis=-1)


def _layer_weights(i, w_in, w_uq, w_ukv):
    wi = w_in[i]
    z = lambda n: jnp.zeros((wi.shape[0], n), F32)
    w_kr = wi[:, IN_OFF[2]:IN_OFF[3]]
    pad = MLA_SLAB - MLA_NOPE - MLA_ROPE
    wlat = jnp.concatenate([wi[:, :IN_OFF[2]], z(MLA_NOPE), w_kr, z(pad), z(MLA_NOPE), _rot_cols(w_kr), z(pad)],
                           axis=1).astype(BF16)
    qscale = jnp.concatenate([jnp.full((DIL_WIDTH,), DIL_HD ** -0.5, F32), jnp.ones((2 * DIL_WIDTH,), F32)])
    wdil = (wi[:, IN_OFF[3]:IN_OFF[6]] * qscale).astype(BF16)
    wgate = wi[:, IN_OFF[6]:].astype(BF16)
    uq = w_uq[i].reshape(MLA_Q_RANK, MLA_HEADS, MLA_NOPE + MLA_ROPE)
    zq = lambda n: jnp.zeros((MLA_Q_RANK, MLA_HEADS, n), F32)
    rope_rot = jnp.concatenate([-uq[..., MLA_NOPE + MLA_ROPE // 2:], uq[..., MLA_NOPE:MLA_NOPE + MLA_ROPE // 2]], -1)
    wq = jnp.concatenate([uq, zq(pad)], axis=-1).reshape(MLA_Q_RANK, -1).astype(BF16)
    wqr = jnp.concatenate([zq(MLA_NOPE), rope_rot, zq(pad)], axis=-1).reshape(MLA_Q_RANK, -1).astype(BF16)
    ukv = w_ukv[i].reshape(MLA_KV_RANK, MLA_HEADS, MLA_NOPE + MLA_V)
    zkv = jnp.zeros((MLA_KV_RANK, MLA_HEADS, MLA_SLAB - MLA_NOPE), F32)
    wk = jnp.concatenate([ukv[..., :MLA_NOPE], zkv], axis=-1).reshape(MLA_KV_RANK, -1).astype(BF16)
    wv = jnp.concatenate([ukv[..., MLA_NOPE:], zkv], axis=-1).reshape(MLA_KV_RANK, -1).astype(BF16)
    return wlat, wdil, wgate, wq, wqr, wk, wv


def kernel(x, p, positions, attn_norm, w_in, q_norm, w_uq, kv_norm, w_ukv, w_br_mla, w_br_dil, w_out, ffn_norm, dense_w_gate, dense_w_up, dense_w_down, router_w, router_b, moe_w_gate, moe_w_up, moe_w_down, ple_norm, ple_w_gate, ple_w_proj, final_norm):
    batch, seq, d = x.shape
    t = batch * seq
    depth = w_in.shape[0]
    xf = x.reshape(t, d)
    cos_t, sin_t = _rope_tables(positions)
    slopes = _alibi_slopes(DIL_HEADS)
    row = lambda v: v.reshape(1, -1)
    h = _norm(xf, attn_norm[0])
    for i in range(depth):
        wlat, wdil, wgate, wq, wqr, wk, wv = _layer_weights(i, w_in, w_uq, w_ukv)
        q, k, v = _mla_proj(h, wlat, wq, wqr, wk, wv, row(q_norm[i]), row(kv_norm[i]), cos_t, sin_t)
        o_mla = _mla_attn(q, k, v, batch, seq)
        qkv_d = _dil_proj(h, wdil, seq)
        o_dil = _dil_attn(qkv_d, slopes, batch, seq)
        moe_layer = i % 2 == 1
        j = i // 2
        xf, h2, *routing = _merge(xf, h, o_mla, o_dil, wgate, w_br_mla[i].astype(BF16), w_br_dil[i].astype(BF16),
                                  w_out[i].astype(BF16), row(ffn_norm[i]),
                                  _router_params(router_w[j], router_b[j]) if moe_layer else None)
        moe = None
        if moe_layer:
            idx_l, wt_l = routing
            moe = _moe(h2, idx_l, moe_w_gate[j].astype(BF16), moe_w_up[j].astype(BF16),
                       moe_w_down[j].astype(BF16)) + (wt_l,)
        else:
            xf = _ffn(xf, h2, dense_w_gate[j].astype(BF16), dense_w_up[j].astype(BF16),
                      dense_w_down[j].astype(BF16))
        last = i == depth - 1
        g_next = final_norm if last else attn_norm[i + 1]
        xf, h = _ple(xf, p[i].reshape(t, -1), ple_w_gate[i].astype(BF16), ple_w_proj[i].astype(BF16),
                     row(ple_norm[i]), row(g_next), F32 if last else BF16, moe)
    return h.reshape(batch, seq, d)
```

```python
import functools

import numpy as np
import jax
import jax.numpy as jnp
from jax import lax
from jax.experimental import pallas as pl
from jax.experimental.pallas import tpu as pltpu

F32 = jnp.float32
BF16 = jnp.bfloat16

D_MODEL = 1024
P_DIM = 256
NORM_EPS = 1e-6
DEPTH = 2

MLA_HEADS = 8
MLA_Q_RANK = 384
MLA_KV_RANK = 256
MLA_NOPE = 64
MLA_ROPE = 32
MLA_V = 64
ROPE_THETA = 10000.0
MLA_SLAB = 128

DIL_CONFIGS = ((128, 1), (512, 4), (2048, 16))
DIL_GROUPS = 3
DIL_HPG = 8
DIL_HEADS = 24
DIL_HD = 64
DIL_WIDTH = DIL_HEADS * DIL_HD
DIL_SPAN = 128

D_FF = 3584
N_EXPERTS = 8
TOP_K = 2

LANES = 128
ROW_TILES = D_MODEL // LANES
NEG = -1e30
LOG2E = 1.4426950408889634
VMEM_LIMIT = 56 * 1024 * 1024

IN_OFF = tuple(int(o) for o in np.cumsum((0, MLA_Q_RANK, MLA_KV_RANK, MLA_ROPE, DIL_WIDTH, DIL_WIDTH,
                                          DIL_WIDTH, D_MODEL, D_MODEL)))


def _cparams(sem):
    return pltpu.CompilerParams(dimension_semantics=sem, vmem_limit_bytes=VMEM_LIMIT)


def _rms(x, g):
    return x * lax.rsqrt(jnp.mean(x * x, axis=-1, keepdims=True) + NORM_EPS) * g


def _sigmoid(x):
    return 1.0 / (1.0 + jnp.exp(-x))


def _alibi_slopes(n):
    def pow2(m):
        start = 2.0 ** (-8.0 / m)
        return [start ** (i + 1) for i in range(m)]
    if float(np.log2(n)).is_integer():
        s = pow2(n)
    else:
        c = 2 ** int(np.floor(np.log2(n)))
        s = pow2(c) + pow2(2 * c)[0::2][: n - c]
    return jnp.asarray(sorted(s, reverse=True), dtype=F32)


def _norm_kernel(x_ref, g_ref, o_ref):
    o_ref[...] = _rms(x_ref[...], g_ref[...]).astype(o_ref.dtype)


def _norm(x, g, tm=1024):
    t, d = x.shape
    return pl.pallas_call(
        _norm_kernel,
        grid=(t // tm,),
        in_specs=[pl.BlockSpec((tm, d), lambda i: (i, 0)), pl.BlockSpec((1, d), lambda i: (0, 0))],
        out_specs=pl.BlockSpec((tm, d), lambda i: (i, 0)),
        out_shape=jax.ShapeDtypeStruct((t, d), BF16),
        compiler_params=_cparams(("parallel",)),
        name="rms_norm",
    )(x, g.reshape(1, d))


def _rope_table_kernel(pos_ref, invf_ref, cos_ref, sin_ref):
    ang = pos_ref[...].astype(F32) * invf_ref[...]
    lane = lax.broadcasted_iota(jnp.int32, ang.shape, 1)
    rope_lane = (lane >= MLA_NOPE) & (lane < MLA_NOPE + MLA_ROPE)
    cos_ref[...] = jnp.where(lane < MLA_NOPE, 1.0, jnp.where(rope_lane, jnp.cos(ang), 0.0))
    sin_ref[...] = jnp.where(rope_lane, jnp.sin(ang), 0.0)


def _rope_tables(positions, tm=2048):
    t = positions.size
    half = MLA_ROPE // 2
    inv_freq = ROPE_THETA ** (-jnp.arange(half, dtype=F32) / half)
    invf = jnp.zeros((1, LANES), F32).at[0, MLA_NOPE:MLA_NOPE + MLA_ROPE].set(jnp.concatenate([inv_freq, inv_freq]))
    pos_b = jnp.broadcast_to(positions.reshape(t, 1), (t, LANES))
    spec = pl.BlockSpec((tm, LANES), lambda i: (i, 0))
    return pl.pallas_call(
        _rope_table_kernel,
        grid=(t // tm,),
        in_specs=[spec, pl.BlockSpec((1, LANES), lambda i: (0, 0))],
        out_specs=[spec, spec],
        out_shape=[jax.ShapeDtypeStruct((t, LANES), F32)] * 2,
        compiler_params=_cparams(("parallel",)),
        name="rope_tables",
    )(pos_b, invf)


def _mla_proj_kernel(h_ref, wlat_ref, wq_ref, wqr_ref, wk_ref, wv_ref, qn_ref, kvn_ref, cos_ref, sin_ref,
                     q_out, k_out, v_out):
    lat = jnp.dot(h_ref[...], wlat_ref[...], preferred_element_type=F32)
    cqn = _rms(lat[:, :MLA_Q_RANK], qn_ref[...]).astype(BF16)
    ckvn = _rms(lat[:, MLA_Q_RANK:MLA_Q_RANK + MLA_KV_RANK], kvn_ref[...]).astype(BF16)
    cos = cos_ref[...]
    sin = sin_ref[...]
    o = MLA_Q_RANK + MLA_KV_RANK
    k_rope = lat[:, o:o + LANES] * cos + lat[:, o + LANES:o + 2 * LANES] * sin
    qa = jnp.dot(cqn, wq_ref[...], preferred_element_type=F32)
    qb = jnp.dot(cqn, wqr_ref[...], preferred_element_type=F32)
    kk = jnp.dot(ckvn, wk_ref[...], preferred_element_type=F32)
    scale = (MLA_NOPE + MLA_ROPE) ** -0.5 * LOG2E
    vv = jnp.dot(ckvn, wv_ref[...], preferred_element_type=F32)
    ones_lane = lax.broadcasted_iota(jnp.int32, cos.shape, 1) >= MLA_V
    for hd in range(MLA_HEADS):
        sl = slice(hd * MLA_SLAB, (hd + 1) * MLA_SLAB)
        q_out[:, sl] = ((qa[:, sl] * cos + qb[:, sl] * sin) * scale).astype(BF16)
        k_out[:, sl] = (kk[:, sl] + k_rope).astype(BF16)
        v_out[:, sl] = jnp.where(ones_lane, 1.0, vv[:, sl]).astype(BF16)


def _mla_proj(h, wlat, wq, wqr, wk, wv, qn, kvn, cos_t, sin_t, tm=512):
    t = h.shape[0]
    row = lambda w: pl.BlockSpec((tm, w), lambda i: (i, 0))
    full = lambda a: pl.BlockSpec(a.shape, lambda i: (0, 0))
    hs = MLA_HEADS * MLA_SLAB
    return pl.pallas_call(
        _mla_proj_kernel,
        grid=(t // tm,),
        in_specs=[row(D_MODEL), full(wlat), full(wq), full(wqr), full(wk), full(wv), full(qn), full(kvn),
                  row(LANES), row(LANES)],
        out_specs=[row(hs), row(hs), row(hs)],
        out_shape=[jax.ShapeDtypeStruct((t, hs), BF16)] * 3,
        compiler_params=_cparams(("parallel",)),
        name="mla_proj",
    )(h, wlat, wq, wqr, wk, wv, qn, kvn, cos_t, sin_t)


MLA_ROW_SPLIT = 2


def _mla_attn_kernel(q_ref, k_ref, v_ref, o_ref, s_scr, *, tq):
    qi = pl.program_id(2)
    nh = q_ref.shape[1] // MLA_SLAB
    tr = tq // MLA_ROW_SPLIT
    rq = lax.broadcasted_iota(jnp.int32, (tr, tq), 0)
    ck = lax.broadcasted_iota(jnp.int32, (tr, tq), 1)
    nl = tq // LANES

    def tile(n):
        def pass1(hh, rg):
            rows = slice(rg * tr, (rg + 1) * tr)
            q = q_ref[rows, hh * MLA_SLAB:(hh + 1) * MLA_SLAB]
            mx = None
            for c in range(n + 1):
                ks = k_ref[c * tq:(c + 1) * tq, hh * MLA_SLAB:(hh + 1) * MLA_SLAB]
                s = lax.dot_general(q, ks, (((1,), (1,)), ((), ())), preferred_element_type=F32)
                if c == n:
                    s = jnp.where(ck <= rq + rg * tr, s, NEG)
                s_scr[hh, c, rows, :] = s
                parts = [s[:, i * LANES:(i + 1) * LANES] for i in range(nl)]
                mx = functools.reduce(jnp.maximum, parts if mx is None else [mx] + parts)
            return jnp.max(mx, axis=-1, keepdims=True)

        def pass2(hh, rg, m):
            rows = slice(rg * tr, (rg + 1) * tr)
            acc = None
            for c in range(n + 1):
                vs = v_ref[c * tq:(c + 1) * tq, hh * MLA_SLAB:(hh + 1) * MLA_SLAB]
                p = jnp.exp2((s_scr[hh, c, rows, :] - m).astype(BF16))
                pv = jnp.dot(p, vs, preferred_element_type=F32)
                acc = pv if acc is None else acc + pv
            o = acc / acc[:, MLA_V:MLA_V + 1]
            o_ref[rows, hh * MLA_V:(hh + 1) * MLA_V] = o[:, :MLA_V].astype(o_ref.dtype)

        chains = [(hh, rg) for hh in range(nh) for rg in range(MLA_ROW_SPLIT)]
        ms = [pass1(*chains[0])]
        for i, chain in enumerate(chains):
            if i + 1 < len(chains):
                ms.append(pass1(*chains[i + 1]))
            pass2(*chain, ms[i])

    for n in range(s_scr.shape[1]):
        pl.when(qi == n)(functools.partial(tile, n))


def _mla_attn(q, k, v, batch, seq, tq=512, nh=4):
    t = q.shape[0]
    nq = seq // tq
    return pl.pallas_call(
        functools.partial(_mla_attn_kernel, tq=tq),
        grid=(batch, MLA_HEADS // nh, nq),
        in_specs=[pl.BlockSpec((tq, nh * MLA_SLAB), lambda b, p, i: (b * nq + i, p)),
                  pl.BlockSpec((seq, nh * MLA_SLAB), lambda b, p, i: (b, p)),
                  pl.BlockSpec((seq, nh * MLA_SLAB), lambda b, p, i: (b, p))],
        out_specs=pl.BlockSpec((tq, nh * MLA_V), lambda b, p, i: (b * nq + i, p)),
        out_shape=jax.ShapeDtypeStruct((t, MLA_HEADS * MLA_V), BF16),
        scratch_shapes=[pltpu.VMEM((nh, nq, tq, tq), F32)],
        compiler_params=_cparams(("parallel", "parallel", "arbitrary")),
        name="mla_attn",
    )(q, k, v)


PHASE_STRIDE = 4


def _phase_of_slot(r):
    if r <= PHASE_STRIDE:
        return list(range(r))
    f2 = r // PHASE_STRIDE
    return [c1 + PHASE_STRIDE * c2 for c1 in range(PHASE_STRIDE) for c2 in range(f2)]


def _dil_proj_kernel(h_ref, w_ref, o_ref, acc_ref, tmp_ref, *, seq):
    res = jnp.dot(h_ref[...], w_ref[...], preferred_element_type=F32)
    g = pl.program_id(0) % DIL_GROUPS
    nl = o_ref.shape[0]

    @pl.when(g == 0)
    def _():
        for j in range(nl):
            o_ref[j] = res[:, j * LANES:(j + 1) * LANES].astype(o_ref.dtype)

    @pl.when(g > 0)
    def _():
        for j in range(nl):
            acc_ref[j] = res[:, j * LANES:(j + 1) * LANES]

    for gi, (_, r) in enumerate(DIL_CONFIGS):
        if r == 1:
            continue

        @pl.when(g == gi)
        def _(r=r):
            f1 = min(r, PHASE_STRIDE)
            l1 = seq // f1
            dst = o_ref if r == f1 else tmp_ref
            for c1 in range(f1):
                for j in range(nl):
                    dst[j, c1 * l1:(c1 + 1) * l1, :] = acc_ref[j, pl.ds(c1, l1, stride=f1), :].astype(dst.dtype)
            if r > f1:
                f2 = r // f1
                l2 = l1 // f2
                for s in range(r):
                    c1, c2 = divmod(s, f2)
                    for j in range(nl):
                        o_ref[j, s * l2:(s + 1) * l2, :] = (
                            tmp_ref[j, pl.ds(c1 * l1 + c2, l2, stride=f2), :].astype(o_ref.dtype))


def _dil_proj(h, w, seq, tn=512):
    t = h.shape[0]
    n = w.shape[1]
    nl = tn // LANES
    return pl.pallas_call(
        functools.partial(_dil_proj_kernel, seq=seq),
        grid=(n // tn, t // seq),
        in_specs=[pl.BlockSpec((seq, D_MODEL), lambda j, b: (b, 0)),
                  pl.BlockSpec((D_MODEL, tn), lambda j, b: (0, j))],
        out_specs=pl.BlockSpec((nl, seq, LANES), lambda j, b: (j, b, 0)),
        out_shape=jax.ShapeDtypeStruct((n // LANES, t, LANES), BF16),
        scratch_shapes=[pltpu.VMEM((nl, seq, LANES), F32)] * 2,
        compiler_params=_cparams(("parallel", "parallel")),
        name="dil_proj",
    )(h, w)


def _dil_attn_kernel(slopes_ref, q0, q1, q2, k0, k1, k2, v0, v1, v2, o_ref, acc_s, m_s, l_s, *, seq):
    pair = pl.program_id(1)
    sp = DIL_SPAN
    head0 = lax.broadcasted_iota(jnp.int32, (sp, LANES), 1) < DIL_HD
    qi = lax.broadcasted_iota(jnp.int32, (sp, 2 * sp), 0)
    kj = lax.broadcasted_iota(jnp.int32, (sp, 2 * sp), 1)
    dist_w = qi + sp - kj
    valid_w = (dist_w >= 0) & (dist_w <= sp)
    dist_1 = dist_w[:, sp:]
    valid_1 = dist_1 >= 0
    qs, ks, vs = (q0, q1, q2), (k0, k1, k2), (v0, v1, v2)
    for g, (_, r) in enumerate(DIL_CONFIGS):
        ln = seq // r
        nb = ln // sp
        sl = [slopes_ref[g * DIL_HPG + 2 * pair + hh] * float(r) for hh in range(2)]
        bias_w = jnp.concatenate([jnp.where(valid_w, -s * dist_w.astype(F32), NEG) for s in sl], axis=0)
        bias_1 = jnp.concatenate([jnp.where(valid_1, -s * dist_1.astype(F32), NEG) for s in sl], axis=0)
        for slot_i, c in enumerate(_phase_of_slot(r)):
            for i in range(nb):
                row0 = slot_i * ln + i * sp
                qb = qs[g][0, row0:row0 + sp, :]
                zero = jnp.zeros_like(qb)
                q2h = jnp.concatenate([jnp.where(head0, qb, zero), jnp.where(head0, zero, qb)], axis=0)
                lo = row0 if i == 0 else row0 - sp
                kw = ks[g][0, lo:row0 + sp, :]
                vw = vs[g][0, lo:row0 + sp, :]
                s = lax.dot_general(q2h, kw, (((1,), (1,)), ((), ())), preferred_element_type=F32)
                s = s + (bias_1 if i == 0 else bias_w)
                m = jnp.max(s, axis=-1, keepdims=True)
                p = jnp.exp(s - m)
                l = jnp.sum(p, axis=-1, keepdims=True)
                o2 = jnp.dot(p.astype(BF16), vw, preferred_element_type=F32)
                dst = pl.ds(i * sp * r + c, sp, stride=r) if r > 1 else pl.ds(i * sp, sp)
                acc_s[g, dst, :] = jnp.where(head0, o2[:sp], o2[sp:])
                m_s[g, dst, :] = jnp.where(head0, m[:sp], m[sp:])
                l_s[g, dst, :] = jnp.where(head0, l[:sp], l[sp:])

    ch = 256

    def combine(i, _):
        rows = pl.ds(pl.multiple_of(i * ch, ch), ch)
        ms = [m_s[g, rows, :] for g in range(DIL_GROUPS)]
        mx = jnp.maximum(jnp.maximum(ms[0], ms[1]), ms[2])
        ws = [jnp.exp(m - mx) for m in ms]
        num = sum(ws[g] * acc_s[g, rows, :] for g in range(DIL_GROUPS))
        den = sum(ws[g] * l_s[g, rows, :] for g in range(DIL_GROUPS))
        o_ref[rows, :] = (num / den).astype(o_ref.dtype)
        return 0

    lax.fori_loop(0, seq // ch, combine, 0)


def _dil_attn(qkv, slopes, batch, seq):
    t = qkv.shape[1]
    pairs = DIL_HPG // 2
    cpg = DIL_HPG * DIL_HD // LANES
    cpt = DIL_WIDTH // LANES

    def spec(tt, g):
        return pl.BlockSpec((1, seq, LANES), lambda b, p: (tt * cpt + g * cpg + p, b, 0))

    in_specs = [pl.BlockSpec(memory_space=pltpu.SMEM)] + [spec(tt, g) for tt in range(3) for g in range(DIL_GROUPS)]
    return pl.pallas_call(
        functools.partial(_dil_attn_kernel, seq=seq),
        grid=(batch, pairs),
        in_specs=in_specs,
        out_specs=pl.BlockSpec((seq, LANES), lambda b, p: (b, p)),
        out_shape=jax.ShapeDtypeStruct((t, DIL_HPG * DIL_HD), BF16),
        scratch_shapes=[pltpu.VMEM((DIL_GROUPS, seq, LANES), F32)] * 3,
        compiler_params=_cparams(("parallel", "parallel")),
        name="dil_attn",
    )(slopes, *([qkv] * 9))


def _to_token_tiles(dst_ref, val):
    n = val.shape[0]
    for s in range(ROW_TILES):
        dst_ref[pl.ds(s, n, stride=ROW_TILES), :] = val[:, s * LANES:(s + 1) * LANES]


def _from_token_tiles(src_ref, n):
    return jnp.concatenate([src_ref[pl.ds(s, n, stride=ROW_TILES), :] for s in range(ROW_TILES)], axis=1)


def _top2(logits):
    lane = lax.broadcasted_iota(jnp.int32, logits.shape, 1)
    m1 = jnp.max(logits, axis=-1, keepdims=True)
    i1 = jnp.min(jnp.where(logits == m1, lane, LANES), axis=-1, keepdims=True)
    rest = jnp.where(lane == i1, NEG, logits)
    m2 = jnp.max(rest, axis=-1, keepdims=True)
    i2 = jnp.min(jnp.where(rest == m2, lane, LANES), axis=-1, keepdims=True)
    e = jnp.exp(m2 - m1)
    w1 = 1.0 / (1.0 + e)
    return (jnp.where(lane == 0, i1, jnp.where(lane == 1, i2, 0)),
            jnp.where(lane == 0, w1, jnp.where(lane == 1, e * w1, 0.0)))


def _merge_kernel(x_ref, h_ref, om_ref, od_ref, wg_ref, wbm_ref, wbd_ref, wo_ref, g_ref, *rest, route):
    gates = jnp.dot(h_ref[...], wg_ref[...], preferred_element_type=F32)
    bm = jnp.dot(om_ref[...], wbm_ref[...], preferred_element_type=F32)
    bd = jnp.dot(od_ref[...], wbd_ref[...], preferred_element_type=F32)
    merged = _sigmoid(gates[:, :D_MODEL]) * bm + _sigmoid(gates[:, D_MODEL:]) * bd
    x1 = x_ref[...] + jnp.dot(merged.astype(BF16), wo_ref[...], preferred_element_type=F32)
    hn = _rms(x1, g_ref[...])
    if route:
        wr_ref, br_ref, x_out, h_out, idx_out, wt_out = rest
        _to_token_tiles(h_out, hn)
        hi = hn.astype(BF16)
        lo = (hn - hi.astype(F32)).astype(BF16)
        w = wr_ref[...]
        w_hi = w.astype(BF16)
        w_lo = (w - w_hi.astype(F32)).astype(BF16)
        both = jnp.dot(hi, jnp.concatenate([w_hi, w_lo], axis=1), preferred_element_type=F32)
        logits = (both[:, :LANES] + both[:, LANES:]
                  + jnp.dot(lo, w_hi, preferred_element_type=F32) + br_ref[...])
        idx_out[...], wt_out[...] = _top2(logits)
    else:
        x_out, h_out = rest
        h_out[...] = hn.astype(h_out.dtype)
    x_out[...] = x1


def _merge(x, h, o_mla, o_dil, wg, wbm, wbd, wo, g_next, router=None, tm=512):
    t = x.shape[0]
    row = lambda w: pl.BlockSpec((tm, w), lambda i: (i, 0))
    full = lambda a: pl.BlockSpec(a.shape, lambda i: (0, 0))
    in_specs = [row(D_MODEL), row(D_MODEL), row(o_mla.shape[1]), row(o_dil.shape[1]),
                full(wg), full(wbm), full(wbd), full(wo), full(g_next)]
    args = [x, h, o_mla, o_dil, wg, wbm, wbd, wo, g_next]
    out_specs = [row(D_MODEL)]
    out_shape = [jax.ShapeDtypeStruct((t, D_MODEL), F32)]
    if router is None:
        out_specs += [row(D_MODEL)]
        out_shape += [jax.ShapeDtypeStruct((t, D_MODEL), BF16)]
    else:
        in_specs += [full(router[0]), full(router[1])]
        args += list(router)
        out_specs += [pl.BlockSpec((tm * ROW_TILES, LANES), lambda i: (i, 0)), row(LANES), row(LANES)]
        out_shape += [jax.ShapeDtypeStruct((t * ROW_TILES, LANES), F32),
                      jax.ShapeDtypeStruct((t, LANES), jnp.int32), jax.ShapeDtypeStruct((t, LANES), F32)]
    return pl.pallas_call(
        functools.partial(_merge_kernel, route=router is not None),
        grid=(t // tm,),
        in_specs=in_specs,
        out_specs=out_specs,
        out_shape=out_shape,
        compiler_params=_cparams(("parallel",)),
        name="merge",
    )(*args)


def _ffn_kernel(x_ref, h_ref, wg_ref, wu_ref, wd_ref, o_ref):
    @pl.when(pl.program_id(1) == 0)
    def _():
        o_ref[...] = x_ref[...]

    h = h_ref[...]
    a = jnp.dot(h, wg_ref[...], preferred_element_type=F32)
    u = jnp.dot(h, wu_ref[...], preferred_element_type=F32)
    o_ref[...] += jnp.dot((a * _sigmoid(a) * u).astype(BF16), wd_ref[...], preferred_element_type=F32)


def _ffn(x, h, wg, wu, wd, tm=512, tf=1792):
    t = x.shape[0]
    nf = wg.shape[1] // tf
    return pl.pallas_call(
        _ffn_kernel,
        grid=(t // tm, nf),
        in_specs=[pl.BlockSpec((tm, D_MODEL), lambda i, f: (i, 0)),
                  pl.BlockSpec((tm, D_MODEL), lambda i, f: (i, 0)),
                  pl.BlockSpec((D_MODEL, tf), lambda i, f: (0, f)),
                  pl.BlockSpec((D_MODEL, tf), lambda i, f: (0, f)),
                  pl.BlockSpec((tf, D_MODEL), lambda i, f: (f, 0))],
        out_specs=pl.BlockSpec((tm, D_MODEL), lambda i, f: (i, 0)),
        out_shape=jax.ShapeDtypeStruct((t, D_MODEL), F32),
        compiler_params=_cparams(("parallel", "arbitrary")),
        name="dense_ffn",
    )(x, h, wg, wu, wd)


def _tile_at(ref, row8):
    return ref.at[pl.ds(pl.multiple_of(row8, ROW_TILES), ROW_TILES), :]


def _dispatch_kernel(pos_ref, fill_ref, h_ref, xs_hbm, zero_ref, sem, sem_fill, *, t, tmd, tm):
    i = pl.program_id(0)

    @pl.when(i == 0)
    def _():
        zero_ref[...] = jnp.zeros_like(zero_ref)
        for e in range(fill_ref.shape[0]):
            @pl.when(fill_ref[e] >= 0)
            def _():
                cp = pltpu.make_async_copy(
                    zero_ref, xs_hbm.at[pl.ds(pl.multiple_of(fill_ref[e], ROW_TILES), tm * ROW_TILES), :], sem_fill)
                cp.start()
                cp.wait()

    for k in range(TOP_K):
        for r in range(tmd):
            pltpu.make_async_copy(_tile_at(h_ref, r * ROW_TILES),
                                  _tile_at(xs_hbm, pos_ref[k * t + i * tmd + r]), sem.at[k]).start(priority=r % 2)
    for k in range(TOP_K):
        pltpu.make_async_copy(h_ref, h_ref, sem.at[k]).wait()


def _dispatch(h, pos8, fill8, n_rows, tm, tmd=512):
    t = h.shape[0] // ROW_TILES
    return pl.pallas_call(
        functools.partial(_dispatch_kernel, t=t, tmd=tmd, tm=tm),
        grid_spec=pltpu.PrefetchScalarGridSpec(
            num_scalar_prefetch=2,
            grid=(t // tmd,),
            in_specs=[pl.BlockSpec((tmd * ROW_TILES, LANES), lambda i, pos, fill: (i, 0))],
            out_specs=pl.BlockSpec(memory_space=pl.ANY),
            scratch_shapes=[pltpu.VMEM((tm * ROW_TILES, LANES), F32),
                            pltpu.SemaphoreType.DMA((TOP_K,)), pltpu.SemaphoreType.DMA(())]),
        out_shape=jax.ShapeDtypeStruct((n_rows * ROW_TILES, LANES), F32),
        compiler_params=_cparams(("arbitrary",)),
        name="moe_dispatch",
    )(pos8, fill8, h)


def _moe_ffn_kernel(te_ref, nv_ref, x_ref, wg_ref, wu_ref, wd_ref, o_ref, acc_ref, xb_ref, *, tm):
    i = pl.program_id(0)
    f = pl.program_id(1)
    valid = i < nv_ref[0]
    last = f == pl.num_programs(1) - 1

    @pl.when(valid & (f == 0))
    def _():
        xb_ref[...] = _from_token_tiles(x_ref, tm).astype(BF16)
        acc_ref[...] = jnp.zeros_like(acc_ref)

    @pl.when(valid)
    def _():
        h = xb_ref[...]
        a = jnp.dot(h, wg_ref[0], preferred_element_type=F32)
        u = jnp.dot(h, wu_ref[0], preferred_element_type=F32)
        acc_ref[...] += jnp.dot((a * _sigmoid(a) * u).astype(BF16), wd_ref[0], preferred_element_type=F32)

    @pl.when(valid & last)
    def _():
        _to_token_tiles(o_ref, acc_ref[...])

    @pl.when(jnp.logical_not(valid) & last)
    def _():
        o_ref[...] = jnp.zeros_like(o_ref)


def _moe_ffn(xs, tile_expert, n_valid, wg, wu, wd, tm, tf=1792):
    n_tiles = xs.shape[0] // (tm * ROW_TILES)
    nf = wg.shape[2] // tf

    def live(i, f, te, nv):
        ok = i < nv[0]
        return jnp.where(ok, i, nv[0] - 1), jnp.where(ok, f, nf - 1)

    return pl.pallas_call(
        functools.partial(_moe_ffn_kernel, tm=tm),
        grid_spec=pltpu.PrefetchScalarGridSpec(
            num_scalar_prefetch=2,
            grid=(n_tiles, nf),
            in_specs=[pl.BlockSpec((tm * ROW_TILES, LANES), lambda *a: (live(*a)[0], 0)),
                      pl.BlockSpec((1, D_MODEL, tf), lambda *a: (a[2][live(*a)[0]], 0, live(*a)[1])),
                      pl.BlockSpec((1, D_MODEL, tf), lambda *a: (a[2][live(*a)[0]], 0, live(*a)[1])),
                      pl.BlockSpec((1, tf, D_MODEL), lambda *a: (a[2][live(*a)[0]], live(*a)[1], 0))],
            out_specs=pl.BlockSpec((tm * ROW_TILES, LANES), lambda i, f, te, nv: (i, 0)),
            scratch_shapes=[pltpu.VMEM((tm, D_MODEL), F32), pltpu.VMEM((tm, D_MODEL), BF16)]),
        out_shape=jax.ShapeDtypeStruct(xs.shape, F32),
        compiler_params=_cparams(("arbitrary", "arbitrary")),
        name="moe_ffn",
    )(tile_expert, n_valid, xs, wg, wu, wd)


def _router_params(w_router, b_router):
    w_pad = jnp.zeros((D_MODEL, LANES), F32).at[:, :N_EXPERTS].set(w_router)
    b_pad = jnp.full((1, LANES), NEG, F32).at[0, :N_EXPERTS].set(b_router)
    return w_pad, b_pad


def _moe(h, idx_l, wg, wu, wd, tm=448):
    t = h.shape[0] // ROW_TILES
    expert = idx_l[:, :TOP_K].T.reshape(-1)
    onehot = (expert[:, None] == jnp.arange(N_EXPERTS)[None, :]).astype(jnp.int32)
    csum = jnp.cumsum(onehot, axis=0)
    counts = csum[-1]
    rank = jnp.sum((csum - onehot) * onehot, axis=1)
    tiles = (counts + tm - 1) // tm
    tile_end = jnp.cumsum(tiles)
    pad_off = (tile_end - tiles) * tm
    pos8 = (pad_off[expert] + rank) * ROW_TILES
    n_tiles = (TOP_K * t) // tm + N_EXPERTS
    n_valid = tile_end[-1:].astype(jnp.int32)
    tile_ids = jnp.minimum(jnp.arange(n_tiles), n_valid[0] - 1)
    tile_expert = jnp.sum(tile_end[None, :] <= tile_ids[:, None], axis=1).astype(jnp.int32)
    spare = n_valid[0] + jnp.arange(N_EXPERTS)
    fill8 = jnp.concatenate([jnp.where(tiles > 0, tile_end - 1, -1), jnp.where(spare < n_tiles, spare, -1)])
    fill8 = jnp.where(fill8 >= 0, fill8 * (tm * ROW_TILES), -1).astype(jnp.int32)
    xs = _dispatch(h, pos8, fill8, n_tiles * tm, tm)
    ys = _moe_ffn(xs, tile_expert, n_valid, wg, wu, wd, tm)
    return ys, pos8


def _ple_body(x, p_ref, wg_ref, wp_ref, g_ref, gn_ref, x_out, h_out):
    gate = _sigmoid(jnp.dot(_rms(x, g_ref[...]).astype(BF16), wg_ref[...], preferred_element_type=F32))
    x2 = x + gate * jnp.dot(p_ref[...].astype(BF16), wp_ref[...], preferred_element_type=F32)
    x_out[...] = x2
    h_out[...] = _rms(x2, gn_ref[...]).astype(h_out.dtype)


def _ple_kernel(x_ref, p_ref, wg_ref, wp_ref, g_ref, gn_ref, x_out, h_out):
    _ple_body(x_ref[...], p_ref, wg_ref, wp_ref, g_ref, gn_ref, x_out, h_out)


def _ple_moe_kernel(pos_ref, x_ref, p_ref, wt_ref, wg_ref, wp_ref, g_ref, gn_ref, y_hbm, x_out, h_out,
                    ybuf, sem, *, t, tm):
    i = pl.program_id(0)
    slot = i % 2

    def gather(block, s):
        for k in range(TOP_K):
            for r in range(tm):
                pltpu.make_async_copy(_tile_at(y_hbm, pos_ref[k * t + block * tm + r]),
                                      _tile_at(ybuf.at[s, k], r * ROW_TILES), sem.at[s]).start(priority=r % 2)

    @pl.when(i == 0)
    def _():
        gather(0, 0)

    @pl.when(i + 1 < pl.num_programs(0))
    def _():
        gather(i + 1, 1 - slot)

    pltpu.make_async_copy(ybuf.at[slot], ybuf.at[slot], sem.at[slot]).wait()
    wt = wt_ref[...]
    x = (x_ref[...] + wt[:, 0:1] * _from_token_tiles(ybuf.at[slot, 0], tm)
         + wt[:, 1:2] * _from_token_tiles(ybuf.at[slot, 1], tm))
    _ple_body(x, p_ref, wg_ref, wp_ref, g_ref, gn_ref, x_out, h_out)


def _ple(x, p, wg, wp, g, g_next, h_dtype, moe=None, tm=512):
    t = x.shape[0]
    out_shape = [jax.ShapeDtypeStruct((t, D_MODEL), F32), jax.ShapeDtypeStruct((t, D_MODEL), h_dtype)]
    if moe is None:
        row = lambda w: pl.BlockSpec((tm, w), lambda i: (i, 0))
        full = lambda a: pl.BlockSpec(a.shape, lambda i: (0, 0))
        return pl.pallas_call(
            _ple_kernel,
            grid=(t // tm,),
            in_specs=[row(D_MODEL), row(P_DIM), full(wg), full(wp), full(g), full(g_next)],
            out_specs=[row(D_MODEL), row(D_MODEL)],
            out_shape=out_shape,
            compiler_params=_cparams(("parallel",)),
            name="ple",
        )(x, p, wg, wp, g, g_next)
    ys, pos8, wt = moe
    row = lambda w: pl.BlockSpec((tm, w), lambda i, pos: (i, 0))
    full = lambda a: pl.BlockSpec(a.shape, lambda i, pos: (0, 0))
    return pl.pallas_call(
        functools.partial(_ple_moe_kernel, t=t, tm=tm),
        grid_spec=pltpu.PrefetchScalarGridSpec(
            num_scalar_prefetch=1,
            grid=(t // tm,),
            in_specs=[row(D_MODEL), row(P_DIM), row(LANES), full(wg), full(wp), full(g), full(g_next),
                      pl.BlockSpec(memory_space=pl.ANY)],
            out_specs=[row(D_MODEL), row(D_MODEL)],
            scratch_shapes=[pltpu.VMEM((2, TOP_K, tm * ROW_TILES, LANES), F32), pltpu.SemaphoreType.DMA((2,))]),
        out_shape=out_shape,
        compiler_params=_cparams(("arbitrary",)),
        name="ple_moe",
    )(pos8, x, p, wt, wg, wp, g, g_next, ys)


def _rot_cols(w):
    half = w.shape[-1] // 2
    return jnp.concatenate([-w[:, half:], w[:, :half]], axis=-1)


def _layer_weights(i, w_in, w_uq, w_ukv):
    wi = w_in[i]
    z = lambda n: jnp.zeros((wi.shape[0], n), F32)
    w_kr = wi[:, IN_OFF[2]:IN_OFF[3]]
    pad = MLA_SLAB - MLA_NOPE - MLA_ROPE
    wlat = jnp.concatenate([wi[:, :IN_OFF[2]], z(MLA_NOPE), w_kr, z(pad), z(MLA_NOPE), _rot_cols(w_kr), z(pad)],
                           axis=1).astype(BF16)
    qscale = jnp.concatenate([jnp.full((DIL_WIDTH,), DIL_HD ** -0.5, F32), jnp.ones((2 * DIL_WIDTH,), F32)])
    wdil = (wi[:, IN_OFF[3]:IN_OFF[6]] * qscale).astype(BF16)
    wgate = wi[:, IN_OFF[6]:].astype(BF16)
    uq = w_uq[i].reshape(MLA_Q_RANK, MLA_HEADS, MLA_NOPE + MLA_ROPE)
    zq = lambda n: jnp.zeros((MLA_Q_RANK, MLA_HEADS, n), F32)
    rope_rot = jnp.concatenate([-uq[..., MLA_NOPE + MLA_ROPE // 2:], uq[..., MLA_NOPE:MLA_NOPE + MLA_ROPE // 2]], -1)
    wq = jnp.concatenate([uq, zq(pad)], axis=-1).reshape(MLA_Q_RANK, -1).astype(BF16)
    wqr = jnp.concatenate([zq(MLA_NOPE), rope_rot, zq(pad)], axis=-1).reshape(MLA_Q_RANK, -1).astype(BF16)
    ukv = w_ukv[i].reshape(MLA_KV_RANK, MLA_HEADS, MLA_NOPE + MLA_V)
    zkv = jnp.zeros((MLA_KV_RANK, MLA_HEADS, MLA_SLAB - MLA_NOPE), F32)
    wk = jnp.concatenate([ukv[..., :MLA_NOPE], zkv], axis=-1).reshape(MLA_KV_RANK, -1).astype(BF16)
    wv = jnp.concatenate([ukv[..., MLA_NOPE:], zkv], axis=-1).reshape(MLA_KV_RANK, -1).astype(BF16)
    return wlat, wdil, wgate, wq, wqr, wk, wv


def kernel(x, p, positions, attn_norm, w_in, q_norm, w_uq, kv_norm, w_ukv, w_br_mla, w_br_dil, w_out, ffn_norm, dense_w_gate, dense_w_up, dense_w_down, router_w, router_b, moe_w_gate, moe_w_up, moe_w_down, ple_norm, ple_w_gate, ple_w_proj, final_norm):
    batch, seq, d = x.shape
    t = batch * seq
    depth = w_in.shape[0]
    xf = x.reshape(t, d)
    cos_t, sin_t = _rope_tables(positions)
    slopes = _alibi_slopes(DIL_HEADS)
    row = lambda v: v.reshape(1, -1)
    h = _norm(xf, attn_norm[0])
    for i in range(depth):
        wlat, wdil, wgate, wq, wqr, wk, wv = _layer_weights(i, w_in, w_uq, w_ukv)
        q, k, v = _mla_proj(h, wlat, wq, wqr, wk, wv, row(q_norm[i]), row(kv_norm[i]), cos_t, sin_t)
        o_mla = _mla_attn(q, k, v, batch, seq)
        qkv_d = _dil_proj(h, wdil, seq)
        o_dil = _dil_attn(qkv_d, slopes, batch, seq)
        moe_layer = i % 2 == 1
        j = i // 2
        xf, h2, *routing = _merge(xf, h, o_mla, o_dil, wgate, w_br_mla[i].astype(BF16), w_br_dil[i].astype(BF16),
                                  w_out[i].astype(BF16), row(ffn_norm[i]),
                                  _router_params(router_w[j], router_b[j]) if moe_layer else None)
        moe = None
        if moe_layer:
            idx_l, wt_l = routing
            moe = _moe(h2, idx_l, moe_w_gate[j].astype(BF16), moe_w_up[j].astype(BF16),
                       moe_w_down[j].astype(BF16)) + (wt_l,)
        else:
            xf = _ffn(xf, h2, dense_w_gate[j].astype(BF16), dense_w_up[j].astype(BF16),
                      dense_w_down[j].astype(BF16))
        last = i == depth - 1
        g_next = final_norm if last else attn_norm[i + 1]
        xf, h = _ple(xf, p[i].reshape(t, -1), ple_w_gate[i].astype(BF16), ple_w_proj[i].astype(BF16),
                     row(ple_norm[i]), row(g_next), F32 if last else BF16, moe)
    return h.reshape(batch, seq, d)
```

```python
import functools

import numpy as np
import jax
import jax.numpy as jnp
from jax import lax
from jax.experimental import pallas as pl
from jax.experimental.pallas import tpu as pltpu

F32 = jnp.float32
BF16 = jnp.bfloat16

D_MODEL = 1024
P_DIM = 256
NORM_EPS = 1e-6
DEPTH = 2

MLA_HEADS = 8
MLA_Q_RANK = 384
MLA_KV_RANK = 256
MLA_NOPE = 64
MLA_ROPE = 32
MLA_V = 64
ROPE_THETA = 10000.0
MLA_SLAB = 128

DIL_CONFIGS = ((128, 1), (512, 4), (2048, 16))
DIL_GROUPS = 3
DIL_HPG = 8
DIL_HEADS = 24
DIL_HD = 64
DIL_WIDTH = DIL_HEADS * DIL_HD
DIL_SPAN = 128

D_FF = 3584
N_EXPERTS = 8
TOP_K = 2

LANES = 128
ROW_TILES = D_MODEL // LANES
NEG = -1e30
LOG2E = 1.4426950408889634
VMEM_LIMIT = 56 * 1024 * 1024

IN_OFF = tuple(int(o) for o in np.cumsum((0, MLA_Q_RANK, MLA_KV_RANK, MLA_ROPE, DIL_WIDTH, DIL_WIDTH,
                                          DIL_WIDTH, D_MODEL, D_MODEL)))


def _cparams(sem):
    return pltpu.CompilerParams(dimension_semantics=sem, vmem_limit_bytes=VMEM_LIMIT)


def _rms(x, g):
    return x * lax.rsqrt(jnp.mean(x * x, axis=-1, keepdims=True) + NORM_EPS) * g


def _sigmoid(x):
    return 1.0 / (1.0 + jnp.exp(-x))


def _alibi_slopes(n):
    def pow2(m):
        start = 2.0 ** (-8.0 / m)
        return [start ** (i + 1) for i in range(m)]
    if float(np.log2(n)).is_integer():
        s = pow2(n)
    else:
        c = 2 ** int(np.floor(np.log2(n)))
        s = pow2(c) + pow2(2 * c)[0::2][: n - c]
    return jnp.asarray(sorted(s, reverse=True), dtype=F32)


def _norm_kernel(x_ref, g_ref, o_ref):
    o_ref[...] = _rms(x_ref[...], g_ref[...]).astype(o_ref.dtype)


def _norm(x, g, tm=1024):
    t, d = x.shape
    return pl.pallas_call(
        _norm_kernel,
        grid=(t // tm,),
        in_specs=[pl.BlockSpec((tm, d), lambda i: (i, 0)), pl.BlockSpec((1, d), lambda i: (0, 0))],
        out_specs=pl.BlockSpec((tm, d), lambda i: (i, 0)),
        out_shape=jax.ShapeDtypeStruct((t, d), BF16),
        compiler_params=_cparams(("parallel",)),
        name="rms_norm",
    )(x, g.reshape(1, d))


def _rope_table_kernel(pos_ref, invf_ref, cos_ref, sin_ref):
    ang = pos_ref[...].astype(F32) * invf_ref[...]
    lane = lax.broadcasted_iota(jnp.int32, ang.shape, 1)
    rope_lane = (lane >= MLA_NOPE) & (lane < MLA_NOPE + MLA_ROPE)
    cos_ref[...] = jnp.where(lane < MLA_NOPE, 1.0, jnp.where(rope_lane, jnp.cos(ang), 0.0))
    sin_ref[...] = jnp.where(rope_lane, jnp.sin(ang), 0.0)


def _rope_tables(positions, tm=2048):
    t = positions.size
    half = MLA_ROPE // 2
    inv_freq = ROPE_THETA ** (-jnp.arange(half, dtype=F32) / half)
    invf = jnp.zeros((1, LANES), F32).at[0, MLA_NOPE:MLA_NOPE + MLA_ROPE].set(jnp.concatenate([inv_freq, inv_freq]))
    pos_b = jnp.broadcast_to(positions.reshape(t, 1), (t, LANES))
    spec = pl.BlockSpec((tm, LANES), lambda i: (i, 0))
    return pl.pallas_call(
        _rope_table_kernel,
        grid=(t // tm,),
        in_specs=[spec, pl.BlockSpec((1, LANES), lambda i: (0, 0))],
        out_specs=[spec, spec],
        out_shape=[jax.ShapeDtypeStruct((t, LANES), F32)] * 2,
        compiler_params=_cparams(("parallel",)),
        name="rope_tables",
    )(pos_b, invf)


def _mla_proj_kernel(h_ref, wlat_ref, wq_ref, wqr_ref, wk_ref, wv_ref, qn_ref, kvn_ref, cos_ref, sin_ref,
                     q_out, k_out, v_out):
    lat = jnp.dot(h_ref[...], wlat_ref[...], preferred_element_type=F32)
    cqn = _rms(lat[:, :MLA_Q_RANK], qn_ref[...]).astype(BF16)
    ckvn = _rms(lat[:, MLA_Q_RANK:MLA_Q_RANK + MLA_KV_RANK], kvn_ref[...]).astype(BF16)
    cos = cos_ref[...]
    sin = sin_ref[...]
    o = MLA_Q_RANK + MLA_KV_RANK
    k_rope = lat[:, o:o + LANES] * cos + lat[:, o + LANES:o + 2 * LANES] * sin
    qa = jnp.dot(cqn, wq_ref[...], preferred_element_type=F32)
    qb = jnp.dot(cqn, wqr_ref[...], preferred_element_type=F32)
    kk = jnp.dot(ckvn, wk_ref[...], preferred_element_type=F32)
    scale = (MLA_NOPE + MLA_ROPE) ** -0.5 * LOG2E
    vv = jnp.dot(ckvn, wv_ref[...], preferred_element_type=F32)
    ones_lane = lax.broadcasted_iota(jnp.int32, cos.shape, 1) >= MLA_V
    for hd in range(MLA_HEADS):
        sl = slice(hd * MLA_SLAB, (hd + 1) * MLA_SLAB)
        q_out[:, sl] = ((qa[:, sl] * cos + qb[:, sl] * sin) * scale).astype(BF16)
        k_out[:, sl] = (kk[:, sl] + k_rope).astype(BF16)
        v_out[:, sl] = jnp.where(ones_lane, 1.0, vv[:, sl]).astype(BF16)


def _mla_proj(h, wlat, wq, wqr, wk, wv, qn, kvn, cos_t, sin_t, tm=512):
    t = h.shape[0]
    row = lambda w: pl.BlockSpec((tm, w), lambda i: (i, 0))
    full = lambda a: pl.BlockSpec(a.shape, lambda i: (0, 0))
    hs = MLA_HEADS * MLA_SLAB
    return pl.pallas_call(
        _mla_proj_kernel,
        grid=(t // tm,),
        in_specs=[row(D_MODEL), full(wlat), full(wq), full(wqr), full(wk), full(wv), full(qn), full(kvn),
                  row(LANES), row(LANES)],
        out_specs=[row(hs), row(hs), row(hs)],
        out_shape=[jax.ShapeDtypeStruct((t, hs), BF16)] * 3,
        compiler_params=_cparams(("parallel",)),
        name="mla_proj",
    )(h, wlat, wq, wqr, wk, wv, qn, kvn, cos_t, sin_t)


MLA_ROW_SPLIT = 2


def _mla_attn_kernel(q_ref, k_ref, v_ref, o_ref, s_scr, *, tq):
    qi = pl.program_id(2)
    nh = q_ref.shape[1] // MLA_SLAB
    tr = tq // MLA_ROW_SPLIT
    rq = lax.broadcasted_iota(jnp.int32, (tr, tq), 0)
    ck = lax.broadcasted_iota(jnp.int32, (tr, tq), 1)
    nl = tq // LANES

    def tile(n):
        def pass1(hh, rg):
            rows = slice(rg * tr, (rg + 1) * tr)
            q = q_ref[rows, hh * MLA_SLAB:(hh + 1) * MLA_SLAB]
            mx = None
            for c in range(n + 1):
                ks = k_ref[c * tq:(c + 1) * tq, hh * MLA_SLAB:(hh + 1) * MLA_SLAB]
                s = lax.dot_general(q, ks, (((1,), (1,)), ((), ())), preferred_element_type=F32)
                if c == n:
                    s = jnp.where(ck <= rq + rg * tr, s, NEG)
                s_scr[hh, c, rows, :] = s
                parts = [s[:, i * LANES:(i + 1) * LANES] for i in range(nl)]
                mx = functools.reduce(jnp.maximum, parts if mx is None else [mx] + parts)
            return jnp.max(mx, axis=-1, keepdims=True)

        def pass2(hh, rg, m):
            rows = slice(rg * tr, (rg + 1) * tr)
            acc = None
            for c in range(n + 1):
                vs = v_ref[c * tq:(c + 1) * tq, hh * MLA_SLAB:(hh + 1) * MLA_SLAB]
                p = jnp.exp2((s_scr[hh, c, rows, :] - m).astype(BF16))
                pv = jnp.dot(p, vs, preferred_element_type=F32)
                acc = pv if acc is None else acc + pv
            o = acc / acc[:, MLA_V:MLA_V + 1]
            o_ref[rows, hh * MLA_V:(hh + 1) * MLA_V] = o[:, :MLA_V].astype(o_ref.dtype)

        chains = [(hh, rg) for hh in range(nh) for rg in range(MLA_ROW_SPLIT)]
        ms = [pass1(*chains[0])]
        for i, chain in enumerate(chains):
            if i + 1 < len(chains):
                ms.append(pass1(*chains[i + 1]))
            pass2(*chain, ms[i])

    for n in range(s_scr.shape[1]):
        pl.when(qi == n)(functools.partial(tile, n))


def _mla_attn(q, k, v, batch, seq, tq=512, nh=4):
    t = q.shape[0]
    nq = seq // tq
    return pl.pallas_call(
        functools.partial(_mla_attn_kernel, tq=tq),
        grid=(batch, MLA_HEADS // nh, nq),
        in_specs=[pl.BlockSpec((tq, nh * MLA_SLAB), lambda b, p, i: (b * nq + i, p)),
                  pl.BlockSpec((seq, nh * MLA_SLAB), lambda b, p, i: (b, p)),
                  pl.BlockSpec((seq, nh * MLA_SLAB), lambda b, p, i: (b, p))],
        out_specs=pl.BlockSpec((tq, nh * MLA_V), lambda b, p, i: (b * nq + i, p)),
        out_shape=jax.ShapeDtypeStruct((t, MLA_HEADS * MLA_V), BF16),
        scratch_shapes=[pltpu.VMEM((nh, nq, tq, tq), F32)],
        compiler_params=_cparams(("parallel", "parallel", "arbitrary")),
        name="mla_attn",
    )(q, k, v)


PHASE_STRIDE = 4


def _phase_of_slot(r):
    if r <= PHASE_STRIDE:
        return list(range(r))
    f2 = r // PHASE_STRIDE
    return [c1 + PHASE_STRIDE * c2 for c1 in range(PHASE_STRIDE) for c2 in range(f2)]


def _dil_proj_kernel(h_ref, w_ref, o_ref, acc_ref, tmp_ref, *, seq, r):
    nl = o_ref.shape[0]

    @pl.when(pl.program_id(0) == 0)
    def _():
        acc_ref[...] = jnp.zeros_like(acc_ref)

    res = jnp.dot(h_ref[...], w_ref[...], preferred_element_type=F32)
    f1 = min(r, PHASE_STRIDE)
    l1 = seq // f1
    dst = o_ref if r == f1 else tmp_ref
    for c1 in range(f1):
        for j in range(nl):
            src = acc_ref[j, pl.ds(c1, l1, stride=f1), :] if f1 > 1 else acc_ref[j]
            dst[j, c1 * l1:(c1 + 1) * l1, :] = src.astype(dst.dtype)
    if r > f1:
        f2 = r // f1
        l2 = l1 // f2
        for s in range(r):
            c1, c2 = divmod(s, f2)
            for j in range(nl):
                o_ref[j, s * l2:(s + 1) * l2, :] = (
                    tmp_ref[j, pl.ds(c1 * l1 + c2, l2, stride=f2), :].astype(o_ref.dtype))
    for j in range(nl):
        acc_ref[j] = res[:, j * LANES:(j + 1) * LANES]


def _dil_proj(h, w, seq, g, tn=512):
    t = h.shape[0]
    nb = t // seq
    nl = tn // LANES
    n_blocks = 3 * nb
    r = DIL_CONFIGS[g][1]

    def mm(s):
        return divmod(jnp.minimum(s, n_blocks - 1), nb)

    def wr(s):
        return divmod(jnp.maximum(s - 1, 0), nb)

    return pl.pallas_call(
        functools.partial(_dil_proj_kernel, seq=seq, r=r),
        grid=(n_blocks + 1,),
        in_specs=[pl.BlockSpec((seq, D_MODEL), lambda s: (mm(s)[1], 0)),
                  pl.BlockSpec((D_MODEL, tn), lambda s: (0, mm(s)[0] * DIL_GROUPS + g))],
        out_specs=pl.BlockSpec((nl, seq, LANES), lambda s: (wr(s)[0], wr(s)[1], 0)),
        out_shape=jax.ShapeDtypeStruct((3 * nl, t, LANES), BF16),
        scratch_shapes=[pltpu.VMEM((nl, seq, LANES), F32)] * 2,
        compiler_params=_cparams(("arbitrary",)),
        name="dil_proj",
    )(h, w)


def _dil_attn_kernel(slopes_ref, q0, q1, q2, k0, k1, k2, v0, v1, v2, o_ref, acc_s, m_s, l_s, *, seq):
    pair = pl.program_id(1)
    sp = DIL_SPAN
    head0 = lax.broadcasted_iota(jnp.int32, (sp, LANES), 1) < DIL_HD
    qi = lax.broadcasted_iota(jnp.int32, (sp, 2 * sp), 0)
    kj = lax.broadcasted_iota(jnp.int32, (sp, 2 * sp), 1)
    dist_w = qi + sp - kj
    valid_w = (dist_w >= 0) & (dist_w <= sp)
    dist_1 = dist_w[:, sp:]
    valid_1 = dist_1 >= 0
    qs, ks, vs = (q0, q1, q2), (k0, k1, k2), (v0, v1, v2)
    for g, (_, r) in enumerate(DIL_CONFIGS):
        ln = seq // r
        nb = ln // sp
        sl = [slopes_ref[g * DIL_HPG + 2 * pair + hh] * float(r) for hh in range(2)]
        bias_w = jnp.concatenate([jnp.where(valid_w, -s * dist_w.astype(F32), NEG) for s in sl], axis=0)
        bias_1 = jnp.concatenate([jnp.where(valid_1, -s * dist_1.astype(F32), NEG) for s in sl], axis=0)
        for slot_i, c in enumerate(_phase_of_slot(r)):
            for i in range(nb):
                row0 = slot_i * ln + i * sp
                qb = qs[g][0, row0:row0 + sp, :]
                zero = jnp.zeros_like(qb)
                q2h = jnp.concatenate([jnp.where(head0, qb, zero), jnp.where(head0, zero, qb)], axis=0)
                lo = row0 if i == 0 else row0 - sp
                kw = ks[g][0, lo:row0 + sp, :]
                vw = vs[g][0, lo:row0 + sp, :]
                s = lax.dot_general(q2h, kw, (((1,), (1,)), ((), ())), preferred_element_type=F32)
                s = s + (bias_1 if i == 0 else bias_w)
                m = jnp.max(s, axis=-1, keepdims=True)
                p = jnp.exp(s - m)
                l = jnp.sum(p, axis=-1, keepdims=True)
                o2 = jnp.dot(p.astype(BF16), vw, preferred_element_type=F32)
                dst = pl.ds(i * sp * r + c, sp, stride=r) if r > 1 else pl.ds(i * sp, sp)
                acc_s[g, dst, :] = jnp.where(head0, o2[:sp], o2[sp:])
                m_s[g, dst, :] = jnp.where(head0, m[:sp], m[sp:])
                l_s[g, dst, :] = jnp.where(head0, l[:sp], l[sp:])

    ch = 256

    def combine(i, _):
        rows = pl.ds(pl.multiple_of(i * ch, ch), ch)
        ms = [m_s[g, rows, :] for g in range(DIL_GROUPS)]
        mx = jnp.maximum(jnp.maximum(ms[0], ms[1]), ms[2])
        ws = [jnp.exp(m - mx) for m in ms]
        num = sum(ws[g] * acc_s[g, rows, :] for g in range(DIL_GROUPS))
        den = sum(ws[g] * l_s[g, rows, :] for g in range(DIL_GROUPS))
        o_ref[rows, :] = (num / den).astype(o_ref.dtype)
        return 0

    lax.fori_loop(0, seq // ch, combine, 0)


def _dil_attn(qkvs, slopes, batch, seq):
    t = qkvs[0].shape[1]
    pairs = DIL_HPG // 2

    def spec(tt):
        return pl.BlockSpec((1, seq, LANES), lambda b, p: (tt * pairs + p, b, 0))

    in_specs = [pl.BlockSpec(memory_space=pltpu.SMEM)] + [spec(tt) for tt in range(3) for _ in range(DIL_GROUPS)]
    return pl.pallas_call(
        functools.partial(_dil_attn_kernel, seq=seq),
        grid=(batch, pairs),
        in_specs=in_specs,
        out_specs=pl.BlockSpec((seq, LANES), lambda b, p: (b, p)),
        out_shape=jax.ShapeDtypeStruct((t, DIL_HPG * DIL_HD), BF16),
        scratch_shapes=[pltpu.VMEM((DIL_GROUPS, seq, LANES), F32)] * 3,
        compiler_params=_cparams(("parallel", "parallel")),
        name="dil_attn",
    )(slopes, *[qkvs[g] for _ in range(3) for g in range(DIL_GROUPS)])


def _to_token_tiles(dst_ref, val):
    n = val.shape[0]
    for s in range(ROW_TILES):
        dst_ref[pl.ds(s, n, stride=ROW_TILES), :] = val[:, s * LANES:(s + 1) * LANES]


def _from_token_tiles(src_ref, n):
    return jnp.concatenate([src_ref[pl.ds(s, n, stride=ROW_TILES), :] for s in range(ROW_TILES)], axis=1)


def _top2(logits):
    lane = lax.broadcasted_iota(jnp.int32, logits.shape, 1)
    m1 = jnp.max(logits, axis=-1, keepdims=True)
    i1 = jnp.min(jnp.where(logits == m1, lane, LANES), axis=-1, keepdims=True)
    rest = jnp.where(lane == i1, NEG, logits)
    m2 = jnp.max(rest, axis=-1, keepdims=True)
    i2 = jnp.min(jnp.where(rest == m2, lane, LANES), axis=-1, keepdims=True)
    e = jnp.exp(m2 - m1)
    w1 = 1.0 / (1.0 + e)
    return (jnp.where(lane == 0, i1, jnp.where(lane == 1, i2, 0)),
            jnp.where(lane == 0, w1, jnp.where(lane == 1, e * w1, 0.0)))


def _merge_kernel(x_ref, h_ref, om_ref, od_ref, wg_ref, wbm_ref, wbd_ref, wo_ref, g_ref, *rest, route):
    gates = jnp.dot(h_ref[...], wg_ref[...], preferred_element_type=F32)
    bm = jnp.dot(om_ref[...], wbm_ref[...], preferred_element_type=F32)
    bd = jnp.dot(od_ref[...], wbd_ref[...], preferred_element_type=F32)
    merged = _sigmoid(gates[:, :D_MODEL]) * bm + _sigmoid(gates[:, D_MODEL:]) * bd
    x1 = x_ref[...] + jnp.dot(merged.astype(BF16), wo_ref[...], preferred_element_type=F32)
    hn = _rms(x1, g_ref[...])
    if route:
        wr_ref, br_ref, x_out, h_out, idx_out, wt_out = rest
        _to_token_tiles(h_out, hn)
        hi = hn.astype(BF16)
        lo = (hn - hi.astype(F32)).astype(BF16)
        w = wr_ref[...]
        w_hi = w.astype(BF16)
        w_lo = (w - w_hi.astype(F32)).astype(BF16)
        both = jnp.dot(hi, jnp.concatenate([w_hi, w_lo], axis=1), preferred_element_type=F32)
        logits = (both[:, :LANES] + both[:, LANES:]
                  + jnp.dot(lo, w_hi, preferred_element_type=F32) + br_ref[...])
        idx_out[...], wt_out[...] = _top2(logits)
    else:
        x_out, h_out = rest
        h_out[...] = hn.astype(h_out.dtype)
    x_out[...] = x1


def _merge(x, h, o_mla, o_dil, wg, wbm, wbd, wo, g_next, router=None, tm=512):
    t = x.shape[0]
    row = lambda w: pl.BlockSpec((tm, w), lambda i: (i, 0))
    full = lambda a: pl.BlockSpec(a.shape, lambda i: (0, 0))
    in_specs = [row(D_MODEL), row(D_MODEL), row(o_mla.shape[1]), row(o_dil.shape[1]),
                full(wg), full(wbm), full(wbd), full(wo), full(g_next)]
    args = [x, h, o_mla, o_dil, wg, wbm, wbd, wo, g_next]
    out_specs = [row(D_MODEL)]
    out_shape = [jax.ShapeDtypeStruct((t, D_MODEL), F32)]
    if router is None:
        out_specs += [row(D_MODEL)]
        out_shape += [jax.ShapeDtypeStruct((t, D_MODEL), BF16)]
    else:
        in_specs += [full(router[0]), full(router[1])]
        args += list(router)
        out_specs += [pl.BlockSpec((tm * ROW_TILES, LANES), lambda i: (i, 0)), row(LANES), row(LANES)]
        out_shape += [jax.ShapeDtypeStruct((t * ROW_TILES, LANES), F32),
                      jax.ShapeDtypeStruct((t, LANES), jnp.int32), jax.ShapeDtypeStruct((t, LANES), F32)]
    return pl.pallas_call(
        functools.partial(_merge_kernel, route=router is not None),
        grid=(t // tm,),
        in_specs=in_specs,
        out_specs=out_specs,
        out_shape=out_shape,
        compiler_params=_cparams(("parallel",)),
        name="merge",
    )(*args)


def _ffn_kernel(x_ref, h_ref, wg_ref, wu_ref, wd_ref, o_ref):
    @pl.when(pl.program_id(1) == 0)
    def _():
        o_ref[...] = x_ref[...]

    h = h_ref[...]
    a = jnp.dot(h, wg_ref[...], preferred_element_type=F32)
    u = jnp.dot(h, wu_ref[...], preferred_element_type=F32)
    o_ref[...] += jnp.dot((a * _sigmoid(a) * u).astype(BF16), wd_ref[...], preferred_element_type=F32)


def _ffn(x, h, wg, wu, wd, tm=512, tf=1792):
    t = x.shape[0]
    nf = wg.shape[1] // tf
    return pl.pallas_call(
        _ffn_kernel,
        grid=(t // tm, nf),
        in_specs=[pl.BlockSpec((tm, D_MODEL), lambda i, f: (i, 0)),
                  pl.BlockSpec((tm, D_MODEL), lambda i, f: (i, 0)),
                  pl.BlockSpec((D_MODEL, tf), lambda i, f: (0, f)),
                  pl.BlockSpec((D_MODEL, tf), lambda i, f: (0, f)),
                  pl.BlockSpec((tf, D_MODEL), lambda i, f: (f, 0))],
        out_specs=pl.BlockSpec((tm, D_MODEL), lambda i, f: (i, 0)),
        out_shape=jax.ShapeDtypeStruct((t, D_MODEL), F32),
        compiler_params=_cparams(("parallel", "arbitrary")),
        name="dense_ffn",
    )(x, h, wg, wu, wd)


def _tile_at(ref, row8):
    return ref.at[pl.ds(pl.multiple_of(row8, ROW_TILES), ROW_TILES), :]


def _dispatch_kernel(pos_ref, fill_ref, h_ref, xs_hbm, zero_ref, sem, sem_fill, *, t, tmd, tm):
    i = pl.program_id(0)

    @pl.when(i == 0)
    def _():
        zero_ref[...] = jnp.zeros_like(zero_ref)
        for e in range(fill_ref.shape[0]):
            @pl.when(fill_ref[e] >= 0)
            def _():
                cp = pltpu.make_async_copy(
                    zero_ref, xs_hbm.at[pl.ds(pl.multiple_of(fill_ref[e], ROW_TILES), tm * ROW_TILES), :], sem_fill)
                cp.start()
                cp.wait()

    for k in range(TOP_K):
        for r in range(tmd):
            pltpu.make_async_copy(_tile_at(h_ref, r * ROW_TILES),
                                  _tile_at(xs_hbm, pos_ref[k * t + i * tmd + r]), sem.at[k]).start(priority=r % 2)
    for k in range(TOP_K):
        pltpu.make_async_copy(h_ref, h_ref, sem.at[k]).wait()


def _dispatch(h, pos8, fill8, n_rows, tm, tmd=512):
    t = h.shape[0] // ROW_TILES
    return pl.pallas_call(
        functools.partial(_dispatch_kernel, t=t, tmd=tmd, tm=tm),
        grid_spec=pltpu.PrefetchScalarGridSpec(
            num_scalar_prefetch=2,
            grid=(t // tmd,),
            in_specs=[pl.BlockSpec((tmd * ROW_TILES, LANES), lambda i, pos, fill: (i, 0))],
            out_specs=pl.BlockSpec(memory_space=pl.ANY),
            scratch_shapes=[pltpu.VMEM((tm * ROW_TILES, LANES), F32),
                            pltpu.SemaphoreType.DMA((TOP_K,)), pltpu.SemaphoreType.DMA(())]),
        out_shape=jax.ShapeDtypeStruct((n_rows * ROW_TILES, LANES), F32),
        compiler_params=_cparams(("arbitrary",)),
        name="moe_dispatch",
    )(pos8, fill8, h)


def _moe_ffn_kernel(te_ref, nv_ref, x_ref, wg_ref, wu_ref, wd_ref, o_ref, acc_ref, xb_ref, *, tm):
    i = pl.program_id(0)
    f = pl.program_id(1)
    valid = i < nv_ref[0]
    last = f == pl.num_programs(1) - 1

    @pl.when(valid & (f == 0))
    def _():
        xb_ref[...] = _from_token_tiles(x_ref, tm).astype(BF16)
        acc_ref[...] = jnp.zeros_like(acc_ref)

    @pl.when(valid)
    def _():
        h = xb_ref[...]
        a = jnp.dot(h, wg_ref[0], preferred_element_type=F32)
        u = jnp.dot(h, wu_ref[0], preferred_element_type=F32)
        acc_ref[...] += jnp.dot((a * _sigmoid(a) * u).astype(BF16), wd_ref[0], preferred_element_type=F32)

    @pl.when(valid & last)
    def _():
        _to_token_tiles(o_ref, acc_ref[...])

    @pl.when(jnp.logical_not(valid) & last)
    def _():
        o_ref[...] = jnp.zeros_like(o_ref)


def _moe_ffn(xs, tile_expert, n_valid, wg, wu, wd, tm, tf=1792):
    n_tiles = xs.shape[0] // (tm * ROW_TILES)
    nf = wg.shape[2] // tf

    def live(i, f, te, nv):
        ok = i < nv[0]
        return jnp.where(ok, i, nv[0] - 1), jnp.where(ok, f, nf - 1)

    return pl.pallas_call(
        functools.partial(_moe_ffn_kernel, tm=tm),
        grid_spec=pltpu.PrefetchScalarGridSpec(
            num_scalar_prefetch=2,
            grid=(n_tiles, nf),
            in_specs=[pl.BlockSpec((tm * ROW_TILES, LANES), lambda *a: (live(*a)[0], 0)),
                      pl.BlockSpec((1, D_MODEL, tf), lambda *a: (a[2][live(*a)[0]], 0, live(*a)[1])),
                      pl.BlockSpec((1, D_MODEL, tf), lambda *a: (a[2][live(*a)[0]], 0, live(*a)[1])),
                      pl.BlockSpec((1, tf, D_MODEL), lambda *a: (a[2][live(*a)[0]], live(*a)[1], 0))],
            out_specs=pl.BlockSpec((tm * ROW_TILES, LANES), lambda i, f, te, nv: (i, 0)),
            scratch_shapes=[pltpu.VMEM((tm, D_MODEL), F32), pltpu.VMEM((tm, D_MODEL), BF16)]),
        out_shape=jax.ShapeDtypeStruct(xs.shape, F32),
        compiler_params=_cparams(("arbitrary", "arbitrary")),
        name="moe_ffn",
    )(tile_expert, n_valid, xs, wg, wu, wd)


def _router_params(w_router, b_router):
    w_pad = jnp.zeros((D_MODEL, LANES), F32).at[:, :N_EXPERTS].set(w_router)
    b_pad = jnp.full((1, LANES), NEG, F32).at[0, :N_EXPERTS].set(b_router)
    return w_pad, b_pad


def _moe(h, idx_l, wg, wu, wd, tm=448):
    t = h.shape[0] // ROW_TILES
    expert = idx_l[:, :TOP_K].T.reshape(-1)
    onehot = (expert[:, None] == jnp.arange(N_EXPERTS)[None, :]).astype(jnp.int32)
    csum = jnp.cumsum(onehot, axis=0)
    counts = csum[-1]
    rank = jnp.sum((csum - onehot) * onehot, axis=1)
    tiles = (counts + tm - 1) // tm
    tile_end = jnp.cumsum(tiles)
    pad_off = (tile_end - tiles) * tm
    pos8 = (pad_off[expert] + rank) * ROW_TILES
    n_tiles = (TOP_K * t) // tm + N_EXPERTS
    n_valid = tile_end[-1:].astype(jnp.int32)
    tile_ids = jnp.minimum(jnp.arange(n_tiles), n_valid[0] - 1)
    tile_expert = jnp.sum(tile_end[None, :] <= tile_ids[:, None], axis=1).astype(jnp.int32)
    spare = n_valid[0] + jnp.arange(N_EXPERTS)
    fill8 = jnp.concatenate([jnp.where(tiles > 0, tile_end - 1, -1), jnp.where(spare < n_tiles, spare, -1)])
    fill8 = jnp.where(fill8 >= 0, fill8 * (tm * ROW_TILES), -1).astype(jnp.int32)
    xs = _dispatch(h, pos8, fill8, n_tiles * tm, tm)
    ys = _moe_ffn(xs, tile_expert, n_valid, wg, wu, wd, tm)
    return ys, pos8


def _ple_body(x, p_ref, wg_ref, wp_ref, g_ref, gn_ref, outs):
    gate = _sigmoid(jnp.dot(_rms(x, g_ref[...]).astype(BF16), wg_ref[...], preferred_element_type=F32))
    x2 = x + gate * jnp.dot(p_ref[...].astype(BF16), wp_ref[...], preferred_element_type=F32)
    if len(outs) == 2:
        outs[0][...] = x2
    outs[-1][...] = _rms(x2, gn_ref[...]).astype(outs[-1].dtype)


def _ple_kernel(x_ref, p_ref, wg_ref, wp_ref, g_ref, gn_ref, *outs):
    _ple_body(x_ref[...], p_ref, wg_ref, wp_ref, g_ref, gn_ref, outs)


def _ple_moe_kernel(pos_ref, x_ref, p_ref, wt_ref, wg_ref, wp_ref, g_ref, gn_ref, y_hbm, *rest, t, tm):
    *outs, ybuf, sem = rest
    i = pl.program_id(0)
    slot = i % 2

    def gather(block, s):
        for k in range(TOP_K):
            for r in range(tm):
                pltpu.make_async_copy(_tile_at(y_hbm, pos_ref[k * t + block * tm + r]),
                                      _tile_at(ybuf.at[s, k], r * ROW_TILES), sem.at[s]).start(priority=r % 2)

    @pl.when(i == 0)
    def _():
        gather(0, 0)

    @pl.when(i + 1 < pl.num_programs(0))
    def _():
        gather(i + 1, 1 - slot)

    pltpu.make_async_copy(ybuf.at[slot], ybuf.at[slot], sem.at[slot]).wait()
    wt = wt_ref[...]
    x = (x_ref[...] + wt[:, 0:1] * _from_token_tiles(ybuf.at[slot, 0], tm)
         + wt[:, 1:2] * _from_token_tiles(ybuf.at[slot, 1], tm))
    _ple_body(x, p_ref, wg_ref, wp_ref, g_ref, gn_ref, outs)


def _ple(x, p, wg, wp, g, g_next, h_dtype, want_x, moe=None, tm=512):
    t = x.shape[0]
    n_out = 2 if want_x else 1
    out_shape = ([jax.ShapeDtypeStruct((t, D_MODEL), F32)] * (n_out - 1)
                 + [jax.ShapeDtypeStruct((t, D_MODEL), h_dtype)])
    if moe is None:
        row = lambda w: pl.BlockSpec((tm, w), lambda i: (i, 0))
        full = lambda a: pl.BlockSpec(a.shape, lambda i: (0, 0))
        return pl.pallas_call(
            _ple_kernel,
            grid=(t // tm,),
            in_specs=[row(D_MODEL), row(P_DIM), full(wg), full(wp), full(g), full(g_next)],
            out_specs=[row(D_MODEL)] * n_out,
            out_shape=out_shape,
            compiler_params=_cparams(("parallel",)),
            name="ple",
        )(x, p, wg, wp, g, g_next)
    ys, pos8, wt = moe
    tm = tm // 2
    row = lambda w: pl.BlockSpec((tm, w), lambda i, pos: (i, 0))
    full = lambda a: pl.BlockSpec(a.shape, lambda i, pos: (0, 0))
    return pl.pallas_call(
        functools.partial(_ple_moe_kernel, t=t, tm=tm),
        grid_spec=pltpu.PrefetchScalarGridSpec(
            num_scalar_prefetch=1,
            grid=(t // tm,),
            in_specs=[row(D_MODEL), row(P_DIM), row(LANES), full(wg), full(wp), full(g), full(g_next),
                      pl.BlockSpec(memory_space=pl.ANY)],
            out_specs=[row(D_MODEL)] * n_out,
            scratch_shapes=[pltpu.VMEM((2, TOP_K, tm * ROW_TILES, LANES), F32), pltpu.SemaphoreType.DMA((2,))]),
        out_shape=out_shape,
        compiler_params=_cparams(("arbitrary",)),
        name="ple_moe",
    )(pos8, x, p, wt, wg, wp, g, g_next, ys)


def _rot_cols(w):
    half = w.shape[-1] // 2
    return jnp.concatenate([-w[:, half:], w[:, :half]], axis=-1)


def _layer_weights(i, w_in, w_uq, w_ukv):
    wi = w_in[i]
    z = lambda n: jnp.zeros((wi.shape[0], n), F32)
    w_kr = wi[:, IN_OFF[2]:IN_OFF[3]]
    pad = MLA_SLAB - MLA_NOPE - MLA_ROPE
    wlat = jnp.concatenate([wi[:, :IN_OFF[2]], z(MLA_NOPE), w_kr, z(pad), z(MLA_NOPE), _rot_cols(w_kr), z(pad)],
                           axis=1).astype(BF16)
    qscale = jnp.concatenate([jnp.full((DIL_WIDTH,), DIL_HD ** -0.5, F32), jnp.ones((2 * DIL_WIDTH,), F32)])
    wdil = (wi[:, IN_OFF[3]:IN_OFF[6]] * qscale).astype(BF16)
    wgate = wi[:, IN_OFF[6]:].astype(BF16)
    uq = w_uq[i].reshape(MLA_Q_RANK, MLA_HEADS, MLA_NOPE + MLA_ROPE)
    zq = lambda n: jnp.zeros((MLA_Q_RANK, MLA_HEADS, n), F32)
    rope_rot = jnp.concatenate([-uq[..., MLA_NOPE + MLA_ROPE // 2:], uq[..., MLA_NOPE:MLA_NOPE + MLA_ROPE // 2]], -1)
    wq = jnp.concatenate([uq, zq(pad)], axis=-1).reshape(MLA_Q_RANK, -1).astype(BF16)
    wqr = jnp.concatenate([zq(MLA_NOPE), rope_rot, zq(pad)], axis=-1).reshape(MLA_Q_RANK, -1).astype(BF16)
    ukv = w_ukv[i].reshape(MLA_KV_RANK, MLA_HEADS, MLA_NOPE + MLA_V)
    zkv = jnp.zeros((MLA_KV_RANK, MLA_HEADS, MLA_SLAB - MLA_NOPE), F32)
    wk = jnp.concatenate([ukv[..., :MLA_NOPE], zkv], axis=-1).reshape(MLA_KV_RANK, -1).astype(BF16)
    wv = jnp.concatenate([ukv[..., MLA_NOPE:], zkv], axis=-1).reshape(MLA_KV_RANK, -1).astype(BF16)
    return wlat, wdil, wgate, wq, wqr, wk, wv


def kernel(x, p, positions, attn_norm, w_in, q_norm, w_uq, kv_norm, w_ukv, w_br_mla, w_br_dil, w_out, ffn_norm, dense_w_gate, dense_w_up, dense_w_down, router_w, router_b, moe_w_gate, moe_w_up, moe_w_down, ple_norm, ple_w_gate, ple_w_proj, final_norm):
    batch, seq, d = x.shape
    t = batch * seq
    depth = w_in.shape[0]
    xf = x.reshape(t, d)
    cos_t, sin_t = _rope_tables(positions)
    slopes = _alibi_slopes(DIL_HEADS)
    row = lambda v: v.reshape(1, -1)
    h = _norm(xf, attn_norm[0])
    for i in range(depth):
        wlat, wdil, wgate, wq, wqr, wk, wv = _layer_weights(i, w_in, w_uq, w_ukv)
        q, k, v = _mla_proj(h, wlat, wq, wqr, wk, wv, row(q_norm[i]), row(kv_norm[i]), cos_t, sin_t)
        o_mla = _mla_attn(q, k, v, batch, seq)
        qkv_d = [_dil_proj(h, wdil, seq, g) for g in range(DIL_GROUPS)]
        o_dil = _dil_attn(qkv_d, slopes, batch, seq)
        moe_layer = i % 2 == 1
        j = i // 2
        xf, h2, *routing = _merge(xf, h, o_mla, o_dil, wgate, w_br_mla[i].astype(BF16), w_br_dil[i].astype(BF16),
                                  w_out[i].astype(BF16), row(ffn_norm[i]),
                                  _router_params(router_w[j], router_b[j]) if moe_layer else None)
        moe = None
        if moe_layer:
            idx_l, wt_l = routing
            moe = _moe(h2, idx_l, moe_w_gate[j].astype(BF16), moe_w_up[j].astype(BF16),
                       moe_w_down[j].astype(BF16)) + (wt_l,)
        else:
            xf = _ffn(xf, h2, dense_w_gate[j].astype(BF16), dense_w_up[j].astype(BF16),
                      dense_w_down[j].astype(BF16))
        last = i == depth - 1
        g_next = final_norm if last else attn_norm[i + 1]
        *xnext, h = _ple(xf, p[i].reshape(t, -1), ple_w_gate[i].astype(BF16), ple_w_proj[i].astype(BF16),
                         row(ple_norm[i]), row(g_next), F32 if last else BF16, not last, moe)
        xf = xnext[0] if xnext else None
    return h.reshape(batch, seq, d)
```

```python
import functools

import numpy as np
import jax
import jax.numpy as jnp
from jax import lax
from jax.experimental import pallas as pl
from jax.experimental.pallas import tpu as pltpu

F32 = jnp.float32
BF16 = jnp.bfloat16

D_MODEL = 1024
P_DIM = 256
NORM_EPS = 1e-6
DEPTH = 2

MLA_HEADS = 8
MLA_Q_RANK = 384
MLA_KV_RANK = 256
MLA_NOPE = 64
MLA_ROPE = 32
MLA_V = 64
ROPE_THETA = 10000.0
MLA_SLAB = 128

DIL_CONFIGS = ((128, 1), (512, 4), (2048, 16))
DIL_GROUPS = 3
DIL_HPG = 8
DIL_HEADS = 24
DIL_HD = 64
DIL_WIDTH = DIL_HEADS * DIL_HD
DIL_SPAN = 128

D_FF = 3584
N_EXPERTS = 8
TOP_K = 2

LANES = 128
ROW_TILES = D_MODEL // LANES
NEG = -1e30
LOG2E = 1.4426950408889634
VMEM_LIMIT = 56 * 1024 * 1024

IN_OFF = tuple(int(o) for o in np.cumsum((0, MLA_Q_RANK, MLA_KV_RANK, MLA_ROPE, DIL_WIDTH, DIL_WIDTH,
                                          DIL_WIDTH, D_MODEL, D_MODEL)))


def _cparams(sem):
    return pltpu.CompilerParams(dimension_semantics=sem, vmem_limit_bytes=VMEM_LIMIT)


def _rms(x, g):
    return x * lax.rsqrt(jnp.mean(x * x, axis=-1, keepdims=True) + NORM_EPS) * g


def _sigmoid(x):
    return 1.0 / (1.0 + jnp.exp(-x))


def _alibi_slopes(n):
    def pow2(m):
        start = 2.0 ** (-8.0 / m)
        return [start ** (i + 1) for i in range(m)]
    if float(np.log2(n)).is_integer():
        s = pow2(n)
    else:
        c = 2 ** int(np.floor(np.log2(n)))
        s = pow2(c) + pow2(2 * c)[0::2][: n - c]
    return jnp.asarray(sorted(s, reverse=True), dtype=F32)


def _norm_kernel(x_ref, g_ref, o_ref):
    o_ref[...] = _rms(x_ref[...], g_ref[...]).astype(o_ref.dtype)


def _norm(x, g, tm=1024):
    t, d = x.shape
    return pl.pallas_call(
        _norm_kernel,
        grid=(t // tm,),
        in_specs=[pl.BlockSpec((tm, d), lambda i: (i, 0)), pl.BlockSpec((1, d), lambda i: (0, 0))],
        out_specs=pl.BlockSpec((tm, d), lambda i: (i, 0)),
        out_shape=jax.ShapeDtypeStruct((t, d), BF16),
        compiler_params=_cparams(("parallel",)),
        name="rms_norm",
    )(x, g.reshape(1, d))


def _rope_table_kernel(pos_ref, invf_ref, cos_ref, sin_ref):
    ang = pos_ref[...].astype(F32) * invf_ref[...]
    lane = lax.broadcasted_iota(jnp.int32, ang.shape, 1)
    rope_lane = (lane >= MLA_NOPE) & (lane < MLA_NOPE + MLA_ROPE)
    cos_ref[...] = jnp.where(lane < MLA_NOPE, 1.0, jnp.where(rope_lane, jnp.cos(ang), 0.0))
    sin_ref[...] = jnp.where(rope_lane, jnp.sin(ang), 0.0)


def _rope_tables(positions, tm=2048):
    t = positions.size
    half = MLA_ROPE // 2
    inv_freq = ROPE_THETA ** (-jnp.arange(half, dtype=F32) / half)
    invf = jnp.zeros((1, LANES), F32).at[0, MLA_NOPE:MLA_NOPE + MLA_ROPE].set(jnp.concatenate([inv_freq, inv_freq]))
    pos_b = jnp.broadcast_to(positions.reshape(t, 1), (t, LANES))
    spec = pl.BlockSpec((tm, LANES), lambda i: (i, 0))
    return pl.pallas_call(
        _rope_table_kernel,
        grid=(t // tm,),
        in_specs=[spec, pl.BlockSpec((1, LANES), lambda i: (0, 0))],
        out_specs=[spec, spec],
        out_shape=[jax.ShapeDtypeStruct((t, LANES), F32)] * 2,
        compiler_params=_cparams(("parallel",)),
        name="rope_tables",
    )(pos_b, invf)


def _mla_proj_kernel(h_ref, wlat_ref, wq_ref, wqr_ref, wk_ref, wv_ref, qn_ref, kvn_ref, cos_ref, sin_ref,
                     q_out, k_out, v_out):
    lat = jnp.dot(h_ref[...], wlat_ref[...], preferred_element_type=F32)
    cqn = _rms(lat[:, :MLA_Q_RANK], qn_ref[...]).astype(BF16)
    ckvn = _rms(lat[:, MLA_Q_RANK:MLA_Q_RANK + MLA_KV_RANK], kvn_ref[...]).astype(BF16)
    cos = cos_ref[...]
    sin = sin_ref[...]
    o = MLA_Q_RANK + MLA_KV_RANK
    k_rope = lat[:, o:o + LANES] * cos + lat[:, o + LANES:o + 2 * LANES] * sin
    qa = jnp.dot(cqn, wq_ref[...], preferred_element_type=F32)
    qb = jnp.dot(cqn, wqr_ref[...], preferred_element_type=F32)
    kk = jnp.dot(ckvn, wk_ref[...], preferred_element_type=F32)
    scale = (MLA_NOPE + MLA_ROPE) ** -0.5 * LOG2E
    vv = jnp.dot(ckvn, wv_ref[...], preferred_element_type=F32)
    ones_lane = lax.broadcasted_iota(jnp.int32, cos.shape, 1) >= MLA_V
    for hd in range(MLA_HEADS):
        sl = slice(hd * MLA_SLAB, (hd + 1) * MLA_SLAB)
        q_out[:, sl] = ((qa[:, sl] * cos + qb[:, sl] * sin) * scale).astype(BF16)
        k_out[:, sl] = (kk[:, sl] + k_rope).astype(BF16)
        v_out[:, sl] = jnp.where(ones_lane, 1.0, vv[:, sl]).astype(BF16)


def _mla_proj(h, wlat, wq, wqr, wk, wv, qn, kvn, cos_t, sin_t, tm=512):
    t = h.shape[0]
    row = lambda w: pl.BlockSpec((tm, w), lambda i: (i, 0))
    full = lambda a: pl.BlockSpec(a.shape, lambda i: (0, 0))
    hs = MLA_HEADS * MLA_SLAB
    return pl.pallas_call(
        _mla_proj_kernel,
        grid=(t // tm,),
        in_specs=[row(D_MODEL), full(wlat), full(wq), full(wqr), full(wk), full(wv), full(qn), full(kvn),
                  row(LANES), row(LANES)],
        out_specs=[row(hs), row(hs), row(hs)],
        out_shape=[jax.ShapeDtypeStruct((t, hs), BF16)] * 3,
        compiler_params=_cparams(("parallel",)),
        name="mla_proj",
    )(h, wlat, wq, wqr, wk, wv, qn, kvn, cos_t, sin_t)


MLA_ROW_SPLIT = 2


def _mla_attn_kernel(q_ref, k_ref, v_ref, o_ref, s_scr, *, tq):
    qi = pl.program_id(2)
    nh = q_ref.shape[1] // MLA_SLAB
    tr = tq // MLA_ROW_SPLIT
    rq = lax.broadcasted_iota(jnp.int32, (tr, tq), 0)
    ck = lax.broadcasted_iota(jnp.int32, (tr, tq), 1)
    nl = tq // LANES

    def tile(n):
        def pass1(hh, rg):
            rows = slice(rg * tr, (rg + 1) * tr)
            q = q_ref[rows, hh * MLA_SLAB:(hh + 1) * MLA_SLAB]
            mx = None
            for c in range(n + 1):
                ks = k_ref[c * tq:(c + 1) * tq, hh * MLA_SLAB:(hh + 1) * MLA_SLAB]
                s = lax.dot_general(q, ks, (((1,), (1,)), ((), ())), preferred_element_type=F32)
                if c == n:
                    s = jnp.where(ck <= rq + rg * tr, s, NEG)
                s_scr[hh, c, rows, :] = s
                parts = [s[:, i * LANES:(i + 1) * LANES] for i in range(nl)]
                mx = functools.reduce(jnp.maximum, parts if mx is None else [mx] + parts)
            return jnp.max(mx, axis=-1, keepdims=True)

        def pass2(hh, rg, m):
            rows = slice(rg * tr, (rg + 1) * tr)
            acc = None
            for c in range(n + 1):
                vs = v_ref[c * tq:(c + 1) * tq, hh * MLA_SLAB:(hh + 1) * MLA_SLAB]
                p = jnp.exp2((s_scr[hh, c, rows, :] - m).astype(BF16))
                pv = jnp.dot(p, vs, preferred_element_type=F32)
                acc = pv if acc is None else acc + pv
            o = acc / acc[:, MLA_V:MLA_V + 1]
            o_ref[rows, hh * MLA_V:(hh + 1) * MLA_V] = o[:, :MLA_V].astype(o_ref.dtype)

        chains = [(hh, rg) for hh in range(nh) for rg in range(MLA_ROW_SPLIT)]
        ms = [pass1(*chains[0])]
        for i, chain in enumerate(chains):
            if i + 1 < len(chains):
                ms.append(pass1(*chains[i + 1]))
            pass2(*chain, ms[i])

    for n in range(s_scr.shape[1]):
        pl.when(qi == n)(functools.partial(tile, n))


def _mla_attn(q, k, v, batch, seq, tq=512, nh=4):
    t = q.shape[0]
    nq = seq // tq
    return pl.pallas_call(
        functools.partial(_mla_attn_kernel, tq=tq),
        grid=(batch, MLA_HEADS // nh, nq),
        in_specs=[pl.BlockSpec((tq, nh * MLA_SLAB), lambda b, p, i: (b * nq + i, p)),
                  pl.BlockSpec((seq, nh * MLA_SLAB), lambda b, p, i: (b, p)),
                  pl.BlockSpec((seq, nh * MLA_SLAB), lambda b, p, i: (b, p))],
        out_specs=pl.BlockSpec((tq, nh * MLA_V), lambda b, p, i: (b * nq + i, p)),
        out_shape=jax.ShapeDtypeStruct((t, MLA_HEADS * MLA_V), BF16),
        scratch_shapes=[pltpu.VMEM((nh, nq, tq, tq), F32)],
        compiler_params=_cparams(("parallel", "parallel", "arbitrary")),
        name="mla_attn",
    )(q, k, v)


PHASE_STRIDE = 4


def _phase_of_slot(r):
    if r <= PHASE_STRIDE:
        return list(range(r))
    f2 = r // PHASE_STRIDE
    return [c1 + PHASE_STRIDE * c2 for c1 in range(PHASE_STRIDE) for c2 in range(f2)]


def _dil_proj_kernel(h_ref, w_ref, o_ref, acc_ref, tmp_ref, *, seq, r):
    nl = o_ref.shape[0]

    @pl.when(pl.program_id(0) == 0)
    def _():
        acc_ref[...] = jnp.zeros_like(acc_ref)

    res = jnp.dot(h_ref[...], w_ref[...], preferred_element_type=F32)
    f1 = min(r, PHASE_STRIDE)
    l1 = seq // f1
    dst = o_ref if r == f1 else tmp_ref
    for c1 in range(f1):
        for j in range(nl):
            src = acc_ref[j, pl.ds(c1, l1, stride=f1), :] if f1 > 1 else acc_ref[j]
            dst[j, c1 * l1:(c1 + 1) * l1, :] = src.astype(dst.dtype)
    if r > f1:
        f2 = r // f1
        l2 = l1 // f2
        for s in range(r):
            c1, c2 = divmod(s, f2)
            for j in range(nl):
                o_ref[j, s * l2:(s + 1) * l2, :] = (
                    tmp_ref[j, pl.ds(c1 * l1 + c2, l2, stride=f2), :].astype(o_ref.dtype))
    for j in range(nl):
        acc_ref[j] = res[:, j * LANES:(j + 1) * LANES]


def _dil_proj(h, w, seq, g, tn=512):
    t = h.shape[0]
    nb = t // seq
    nl = tn // LANES
    n_blocks = 3 * nb
    r = DIL_CONFIGS[g][1]

    def mm(s):
        b, tt = divmod(jnp.minimum(s, n_blocks - 1), 3)
        return tt, b

    def wr(s):
        b, tt = divmod(jnp.maximum(s - 1, 0), 3)
        return tt, b

    return pl.pallas_call(
        functools.partial(_dil_proj_kernel, seq=seq, r=r),
        grid=(n_blocks + 1,),
        in_specs=[pl.BlockSpec((seq, D_MODEL), lambda s: (mm(s)[1], 0)),
                  pl.BlockSpec((D_MODEL, tn), lambda s: (0, mm(s)[0] * DIL_GROUPS + g))],
        out_specs=pl.BlockSpec((nl, seq, LANES), lambda s: (wr(s)[0], wr(s)[1], 0)),
        out_shape=jax.ShapeDtypeStruct((3 * nl, t, LANES), BF16),
        scratch_shapes=[pltpu.VMEM((nl, seq, LANES), F32)] * 2,
        compiler_params=_cparams(("arbitrary",)),
        name="dil_proj",
    )(h, w)


def _dil_attn_kernel(slopes_ref, q0, q1, q2, k0, k1, k2, v0, v1, v2, o_ref, acc_s, m_s, l_s, *, seq):
    pair = pl.program_id(1)
    sp = DIL_SPAN
    head0 = lax.broadcasted_iota(jnp.int32, (sp, LANES), 1) < DIL_HD
    qi = lax.broadcasted_iota(jnp.int32, (sp, 2 * sp), 0)
    kj = lax.broadcasted_iota(jnp.int32, (sp, 2 * sp), 1)
    dist_w = qi + sp - kj
    valid_w = (dist_w >= 0) & (dist_w <= sp)
    dist_1 = dist_w[:, sp:]
    valid_1 = dist_1 >= 0
    qs, ks, vs = (q0, q1, q2), (k0, k1, k2), (v0, v1, v2)
    for g, (_, r) in enumerate(DIL_CONFIGS):
        ln = seq // r
        nb = ln // sp
        sl = [slopes_ref[g * DIL_HPG + 2 * pair + hh] * float(r) for hh in range(2)]
        bias_w = jnp.concatenate([jnp.where(valid_w, -s * dist_w.astype(F32), NEG) for s in sl], axis=0)
        bias_1 = jnp.concatenate([jnp.where(valid_1, -s * dist_1.astype(F32), NEG) for s in sl], axis=0)
        for slot_i, c in enumerate(_phase_of_slot(r)):
            for i in range(nb):
                row0 = slot_i * ln + i * sp
                qb = qs[g][0, row0:row0 + sp, :]
                zero = jnp.zeros_like(qb)
                q2h = jnp.concatenate([jnp.where(head0, qb, zero), jnp.where(head0, zero, qb)], axis=0)
                lo = row0 if i == 0 else row0 - sp
                kw = ks[g][0, lo:row0 + sp, :]
                vw = vs[g][0, lo:row0 + sp, :]
                s = lax.dot_general(q2h, kw, (((1,), (1,)), ((), ())), preferred_element_type=F32)
                s = s + (bias_1 if i == 0 else bias_w)
                m = jnp.max(s, axis=-1, keepdims=True)
                p = jnp.exp(s - m)
                l = jnp.sum(p, axis=-1, keepdims=True)
                o2 = jnp.dot(p.astype(BF16), vw, preferred_element_type=F32)
                dst = pl.ds(i * sp * r + c, sp, stride=r) if r > 1 else pl.ds(i * sp, sp)
                acc_s[g, dst, :] = jnp.where(head0, o2[:sp], o2[sp:])
                m_s[g, dst, :] = jnp.where(head0, m[:sp], m[sp:])
                l_s[g, dst, :] = jnp.where(head0, l[:sp], l[sp:])

    ch = 256

    def combine(i, _):
        rows = pl.ds(pl.multiple_of(i * ch, ch), ch)
        ms = [m_s[g, rows, :] for g in range(DIL_GROUPS)]
        mx = jnp.maximum(jnp.maximum(ms[0], ms[1]), ms[2])
        ws = [jnp.exp(m - mx) for m in ms]
        num = sum(ws[g] * acc_s[g, rows, :] for g in range(DIL_GROUPS))
        den = sum(ws[g] * l_s[g, rows, :] for g in range(DIL_GROUPS))
        o_ref[rows, :] = (num / den).astype(o_ref.dtype)
        return 0

    lax.fori_loop(0, seq // ch, combine, 0)


def _dil_attn(qkvs, slopes, batch, seq):
    t = qkvs[0].shape[1]
    pairs = DIL_HPG // 2

    def spec(tt):
        return pl.BlockSpec((1, seq, LANES), lambda b, p: (tt * pairs + p, b, 0))

    in_specs = [pl.BlockSpec(memory_space=pltpu.SMEM)] + [spec(tt) for tt in range(3) for _ in range(DIL_GROUPS)]
    return pl.pallas_call(
        functools.partial(_dil_attn_kernel, seq=seq),
        grid=(batch, pairs),
        in_specs=in_specs,
        out_specs=pl.BlockSpec((seq, LANES), lambda b, p: (b, p)),
        out_shape=jax.ShapeDtypeStruct((t, DIL_HPG * DIL_HD), BF16),
        scratch_shapes=[pltpu.VMEM((DIL_GROUPS, seq, LANES), F32)] * 3,
        compiler_params=_cparams(("parallel", "parallel")),
        name="dil_attn",
    )(slopes, *[qkvs[g] for _ in range(3) for g in range(DIL_GROUPS)])


def _to_token_tiles(dst_ref, val):
    n = val.shape[0]
    for s in range(ROW_TILES):
        dst_ref[pl.ds(s, n, stride=ROW_TILES), :] = val[:, s * LANES:(s + 1) * LANES]


def _from_token_tiles(src_ref, n):
    return jnp.concatenate([src_ref[pl.ds(s, n, stride=ROW_TILES), :] for s in range(ROW_TILES)], axis=1)


def _top2(logits):
    lane = lax.broadcasted_iota(jnp.int32, logits.shape, 1)
    m1 = jnp.max(logits, axis=-1, keepdims=True)
    i1 = jnp.min(jnp.where(logits == m1, lane, LANES), axis=-1, keepdims=True)
    rest = jnp.where(lane == i1, NEG, logits)
    m2 = jnp.max(rest, axis=-1, keepdims=True)
    i2 = jnp.min(jnp.where(rest == m2, lane, LANES), axis=-1, keepdims=True)
    e = jnp.exp(m2 - m1)
    w1 = 1.0 / (1.0 + e)
    return (jnp.where(lane == 0, i1, jnp.where(lane == 1, i2, 0)),
            jnp.where(lane == 0, w1, jnp.where(lane == 1, e * w1, 0.0)))


def _merge_kernel(x_ref, h_ref, om_ref, od_ref, wg_ref, wbm_ref, wbd_ref, wo_ref, g_ref, *rest, route):
    gates = jnp.dot(h_ref[...], wg_ref[...], preferred_element_type=F32)
    bm = jnp.dot(om_ref[...], wbm_ref[...], preferred_element_type=F32)
    bd = jnp.dot(od_ref[...], wbd_ref[...], preferred_element_type=F32)
    merged = _sigmoid(gates[:, :D_MODEL]) * bm + _sigmoid(gates[:, D_MODEL:]) * bd
    x1 = x_ref[...] + jnp.dot(merged.astype(BF16), wo_ref[...], preferred_element_type=F32)
    hn = _rms(x1, g_ref[...])
    if route:
        wr_ref, br_ref, x_out, h_out, idx_out, wt_out = rest
        _to_token_tiles(h_out, hn)
        hi = hn.astype(BF16)
        lo = (hn - hi.astype(F32)).astype(BF16)
        w = wr_ref[...]
        w_hi = w.astype(BF16)
        w_lo = (w - w_hi.astype(F32)).astype(BF16)
        both = jnp.dot(hi, jnp.concatenate([w_hi, w_lo], axis=1), preferred_element_type=F32)
        logits = (both[:, :LANES] + both[:, LANES:]
                  + jnp.dot(lo, w_hi, preferred_element_type=F32) + br_ref[...])
        idx_out[...], wt_out[...] = _top2(logits)
    else:
        x_out, h_out = rest
        h_out[...] = hn.astype(h_out.dtype)
    x_out[...] = x1


def _merge(x, h, o_mla, o_dil, wg, wbm, wbd, wo, g_next, router=None, tm=512):
    t = x.shape[0]
    row = lambda w: pl.BlockSpec((tm, w), lambda i: (i, 0))
    full = lambda a: pl.BlockSpec(a.shape, lambda i: (0, 0))
    in_specs = [row(D_MODEL), row(D_MODEL), row(o_mla.shape[1]), row(o_dil.shape[1]),
                full(wg), full(wbm), full(wbd), full(wo), full(g_next)]
    args = [x, h, o_mla, o_dil, wg, wbm, wbd, wo, g_next]
    out_specs = [row(D_MODEL)]
    out_shape = [jax.ShapeDtypeStruct((t, D_MODEL), F32)]
    if router is None:
        out_specs += [row(D_MODEL)]
        out_shape += [jax.ShapeDtypeStruct((t, D_MODEL), BF16)]
    else:
        in_specs += [full(router[0]), full(router[1])]
        args += list(router)
        out_specs += [pl.BlockSpec((tm * ROW_TILES, LANES), lambda i: (i, 0)), row(LANES), row(LANES)]
        out_shape += [jax.ShapeDtypeStruct((t * ROW_TILES, LANES), F32),
                      jax.ShapeDtypeStruct((t, LANES), jnp.int32), jax.ShapeDtypeStruct((t, LANES), F32)]
    return pl.pallas_call(
        functools.partial(_merge_kernel, route=router is not None),
        grid=(t // tm,),
        in_specs=in_specs,
        out_specs=out_specs,
        out_shape=out_shape,
        compiler_params=_cparams(("parallel",)),
        name="merge",
    )(*args)


def _ffn_kernel(x_ref, h_ref, wg_ref, wu_ref, wd_ref, o_ref):
    @pl.when(pl.program_id(1) == 0)
    def _():
        o_ref[...] = x_ref[...]

    h = h_ref[...]
    a = jnp.dot(h, wg_ref[...], preferred_element_type=F32)
    u = jnp.dot(h, wu_ref[...], preferred_element_type=F32)
    o_ref[...] += jnp.dot((a * _sigmoid(a) * u).astype(BF16), wd_ref[...], preferred_element_type=F32)


def _ffn(x, h, wg, wu, wd, tm=512, tf=1792):
    t = x.shape[0]
    nf = wg.shape[1] // tf
    return pl.pallas_call(
        _ffn_kernel,
        grid=(t // tm, nf),
        in_specs=[pl.BlockSpec((tm, D_MODEL), lambda i, f: (i, 0)),
                  pl.BlockSpec((tm, D_MODEL), lambda i, f: (i, 0)),
                  pl.BlockSpec((D_MODEL, tf), lambda i, f: (0, f)),
                  pl.BlockSpec((D_MODEL, tf), lambda i, f: (0, f)),
                  pl.BlockSpec((tf, D_MODEL), lambda i, f: (f, 0))],
        out_specs=pl.BlockSpec((tm, D_MODEL), lambda i, f: (i, 0)),
        out_shape=jax.ShapeDtypeStruct((t, D_MODEL), F32),
        compiler_params=_cparams(("parallel", "arbitrary")),
        name="dense_ffn",
    )(x, h, wg, wu, wd)


def _tile_at(ref, row8):
    return ref.at[pl.ds(pl.multiple_of(row8, ROW_TILES), ROW_TILES), :]


def _dispatch_kernel(pos_ref, fill_ref, h_ref, xs_hbm, zero_ref, sem, sem_fill, *, t, tmd, tm):
    i = pl.program_id(0)

    @pl.when(i == 0)
    def _():
        zero_ref[...] = jnp.zeros_like(zero_ref)
        for e in range(fill_ref.shape[0]):
            @pl.when(fill_ref[e] >= 0)
            def _():
                cp = pltpu.make_async_copy(
                    zero_ref, xs_hbm.at[pl.ds(pl.multiple_of(fill_ref[e], ROW_TILES), tm * ROW_TILES), :], sem_fill)
                cp.start()
                cp.wait()

    for k in range(TOP_K):
        for r in range(tmd):
            pltpu.make_async_copy(_tile_at(h_ref, r * ROW_TILES),
                                  _tile_at(xs_hbm, pos_ref[k * t + i * tmd + r]), sem.at[k]).start(priority=r % 2)
    for k in range(TOP_K):
        pltpu.make_async_copy(h_ref, h_ref, sem.at[k]).wait()


def _dispatch(h, pos8, fill8, n_rows, tm, tmd=512):
    t = h.shape[0] // ROW_TILES
    return pl.pallas_call(
        functools.partial(_dispatch_kernel, t=t, tmd=tmd, tm=tm),
        grid_spec=pltpu.PrefetchScalarGridSpec(
            num_scalar_prefetch=2,
            grid=(t // tmd,),
            in_specs=[pl.BlockSpec((tmd * ROW_TILES, LANES), lambda i, pos, fill: (i, 0))],
            out_specs=pl.BlockSpec(memory_space=pl.ANY),
            scratch_shapes=[pltpu.VMEM((tm * ROW_TILES, LANES), F32),
                            pltpu.SemaphoreType.DMA((TOP_K,)), pltpu.SemaphoreType.DMA(())]),
        out_shape=jax.ShapeDtypeStruct((n_rows * ROW_TILES, LANES), F32),
        compiler_params=_cparams(("arbitrary",)),
        name="moe_dispatch",
    )(pos8, fill8, h)


def _moe_ffn_kernel(te_ref, nv_ref, x_ref, wg_ref, wu_ref, wd_ref, o_ref, acc_ref, xb_ref, *, tm):
    i = pl.program_id(0)
    f = pl.program_id(1)
    valid = i < nv_ref[0]
    last = f == pl.num_programs(1) - 1

    @pl.when(valid & (f == 0))
    def _():
        xb_ref[...] = _from_token_tiles(x_ref, tm).astype(BF16)
        acc_ref[...] = jnp.zeros_like(acc_ref)

    @pl.when(valid)
    def _():
        h = xb_ref[...]
        a = jnp.dot(h, wg_ref[0], preferred_element_type=F32)
        u = jnp.dot(h, wu_ref[0], preferred_element_type=F32)
        acc_ref[...] += jnp.dot((a * _sigmoid(a) * u).astype(BF16), wd_ref[0], preferred_element_type=F32)

    @pl.when(valid & last)
    def _():
        _to_token_tiles(o_ref, acc_ref[...])

    @pl.when(jnp.logical_not(valid) & last)
    def _():
        o_ref[...] = jnp.zeros_like(o_ref)


def _moe_ffn(xs, tile_expert, n_valid, wg, wu, wd, tm, tf=1792):
    n_tiles = xs.shape[0] // (tm * ROW_TILES)
    nf = wg.shape[2] // tf

    def live(i, f, te, nv):
        ok = i < nv[0]
        return jnp.where(ok, i, nv[0] - 1), jnp.where(ok, f, nf - 1)

    return pl.pallas_call(
        functools.partial(_moe_ffn_kernel, tm=tm),
        grid_spec=pltpu.PrefetchScalarGridSpec(
            num_scalar_prefetch=2,
            grid=(n_tiles, nf),
            in_specs=[pl.BlockSpec((tm * ROW_TILES, LANES), lambda *a: (live(*a)[0], 0)),
                      pl.BlockSpec((1, D_MODEL, tf), lambda *a: (a[2][live(*a)[0]], 0, live(*a)[1])),
                      pl.BlockSpec((1, D_MODEL, tf), lambda *a: (a[2][live(*a)[0]], 0, live(*a)[1])),
                      pl.BlockSpec((1, tf, D_MODEL), lambda *a: (a[2][live(*a)[0]], live(*a)[1], 0))],
            out_specs=pl.BlockSpec((tm * ROW_TILES, LANES), lambda i, f, te, nv: (i, 0)),
            scratch_shapes=[pltpu.VMEM((tm, D_MODEL), F32), pltpu.VMEM((tm, D_MODEL), BF16)]),
        out_shape=jax.ShapeDtypeStruct(xs.shape, F32),
        compiler_params=_cparams(("arbitrary", "arbitrary")),
        name="moe_ffn",
    )(tile_expert, n_valid, xs, wg, wu, wd)


def _router_params(w_router, b_router):
    w_pad = jnp.zeros((D_MODEL, LANES), F32).at[:, :N_EXPERTS].set(w_router)
    b_pad = jnp.full((1, LANES), NEG, F32).at[0, :N_EXPERTS].set(b_router)
    return w_pad, b_pad


def _moe(h, idx_l, wg, wu, wd, tm=448):
    t = h.shape[0] // ROW_TILES
    expert = idx_l[:, :TOP_K].T.reshape(-1)
    onehot = (expert[:, None] == jnp.arange(N_EXPERTS)[None, :]).astype(jnp.int32)
    csum = jnp.cumsum(onehot, axis=0)
    counts = csum[-1]
    rank = jnp.sum((csum - onehot) * onehot, axis=1)
    tiles = (counts + tm - 1) // tm
    tile_end = jnp.cumsum(tiles)
    pad_off = (tile_end - tiles) * tm
    pos8 = (pad_off[expert] + rank) * ROW_TILES
    n_tiles = (TOP_K * t) // tm + N_EXPERTS
    n_valid = tile_end[-1:].astype(jnp.int32)
    tile_ids = jnp.minimum(jnp.arange(n_tiles), n_valid[0] - 1)
    tile_expert = jnp.sum(tile_end[None, :] <= tile_ids[:, None], axis=1).astype(jnp.int32)
    spare = n_valid[0] + jnp.arange(N_EXPERTS)
    fill8 = jnp.concatenate([jnp.where(tiles > 0, tile_end - 1, -1), jnp.where(spare < n_tiles, spare, -1)])
    fill8 = jnp.where(fill8 >= 0, fill8 * (tm * ROW_TILES), -1).astype(jnp.int32)
    xs = _dispatch(h, pos8, fill8, n_tiles * tm, tm)
    ys = _moe_ffn(xs, tile_expert, n_valid, wg, wu, wd, tm)
    return ys, pos8


def _ple_body(x, p_ref, wg_ref, wp_ref, g_ref, gn_ref, outs):
    gate = _sigmoid(jnp.dot(_rms(x, g_ref[...]).astype(BF16), wg_ref[...], preferred_element_type=F32))
    x2 = x + gate * jnp.dot(p_ref[...].astype(BF16), wp_ref[...], preferred_element_type=F32)
    if len(outs) == 2:
        outs[0][...] = x2
    outs[-1][...] = _rms(x2, gn_ref[...]).astype(outs[-1].dtype)


def _ple_kernel(x_ref, p_ref, wg_ref, wp_ref, g_ref, gn_ref, *outs):
    _ple_body(x_ref[...], p_ref, wg_ref, wp_ref, g_ref, gn_ref, outs)


def _ple_moe_kernel(pos_ref, x_ref, p_ref, wt_ref, wg_ref, wp_ref, g_ref, gn_ref, y_hbm, *rest, t, tm):
    *outs, ybuf, sem = rest
    i = pl.program_id(0)
    slot = i % 2

    def gather(block, s):
        for k in range(TOP_K):
            for r in range(tm):
                pltpu.make_async_copy(_tile_at(y_hbm, pos_ref[k * t + block * tm + r]),
                                      _tile_at(ybuf.at[s, k], r * ROW_TILES), sem.at[s]).start(priority=r % 2)

    @pl.when(i == 0)
    def _():
        gather(0, 0)

    @pl.when(i + 1 < pl.num_programs(0))
    def _():
        gather(i + 1, 1 - slot)

    pltpu.make_async_copy(ybuf.at[slot], ybuf.at[slot], sem.at[slot]).wait()
    wt = wt_ref[...]
    x = (x_ref[...] + wt[:, 0:1] * _from_token_tiles(ybuf.at[slot, 0], tm)
         + wt[:, 1:2] * _from_token_tiles(ybuf.at[slot, 1], tm))
    _ple_body(x, p_ref, wg_ref, wp_ref, g_ref, gn_ref, outs)


def _ple(x, p, wg, wp, g, g_next, h_dtype, want_x, moe=None, tm=512):
    t = x.shape[0]
    n_out = 2 if want_x else 1
    out_shape = ([jax.ShapeDtypeStruct((t, D_MODEL), F32)] * (n_out - 1)
                 + [jax.ShapeDtypeStruct((t, D_MODEL), h_dtype)])
    if moe is None:
        row = lambda w: pl.BlockSpec((tm, w), lambda i: (i, 0))
        full = lambda a: pl.BlockSpec(a.shape, lambda i: (0, 0))
        return pl.pallas_call(
            _ple_kernel,
            grid=(t // tm,),
            in_specs=[row(D_MODEL), row(P_DIM), full(wg), full(wp), full(g), full(g_next)],
            out_specs=[row(D_MODEL)] * n_out,
            out_shape=out_shape,
            compiler_params=_cparams(("parallel",)),
            name="ple",
        )(x, p, wg, wp, g, g_next)
    ys, pos8, wt = moe
    tm = tm // 2
    row = lambda w: pl.BlockSpec((tm, w), lambda i, pos: (i, 0))
    full = lambda a: pl.BlockSpec(a.shape, lambda i, pos: (0, 0))
    return pl.pallas_call(
        functools.partial(_ple_moe_kernel, t=t, tm=tm),
        grid_spec=pltpu.PrefetchScalarGridSpec(
            num_scalar_prefetch=1,
            grid=(t // tm,),
            in_specs=[row(D_MODEL), row(P_DIM), row(LANES), full(wg), full(wp), full(g), full(g_next),
                      pl.BlockSpec(memory_space=pl.ANY)],
            out_specs=[row(D_MODEL)] * n_out,
            scratch_shapes=[pltpu.VMEM((2, TOP_K, tm * ROW_TILES, LANES), F32), pltpu.SemaphoreType.DMA((2,))]),
        out_shape=out_shape,
        compiler_params=_cparams(("arbitrary",)),
        name="ple_moe",
    )(pos8, x, p, wt, wg, wp, g, g_next, ys)


def _rot_cols(w):
    half = w.shape[-1] // 2
    return jnp.concatenate([-w[:, half:], w[:, :half]], axis=-1)


def _layer_weights(i, w_in, w_uq, w_ukv):
    wi = w_in[i]
    z = lambda n: jnp.zeros((wi.shape[0], n), F32)
    w_kr = wi[:, IN_OFF[2]:IN_OFF[3]]
    pad = MLA_SLAB - MLA_NOPE - MLA_ROPE
    wlat = jnp.concatenate([wi[:, :IN_OFF[2]], z(MLA_NOPE), w_kr, z(pad), z(MLA_NOPE), _rot_cols(w_kr), z(pad)],
                           axis=1).astype(BF16)
    qscale = jnp.concatenate([jnp.full((DIL_WIDTH,), DIL_HD ** -0.5, F32), jnp.ones((2 * DIL_WIDTH,), F32)])
    wdil = (wi[:, IN_OFF[3]:IN_OFF[6]] * qscale).astype(BF16)
    wgate = wi[:, IN_OFF[6]:].astype(BF16)
    uq = w_uq[i].reshape(MLA_Q_RANK, MLA_HEADS, MLA_NOPE + MLA_ROPE)
    zq = lambda n: jnp.zeros((MLA_Q_RANK, MLA_HEADS, n), F32)
    rope_rot = jnp.concatenate([-uq[..., MLA_NOPE + MLA_ROPE // 2:], uq[..., MLA_NOPE:MLA_NOPE + MLA_ROPE // 2]], -1)
    wq = jnp.concatenate([uq, zq(pad)], axis=-1).reshape(MLA_Q_RANK, -1).astype(BF16)
    wqr = jnp.concatenate([zq(MLA_NOPE), rope_rot, zq(pad)], axis=-1).reshape(MLA_Q_RANK, -1).astype(BF16)
    ukv = w_ukv[i].reshape(MLA_KV_RANK, MLA_HEADS, MLA_NOPE + MLA_V)
    zkv = jnp.zeros((MLA_KV_RANK, MLA_HEADS, MLA_SLAB - MLA_NOPE), F32)
    wk = jnp.concatenate([ukv[..., :MLA_NOPE], zkv], axis=-1).reshape(MLA_KV_RANK, -1).astype(BF16)
    wv = jnp.concatenate([ukv[..., MLA_NOPE:], zkv], axis=-1).reshape(MLA_KV_RANK, -1).astype(BF16)
    return wlat, wdil, wgate, wq, wqr, wk, wv


def kernel(x, p, positions, attn_norm, w_in, q_norm, w_uq, kv_norm, w_ukv, w_br_mla, w_br_dil, w_out, ffn_norm, dense_w_gate, dense_w_up, dense_w_down, router_w, router_b, moe_w_gate, moe_w_up, moe_w_down, ple_norm, ple_w_gate, ple_w_proj, final_norm):
    batch, seq, d = x.shape
    t = batch * seq
    depth = w_in.shape[0]
    xf = x.reshape(t, d)
    cos_t, sin_t = _rope_tables(positions)
    slopes = _alibi_slopes(DIL_HEADS)
    row = lambda v: v.reshape(1, -1)
    h = _norm(xf, attn_norm[0])
    for i in range(depth):
        wlat, wdil, wgate, wq, wqr, wk, wv = _layer_weights(i, w_in, w_uq, w_ukv)
        q, k, v = _mla_proj(h, wlat, wq, wqr, wk, wv, row(q_norm[i]), row(kv_norm[i]), cos_t, sin_t)
        o_mla = _mla_attn(q, k, v, batch, seq)
        qkv_d = [_dil_proj(h, wdil, seq, g) for g in range(DIL_GROUPS)]
        o_dil = _dil_attn(qkv_d, slopes, batch, seq)
        moe_layer = i % 2 == 1
        j = i // 2
        xf, h2, *routing = _merge(xf, h, o_mla, o_dil, wgate, w_br_mla[i].astype(BF16), w_br_dil[i].astype(BF16),
                                  w_out[i].astype(BF16), row(ffn_norm[i]),
                                  _router_params(router_w[j], router_b[j]) if moe_layer else None)
        moe = None
        if moe_layer:
            idx_l, wt_l = routing
            moe = _moe(h2, idx_l, moe_w_gate[j].astype(BF16), moe_w_up[j].astype(BF16),
                       moe_w_down[j].astype(BF16)) + (wt_l,)
        else:
            xf = _ffn(xf, h2, dense_w_gate[j].astype(BF16), dense_w_up[j].astype(BF16),
                      dense_w_down[j].astype(BF16))
        last = i == depth - 1
        g_next = final_norm if last else attn_norm[i + 1]
        *xnext, h = _ple(xf, p[i].reshape(t, -1), ple_w_gate[i].astype(BF16), ple_w_proj[i].astype(BF16),
                         row(ple_norm[i]), row(g_next), F32 if last else BF16, not last, moe)
        xf = xnext[0] if xnext else None
    return h.reshape(batch, seq, d)
```

```python
import functools

import numpy as np
import jax
import jax.numpy as jnp
from jax import lax
from jax.experimental import pallas as pl
from jax.experimental.pallas import tpu as pltpu

F32 = jnp.float32
BF16 = jnp.bfloat16

D_MODEL = 1024
P_DIM = 256
NORM_EPS = 1e-6
DEPTH = 2

MLA_HEADS = 8
MLA_Q_RANK = 384
MLA_KV_RANK = 256
MLA_NOPE = 64
MLA_ROPE = 32
MLA_V = 64
ROPE_THETA = 10000.0
MLA_SLAB = 128

DIL_CONFIGS = ((128, 1), (512, 4), (2048, 16))
DIL_GROUPS = 3
DIL_HPG = 8
DIL_HEADS = 24
DIL_HD = 64
DIL_WIDTH = DIL_HEADS * DIL_HD
DIL_SPAN = 128

D_FF = 3584
N_EXPERTS = 8
TOP_K = 2

LANES = 128
ROW_TILES = D_MODEL // LANES
NEG = -1e30
LOG2E = 1.4426950408889634
VMEM_LIMIT = 56 * 1024 * 1024

IN_OFF = tuple(int(o) for o in np.cumsum((0, MLA_Q_RANK, MLA_KV_RANK, MLA_ROPE, DIL_WIDTH, DIL_WIDTH,
                                          DIL_WIDTH, D_MODEL, D_MODEL)))


def _cparams(sem):
    return pltpu.CompilerParams(dimension_semantics=sem, vmem_limit_bytes=VMEM_LIMIT)


def _rms(x, g):
    return x * lax.rsqrt(jnp.mean(x * x, axis=-1, keepdims=True) + NORM_EPS) * g


def _sigmoid(x):
    return 1.0 / (1.0 + jnp.exp(-x))


def _alibi_slopes(n):
    def pow2(m):
        start = 2.0 ** (-8.0 / m)
        return [start ** (i + 1) for i in range(m)]
    if float(np.log2(n)).is_integer():
        s = pow2(n)
    else:
        c = 2 ** int(np.floor(np.log2(n)))
        s = pow2(c) + pow2(2 * c)[0::2][: n - c]
    return jnp.asarray(sorted(s, reverse=True), dtype=F32)


def _norm_kernel(x_ref, g_ref, o_ref):
    o_ref[...] = _rms(x_ref[...], g_ref[...]).astype(o_ref.dtype)


def _norm(x, g, tm=1024):
    t, d = x.shape
    return pl.pallas_call(
        _norm_kernel,
        grid=(t // tm,),
        in_specs=[pl.BlockSpec((tm, d), lambda i: (i, 0)), pl.BlockSpec((1, d), lambda i: (0, 0))],
        out_specs=pl.BlockSpec((tm, d), lambda i: (i, 0)),
        out_shape=jax.ShapeDtypeStruct((t, d), BF16),
        compiler_params=_cparams(("parallel",)),
        name="rms_norm",
    )(x, g.reshape(1, d))


def _rope_table_kernel(pos_ref, invf_ref, cos_ref, sin_ref):
    ang = pos_ref[...].astype(F32) * invf_ref[...]
    lane = lax.broadcasted_iota(jnp.int32, ang.shape, 1)
    rope_lane = (lane >= MLA_NOPE) & (lane < MLA_NOPE + MLA_ROPE)
    cos_ref[...] = jnp.where(lane < MLA_NOPE, 1.0, jnp.where(rope_lane, jnp.cos(ang), 0.0))
    sin_ref[...] = jnp.where(rope_lane, jnp.sin(ang), 0.0)


def _rope_tables(positions, tm=2048):
    t = positions.size
    half = MLA_ROPE // 2
    inv_freq = ROPE_THETA ** (-jnp.arange(half, dtype=F32) / half)
    invf = jnp.zeros((1, LANES), F32).at[0, MLA_NOPE:MLA_NOPE + MLA_ROPE].set(jnp.concatenate([inv_freq, inv_freq]))
    pos_b = jnp.broadcast_to(positions.reshape(t, 1), (t, LANES))
    spec = pl.BlockSpec((tm, LANES), lambda i: (i, 0))
    return pl.pallas_call(
        _rope_table_kernel,
        grid=(t // tm,),
        in_specs=[spec, pl.BlockSpec((1, LANES), lambda i: (0, 0))],
        out_specs=[spec, spec],
        out_shape=[jax.ShapeDtypeStruct((t, LANES), F32)] * 2,
        compiler_params=_cparams(("parallel",)),
        name="rope_tables",
    )(pos_b, invf)


def _mla_proj_kernel(h_ref, wlat_ref, wq_ref, wqr_ref, wk_ref, wv_ref, qn_ref, kvn_ref, cos_ref, sin_ref,
                     q_out, k_out, v_out):
    lat = jnp.dot(h_ref[...], wlat_ref[...], preferred_element_type=F32)
    cqn = _rms(lat[:, :MLA_Q_RANK], qn_ref[...]).astype(BF16)
    ckvn = _rms(lat[:, MLA_Q_RANK:MLA_Q_RANK + MLA_KV_RANK], kvn_ref[...]).astype(BF16)
    cos = cos_ref[...]
    sin = sin_ref[...]
    o = MLA_Q_RANK + MLA_KV_RANK
    k_rope = lat[:, o:o + LANES] * cos + lat[:, o + LANES:o + 2 * LANES] * sin
    qa = jnp.dot(cqn, wq_ref[...], preferred_element_type=F32)
    qb = jnp.dot(cqn, wqr_ref[...], preferred_element_type=F32)
    kk = jnp.dot(ckvn, wk_ref[...], preferred_element_type=F32)
    scale = (MLA_NOPE + MLA_ROPE) ** -0.5 * LOG2E
    vv = jnp.dot(ckvn, wv_ref[...], preferred_element_type=F32)
    ones_lane = lax.broadcasted_iota(jnp.int32, cos.shape, 1) >= MLA_V
    for hd in range(MLA_HEADS):
        sl = slice(hd * MLA_SLAB, (hd + 1) * MLA_SLAB)
        q_out[:, sl] = ((qa[:, sl] * cos + qb[:, sl] * sin) * scale).astype(BF16)
        k_out[:, sl] = (kk[:, sl] + k_rope).astype(BF16)
        v_out[:, sl] = jnp.where(ones_lane, 1.0, vv[:, sl]).astype(BF16)


def _mla_proj(h, wlat, wq, wqr, wk, wv, qn, kvn, cos_t, sin_t, tm=512):
    t = h.shape[0]
    row = lambda w: pl.BlockSpec((tm, w), lambda i: (i, 0))
    full = lambda a: pl.BlockSpec(a.shape, lambda i: (0, 0))
    hs = MLA_HEADS * MLA_SLAB
    return pl.pallas_call(
        _mla_proj_kernel,
        grid=(t // tm,),
        in_specs=[row(D_MODEL), full(wlat), full(wq), full(wqr), full(wk), full(wv), full(qn), full(kvn),
                  row(LANES), row(LANES)],
        out_specs=[row(hs), row(hs), row(hs)],
        out_shape=[jax.ShapeDtypeStruct((t, hs), BF16)] * 3,
        compiler_params=_cparams(("parallel",)),
        name="mla_proj",
    )(h, wlat, wq, wqr, wk, wv, qn, kvn, cos_t, sin_t)


MLA_ROW_SPLIT = 2


def _mla_attn_kernel(q_ref, k_ref, v_ref, o_ref, s_scr, *, tq):
    qi = pl.program_id(2)
    nh = q_ref.shape[1] // MLA_SLAB
    tr = tq // MLA_ROW_SPLIT

    def tile(n):
        def keys(c, rg):
            return (rg + 1) * tr if c == n else tq

        def pass1(hh, rg):
            rows = slice(rg * tr, (rg + 1) * tr)
            q = q_ref[rows, hh * MLA_SLAB:(hh + 1) * MLA_SLAB]
            mx = None
            for c in range(n + 1):
                kw = keys(c, rg)
                ks = k_ref[c * tq:c * tq + kw, hh * MLA_SLAB:(hh + 1) * MLA_SLAB]
                s = lax.dot_general(q, ks, (((1,), (1,)), ((), ())), preferred_element_type=F32)
                if c == n:
                    rq = lax.broadcasted_iota(jnp.int32, (tr, kw), 0)
                    ck = lax.broadcasted_iota(jnp.int32, (tr, kw), 1)
                    s = jnp.where(ck <= rq + rg * tr, s, NEG)
                s_scr[hh, c, rows, :kw] = s
                parts = [s[:, i * LANES:(i + 1) * LANES] for i in range(kw // LANES)]
                mx = functools.reduce(jnp.maximum, parts if mx is None else [mx] + parts)
            return jnp.max(mx, axis=-1, keepdims=True)

        def pass2(hh, rg, m):
            rows = slice(rg * tr, (rg + 1) * tr)
            acc = None
            for c in range(n + 1):
                kw = keys(c, rg)
                vs = v_ref[c * tq:c * tq + kw, hh * MLA_SLAB:(hh + 1) * MLA_SLAB]
                p = jnp.exp2((s_scr[hh, c, rows, :kw] - m).astype(BF16))
                pv = jnp.dot(p, vs, preferred_element_type=F32)
                acc = pv if acc is None else acc + pv
            o = acc / acc[:, MLA_V:MLA_V + 1]
            o_ref[rows, hh * MLA_V:(hh + 1) * MLA_V] = o[:, :MLA_V].astype(o_ref.dtype)

        chains = [(hh, rg) for hh in range(nh) for rg in range(MLA_ROW_SPLIT)]
        ms = [pass1(*chains[0])]
        for i, chain in enumerate(chains):
            if i + 1 < len(chains):
                ms.append(pass1(*chains[i + 1]))
            pass2(*chain, ms[i])

    for n in range(s_scr.shape[1]):
        pl.when(qi == n)(functools.partial(tile, n))


def _mla_attn(q, k, v, batch, seq, tq=512, nh=4):
    t = q.shape[0]
    nq = seq // tq
    return pl.pallas_call(
        functools.partial(_mla_attn_kernel, tq=tq),
        grid=(batch, MLA_HEADS // nh, nq),
        in_specs=[pl.BlockSpec((tq, nh * MLA_SLAB), lambda b, p, i: (b * nq + i, p)),
                  pl.BlockSpec((seq, nh * MLA_SLAB), lambda b, p, i: (b, p)),
                  pl.BlockSpec((seq, nh * MLA_SLAB), lambda b, p, i: (b, p))],
        out_specs=pl.BlockSpec((tq, nh * MLA_V), lambda b, p, i: (b * nq + i, p)),
        out_shape=jax.ShapeDtypeStruct((t, MLA_HEADS * MLA_V), BF16),
        scratch_shapes=[pltpu.VMEM((nh, nq, tq, tq), F32)],
        compiler_params=_cparams(("parallel", "parallel", "arbitrary")),
        name="mla_attn",
    )(q, k, v)


PHASE_STRIDE = 4


def _phase_of_slot(r):
    if r <= PHASE_STRIDE:
        return list(range(r))
    f2 = r // PHASE_STRIDE
    return [c1 + PHASE_STRIDE * c2 for c1 in range(PHASE_STRIDE) for c2 in range(f2)]


def _dil_proj_kernel(h_ref, w_ref, o_ref, acc_ref, tmp_ref, *, seq, r):
    nl = o_ref.shape[0]

    @pl.when(pl.program_id(0) == 0)
    def _():
        acc_ref[...] = jnp.zeros_like(acc_ref)

    res = jnp.dot(h_ref[...], w_ref[...], preferred_element_type=F32)
    f1 = min(r, PHASE_STRIDE)
    l1 = seq // f1
    dst = o_ref if r == f1 else tmp_ref
    for c1 in range(f1):
        for j in range(nl):
            src = acc_ref[j, pl.ds(c1, l1, stride=f1), :] if f1 > 1 else acc_ref[j]
            dst[j, c1 * l1:(c1 + 1) * l1, :] = src.astype(dst.dtype)
    if r > f1:
        f2 = r // f1
        l2 = l1 // f2
        for s in range(r):
            c1, c2 = divmod(s, f2)
            for j in range(nl):
                o_ref[j, s * l2:(s + 1) * l2, :] = (
                    tmp_ref[j, pl.ds(c1 * l1 + c2, l2, stride=f2), :].astype(o_ref.dtype))
    for j in range(nl):
        acc_ref[j] = res[:, j * LANES:(j + 1) * LANES]


def _dil_proj(h, w, seq, g, tn=512):
    t = h.shape[0]
    nb = t // seq
    nl = tn // LANES
    n_blocks = 3 * nb
    r = DIL_CONFIGS[g][1]

    def mm(s):
        b, tt = divmod(jnp.minimum(s, n_blocks - 1), 3)
        return tt, b

    def wr(s):
        b, tt = divmod(jnp.maximum(s - 1, 0), 3)
        return tt, b

    return pl.pallas_call(
        functools.partial(_dil_proj_kernel, seq=seq, r=r),
        grid=(n_blocks + 1,),
        in_specs=[pl.BlockSpec((seq, D_MODEL), lambda s: (mm(s)[1], 0)),
                  pl.BlockSpec((D_MODEL, tn), lambda s: (0, mm(s)[0] * DIL_GROUPS + g))],
        out_specs=pl.BlockSpec((nl, seq, LANES), lambda s: (wr(s)[0], wr(s)[1], 0)),
        out_shape=jax.ShapeDtypeStruct((3 * nl, t, LANES), BF16),
        scratch_shapes=[pltpu.VMEM((nl, seq, LANES), F32)] * 2,
        compiler_params=_cparams(("arbitrary",)),
        name="dil_proj",
    )(h, w)


def _dil_attn_kernel(slopes_ref, q0, q1, q2, k0, k1, k2, v0, v1, v2, o_ref, acc_s, m_s, l_s, *, seq):
    pair = pl.program_id(1)
    sp = DIL_SPAN
    head0 = lax.broadcasted_iota(jnp.int32, (sp, LANES), 1) < DIL_HD
    qi = lax.broadcasted_iota(jnp.int32, (sp, 2 * sp), 0)
    kj = lax.broadcasted_iota(jnp.int32, (sp, 2 * sp), 1)
    dist_w = qi + sp - kj
    valid_w = (dist_w >= 0) & (dist_w <= sp)
    dist_1 = dist_w[:, sp:]
    valid_1 = dist_1 >= 0
    qs, ks, vs = (q0, q1, q2), (k0, k1, k2), (v0, v1, v2)
    for g, (_, r) in enumerate(DIL_CONFIGS):
        ln = seq // r
        nb = ln // sp
        sl = [slopes_ref[g * DIL_HPG + 2 * pair + hh] * float(r) for hh in range(2)]
        bias_w = jnp.concatenate([jnp.where(valid_w, -s * dist_w.astype(F32), NEG) for s in sl], axis=0)
        bias_1 = jnp.concatenate([jnp.where(valid_1, -s * dist_1.astype(F32), NEG) for s in sl], axis=0)
        for slot_i, c in enumerate(_phase_of_slot(r)):
            for i in range(nb):
                row0 = slot_i * ln + i * sp
                qb = qs[g][0, row0:row0 + sp, :]
                zero = jnp.zeros_like(qb)
                q2h = jnp.concatenate([jnp.where(head0, qb, zero), jnp.where(head0, zero, qb)], axis=0)
                lo = row0 if i == 0 else row0 - sp
                kw = ks[g][0, lo:row0 + sp, :]
                vw = vs[g][0, lo:row0 + sp, :]
                s = lax.dot_general(q2h, kw, (((1,), (1,)), ((), ())), preferred_element_type=F32)
                s = s + (bias_1 if i == 0 else bias_w)
                m = jnp.max(s, axis=-1, keepdims=True)
                p = jnp.exp(s - m)
                v1 = jnp.concatenate([vw, jnp.ones_like(vw)], axis=1)
                o2 = jnp.dot(p.astype(BF16), v1, preferred_element_type=F32)
                dst = pl.ds(i * sp * r + c, sp, stride=r) if r > 1 else pl.ds(i * sp, sp)
                acc_s[g, dst, :] = jnp.where(head0, o2[:sp, :LANES], o2[sp:, :LANES])
                m_s[g, dst, :] = jnp.where(head0, m[:sp], m[sp:])
                l_s[g, dst, :] = jnp.where(head0, o2[:sp, LANES:], o2[sp:, LANES:])

    ch = 256

    def combine(i, _):
        rows = pl.ds(pl.multiple_of(i * ch, ch), ch)
        ms = [m_s[g, rows, :] for g in range(DIL_GROUPS)]
        mx = jnp.maximum(jnp.maximum(ms[0], ms[1]), ms[2])
        ws = [jnp.exp(m - mx) for m in ms]
        num = sum(ws[g] * acc_s[g, rows, :] for g in range(DIL_GROUPS))
        den = sum(ws[g] * l_s[g, rows, :] for g in range(DIL_GROUPS))
        o_ref[rows, :] = (num / den).astype(o_ref.dtype)
        return 0

    lax.fori_loop(0, seq // ch, combine, 0)


def _dil_attn(qkvs, slopes, batch, seq):
    t = qkvs[0].shape[1]
    pairs = DIL_HPG // 2

    def spec(tt):
        return pl.BlockSpec((1, seq, LANES), lambda b, p: (tt * pairs + p, b, 0))

    in_specs = [pl.BlockSpec(memory_space=pltpu.SMEM)] + [spec(tt) for tt in range(3) for _ in range(DIL_GROUPS)]
    return pl.pallas_call(
        functools.partial(_dil_attn_kernel, seq=seq),
        grid=(batch, pairs),
        in_specs=in_specs,
        out_specs=pl.BlockSpec((seq, LANES), lambda b, p: (b, p)),
        out_shape=jax.ShapeDtypeStruct((t, DIL_HPG * DIL_HD), BF16),
        scratch_shapes=[pltpu.VMEM((DIL_GROUPS, seq, LANES), F32)] * 3,
        compiler_params=_cparams(("parallel", "parallel")),
        name="dil_attn",
    )(slopes, *[qkvs[g] for _ in range(3) for g in range(DIL_GROUPS)])


def _to_token_tiles(dst_ref, val):
    n = val.shape[0]
    for s in range(ROW_TILES):
        dst_ref[pl.ds(s, n, stride=ROW_TILES), :] = val[:, s * LANES:(s + 1) * LANES]


def _from_token_tiles(src_ref, n):
    return jnp.concatenate([src_ref[pl.ds(s, n, stride=ROW_TILES), :] for s in range(ROW_TILES)], axis=1)


def _top2(logits):
    lane = lax.broadcasted_iota(jnp.int32, logits.shape, 1)
    m1 = jnp.max(logits, axis=-1, keepdims=True)
    i1 = jnp.min(jnp.where(logits == m1, lane, LANES), axis=-1, keepdims=True)
    rest = jnp.where(lane == i1, NEG, logits)
    m2 = jnp.max(rest, axis=-1, keepdims=True)
    i2 = jnp.min(jnp.where(rest == m2, lane, LANES), axis=-1, keepdims=True)
    e = jnp.exp(m2 - m1)
    w1 = 1.0 / (1.0 + e)
    return (jnp.where(lane == 0, i1, jnp.where(lane == 1, i2, 0)),
            jnp.where(lane == 0, w1, jnp.where(lane == 1, e * w1, 0.0)))


def _merge_kernel(x_ref, h_ref, om_ref, od_ref, wg_ref, wbm_ref, wbd_ref, wo_ref, g_ref, *rest, route):
    gates = jnp.dot(h_ref[...], wg_ref[...], preferred_element_type=F32)
    bm = jnp.dot(om_ref[...], wbm_ref[...], preferred_element_type=F32)
    bd = jnp.dot(od_ref[...], wbd_ref[...], preferred_element_type=F32)
    merged = _sigmoid(gates[:, :D_MODEL]) * bm + _sigmoid(gates[:, D_MODEL:]) * bd
    x1 = x_ref[...] + jnp.dot(merged.astype(BF16), wo_ref[...], preferred_element_type=F32)
    hn = _rms(x1, g_ref[...])
    if route:
        wr_ref, br_ref, x_out, h_out, idx_out, wt_out = rest
        _to_token_tiles(h_out, hn)
        hi = hn.astype(BF16)
        lo = (hn - hi.astype(F32)).astype(BF16)
        w = wr_ref[...]
        w_hi = w.astype(BF16)
        w_lo = (w - w_hi.astype(F32)).astype(BF16)
        both = jnp.dot(hi, jnp.concatenate([w_hi, w_lo], axis=1), preferred_element_type=F32)
        logits = (both[:, :LANES] + both[:, LANES:]
                  + jnp.dot(lo, w_hi, preferred_element_type=F32) + br_ref[...])
        idx_out[...], wt_out[...] = _top2(logits)
    else:
        x_out, h_out = rest
        h_out[...] = hn.astype(h_out.dtype)
    x_out[...] = x1


def _merge(x, h, o_mla, o_dil, wg, wbm, wbd, wo, g_next, router=None, tm=512):
    t = x.shape[0]
    row = lambda w: pl.BlockSpec((tm, w), lambda i: (i, 0))
    full = lambda a: pl.BlockSpec(a.shape, lambda i: (0, 0))
    in_specs = [row(D_MODEL), row(D_MODEL), row(o_mla.shape[1]), row(o_dil.shape[1]),
                full(wg), full(wbm), full(wbd), full(wo), full(g_next)]
    args = [x, h, o_mla, o_dil, wg, wbm, wbd, wo, g_next]
    out_specs = [row(D_MODEL)]
    out_shape = [jax.ShapeDtypeStruct((t, D_MODEL), F32)]
    if router is None:
        out_specs += [row(D_MODEL)]
        out_shape += [jax.ShapeDtypeStruct((t, D_MODEL), BF16)]
    else:
        in_specs += [full(router[0]), full(router[1])]
        args += list(router)
        out_specs += [pl.BlockSpec((tm * ROW_TILES, LANES), lambda i: (i, 0)), row(LANES), row(LANES)]
        out_shape += [jax.ShapeDtypeStruct((t * ROW_TILES, LANES), F32),
                      jax.ShapeDtypeStruct((t, LANES), jnp.int32), jax.ShapeDtypeStruct((t, LANES), F32)]
    return pl.pallas_call(
        functools.partial(_merge_kernel, route=router is not None),
        grid=(t // tm,),
        in_specs=in_specs,
        out_specs=out_specs,
        out_shape=out_shape,
        compiler_params=_cparams(("parallel",)),
        name="merge",
    )(*args)


def _ffn_kernel(x_ref, h_ref, wg_ref, wu_ref, wd_ref, o_ref):
    @pl.when(pl.program_id(1) == 0)
    def _():
        o_ref[...] = x_ref[...]

    h = h_ref[...]
    a = jnp.dot(h, wg_ref[...], preferred_element_type=F32)
    u = jnp.dot(h, wu_ref[...], preferred_element_type=F32)
    o_ref[...] += jnp.dot((a * _sigmoid(a) * u).astype(BF16), wd_ref[...], preferred_element_type=F32)


def _ffn(x, h, wg, wu, wd, tm=512, tf=1792):
    t = x.shape[0]
    nf = wg.shape[1] // tf
    return pl.pallas_call(
        _ffn_kernel,
        grid=(t // tm, nf),
        in_specs=[pl.BlockSpec((tm, D_MODEL), lambda i, f: (i, 0)),
                  pl.BlockSpec((tm, D_MODEL), lambda i, f: (i, 0)),
                  pl.BlockSpec((D_MODEL, tf), lambda i, f: (0, f)),
                  pl.BlockSpec((D_MODEL, tf), lambda i, f: (0, f)),
                  pl.BlockSpec((tf, D_MODEL), lambda i, f: (f, 0))],
        out_specs=pl.BlockSpec((tm, D_MODEL), lambda i, f: (i, 0)),
        out_shape=jax.ShapeDtypeStruct((t, D_MODEL), F32),
        compiler_params=_cparams(("parallel", "arbitrary")),
        name="dense_ffn",
    )(x, h, wg, wu, wd)


def _tile_at(ref, row8):
    return ref.at[pl.ds(pl.multiple_of(row8, ROW_TILES), ROW_TILES), :]


def _dispatch_kernel(pos_ref, fill_ref, h_ref, xs_hbm, zero_ref, sem, sem_fill, *, t, tmd, tm):
    i = pl.program_id(0)

    @pl.when(i == 0)
    def _():
        zero_ref[...] = jnp.zeros_like(zero_ref)
        for e in range(fill_ref.shape[0]):
            @pl.when(fill_ref[e] >= 0)
            def _():
                cp = pltpu.make_async_copy(
                    zero_ref, xs_hbm.at[pl.ds(pl.multiple_of(fill_ref[e], ROW_TILES), tm * ROW_TILES), :], sem_fill)
                cp.start()
                cp.wait()

    for k in range(TOP_K):
        for r in range(tmd):
            pltpu.make_async_copy(_tile_at(h_ref, r * ROW_TILES),
                                  _tile_at(xs_hbm, pos_ref[k * t + i * tmd + r]), sem.at[k]).start(priority=r % 2)
    for k in range(TOP_K):
        pltpu.make_async_copy(h_ref, h_ref, sem.at[k]).wait()


def _dispatch(h, pos8, fill8, n_rows, tm, tmd=512):
    t = h.shape[0] // ROW_TILES
    return pl.pallas_call(
        functools.partial(_dispatch_kernel, t=t, tmd=tmd, tm=tm),
        grid_spec=pltpu.PrefetchScalarGridSpec(
            num_scalar_prefetch=2,
            grid=(t // tmd,),
            in_specs=[pl.BlockSpec((tmd * ROW_TILES, LANES), lambda i, pos, fill: (i, 0))],
            out_specs=pl.BlockSpec(memory_space=pl.ANY),
            scratch_shapes=[pltpu.VMEM((tm * ROW_TILES, LANES), F32),
                            pltpu.SemaphoreType.DMA((TOP_K,)), pltpu.SemaphoreType.DMA(())]),
        out_shape=jax.ShapeDtypeStruct((n_rows * ROW_TILES, LANES), F32),
        compiler_params=_cparams(("arbitrary",)),
        name="moe_dispatch",
    )(pos8, fill8, h)


def _moe_ffn_kernel(te_ref, nv_ref, x_ref, wg_ref, wu_ref, wd_ref, o_ref, acc_ref, xb_ref, *, tm):
    i = pl.program_id(0)
    f = pl.program_id(1)
    valid = i < nv_ref[0]
    last = f == pl.num_programs(1) - 1

    @pl.when(valid & (f == 0))
    def _():
        xb_ref[...] = _from_token_tiles(x_ref, tm).astype(BF16)
        acc_ref[...] = jnp.zeros_like(acc_ref)

    @pl.when(valid)
    def _():
        h = xb_ref[...]
        a = jnp.dot(h, wg_ref[0], preferred_element_type=F32)
        u = jnp.dot(h, wu_ref[0], preferred_element_type=F32)
        acc_ref[...] += jnp.dot((a * _sigmoid(a) * u).astype(BF16), wd_ref[0], preferred_element_type=F32)

    @pl.when(valid & last)
    def _():
        _to_token_tiles(o_ref, acc_ref[...])

    @pl.when(jnp.logical_not(valid) & last)
    def _():
        o_ref[...] = jnp.zeros_like(o_ref)


def _moe_ffn(xs, tile_expert, n_valid, wg, wu, wd, tm, tf=1792):
    n_tiles = xs.shape[0] // (tm * ROW_TILES)
    nf = wg.shape[2] // tf

    def live(i, f, te, nv):
        ok = i < nv[0]
        return jnp.where(ok, i, nv[0] - 1), jnp.where(ok, f, nf - 1)

    return pl.pallas_call(
        functools.partial(_moe_ffn_kernel, tm=tm),
        grid_spec=pltpu.PrefetchScalarGridSpec(
            num_scalar_prefetch=2,
            grid=(n_tiles, nf),
            in_specs=[pl.BlockSpec((tm * ROW_TILES, LANES), lambda *a: (live(*a)[0], 0)),
                      pl.BlockSpec((1, D_MODEL, tf), lambda *a: (a[2][live(*a)[0]], 0, live(*a)[1])),
                      pl.BlockSpec((1, D_MODEL, tf), lambda *a: (a[2][live(*a)[0]], 0, live(*a)[1])),
                      pl.BlockSpec((1, tf, D_MODEL), lambda *a: (a[2][live(*a)[0]], live(*a)[1], 0))],
            out_specs=pl.BlockSpec((tm * ROW_TILES, LANES), lambda i, f, te, nv: (i, 0)),
            scratch_shapes=[pltpu.VMEM((tm, D_MODEL), F32), pltpu.VMEM((tm, D_MODEL), BF16)]),
        out_shape=jax.ShapeDtypeStruct(xs.shape, F32),
        compiler_params=_cparams(("arbitrary", "arbitrary")),
        name="moe_ffn",
    )(tile_expert, n_valid, xs, wg, wu, wd)


def _router_params(w_router, b_router):
    w_pad = jnp.zeros((D_MODEL, LANES), F32).at[:, :N_EXPERTS].set(w_router)
    b_pad = jnp.full((1, LANES), NEG, F32).at[0, :N_EXPERTS].set(b_router)
    return w_pad, b_pad


def _moe(h, idx_l, wg, wu, wd, tm=448):
    t = h.shape[0] // ROW_TILES
    expert = idx_l[:, :TOP_K].T.reshape(-1)
    onehot = (expert[:, None] == jnp.arange(N_EXPERTS)[None, :]).astype(jnp.int32)
    csum = jnp.cumsum(onehot, axis=0)
    counts = csum[-1]
    rank = jnp.sum((csum - onehot) * onehot, axis=1)
    tiles = (counts + tm - 1) // tm
    tile_end = jnp.cumsum(tiles)
    pad_off = (tile_end - tiles) * tm
    pos8 = (pad_off[expert] + rank) * ROW_TILES
    n_tiles = (TOP_K * t) // tm + N_EXPERTS
    n_valid = tile_end[-1:].astype(jnp.int32)
    tile_ids = jnp.minimum(jnp.arange(n_tiles), n_valid[0] - 1)
    tile_expert = jnp.sum(tile_end[None, :] <= tile_ids[:, None], axis=1).astype(jnp.int32)
    spare = n_valid[0] + jnp.arange(N_EXPERTS)
    fill8 = jnp.concatenate([jnp.where(tiles > 0, tile_end - 1, -1), jnp.where(spare < n_tiles, spare, -1)])
    fill8 = jnp.where(fill8 >= 0, fill8 * (tm * ROW_TILES), -1).astype(jnp.int32)
    xs = _dispatch(h, pos8, fill8, n_tiles * tm, tm)
    ys = _moe_ffn(xs, tile_expert, n_valid, wg, wu, wd, tm)
    return ys, pos8


def _ple_body(x, p_ref, wg_ref, wp_ref, g_ref, gn_ref, outs):
    gate = _sigmoid(jnp.dot(_rms(x, g_ref[...]).astype(BF16), wg_ref[...], preferred_element_type=F32))
    x2 = x + gate * jnp.dot(p_ref[...].astype(BF16), wp_ref[...], preferred_element_type=F32)
    if len(outs) == 2:
        outs[0][...] = x2
    outs[-1][...] = _rms(x2, gn_ref[...]).astype(outs[-1].dtype)


def _ple_kernel(x_ref, p_ref, wg_ref, wp_ref, g_ref, gn_ref, *outs):
    _ple_body(x_ref[...], p_ref, wg_ref, wp_ref, g_ref, gn_ref, outs)


def _ple_moe_kernel(pos_ref, x_ref, p_ref, wt_ref, wg_ref, wp_ref, g_ref, gn_ref, y_hbm, *rest, t, tm):
    *outs, ybuf, sem = rest
    i = pl.program_id(0)
    slot = i % 2

    def gather(block, s):
        for k in range(TOP_K):
            for r in range(tm):
                pltpu.make_async_copy(_tile_at(y_hbm, pos_ref[k * t + block * tm + r]),
                                      _tile_at(ybuf.at[s, k], r * ROW_TILES), sem.at[s]).start(priority=r % 2)

    @pl.when(i == 0)
    def _():
        gather(0, 0)

    @pl.when(i + 1 < pl.num_programs(0))
    def _():
        gather(i + 1, 1 - slot)

    pltpu.make_async_copy(ybuf.at[slot], ybuf.at[slot], sem.at[slot]).wait()
    wt = wt_ref[...]
    x = (x_ref[...] + wt[:, 0:1] * _from_token_tiles(ybuf.at[slot, 0], tm)
         + wt[:, 1:2] * _from_token_tiles(ybuf.at[slot, 1], tm))
    _ple_body(x, p_ref, wg_ref, wp_ref, g_ref, gn_ref, outs)


def _ple(x, p, wg, wp, g, g_next, h_dtype, want_x, moe=None, tm=512):
    t = x.shape[0]
    n_out = 2 if want_x else 1
    out_shape = ([jax.ShapeDtypeStruct((t, D_MODEL), F32)] * (n_out - 1)
                 + [jax.ShapeDtypeStruct((t, D_MODEL), h_dtype)])
    if moe is None:
        row = lambda w: pl.BlockSpec((tm, w), lambda i: (i, 0))
        full = lambda a: pl.BlockSpec(a.shape, lambda i: (0, 0))
        return pl.pallas_call(
            _ple_kernel,
            grid=(t // tm,),
            in_specs=[row(D_MODEL), row(P_DIM), full(wg), full(wp), full(g), full(g_next)],
            out_specs=[row(D_MODEL)] * n_out,
            out_shape=out_shape,
            compiler_params=_cparams(("parallel",)),
            name="ple",
        )(x, p, wg, wp, g, g_next)
    ys, pos8, wt = moe
    tm = tm // 2
    row = lambda w: pl.BlockSpec((tm, w), lambda i, pos: (i, 0))
    full = lambda a: pl.BlockSpec(a.shape, lambda i, pos: (0, 0))
    return pl.pallas_call(
        functools.partial(_ple_moe_kernel, t=t, tm=tm),
        grid_spec=pltpu.PrefetchScalarGridSpec(
            num_scalar_prefetch=1,
            grid=(t // tm,),
            in_specs=[row(D_MODEL), row(P_DIM), row(LANES), full(wg), full(wp), full(g), full(g_next),
                      pl.BlockSpec(memory_space=pl.ANY)],
            out_specs=[row(D_MODEL)] * n_out,
            scratch_shapes=[pltpu.VMEM((2, TOP_K, tm * ROW_TILES, LANES), F32), pltpu.SemaphoreType.DMA((2,))]),
        out_shape=out_shape,
        compiler_params=_cparams(("arbitrary",)),
        name="ple_moe",
    )(pos8, x, p, wt, wg, wp, g, g_next, ys)


def _rot_cols(w):
    half = w.shape[-1] // 2
    return jnp.concatenate([-w[:, half:], w[:, :half]], axis=-1)


def _layer_weights(i, w_in, w_uq, w_ukv):
    wi = w_in[i]
    z = lambda n: jnp.zeros((wi.shape[0], n), F32)
    w_kr = wi[:, IN_OFF[2]:IN_OFF[3]]
    pad = MLA_SLAB - MLA_NOPE - MLA_ROPE
    wlat = jnp.concatenate([wi[:, :IN_OFF[2]], z(MLA_NOPE), w_kr, z(pad), z(MLA_NOPE), _rot_cols(w_kr), z(pad)],
                           axis=1).astype(BF16)
    qscale = jnp.concatenate([jnp.full((DIL_WIDTH,), DIL_HD ** -0.5, F32), jnp.ones((2 * DIL_WIDTH,), F32)])
    wdil = (wi[:, IN_OFF[3]:IN_OFF[6]] * qscale).astype(BF16)
    wgate = wi[:, IN_OFF[6]:].astype(BF16)
    uq = w_uq[i].reshape(MLA_Q_RANK, MLA_HEADS, MLA_NOPE + MLA_ROPE)
    zq = lambda n: jnp.zeros((MLA_Q_RANK, MLA_HEADS, n), F32)
    rope_rot = jnp.concatenate([-uq[..., MLA_NOPE + MLA_ROPE // 2:], uq[..., MLA_NOPE:MLA_NOPE + MLA_ROPE // 2]], -1)
    wq = jnp.concatenate([uq, zq(pad)], axis=-1).reshape(MLA_Q_RANK, -1).astype(BF16)
    wqr = jnp.concatenate([zq(MLA_NOPE), rope_rot, zq(pad)], axis=-1).reshape(MLA_Q_RANK, -1).astype(BF16)
    ukv = w_ukv[i].reshape(MLA_KV_RANK, MLA_HEADS, MLA_NOPE + MLA_V)
    zkv = jnp.zeros((MLA_KV_RANK, MLA_HEADS, MLA_SLAB - MLA_NOPE), F32)
    wk = jnp.concatenate([ukv[..., :MLA_NOPE], zkv], axis=-1).reshape(MLA_KV_RANK, -1).astype(BF16)
    wv = jnp.concatenate([ukv[..., MLA_NOPE:], zkv], axis=-1).reshape(MLA_KV_RANK, -1).astype(BF16)
    return wlat, wdil, wgate, wq, wqr, wk, wv


def kernel(x, p, positions, attn_norm, w_in, q_norm, w_uq, kv_norm, w_ukv, w_br_mla, w_br_dil, w_out, ffn_norm, dense_w_gate, dense_w_up, dense_w_down, router_w, router_b, moe_w_gate, moe_w_up, moe_w_down, ple_norm, ple_w_gate, ple_w_proj, final_norm):
    batch, seq, d = x.shape
    t = batch * seq
    depth = w_in.shape[0]
    xf = x.reshape(t, d)
    cos_t, sin_t = _rope_tables(positions)
    slopes = _alibi_slopes(DIL_HEADS)
    row = lambda v: v.reshape(1, -1)
    h = _norm(xf, attn_norm[0])
    for i in range(depth):
        wlat, wdil, wgate, wq, wqr, wk, wv = _layer_weights(i, w_in, w_uq, w_ukv)
        q, k, v = _mla_proj(h, wlat, wq, wqr, wk, wv, row(q_norm[i]), row(kv_norm[i]), cos_t, sin_t)
        o_mla = _mla_attn(q, k, v, batch, seq)
        qkv_d = [_dil_proj(h, wdil, seq, g) for g in range(DIL_GROUPS)]
        o_dil = _dil_attn(qkv_d, slopes, batch, seq)
        moe_layer = i % 2 == 1
        j = i // 2
        xf, h2, *routing = _merge(xf, h, o_mla, o_dil, wgate, w_br_mla[i].astype(BF16), w_br_dil[i].astype(BF16),
                                  w_out[i].astype(BF16), row(ffn_norm[i]),
                                  _router_params(router_w[j], router_b[j]) if moe_layer else None)
        moe = None
        if moe_layer:
            idx_l, wt_l = routing
            moe = _moe(h2, idx_l, moe_w_gate[j].astype(BF16), moe_w_up[j].astype(BF16),
                       moe_w_down[j].astype(BF16)) + (wt_l,)
        else:
            xf = _ffn(xf, h2, dense_w_gate[j].astype(BF16), dense_w_up[j].astype(BF16),
                      dense_w_down[j].astype(BF16))
        last = i == depth - 1
        g_next = final_norm if last else attn_norm[i + 1]
        *xnext, h = _ple(xf, p[i].reshape(t, -1), ple_w_gate[i].astype(BF16), ple_w_proj[i].astype(BF16),
                         row(ple_norm[i]), row(g_next), F32 if last else BF16, not last, moe)
        xf = xnext[0] if xnext else None
    return h.reshape(batch, seq, d)
```

```python
import functools

import numpy as np
import jax
import jax.numpy as jnp
from jax import lax
from jax.experimental import pallas as pl
from jax.experimental.pallas import tpu as pltpu

F32 = jnp.float32
BF16 = jnp.bfloat16

D_MODEL = 1024
P_DIM = 256
NORM_EPS = 1e-6
DEPTH = 2

MLA_HEADS = 8
MLA_Q_RANK = 384
MLA_KV_RANK = 256
MLA_NOPE = 64
MLA_ROPE = 32
MLA_V = 64
ROPE_THETA = 10000.0
MLA_SLAB = 128

DIL_CONFIGS = ((128, 1), (512, 4), (2048, 16))
DIL_GROUPS = 3
DIL_HPG = 8
DIL_HEADS = 24
DIL_HD = 64
DIL_WIDTH = DIL_HEADS * DIL_HD
DIL_SPAN = 128

D_FF = 3584
N_EXPERTS = 8
TOP_K = 2

LANES = 128
ROW_TILES = D_MODEL // LANES
NEG = -1e30
LOG2E = 1.4426950408889634
VMEM_LIMIT = 56 * 1024 * 1024

IN_OFF = tuple(int(o) for o in np.cumsum((0, MLA_Q_RANK, MLA_KV_RANK, MLA_ROPE, DIL_WIDTH, DIL_WIDTH,
                                          DIL_WIDTH, D_MODEL, D_MODEL)))


def _cparams(sem):
    return pltpu.CompilerParams(dimension_semantics=sem, vmem_limit_bytes=VMEM_LIMIT)


def _rms(x, g):
    return x * lax.rsqrt(jnp.mean(x * x, axis=-1, keepdims=True) + NORM_EPS) * g


def _sigmoid(x):
    return 1.0 / (1.0 + jnp.exp(-x))


def _alibi_slopes(n):
    def pow2(m):
        start = 2.0 ** (-8.0 / m)
        return [start ** (i + 1) for i in range(m)]
    if float(np.log2(n)).is_integer():
        s = pow2(n)
    else:
        c = 2 ** int(np.floor(np.log2(n)))
        s = pow2(c) + pow2(2 * c)[0::2][: n - c]
    return jnp.asarray(sorted(s, reverse=True), dtype=F32)


def _norm_kernel(x_ref, g_ref, o_ref):
    o_ref[...] = _rms(x_ref[...], g_ref[...]).astype(o_ref.dtype)


def _norm(x, g, tm=1024):
    t, d = x.shape
    return pl.pallas_call(
        _norm_kernel,
        grid=(t // tm,),
        in_specs=[pl.BlockSpec((tm, d), lambda i: (i, 0)), pl.BlockSpec((1, d), lambda i: (0, 0))],
        out_specs=pl.BlockSpec((tm, d), lambda i: (i, 0)),
        out_shape=jax.ShapeDtypeStruct((t, d), BF16),
        compiler_params=_cparams(("parallel",)),
        name="rms_norm",
    )(x, g.reshape(1, d))


def _rope_table_kernel(pos_ref, invf_ref, cos_ref, sin_ref):
    ang = pos_ref[...].astype(F32) * invf_ref[...]
    lane = lax.broadcasted_iota(jnp.int32, ang.shape, 1)
    rope_lane = (lane >= MLA_NOPE) & (lane < MLA_NOPE + MLA_ROPE)
    cos_ref[...] = jnp.where(lane < MLA_NOPE, 1.0, jnp.where(rope_lane, jnp.cos(ang), 0.0))
    sin_ref[...] = jnp.where(rope_lane, jnp.sin(ang), 0.0)


def _rope_tables(positions, tm=2048):
    t = positions.size
    half = MLA_ROPE // 2
    inv_freq = ROPE_THETA ** (-jnp.arange(half, dtype=F32) / half)
    invf = jnp.zeros((1, LANES), F32).at[0, MLA_NOPE:MLA_NOPE + MLA_ROPE].set(jnp.concatenate([inv_freq, inv_freq]))
    pos_b = jnp.broadcast_to(positions.reshape(t, 1), (t, LANES))
    spec = pl.BlockSpec((tm, LANES), lambda i: (i, 0))
    return pl.pallas_call(
        _rope_table_kernel,
        grid=(t // tm,),
        in_specs=[spec, pl.BlockSpec((1, LANES), lambda i: (0, 0))],
        out_specs=[spec, spec],
        out_shape=[jax.ShapeDtypeStruct((t, LANES), F32)] * 2,
        compiler_params=_cparams(("parallel",)),
        name="rope_tables",
    )(pos_b, invf)


def _mla_proj_kernel(h_ref, wlat_ref, wq_ref, wqr_ref, wk_ref, wv_ref, qn_ref, kvn_ref, cos_ref, sin_ref,
                     q_out, k_out, v_out):
    lat = jnp.dot(h_ref[...], wlat_ref[...], preferred_element_type=F32)
    cqn = _rms(lat[:, :MLA_Q_RANK], qn_ref[...]).astype(BF16)
    ckvn = _rms(lat[:, MLA_Q_RANK:MLA_Q_RANK + MLA_KV_RANK], kvn_ref[...]).astype(BF16)
    cos = cos_ref[...]
    sin = sin_ref[...]
    o = MLA_Q_RANK + MLA_KV_RANK
    k_rope = lat[:, o:o + LANES] * cos + lat[:, o + LANES:o + 2 * LANES] * sin
    qa = jnp.dot(cqn, wq_ref[...], preferred_element_type=F32)
    qb = jnp.dot(cqn, wqr_ref[...], preferred_element_type=F32)
    kk = jnp.dot(ckvn, wk_ref[...], preferred_element_type=F32)
    scale = (MLA_NOPE + MLA_ROPE) ** -0.5 * LOG2E
    vv = jnp.dot(ckvn, wv_ref[...], preferred_element_type=F32)
    ones_lane = lax.broadcasted_iota(jnp.int32, cos.shape, 1) >= MLA_V
    for hd in range(MLA_HEADS):
        sl = slice(hd * MLA_SLAB, (hd + 1) * MLA_SLAB)
        q_out[:, sl] = ((qa[:, sl] * cos + qb[:, sl] * sin) * scale).astype(BF16)
        k_out[:, sl] = (kk[:, sl] + k_rope).astype(BF16)
        v_out[:, sl] = jnp.where(ones_lane, 1.0, vv[:, sl]).astype(BF16)


def _mla_proj(h, wlat, wq, wqr, wk, wv, qn, kvn, cos_t, sin_t, tm=512):
    t = h.shape[0]
    row = lambda w: pl.BlockSpec((tm, w), lambda i: (i, 0))
    full = lambda a: pl.BlockSpec(a.shape, lambda i: (0, 0))
    hs = MLA_HEADS * MLA_SLAB
    return pl.pallas_call(
        _mla_proj_kernel,
        grid=(t // tm,),
        in_specs=[row(D_MODEL), full(wlat), full(wq), full(wqr), full(wk), full(wv), full(qn), full(kvn),
                  row(LANES), row(LANES)],
        out_specs=[row(hs), row(hs), row(hs)],
        out_shape=[jax.ShapeDtypeStruct((t, hs), BF16)] * 3,
        compiler_params=_cparams(("parallel",)),
        name="mla_proj",
    )(h, wlat, wq, wqr, wk, wv, qn, kvn, cos_t, sin_t)


MLA_ROW_SPLIT = 2


def _mla_attn_kernel(q_ref, k_ref, v_ref, o_ref, s_scr, *, tq):
    qi = pl.program_id(2)
    nh = q_ref.shape[1] // MLA_SLAB
    tr = tq // MLA_ROW_SPLIT

    def tile(n):
        def keys(c, rg):
            return (rg + 1) * tr if c == n else tq

        def pass1(hh, rg):
            rows = slice(rg * tr, (rg + 1) * tr)
            q = q_ref[rows, hh * MLA_SLAB:(hh + 1) * MLA_SLAB]
            mx = None
            for c in range(n + 1):
                kw = keys(c, rg)
                ks = k_ref[c * tq:c * tq + kw, hh * MLA_SLAB:(hh + 1) * MLA_SLAB]
                s = lax.dot_general(q, ks, (((1,), (1,)), ((), ())), preferred_element_type=F32)
                if c == n:
                    rq = lax.broadcasted_iota(jnp.int32, (tr, kw), 0)
                    ck = lax.broadcasted_iota(jnp.int32, (tr, kw), 1)
                    s = jnp.where(ck <= rq + rg * tr, s, NEG)
                s_scr[hh, c, rows, :kw] = s
                parts = [s[:, i * LANES:(i + 1) * LANES] for i in range(kw // LANES)]
                mx = functools.reduce(jnp.maximum, parts if mx is None else [mx] + parts)
            return jnp.max(mx, axis=-1, keepdims=True)

        def pass2(hh, rg, m):
            rows = slice(rg * tr, (rg + 1) * tr)
            acc = None
            for c in range(n + 1):
                kw = keys(c, rg)
                vs = v_ref[c * tq:c * tq + kw, hh * MLA_SLAB:(hh + 1) * MLA_SLAB]
                p = jnp.exp2((s_scr[hh, c, rows, :kw] - m).astype(BF16))
                pv = jnp.dot(p, vs, preferred_element_type=F32)
                acc = pv if acc is None else acc + pv
            o = acc / acc[:, MLA_V:MLA_V + 1]
            o_ref[rows, hh * MLA_V:(hh + 1) * MLA_V] = o[:, :MLA_V].astype(o_ref.dtype)

        chains = [(hh, rg) for hh in range(nh) for rg in range(MLA_ROW_SPLIT)]
        ms = [pass1(*chains[0])]
        for i, chain in enumerate(chains):
            if i + 1 < len(chains):
                ms.append(pass1(*chains[i + 1]))
            pass2(*chain, ms[i])

    for n in range(s_scr.shape[1]):
        pl.when(qi == n)(functools.partial(tile, n))


def _mla_attn(q, k, v, batch, seq, tq=512, nh=4):
    t = q.shape[0]
    nq = seq // tq
    return pl.pallas_call(
        functools.partial(_mla_attn_kernel, tq=tq),
        grid=(batch, MLA_HEADS // nh, nq),
        in_specs=[pl.BlockSpec((tq, nh * MLA_SLAB), lambda b, p, i: (b * nq + i, p)),
                  pl.BlockSpec((seq, nh * MLA_SLAB), lambda b, p, i: (b, p)),
                  pl.BlockSpec((seq, nh * MLA_SLAB), lambda b, p, i: (b, p))],
        out_specs=pl.BlockSpec((tq, nh * MLA_V), lambda b, p, i: (b * nq + i, p)),
        out_shape=jax.ShapeDtypeStruct((t, MLA_HEADS * MLA_V), BF16),
        scratch_shapes=[pltpu.VMEM((nh, nq, tq, tq), F32)],
        compiler_params=_cparams(("parallel", "parallel", "arbitrary")),
        name="mla_attn",
    )(q, k, v)


PHASE_STRIDE = 4


def _phase_of_slot(r):
    if r <= PHASE_STRIDE:
        return list(range(r))
    f2 = r // PHASE_STRIDE
    return [c1 + PHASE_STRIDE * c2 for c1 in range(PHASE_STRIDE) for c2 in range(f2)]


def _dil_proj_kernel(h_ref, w_ref, o_ref, acc_ref, tmp_ref, *, seq, r):
    nl = o_ref.shape[0]

    @pl.when(pl.program_id(0) == 0)
    def _():
        acc_ref[...] = jnp.zeros_like(acc_ref)

    res = jnp.dot(h_ref[...], w_ref[...], preferred_element_type=F32)
    f1 = min(r, PHASE_STRIDE)
    l1 = seq // f1
    dst = o_ref if r == f1 else tmp_ref
    for c1 in range(f1):
        for j in range(nl):
            src = acc_ref[j, pl.ds(c1, l1, stride=f1), :] if f1 > 1 else acc_ref[j]
            dst[j, c1 * l1:(c1 + 1) * l1, :] = src.astype(dst.dtype)
    if r > f1:
        f2 = r // f1
        l2 = l1 // f2
        for s in range(r):
            c1, c2 = divmod(s, f2)
            for j in range(nl):
                o_ref[j, s * l2:(s + 1) * l2, :] = (
                    tmp_ref[j, pl.ds(c1 * l1 + c2, l2, stride=f2), :].astype(o_ref.dtype))
    for j in range(nl):
        acc_ref[j] = res[:, j * LANES:(j + 1) * LANES]


def _dil_proj(h, w, seq, g, tn=512):
    t = h.shape[0]
    nb = t // seq
    nl = tn // LANES
    n_blocks = 3 * nb
    r = DIL_CONFIGS[g][1]

    def mm(s):
        b, tt = divmod(jnp.minimum(s, n_blocks - 1), 3)
        return tt, b

    def wr(s):
        b, tt = divmod(jnp.maximum(s - 1, 0), 3)
        return tt, b

    return pl.pallas_call(
        functools.partial(_dil_proj_kernel, seq=seq, r=r),
        grid=(n_blocks + 1,),
        in_specs=[pl.BlockSpec((seq, D_MODEL), lambda s: (mm(s)[1], 0)),
                  pl.BlockSpec((D_MODEL, tn), lambda s: (0, mm(s)[0] * DIL_GROUPS + g))],
        out_specs=pl.BlockSpec((nl, seq, LANES), lambda s: (wr(s)[0], wr(s)[1], 0)),
        out_shape=jax.ShapeDtypeStruct((3 * nl, t, LANES), BF16),
        scratch_shapes=[pltpu.VMEM((nl, seq, LANES), F32)] * 2,
        compiler_params=_cparams(("arbitrary",)),
        name="dil_proj",
    )(h, w)


def _dil_attn_kernel(slopes_ref, q0, q1, q2, k0, k1, k2, v0, v1, v2, o_ref, acc_s, m_s, l_s, *, seq):
    pair = pl.program_id(1)
    sp = DIL_SPAN
    head0 = lax.broadcasted_iota(jnp.int32, (sp, LANES), 1) < DIL_HD
    qi = lax.broadcasted_iota(jnp.int32, (sp, 2 * sp), 0)
    kj = lax.broadcasted_iota(jnp.int32, (sp, 2 * sp), 1)
    dist_w = qi + sp - kj
    valid_w = (dist_w >= 0) & (dist_w <= sp)
    dist_1 = dist_w[:, sp:]
    valid_1 = dist_1 >= 0
    qs, ks, vs = (q0, q1, q2), (k0, k1, k2), (v0, v1, v2)
    for g, (_, r) in enumerate(DIL_CONFIGS):
        ln = seq // r
        nb = ln // sp
        sl = [slopes_ref[g * DIL_HPG + 2 * pair + hh] * float(r) for hh in range(2)]
        bias_w = jnp.concatenate([jnp.where(valid_w, -s * dist_w.astype(F32), NEG) for s in sl], axis=0)
        bias_1 = jnp.concatenate([jnp.where(valid_1, -s * dist_1.astype(F32), NEG) for s in sl], axis=0)
        for slot_i, c in enumerate(_phase_of_slot(r)):
            for i in range(nb):
                row0 = slot_i * ln + i * sp
                qb = qs[g][0, row0:row0 + sp, :]
                zero = jnp.zeros_like(qb)
                q2h = jnp.concatenate([jnp.where(head0, qb, zero), jnp.where(head0, zero, qb)], axis=0)
                lo = row0 if i == 0 else row0 - sp
                kw = ks[g][0, lo:row0 + sp, :]
                vw = vs[g][0, lo:row0 + sp, :]
                s = lax.dot_general(q2h, kw, (((1,), (1,)), ((), ())), preferred_element_type=F32)
                s = s + (bias_1 if i == 0 else bias_w)
                m = jnp.max(s, axis=-1, keepdims=True)
                p = jnp.exp((s - m).astype(BF16))
                l = jnp.sum(p.astype(F32), axis=-1, keepdims=True)
                o2 = jnp.dot(p, vw, preferred_element_type=F32)
                dst = pl.ds(i * sp * r + c, sp, stride=r) if r > 1 else pl.ds(i * sp, sp)
                acc_s[g, dst, :] = jnp.where(head0, o2[:sp], o2[sp:])
                m_s[g, dst, :] = jnp.where(head0, m[:sp], m[sp:])
                l_s[g, dst, :] = jnp.where(head0, l[:sp], l[sp:])

    ch = 256

    def combine(i, _):
        rows = pl.ds(pl.multiple_of(i * ch, ch), ch)
        ms = [m_s[g, rows, :] for g in range(DIL_GROUPS)]
        mx = jnp.maximum(jnp.maximum(ms[0], ms[1]), ms[2])
        ws = [jnp.exp(m - mx) for m in ms]
        num = sum(ws[g] * acc_s[g, rows, :] for g in range(DIL_GROUPS))
        den = sum(ws[g] * l_s[g, rows, :] for g in range(DIL_GROUPS))
        o_ref[rows, :] = (num / den).astype(o_ref.dtype)
        return 0

    lax.fori_loop(0, seq // ch, combine, 0)


def _dil_attn(qkvs, slopes, batch, seq):
    t = qkvs[0].shape[1]
    pairs = DIL_HPG // 2

    def spec(tt):
        return pl.BlockSpec((1, seq, LANES), lambda b, p: (tt * pairs + p, b, 0))

    in_specs = [pl.BlockSpec(memory_space=pltpu.SMEM)] + [spec(tt) for tt in range(3) for _ in range(DIL_GROUPS)]
    return pl.pallas_call(
        functools.partial(_dil_attn_kernel, seq=seq),
        grid=(batch, pairs),
        in_specs=in_specs,
        out_specs=pl.BlockSpec((seq, LANES), lambda b, p: (b, p)),
        out_shape=jax.ShapeDtypeStruct((t, DIL_HPG * DIL_HD), BF16),
        scratch_shapes=[pltpu.VMEM((DIL_GROUPS, seq, LANES), F32)] * 3,
        compiler_params=_cparams(("parallel", "parallel")),
        name="dil_attn",
    )(slopes, *[qkvs[g] for _ in range(3) for g in range(DIL_GROUPS)])


def _to_token_tiles(dst_ref, val):
    n = val.shape[0]
    for s in range(ROW_TILES):
        dst_ref[pl.ds(s, n, stride=ROW_TILES), :] = val[:, s * LANES:(s + 1) * LANES]


def _from_token_tiles(src_ref, n):
    return jnp.concatenate([src_ref[pl.ds(s, n, stride=ROW_TILES), :] for s in range(ROW_TILES)], axis=1)


def _top2(logits):
    lane = lax.broadcasted_iota(jnp.int32, logits.shape, 1)
    m1 = jnp.max(logits, axis=-1, keepdims=True)
    i1 = jnp.min(jnp.where(logits == m1, lane, LANES), axis=-1, keepdims=True)
    rest = jnp.where(lane == i1, NEG, logits)
    m2 = jnp.max(rest, axis=-1, keepdims=True)
    i2 = jnp.min(jnp.where(rest == m2, lane, LANES), axis=-1, keepdims=True)
    e = jnp.exp(m2 - m1)
    w1 = 1.0 / (1.0 + e)
    return (jnp.where(lane == 0, i1, jnp.where(lane == 1, i2, 0)),
            jnp.where(lane == 0, w1, jnp.where(lane == 1, e * w1, 0.0)))


def _merge_kernel(x_ref, h_ref, om_ref, od_ref, wg_ref, wbm_ref, wbd_ref, wo_ref, g_ref, *rest, route):
    gates = jnp.dot(h_ref[...], wg_ref[...], preferred_element_type=F32)
    bm = jnp.dot(om_ref[...], wbm_ref[...], preferred_element_type=F32)
    bd = jnp.dot(od_ref[...], wbd_ref[...], preferred_element_type=F32)
    merged = _sigmoid(gates[:, :D_MODEL]) * bm + _sigmoid(gates[:, D_MODEL:]) * bd
    x1 = x_ref[...] + jnp.dot(merged.astype(BF16), wo_ref[...], preferred_element_type=F32)
    hn = _rms(x1, g_ref[...])
    if route:
        wr_ref, br_ref, x_out, h_out, idx_out, wt_out = rest
        _to_token_tiles(h_out, hn)
        hi = hn.astype(BF16)
        lo = (hn - hi.astype(F32)).astype(BF16)
        w = wr_ref[...]
        w_hi = w.astype(BF16)
        w_lo = (w - w_hi.astype(F32)).astype(BF16)
        both = jnp.dot(hi, jnp.concatenate([w_hi, w_lo], axis=1), preferred_element_type=F32)
        logits = (both[:, :LANES] + both[:, LANES:]
                  + jnp.dot(lo, w_hi, preferred_element_type=F32) + br_ref[...])
        idx_out[...], wt_out[...] = _top2(logits)
    else:
        x_out, h_out = rest
        h_out[...] = hn.astype(h_out.dtype)
    x_out[...] = x1


def _merge(x, h, o_mla, o_dil, wg, wbm, wbd, wo, g_next, router=None, tm=512):
    t = x.shape[0]
    row = lambda w: pl.BlockSpec((tm, w), lambda i: (i, 0))
    full = lambda a: pl.BlockSpec(a.shape, lambda i: (0, 0))
    in_specs = [row(D_MODEL), row(D_MODEL), row(o_mla.shape[1]), row(o_dil.shape[1]),
                full(wg), full(wbm), full(wbd), full(wo), full(g_next)]
    args = [x, h, o_mla, o_dil, wg, wbm, wbd, wo, g_next]
    out_specs = [row(D_MODEL)]
    out_shape = [jax.ShapeDtypeStruct((t, D_MODEL), F32)]
    if router is None:
        out_specs += [row(D_MODEL)]
        out_shape += [jax.ShapeDtypeStruct((t, D_MODEL), BF16)]
    else:
        in_specs += [full(router[0]), full(router[1])]
        args += list(router)
        out_specs += [pl.BlockSpec((tm * ROW_TILES, LANES), lambda i: (i, 0)), row(LANES), row(LANES)]
        out_shape += [jax.ShapeDtypeStruct((t * ROW_TILES, LANES), F32),
                      jax.ShapeDtypeStruct((t, LANES), jnp.int32), jax.ShapeDtypeStruct((t, LANES), F32)]
    return pl.pallas_call(
        functools.partial(_merge_kernel, route=router is not None),
        grid=(t // tm,),
        in_specs=in_specs,
        out_specs=out_specs,
        out_shape=out_shape,
        compiler_params=_cparams(("parallel",)),
        name="merge",
    )(*args)


def _ffn_kernel(x_ref, h_ref, wg_ref, wu_ref, wd_ref, o_ref):
    @pl.when(pl.program_id(1) == 0)
    def _():
        o_ref[...] = x_ref[...]

    h = h_ref[...]
    a = jnp.dot(h, wg_ref[...], preferred_element_type=F32)
    u = jnp.dot(h, wu_ref[...], preferred_element_type=F32)
    o_ref[...] += jnp.dot((a * _sigmoid(a) * u).astype(BF16), wd_ref[...], preferred_element_type=F32)


def _ffn(x, h, wg, wu, wd, tm=512, tf=1792):
    t = x.shape[0]
    nf = wg.shape[1] // tf
    return pl.pallas_call(
        _ffn_kernel,
        grid=(t // tm, nf),
        in_specs=[pl.BlockSpec((tm, D_MODEL), lambda i, f: (i, 0)),
                  pl.BlockSpec((tm, D_MODEL), lambda i, f: (i, 0)),
                  pl.BlockSpec((D_MODEL, tf), lambda i, f: (0, f)),
                  pl.BlockSpec((D_MODEL, tf), lambda i, f: (0, f)),
                  pl.BlockSpec((tf, D_MODEL), lambda i, f: (f, 0))],
        out_specs=pl.BlockSpec((tm, D_MODEL), lambda i, f: (i, 0)),
        out_shape=jax.ShapeDtypeStruct((t, D_MODEL), F32),
        compiler_params=_cparams(("parallel", "arbitrary")),
        name="dense_ffn",
    )(x, h, wg, wu, wd)


def _tile_at(ref, row8):
    return ref.at[pl.ds(pl.multiple_of(row8, ROW_TILES), ROW_TILES), :]


def _dispatch_kernel(pos_ref, fill_ref, h_ref, xs_hbm, zero_ref, sem, sem_fill, *, t, tmd, tm):
    i = pl.program_id(0)

    @pl.when(i == 0)
    def _():
        zero_ref[...] = jnp.zeros_like(zero_ref)
        for e in range(fill_ref.shape[0]):
            @pl.when(fill_ref[e] >= 0)
            def _():
                cp = pltpu.make_async_copy(
                    zero_ref, xs_hbm.at[pl.ds(pl.multiple_of(fill_ref[e], ROW_TILES), tm * ROW_TILES), :], sem_fill)
                cp.start()
                cp.wait()

    for k in range(TOP_K):
        for r in range(tmd):
            pltpu.make_async_copy(_tile_at(h_ref, r * ROW_TILES),
                                  _tile_at(xs_hbm, pos_ref[k * t + i * tmd + r]), sem.at[k]).start(priority=r % 2)
    for k in range(TOP_K):
        pltpu.make_async_copy(h_ref, h_ref, sem.at[k]).wait()


def _dispatch(h, pos8, fill8, n_rows, tm, tmd=512):
    t = h.shape[0] // ROW_TILES
    return pl.pallas_call(
        functools.partial(_dispatch_kernel, t=t, tmd=tmd, tm=tm),
        grid_spec=pltpu.PrefetchScalarGridSpec(
            num_scalar_prefetch=2,
            grid=(t // tmd,),
            in_specs=[pl.BlockSpec((tmd * ROW_TILES, LANES), lambda i, pos, fill: (i, 0))],
            out_specs=pl.BlockSpec(memory_space=pl.ANY),
            scratch_shapes=[pltpu.VMEM((tm * ROW_TILES, LANES), F32),
                            pltpu.SemaphoreType.DMA((TOP_K,)), pltpu.SemaphoreType.DMA(())]),
        out_shape=jax.ShapeDtypeStruct((n_rows * ROW_TILES, LANES), F32),
        compiler_params=_cparams(("arbitrary",)),
        name="moe_dispatch",
    )(pos8, fill8, h)


def _moe_ffn_kernel(te_ref, nv_ref, x_ref, wg_ref, wu_ref, wd_ref, o_ref, acc_ref, xb_ref, *, tm):
    i = pl.program_id(0)
    f = pl.program_id(1)
    valid = i < nv_ref[0]
    last = f == pl.num_programs(1) - 1

    @pl.when(valid & (f == 0))
    def _():
        xb_ref[...] = _from_token_tiles(x_ref, tm).astype(BF16)
        acc_ref[...] = jnp.zeros_like(acc_ref)

    @pl.when(valid)
    def _():
        h = xb_ref[...]
        a = jnp.dot(h, wg_ref[0], preferred_element_type=F32)
        u = jnp.dot(h, wu_ref[0], preferred_element_type=F32)
        acc_ref[...] += jnp.dot((a * _sigmoid(a) * u).astype(BF16), wd_ref[0], preferred_element_type=F32)

    @pl.when(valid & last)
    def _():
        _to_token_tiles(o_ref, acc_ref[...])

    @pl.when(jnp.logical_not(valid) & last)
    def _():
        o_ref[...] = jnp.zeros_like(o_ref)


def _moe_ffn(xs, tile_expert, n_valid, wg, wu, wd, tm, tf=1792):
    n_tiles = xs.shape[0] // (tm * ROW_TILES)
    nf = wg.shape[2] // tf

    def live(i, f, te, nv):
        ok = i < nv[0]
        return jnp.where(ok, i, nv[0] - 1), jnp.where(ok, f, nf - 1)

    return pl.pallas_call(
        functools.partial(_moe_ffn_kernel, tm=tm),
        grid_spec=pltpu.PrefetchScalarGridSpec(
            num_scalar_prefetch=2,
            grid=(n_tiles, nf),
            in_specs=[pl.BlockSpec((tm * ROW_TILES, LANES), lambda *a: (live(*a)[0], 0)),
                      pl.BlockSpec((1, D_MODEL, tf), lambda *a: (a[2][live(*a)[0]], 0, live(*a)[1])),
                      pl.BlockSpec((1, D_MODEL, tf), lambda *a: (a[2][live(*a)[0]], 0, live(*a)[1])),
                      pl.BlockSpec((1, tf, D_MODEL), lambda *a: (a[2][live(*a)[0]], live(*a)[1], 0))],
            out_specs=pl.BlockSpec((tm * ROW_TILES, LANES), lambda i, f, te, nv: (i, 0)),
            scratch_shapes=[pltpu.VMEM((tm, D_MODEL), F32), pltpu.VMEM((tm, D_MODEL), BF16)]),
        out_shape=jax.ShapeDtypeStruct(xs.shape, F32),
        compiler_params=_cparams(("arbitrary", "arbitrary")),
        name="moe_ffn",
    )(tile_expert, n_valid, xs, wg, wu, wd)


def _router_params(w_router, b_router):
    w_pad = jnp.zeros((D_MODEL, LANES), F32).at[:, :N_EXPERTS].set(w_router)
    b_pad = jnp.full((1, LANES), NEG, F32).at[0, :N_EXPERTS].set(b_router)
    return w_pad, b_pad


def _moe(h, idx_l, wg, wu, wd, tm=448):
    t = h.shape[0] // ROW_TILES
    expert = idx_l[:, :TOP_K].T.reshape(-1)
    onehot = (expert[:, None] == jnp.arange(N_EXPERTS)[None, :]).astype(jnp.int32)
    csum = jnp.cumsum(onehot, axis=0)
    counts = csum[-1]
    rank = jnp.sum((csum - onehot) * onehot, axis=1)
    tiles = (counts + tm - 1) // tm
    tile_end = jnp.cumsum(tiles)
    pad_off = (tile_end - tiles) * tm
    pos8 = (pad_off[expert] + rank) * ROW_TILES
    n_tiles = (TOP_K * t) // tm + N_EXPERTS
    n_valid = tile_end[-1:].astype(jnp.int32)
    tile_ids = jnp.minimum(jnp.arange(n_tiles), n_valid[0] - 1)
    tile_expert = jnp.sum(tile_end[None, :] <= tile_ids[:, None], axis=1).astype(jnp.int32)
    spare = n_valid[0] + jnp.arange(N_EXPERTS)
    fill8 = jnp.concatenate([jnp.where(tiles > 0, tile_end - 1, -1), jnp.where(spare < n_tiles, spare, -1)])
    fill8 = jnp.where(fill8 >= 0, fill8 * (tm * ROW_TILES), -1).astype(jnp.int32)
    xs = _dispatch(h, pos8, fill8, n_tiles * tm, tm)
    ys = _moe_ffn(xs, tile_expert, n_valid, wg, wu, wd, tm)
    return ys, pos8


def _ple_body(x, p_ref, wg_ref, wp_ref, g_ref, gn_ref, outs):
    gate = _sigmoid(jnp.dot(_rms(x, g_ref[...]).astype(BF16), wg_ref[...], preferred_element_type=F32))
    x2 = x + gate * jnp.dot(p_ref[...].astype(BF16), wp_ref[...], preferred_element_type=F32)
    if len(outs) == 2:
        outs[0][...] = x2
    outs[-1][...] = _rms(x2, gn_ref[...]).astype(outs[-1].dtype)


def _ple_kernel(x_ref, p_ref, wg_ref, wp_ref, g_ref, gn_ref, *outs):
    _ple_body(x_ref[...], p_ref, wg_ref, wp_ref, g_ref, gn_ref, outs)


def _ple_moe_kernel(pos_ref, x_ref, p_ref, wt_ref, wg_ref, wp_ref, g_ref, gn_ref, y_hbm, *rest, t, tm):
    *outs, ybuf, sem = rest
    i = pl.program_id(0)
    slot = i % 2

    def gather(block, s):
        for k in range(TOP_K):
            for r in range(tm):
                pltpu.make_async_copy(_tile_at(y_hbm, pos_ref[k * t + block * tm + r]),
                                      _tile_at(ybuf.at[s, k], r * ROW_TILES), sem.at[s]).start(priority=r % 2)

    @pl.when(i == 0)
    def _():
        gather(0, 0)

    @pl.when(i + 1 < pl.num_programs(0))
    def _():
        gather(i + 1, 1 - slot)

    pltpu.make_async_copy(ybuf.at[slot], ybuf.at[slot], sem.at[slot]).wait()
    wt = wt_ref[...]
    x = (x_ref[...] + wt[:, 0:1] * _from_token_tiles(ybuf.at[slot, 0], tm)
         + wt[:, 1:2] * _from_token_tiles(ybuf.at[slot, 1], tm))
    _ple_body(x, p_ref, wg_ref, wp_ref, g_ref, gn_ref, outs)


def _ple(x, p, wg, wp, g, g_next, h_dtype, want_x, moe=None, tm=512):
    t = x.shape[0]
    n_out = 2 if want_x else 1
    out_shape = ([jax.ShapeDtypeStruct((t, D_MODEL), F32)] * (n_out - 1)
                 + [jax.ShapeDtypeStruct((t, D_MODEL), h_dtype)])
    if moe is None:
        row = lambda w: pl.BlockSpec((tm, w), lambda i: (i, 0))
        full = lambda a: pl.BlockSpec(a.shape, lambda i: (0, 0))
        return pl.pallas_call(
            _ple_kernel,
            grid=(t // tm,),
            in_specs=[row(D_MODEL), row(P_DIM), full(wg), full(wp), full(g), full(g_next)],
            out_specs=[row(D_MODEL)] * n_out,
            out_shape=out_shape,
            compiler_params=_cparams(("parallel",)),
            name="ple",
        )(x, p, wg, wp, g, g_next)
    ys, pos8, wt = moe
    tm = tm // 2
    row = lambda w: pl.BlockSpec((tm, w), lambda i, pos: (i, 0))
    full = lambda a: pl.BlockSpec(a.shape, lambda i, pos: (0, 0))
    return pl.pallas_call(
        functools.partial(_ple_moe_kernel, t=t, tm=tm),
        grid_spec=pltpu.PrefetchScalarGridSpec(
            num_scalar_prefetch=1,
            grid=(t // tm,),
            in_specs=[row(D_MODEL), row(P_DIM), row(LANES), full(wg), full(wp), full(g), full(g_next),
                      pl.BlockSpec(memory_space=pl.ANY)],
            out_specs=[row(D_MODEL)] * n_out,
            scratch_shapes=[pltpu.VMEM((2, TOP_K, tm * ROW_TILES, LANES), F32), pltpu.SemaphoreType.DMA((2,))]),
        out_shape=out_shape,
        compiler_params=_cparams(("arbitrary",)),
        name="ple_moe",
    )(pos8, x, p, wt, wg, wp, g, g_next, ys)


def _rot_cols(w):
    half = w.shape[-1] // 2
    return jnp.concatenate([-w[:, half:], w[:, :half]], axis=-1)


def _layer_weights(i, w_in, w_uq, w_ukv):
    wi = w_in[i]
    z = lambda n: jnp.zeros((wi.shape[0], n), F32)
    w_kr = wi[:, IN_OFF[2]:IN_OFF[3]]
    pad = MLA_SLAB - MLA_NOPE - MLA_ROPE
    wlat = jnp.concatenate([wi[:, :IN_OFF[2]], z(MLA_NOPE), w_kr, z(pad), z(MLA_NOPE), _rot_cols(w_kr), z(pad)],
                           axis=1).astype(BF16)
    qscale = jnp.concatenate([jnp.full((DIL_WIDTH,), DIL_HD ** -0.5, F32), jnp.ones((2 * DIL_WIDTH,), F32)])
    wdil = (wi[:, IN_OFF[3]:IN_OFF[6]] * qscale).astype(BF16)
    wgate = wi[:, IN_OFF[6]:].astype(BF16)
    uq = w_uq[i].reshape(MLA_Q_RANK, MLA_HEADS, MLA_NOPE + MLA_ROPE)
    zq = lambda n: jnp.zeros((MLA_Q_RANK, MLA_HEADS, n), F32)
    rope_rot = jnp.concatenate([-uq[..., MLA_NOPE + MLA_ROPE // 2:], uq[..., MLA_NOPE:MLA_NOPE + MLA_ROPE // 2]], -1)
    wq = jnp.concatenate([uq, zq(pad)], axis=-1).reshape(MLA_Q_RANK, -1).astype(BF16)
    wqr = jnp.concatenate([zq(MLA_NOPE), rope_rot, zq(pad)], axis=-1).reshape(MLA_Q_RANK, -1).astype(BF16)
    ukv = w_ukv[i].reshape(MLA_KV_RANK, MLA_HEADS, MLA_NOPE + MLA_V)
    zkv = jnp.zeros((MLA_KV_RANK, MLA_HEADS, MLA_SLAB - MLA_NOPE), F32)
    wk = jnp.concatenate([ukv[..., :MLA_NOPE], zkv], axis=-1).reshape(MLA_KV_RANK, -1).astype(BF16)
    wv = jnp.concatenate([ukv[..., MLA_NOPE:], zkv], axis=-1).reshape(MLA_KV_RANK, -1).astype(BF16)
    return wlat, wdil, wgate, wq, wqr, wk, wv


def kernel(x, p, positions, attn_norm, w_in, q_norm, w_uq, kv_norm, w_ukv, w_br_mla, w_br_dil, w_out, ffn_norm, dense_w_gate, dense_w_up, dense_w_down, router_w, router_b, moe_w_gate, moe_w_up, moe_w_down, ple_norm, ple_w_gate, ple_w_proj, final_norm):
    batch, seq, d = x.shape
    t = batch * seq
    depth = w_in.shape[0]
    xf = x.reshape(t, d)
    cos_t, sin_t = _rope_tables(positions)
    slopes = _alibi_slopes(DIL_HEADS)
    row = lambda v: v.reshape(1, -1)
    h = _norm(xf, attn_norm[0])
    for i in range(depth):
        wlat, wdil, wgate, wq, wqr, wk, wv = _layer_weights(i, w_in, w_uq, w_ukv)
        q, k, v = _mla_proj(h, wlat, wq, wqr, wk, wv, row(q_norm[i]), row(kv_norm[i]), cos_t, sin_t)
        o_mla = _mla_attn(q, k, v, batch, seq)
        qkv_d = [_dil_proj(h, wdil, seq, g) for g in range(DIL_GROUPS)]
        o_dil = _dil_attn(qkv_d, slopes, batch, seq)
        moe_layer = i % 2 == 1
        j = i // 2
        xf, h2, *routing = _merge(xf, h, o_mla, o_dil, wgate, w_br_mla[i].astype(BF16), w_br_dil[i].astype(BF16),
                                  w_out[i].astype(BF16), row(ffn_norm[i]),
                                  _router_params(router_w[j], router_b[j]) if moe_layer else None)
        moe = None
        if moe_layer:
            idx_l, wt_l = routing
            moe = _moe(h2, idx_l, moe_w_gate[j].astype(BF16), moe_w_up[j].astype(BF16),
                       moe_w_down[j].astype(BF16)) + (wt_l,)
        else:
            xf = _ffn(xf, h2, dense_w_gate[j].astype(BF16), dense_w_up[j].astype(BF16),
                      dense_w_down[j].astype(BF16))
        last = i == depth - 1
        g_next = final_norm if last else attn_norm[i + 1]
        *xnext, h = _ple(xf, p[i].reshape(t, -1), ple_w_gate[i].astype(BF16), ple_w_proj[i].astype(BF16),
                         row(ple_norm[i]), row(g_next), F32 if last else BF16, not last, moe)
        xf = xnext[0] if xnext else None
    return h.reshape(batch, seq, d)
```

```python
import functools

import numpy as np
import jax
import jax.numpy as jnp
from jax import lax
from jax.experimental import pallas as pl
from jax.experimental.pallas import tpu as pltpu

F32 = jnp.float32
BF16 = jnp.bfloat16

D_MODEL = 1024
P_DIM = 256
NORM_EPS = 1e-6
DEPTH = 2

MLA_HEADS = 8
MLA_Q_RANK = 384
MLA_KV_RANK = 256
MLA_NOPE = 64
MLA_ROPE = 32
MLA_V = 64
ROPE_THETA = 10000.0
MLA_SLAB = 128

DIL_CONFIGS = ((128, 1), (512, 4), (2048, 16))
DIL_GROUPS = 3
DIL_HPG = 8
DIL_HEADS = 24
DIL_HD = 64
DIL_WIDTH = DIL_HEADS * DIL_HD
DIL_SPAN = 128

D_FF = 3584
N_EXPERTS = 8
TOP_K = 2

LANES = 128
ROW_TILES = D_MODEL // LANES
NEG = -1e30
LOG2E = 1.4426950408889634
VMEM_LIMIT = 56 * 1024 * 1024

IN_OFF = tuple(int(o) for o in np.cumsum((0, MLA_Q_RANK, MLA_KV_RANK, MLA_ROPE, DIL_WIDTH, DIL_WIDTH,
                                          DIL_WIDTH, D_MODEL, D_MODEL)))


def _cparams(sem):
    return pltpu.CompilerParams(dimension_semantics=sem, vmem_limit_bytes=VMEM_LIMIT)


def _rms(x, g):
    return x * lax.rsqrt(jnp.mean(x * x, axis=-1, keepdims=True) + NORM_EPS) * g


def _sigmoid(x):
    return 1.0 / (1.0 + jnp.exp(-x))


def _alibi_slopes(n):
    def pow2(m):
        start = 2.0 ** (-8.0 / m)
        return [start ** (i + 1) for i in range(m)]
    if float(np.log2(n)).is_integer():
        s = pow2(n)
    else:
        c = 2 ** int(np.floor(np.log2(n)))
        s = pow2(c) + pow2(2 * c)[0::2][: n - c]
    return jnp.asarray(sorted(s, reverse=True), dtype=F32)


def _norm_kernel(x_ref, g_ref, o_ref):
    o_ref[...] = _rms(x_ref[...], g_ref[...]).astype(o_ref.dtype)


def _norm(x, g, tm=1024):
    t, d = x.shape
    return pl.pallas_call(
        _norm_kernel,
        grid=(t // tm,),
        in_specs=[pl.BlockSpec((tm, d), lambda i: (i, 0)), pl.BlockSpec((1, d), lambda i: (0, 0))],
        out_specs=pl.BlockSpec((tm, d), lambda i: (i, 0)),
        out_shape=jax.ShapeDtypeStruct((t, d), BF16),
        compiler_params=_cparams(("parallel",)),
        name="rms_norm",
    )(x, g.reshape(1, d))


def _rope_table_kernel(pos_ref, invf_ref, cos_ref, sin_ref):
    ang = pos_ref[...].astype(F32) * invf_ref[...]
    lane = lax.broadcasted_iota(jnp.int32, ang.shape, 1)
    rope_lane = (lane >= MLA_NOPE) & (lane < MLA_NOPE + MLA_ROPE)
    cos_ref[...] = jnp.where(lane < MLA_NOPE, 1.0, jnp.where(rope_lane, jnp.cos(ang), 0.0))
    sin_ref[...] = jnp.where(rope_lane, jnp.sin(ang), 0.0)


def _rope_tables(positions, tm=2048):
    t = positions.size
    half = MLA_ROPE // 2
    inv_freq = ROPE_THETA ** (-jnp.arange(half, dtype=F32) / half)
    invf = jnp.zeros((1, LANES), F32).at[0, MLA_NOPE:MLA_NOPE + MLA_ROPE].set(jnp.concatenate([inv_freq, inv_freq]))
    pos_b = jnp.broadcast_to(positions.reshape(t, 1), (t, LANES))
    spec = pl.BlockSpec((tm, LANES), lambda i: (i, 0))
    return pl.pallas_call(
        _rope_table_kernel,
        grid=(t // tm,),
        in_specs=[spec, pl.BlockSpec((1, LANES), lambda i: (0, 0))],
        out_specs=[spec, spec],
        out_shape=[jax.ShapeDtypeStruct((t, LANES), F32)] * 2,
        compiler_params=_cparams(("parallel",)),
        name="rope_tables",
    )(pos_b, invf)


def _mla_proj_kernel(h_ref, wlat_ref, wq_ref, wqr_ref, wk_ref, wv_ref, qn_ref, kvn_ref, cos_ref, sin_ref,
                     q_out, k_out, v_out):
    lat = jnp.dot(h_ref[...], wlat_ref[...], preferred_element_type=F32)
    cqn = _rms(lat[:, :MLA_Q_RANK], qn_ref[...]).astype(BF16)
    ckvn = _rms(lat[:, MLA_Q_RANK:MLA_Q_RANK + MLA_KV_RANK], kvn_ref[...]).astype(BF16)
    cos = cos_ref[...]
    sin = sin_ref[...]
    o = MLA_Q_RANK + MLA_KV_RANK
    k_rope = lat[:, o:o + LANES] * cos + lat[:, o + LANES:o + 2 * LANES] * sin
    qa = jnp.dot(cqn, wq_ref[...], preferred_element_type=F32)
    qb = jnp.dot(cqn, wqr_ref[...], preferred_element_type=F32)
    kk = jnp.dot(ckvn, wk_ref[...], preferred_element_type=F32)
    scale = (MLA_NOPE + MLA_ROPE) ** -0.5 * LOG2E
    vv = jnp.dot(ckvn, wv_ref[...], preferred_element_type=F32)
    ones_lane = lax.broadcasted_iota(jnp.int32, cos.shape, 1) >= MLA_V
    for hd in range(MLA_HEADS):
        sl = slice(hd * MLA_SLAB, (hd + 1) * MLA_SLAB)
        q_out[:, sl] = ((qa[:, sl] * cos + qb[:, sl] * sin) * scale).astype(BF16)
        k_out[:, sl] = (kk[:, sl] + k_rope).astype(BF16)
        v_out[:, sl] = jnp.where(ones_lane, 1.0, vv[:, sl]).astype(BF16)


def _mla_proj(h, wlat, wq, wqr, wk, wv, qn, kvn, cos_t, sin_t, tm=512):
    t = h.shape[0]
    row = lambda w: pl.BlockSpec((tm, w), lambda i: (i, 0))
    full = lambda a: pl.BlockSpec(a.shape, lambda i: (0, 0))
    hs = MLA_HEADS * MLA_SLAB
    return pl.pallas_call(
        _mla_proj_kernel,
        grid=(t // tm,),
        in_specs=[row(D_MODEL), full(wlat), full(wq), full(wqr), full(wk), full(wv), full(qn), full(kvn),
                  row(LANES), row(LANES)],
        out_specs=[row(hs), row(hs), row(hs)],
        out_shape=[jax.ShapeDtypeStruct((t, hs), BF16)] * 3,
        compiler_params=_cparams(("parallel",)),
        name="mla_proj",
    )(h, wlat, wq, wqr, wk, wv, qn, kvn, cos_t, sin_t)


MLA_ROW_SPLIT = 2


def _mla_attn_kernel(q_ref, k_ref, v_ref, o_ref, s_scr, *, tq):
    qi = pl.program_id(2)
    nh = q_ref.shape[1] // MLA_SLAB
    tr = tq // MLA_ROW_SPLIT

    def tile(n):
        def keys(c, rg):
            return (rg + 1) * tr if c == n else tq

        def pass1(hh, rg):
            rows = slice(rg * tr, (rg + 1) * tr)
            q = q_ref[rows, hh * MLA_SLAB:(hh + 1) * MLA_SLAB]
            mx = None
            for c in range(n + 1):
                kw = keys(c, rg)
                ks = k_ref[c * tq:c * tq + kw, hh * MLA_SLAB:(hh + 1) * MLA_SLAB]
                s = lax.dot_general(q, ks, (((1,), (1,)), ((), ())), preferred_element_type=F32)
                if c == n:
                    rq = lax.broadcasted_iota(jnp.int32, (tr, kw), 0)
                    ck = lax.broadcasted_iota(jnp.int32, (tr, kw), 1)
                    s = jnp.where(ck <= rq + rg * tr, s, NEG)
                s_scr[hh, c, rows, :kw] = s
                parts = [s[:, i * LANES:(i + 1) * LANES] for i in range(kw // LANES)]
                mx = functools.reduce(jnp.maximum, parts if mx is None else [mx] + parts)
            return jnp.max(mx, axis=-1, keepdims=True)

        def pass2(hh, rg, m):
            rows = slice(rg * tr, (rg + 1) * tr)
            acc = None
            for c in range(n + 1):
                kw = keys(c, rg)
                vs = v_ref[c * tq:c * tq + kw, hh * MLA_SLAB:(hh + 1) * MLA_SLAB]
                p = jnp.exp2((s_scr[hh, c, rows, :kw] - m).astype(BF16))
                pv = jnp.dot(p, vs, preferred_element_type=F32)
                acc = pv if acc is None else acc + pv
            o = acc / acc[:, MLA_V:MLA_V + 1]
            o_ref[rows, hh * MLA_V:(hh + 1) * MLA_V] = o[:, :MLA_V].astype(o_ref.dtype)

        chains = [(hh, rg) for hh in range(nh) for rg in range(MLA_ROW_SPLIT)]
        ms = [pass1(*chains[0])]
        for i, chain in enumerate(chains):
            if i + 1 < len(chains):
                ms.append(pass1(*chains[i + 1]))
            pass2(*chain, ms[i])

    for n in range(s_scr.shape[1]):
        pl.when(qi == n)(functools.partial(tile, n))


def _mla_attn(q, k, v, batch, seq, tq=512, nh=4):
    t = q.shape[0]
    nq = seq // tq
    return pl.pallas_call(
        functools.partial(_mla_attn_kernel, tq=tq),
        grid=(batch, MLA_HEADS // nh, nq),
        in_specs=[pl.BlockSpec((tq, nh * MLA_SLAB), lambda b, p, i: (b * nq + i, p)),
                  pl.BlockSpec((seq, nh * MLA_SLAB), lambda b, p, i: (b, p)),
                  pl.BlockSpec((seq, nh * MLA_SLAB), lambda b, p, i: (b, p))],
        out_specs=pl.BlockSpec((tq, nh * MLA_V), lambda b, p, i: (b * nq + i, p)),
        out_shape=jax.ShapeDtypeStruct((t, MLA_HEADS * MLA_V), BF16),
        scratch_shapes=[pltpu.VMEM((nh, nq, tq, tq), F32)],
        compiler_params=_cparams(("parallel", "parallel", "arbitrary")),
        name="mla_attn",
    )(q, k, v)


PHASE_STRIDE = 4


def _phase_of_slot(r):
    if r <= PHASE_STRIDE:
        return list(range(r))
    f2 = r // PHASE_STRIDE
    return [c1 + PHASE_STRIDE * c2 for c1 in range(PHASE_STRIDE) for c2 in range(f2)]


def _dil_proj_kernel(h_ref, w_ref, o_ref, acc_ref, tmp_ref, *, seq, r):
    nl = o_ref.shape[0]

    @pl.when(pl.program_id(0) == 0)
    def _():
        acc_ref[...] = jnp.zeros_like(acc_ref)

    res = jnp.dot(h_ref[...], w_ref[...], preferred_element_type=F32)
    f1 = min(r, PHASE_STRIDE)
    l1 = seq // f1
    dst = o_ref if r == f1 else tmp_ref
    for c1 in range(f1):
        for j in range(nl):
            src = acc_ref[j, pl.ds(c1, l1, stride=f1), :] if f1 > 1 else acc_ref[j]
            dst[j, c1 * l1:(c1 + 1) * l1, :] = src.astype(dst.dtype)
    if r > f1:
        f2 = r // f1
        l2 = l1 // f2
        for s in range(r):
            c1, c2 = divmod(s, f2)
            for j in range(nl):
                o_ref[j, s * l2:(s + 1) * l2, :] = (
                    tmp_ref[j, pl.ds(c1 * l1 + c2, l2, stride=f2), :].astype(o_ref.dtype))
    for j in range(nl):
        acc_ref[j] = res[:, j * LANES:(j + 1) * LANES]


def _dil_proj(h, w, seq, g, tn=512):
    t = h.shape[0]
    nb = t // seq
    nl = tn // LANES
    n_blocks = 3 * nb
    r = DIL_CONFIGS[g][1]

    def mm(s):
        b, tt = divmod(jnp.minimum(s, n_blocks - 1), 3)
        return tt, b

    def wr(s):
        b, tt = divmod(jnp.maximum(s - 1, 0), 3)
        return tt, b

    return pl.pallas_call(
        functools.partial(_dil_proj_kernel, seq=seq, r=r),
        grid=(n_blocks + 1,),
        in_specs=[pl.BlockSpec((seq, D_MODEL), lambda s: (mm(s)[1], 0)),
                  pl.BlockSpec((D_MODEL, tn), lambda s: (0, mm(s)[0] * DIL_GROUPS + g))],
        out_specs=pl.BlockSpec((nl, seq, LANES), lambda s: (wr(s)[0], wr(s)[1], 0)),
        out_shape=jax.ShapeDtypeStruct((3 * nl, t, LANES), BF16),
        scratch_shapes=[pltpu.VMEM((nl, seq, LANES), F32)] * 2,
        compiler_params=_cparams(("arbitrary",)),
        name="dil_proj",
    )(h, w)


def _dil_attn_kernel(slopes_ref, q0, q1, q2, k0, k1, k2, v0, v1, v2, o_ref, acc_s, m_s, l_s, *, seq):
    pair = pl.program_id(1)
    sp = DIL_SPAN
    head0 = lax.broadcasted_iota(jnp.int32, (sp, LANES), 1) < DIL_HD
    qi = lax.broadcasted_iota(jnp.int32, (sp, 2 * sp), 0)
    kj = lax.broadcasted_iota(jnp.int32, (sp, 2 * sp), 1)
    dist_w = qi + sp - kj
    valid_w = (dist_w >= 0) & (dist_w <= sp)
    dist_1 = dist_w[:, sp:]
    valid_1 = dist_1 >= 0
    qs, ks, vs = (q0, q1, q2), (k0, k1, k2), (v0, v1, v2)
    for g, (_, r) in enumerate(DIL_CONFIGS):
        ln = seq // r
        nb = ln // sp
        sl = [slopes_ref[g * DIL_HPG + 2 * pair + hh] * float(r) for hh in range(2)]
        bias_w = jnp.concatenate([jnp.where(valid_w, -s * dist_w.astype(F32), NEG) for s in sl], axis=0)
        bias_1 = jnp.concatenate([jnp.where(valid_1, -s * dist_1.astype(F32), NEG) for s in sl], axis=0)
        for slot_i, c in enumerate(_phase_of_slot(r)):
            for i in range(nb):
                row0 = slot_i * ln + i * sp
                qb = qs[g][0, row0:row0 + sp, :]
                zero = jnp.zeros_like(qb)
                q2h = jnp.concatenate([jnp.where(head0, qb, zero), jnp.where(head0, zero, qb)], axis=0)
                lo = row0 if i == 0 else row0 - sp
                kw = ks[g][0, lo:row0 + sp, :]
                vw = vs[g][0, lo:row0 + sp, :]
                s = lax.dot_general(q2h, kw, (((1,), (1,)), ((), ())), preferred_element_type=F32)
                s = s + (bias_1 if i == 0 else bias_w)
                m = jnp.max(s, axis=-1, keepdims=True)
                p = jnp.exp((s - m).astype(BF16))
                l = jnp.sum(p.astype(F32), axis=-1, keepdims=True)
                o2 = jnp.dot(p, vw, preferred_element_type=F32)
                dst = pl.ds(i * sp * r + c, sp, stride=r) if r > 1 else pl.ds(i * sp, sp)
                acc_s[g, dst, :] = jnp.where(head0, o2[:sp], o2[sp:])
                m_s[g, dst, :] = jnp.where(head0, m[:sp], m[sp:])
                l_s[g, dst, :] = jnp.where(head0, l[:sp], l[sp:])

    ch = 256

    def combine(i, _):
        rows = pl.ds(pl.multiple_of(i * ch, ch), ch)
        ms = [m_s[g, rows, :] for g in range(DIL_GROUPS)]
        mx = jnp.maximum(jnp.maximum(ms[0], ms[1]), ms[2])
        ws = [jnp.exp(m - mx) for m in ms]
        num = sum(ws[g] * acc_s[g, rows, :] for g in range(DIL_GROUPS))
        den = sum(ws[g] * l_s[g, rows, :] for g in range(DIL_GROUPS))
        o_ref[rows, :] = (num / den).astype(o_ref.dtype)
        return 0

    lax.fori_loop(0, seq // ch, combine, 0)


def _dil_attn(qkvs, slopes, batch, seq):
    t = qkvs[0].shape[1]
    pairs = DIL_HPG // 2

    def spec(tt):
        return pl.BlockSpec((1, seq, LANES), lambda b, p: (tt * pairs + p, b, 0))

    in_specs = [pl.BlockSpec(memory_space=pltpu.SMEM)] + [spec(tt) for tt in range(3) for _ in range(DIL_GROUPS)]
    return pl.pallas_call(
        functools.partial(_dil_attn_kernel, seq=seq),
        grid=(batch, pairs),
        in_specs=in_specs,
        out_specs=pl.BlockSpec((seq, LANES), lambda b, p: (b, p)),
        out_shape=jax.ShapeDtypeStruct((t, DIL_HPG * DIL_HD), BF16),
        scratch_shapes=[pltpu.VMEM((DIL_GROUPS, seq, LANES), F32)] * 3,
        compiler_params=_cparams(("parallel", "parallel")),
        name="dil_attn",
    )(slopes, *[qkvs[g] for _ in range(3) for g in range(DIL_GROUPS)])


def _to_token_tiles(dst_ref, val):
    n = val.shape[0]
    for s in range(ROW_TILES):
        dst_ref[pl.ds(s, n, stride=ROW_TILES), :] = val[:, s * LANES:(s + 1) * LANES]


def _from_token_tiles(src_ref, n):
    return jnp.concatenate([src_ref[pl.ds(s, n, stride=ROW_TILES), :] for s in range(ROW_TILES)], axis=1)


def _top2(logits):
    lane = lax.broadcasted_iota(jnp.int32, logits.shape, 1)
    m1 = jnp.max(logits, axis=-1, keepdims=True)
    i1 = jnp.min(jnp.where(logits == m1, lane, LANES), axis=-1, keepdims=True)
    rest = jnp.where(lane == i1, NEG, logits)
    m2 = jnp.max(rest, axis=-1, keepdims=True)
    i2 = jnp.min(jnp.where(rest == m2, lane, LANES), axis=-1, keepdims=True)
    e = jnp.exp(m2 - m1)
    w1 = 1.0 / (1.0 + e)
    return (jnp.where(lane == 0, i1, jnp.where(lane == 1, i2, 0)),
            jnp.where(lane == 0, w1, jnp.where(lane == 1, e * w1, 0.0)))


def _merge_kernel(x_ref, h_ref, om_ref, od_ref, wg_ref, wbm_ref, wbd_ref, wo_ref, g_ref, *rest, route):
    gates = jnp.dot(h_ref[...], wg_ref[...], preferred_element_type=F32)
    bm = jnp.dot(om_ref[...], wbm_ref[...], preferred_element_type=F32)
    bd = jnp.dot(od_ref[...], wbd_ref[...], preferred_element_type=F32)
    merged = _sigmoid(gates[:, :D_MODEL]) * bm + _sigmoid(gates[:, D_MODEL:]) * bd
    x1 = x_ref[...] + jnp.dot(merged.astype(BF16), wo_ref[...], preferred_element_type=F32)
    hn = _rms(x1, g_ref[...])
    if route:
        wr_ref, br_ref, x_out, h_out, idx_out, wt_out = rest
        _to_token_tiles(h_out, hn)
        hi = hn.astype(BF16)
        lo = (hn - hi.astype(F32)).astype(BF16)
        w = wr_ref[...]
        w_hi = w.astype(BF16)
        w_lo = (w - w_hi.astype(F32)).astype(BF16)
        both = jnp.dot(hi, jnp.concatenate([w_hi, w_lo], axis=1), preferred_element_type=F32)
        logits = (both[:, :LANES] + both[:, LANES:]
                  + jnp.dot(lo, w_hi, preferred_element_type=F32) + br_ref[...])
        idx_out[...], wt_out[...] = _top2(logits)
    else:
        x_out, h_out = rest
        h_out[...] = hn.astype(h_out.dtype)
    x_out[...] = x1


def _merge(x, h, o_mla, o_dil, wg, wbm, wbd, wo, g_next, router=None, tm=512):
    t = x.shape[0]
    row = lambda w: pl.BlockSpec((tm, w), lambda i: (i, 0))
    full = lambda a: pl.BlockSpec(a.shape, lambda i: (0, 0))
    in_specs = [row(D_MODEL), row(D_MODEL), row(o_mla.shape[1]), row(o_dil.shape[1]),
                full(wg), full(wbm), full(wbd), full(wo), full(g_next)]
    args = [x, h, o_mla, o_dil, wg, wbm, wbd, wo, g_next]
    out_specs = [row(D_MODEL)]
    out_shape = [jax.ShapeDtypeStruct((t, D_MODEL), F32)]
    if router is None:
        out_specs += [row(D_MODEL)]
        out_shape += [jax.ShapeDtypeStruct((t, D_MODEL), BF16)]
    else:
        in_specs += [full(router[0]), full(router[1])]
        args += list(router)
        out_specs += [pl.BlockSpec((tm * ROW_TILES, LANES), lambda i: (i, 0)), row(LANES), row(LANES)]
        out_shape += [jax.ShapeDtypeStruct((t * ROW_TILES, LANES), F32),
                      jax.ShapeDtypeStruct((t, LANES), jnp.int32), jax.ShapeDtypeStruct((t, LANES), F32)]
    return pl.pallas_call(
        functools.partial(_merge_kernel, route=router is not None),
        grid=(t // tm,),
        in_specs=in_specs,
        out_specs=out_specs,
        out_shape=out_shape,
        compiler_params=_cparams(("parallel",)),
        name="merge",
    )(*args)


def _ffn_kernel(x_ref, h_ref, wg_ref, wu_ref, wd_ref, o_ref):
    @pl.when(pl.program_id(1) == 0)
    def _():
        o_ref[...] = x_ref[...]

    h = h_ref[...]
    a = jnp.dot(h, wg_ref[...], preferred_element_type=F32)
    u = jnp.dot(h, wu_ref[...], preferred_element_type=F32)
    o_ref[...] += jnp.dot((a * _sigmoid(a) * u).astype(BF16), wd_ref[...], preferred_element_type=F32)


def _ffn(x, h, wg, wu, wd, tm=512, tf=1792):
    t = x.shape[0]
    nf = wg.shape[1] // tf
    return pl.pallas_call(
        _ffn_kernel,
        grid=(t // tm, nf),
        in_specs=[pl.BlockSpec((tm, D_MODEL), lambda i, f: (i, 0)),
                  pl.BlockSpec((tm, D_MODEL), lambda i, f: (i, 0)),
                  pl.BlockSpec((D_MODEL, tf), lambda i, f: (0, f)),
                  pl.BlockSpec((D_MODEL, tf), lambda i, f: (0, f)),
                  pl.BlockSpec((tf, D_MODEL), lambda i, f: (f, 0))],
        out_specs=pl.BlockSpec((tm, D_MODEL), lambda i, f: (i, 0)),
        out_shape=jax.ShapeDtypeStruct((t, D_MODEL), F32),
        compiler_params=_cparams(("parallel", "arbitrary")),
        name="dense_ffn",
    )(x, h, wg, wu, wd)


def _tile_at(ref, row8):
    return ref.at[pl.ds(pl.multiple_of(row8, ROW_TILES), ROW_TILES), :]


def _dispatch_kernel(pos_ref, fill_ref, h_ref, xs_hbm, zero_ref, sem, sem_fill, *, t, tmd, tm):
    i = pl.program_id(0)

    @pl.when(i == 0)
    def _():
        zero_ref[...] = jnp.zeros_like(zero_ref)
        for e in range(fill_ref.shape[0]):
            @pl.when(fill_ref[e] >= 0)
            def _():
                cp = pltpu.make_async_copy(
                    zero_ref, xs_hbm.at[pl.ds(pl.multiple_of(fill_ref[e], ROW_TILES), tm * ROW_TILES), :], sem_fill)
                cp.start()
                cp.wait()

    for k in range(TOP_K):
        for r in range(tmd):
            pltpu.make_async_copy(_tile_at(h_ref, r * ROW_TILES),
                                  _tile_at(xs_hbm, pos_ref[k * t + i * tmd + r]), sem.at[k]).start(priority=r % 2)
    for k in range(TOP_K):
        pltpu.make_async_copy(h_ref, h_ref, sem.at[k]).wait()


def _dispatch(h, pos8, fill8, n_rows, tm, tmd=512):
    t = h.shape[0] // ROW_TILES
    return pl.pallas_call(
        functools.partial(_dispatch_kernel, t=t, tmd=tmd, tm=tm),
        grid_spec=pltpu.PrefetchScalarGridSpec(
            num_scalar_prefetch=2,
            grid=(t // tmd,),
            in_specs=[pl.BlockSpec((tmd * ROW_TILES, LANES), lambda i, pos, fill: (i, 0))],
            out_specs=pl.BlockSpec(memory_space=pl.ANY),
            scratch_shapes=[pltpu.VMEM((tm * ROW_TILES, LANES), F32),
                            pltpu.SemaphoreType.DMA((TOP_K,)), pltpu.SemaphoreType.DMA(())]),
        out_shape=jax.ShapeDtypeStruct((n_rows * ROW_TILES, LANES), F32),
        compiler_params=_cparams(("arbitrary",)),
        name="moe_dispatch",
    )(pos8, fill8, h)


def _moe_ffn_kernel(te_ref, nv_ref, x_ref, wg_ref, wu_ref, wd_ref, o_ref, acc_ref, xb_ref, *, tm):
    i = pl.program_id(0)
    f = pl.program_id(1)
    valid = i < nv_ref[0]
    last = f == pl.num_programs(1) - 1

    @pl.when(valid & (f == 0))
    def _():
        xb_ref[...] = _from_token_tiles(x_ref, tm).astype(BF16)
        acc_ref[...] = jnp.zeros_like(acc_ref)

    @pl.when(valid)
    def _():
        h = xb_ref[...]
        a = jnp.dot(h, wg_ref[0], preferred_element_type=F32)
        u = jnp.dot(h, wu_ref[0], preferred_element_type=F32)
        acc_ref[...] += jnp.dot((a * _sigmoid(a) * u).astype(BF16), wd_ref[0], preferred_element_type=F32)

    @pl.when(valid & last)
    def _():
        _to_token_tiles(o_ref, acc_ref[...])

    @pl.when(jnp.logical_not(valid) & last)
    def _():
        o_ref[...] = jnp.zeros_like(o_ref)


def _moe_ffn(xs, tile_expert, n_valid, wg, wu, wd, tm, tf=1792):
    n_tiles = xs.shape[0] // (tm * ROW_TILES)
    nf = wg.shape[2] // tf

    def live(i, f, te, nv):
        ok = i < nv[0]
        return jnp.where(ok, i, nv[0] - 1), jnp.where(ok, f, nf - 1)

    return pl.pallas_call(
        functools.partial(_moe_ffn_kernel, tm=tm),
        grid_spec=pltpu.PrefetchScalarGridSpec(
            num_scalar_prefetch=2,
            grid=(n_tiles, nf),
            in_specs=[pl.BlockSpec((tm * ROW_TILES, LANES), lambda *a: (live(*a)[0], 0)),
                      pl.BlockSpec((1, D_MODEL, tf), lambda *a: (a[2][live(*a)[0]], 0, live(*a)[1])),
                      pl.BlockSpec((1, D_MODEL, tf), lambda *a: (a[2][live(*a)[0]], 0, live(*a)[1])),
                      pl.BlockSpec((1, tf, D_MODEL), lambda *a: (a[2][live(*a)[0]], live(*a)[1], 0))],
            out_specs=pl.BlockSpec((tm * ROW_TILES, LANES), lambda i, f, te, nv: (i, 0)),
            scratch_shapes=[pltpu.VMEM((tm, D_MODEL), F32), pltpu.VMEM((tm, D_MODEL), BF16)]),
        out_shape=jax.ShapeDtypeStruct(xs.shape, F32),
        compiler_params=_cparams(("arbitrary", "arbitrary")),
        name="moe_ffn",
    )(tile_expert, n_valid, xs, wg, wu, wd)


def _router_params(w_router, b_router):
    w_pad = jnp.zeros((D_MODEL, LANES), F32).at[:, :N_EXPERTS].set(w_router)
    b_pad = jnp.full((1, LANES), NEG, F32).at[0, :N_EXPERTS].set(b_router)
    return w_pad, b_pad


def _moe(h, idx_l, wg, wu, wd, tm=448):
    t = h.shape[0] // ROW_TILES
    expert = idx_l[:, :TOP_K].T.reshape(-1)
    onehot = (expert[:, None] == jnp.arange(N_EXPERTS)[None, :]).astype(jnp.int32)
    csum = jnp.cumsum(onehot, axis=0)
    counts = csum[-1]
    rank = jnp.sum((csum - onehot) * onehot, axis=1)
    tiles = (counts + tm - 1) // tm
    tile_end = jnp.cumsum(tiles)
    pad_off = (tile_end - tiles) * tm
    pos8 = (pad_off[expert] + rank) * ROW_TILES
    n_tiles = (TOP_K * t) // tm + N_EXPERTS
    n_valid = tile_end[-1:].astype(jnp.int32)
    tile_ids = jnp.minimum(jnp.arange(n_tiles), n_valid[0] - 1)
    tile_expert = jnp.sum(tile_end[None, :] <= tile_ids[:, None], axis=1).astype(jnp.int32)
    spare = n_valid[0] + jnp.arange(N_EXPERTS)
    fill8 = jnp.concatenate([jnp.where(tiles > 0, tile_end - 1, -1), jnp.where(spare < n_tiles, spare, -1)])
    fill8 = jnp.where(fill8 >= 0, fill8 * (tm * ROW_TILES), -1).astype(jnp.int32)
    xs = _dispatch(h, pos8, fill8, n_tiles * tm, tm)
    ys = _moe_ffn(xs, tile_expert, n_valid, wg, wu, wd, tm)
    return ys, pos8


def _ple_body(x, p_ref, wg_ref, wp_ref, g_ref, gn_ref, outs):
    gate = _sigmoid(jnp.dot(_rms(x, g_ref[...]).astype(BF16), wg_ref[...], preferred_element_type=F32))
    x2 = x + gate * jnp.dot(p_ref[...].astype(BF16), wp_ref[...], preferred_element_type=F32)
    if len(outs) == 2:
        outs[0][...] = x2
    outs[-1][...] = _rms(x2, gn_ref[...]).astype(outs[-1].dtype)


def _ple_kernel(x_ref, p_ref, wg_ref, wp_ref, g_ref, gn_ref, *outs):
    _ple_body(x_ref[...], p_ref, wg_ref, wp_ref, g_ref, gn_ref, outs)


def _ple_moe_kernel(pos_ref, x_ref, p_ref, wt_ref, wg_ref, wp_ref, g_ref, gn_ref, y_hbm, *rest, t, tm):
    *outs, ybuf, sem = rest
    i = pl.program_id(0)
    slot = i % 2

    def gather(block, s):
        for k in range(TOP_K):
            for r in range(tm):
                pltpu.make_async_copy(_tile_at(y_hbm, pos_ref[k * t + block * tm + r]),
                                      _tile_at(ybuf.at[s, k], r * ROW_TILES), sem.at[s]).start(priority=r % 2)

    def wait(s):
        pltpu.make_async_copy(ybuf.at[s], ybuf.at[s], sem.at[s]).wait()

    last = pl.num_programs(0) - 1

    @pl.when(i == 0)
    def _():
        gather(0, 0)

    wait(slot)
    gather(jnp.minimum(i + 1, last), 1 - slot)
    wt = wt_ref[...]
    x = (x_ref[...] + wt[:, 0:1] * _from_token_tiles(ybuf.at[slot, 0], tm)
         + wt[:, 1:2] * _from_token_tiles(ybuf.at[slot, 1], tm))
    _ple_body(x, p_ref, wg_ref, wp_ref, g_ref, gn_ref, outs)

    @pl.when(i == last)
    def _():
        wait(1 - slot)


def _ple(x, p, wg, wp, g, g_next, h_dtype, want_x, moe=None, tm=512):
    t = x.shape[0]
    n_out = 2 if want_x else 1
    out_shape = ([jax.ShapeDtypeStruct((t, D_MODEL), F32)] * (n_out - 1)
                 + [jax.ShapeDtypeStruct((t, D_MODEL), h_dtype)])
    if moe is None:
        row = lambda w: pl.BlockSpec((tm, w), lambda i: (i, 0))
        full = lambda a: pl.BlockSpec(a.shape, lambda i: (0, 0))
        return pl.pallas_call(
            _ple_kernel,
            grid=(t // tm,),
            in_specs=[row(D_MODEL), row(P_DIM), full(wg), full(wp), full(g), full(g_next)],
            out_specs=[row(D_MODEL)] * n_out,
            out_shape=out_shape,
            compiler_params=_cparams(("parallel",)),
            name="ple",
        )(x, p, wg, wp, g, g_next)
    ys, pos8, wt = moe
    tm = tm // 2
    row = lambda w: pl.BlockSpec((tm, w), lambda i, pos: (i, 0))
    full = lambda a: pl.BlockSpec(a.shape, lambda i, pos: (0, 0))
    return pl.pallas_call(
        functools.partial(_ple_moe_kernel, t=t, tm=tm),
        grid_spec=pltpu.PrefetchScalarGridSpec(
            num_scalar_prefetch=1,
            grid=(t // tm,),
            in_specs=[row(D_MODEL), row(P_DIM), row(LANES), full(wg), full(wp), full(g), full(g_next),
                      pl.BlockSpec(memory_space=pl.ANY)],
            out_specs=[row(D_MODEL)] * n_out,
            scratch_shapes=[pltpu.VMEM((2, TOP_K, tm * ROW_TILES, LANES), F32), pltpu.SemaphoreType.DMA((2,))]),
        out_shape=out_shape,
        compiler_params=_cparams(("arbitrary",)),
        name="ple_moe",
    )(pos8, x, p, wt, wg, wp, g, g_next, ys)


def _rot_cols(w):
    half = w.shape[-1] // 2
    return jnp.concatenate([-w[:, half:], w[:, :half]], axis=-1)


def _layer_weights(i, w_in, w_uq, w_ukv):
    wi = w_in[i]
    z = lambda n: jnp.zeros((wi.shape[0], n), F32)
    w_kr = wi[:, IN_OFF[2]:IN_OFF[3]]
    pad = MLA_SLAB - MLA_NOPE - MLA_ROPE
    wlat = jnp.concatenate([wi[:, :IN_OFF[2]], z(MLA_NOPE), w_kr, z(pad), z(MLA_NOPE), _rot_cols(w_kr), z(pad)],
                           axis=1).astype(BF16)
    qscale = jnp.concatenate([jnp.full((DIL_WIDTH,), DIL_HD ** -0.5, F32), jnp.ones((2 * DIL_WIDTH,), F32)])
    wdil = (wi[:, IN_OFF[3]:IN_OFF[6]] * qscale).astype(BF16)
    wgate = wi[:, IN_OFF[6]:].astype(BF16)
    uq = w_uq[i].reshape(MLA_Q_RANK, MLA_HEADS, MLA_NOPE + MLA_ROPE)
    zq = lambda n: jnp.zeros((MLA_Q_RANK, MLA_HEADS, n), F32)
    rope_rot = jnp.concatenate([-uq[..., MLA_NOPE + MLA_ROPE // 2:], uq[..., MLA_NOPE:MLA_NOPE + MLA_ROPE // 2]], -1)
    wq = jnp.concatenate([uq, zq(pad)], axis=-1).reshape(MLA_Q_RANK, -1).astype(BF16)
    wqr = jnp.concatenate([zq(MLA_NOPE), rope_rot, zq(pad)], axis=-1).reshape(MLA_Q_RANK, -1).astype(BF16)
    ukv = w_ukv[i].reshape(MLA_KV_RANK, MLA_HEADS, MLA_NOPE + MLA_V)
    zkv = jnp.zeros((MLA_KV_RANK, MLA_HEADS, MLA_SLAB - MLA_NOPE), F32)
    wk = jnp.concatenate([ukv[..., :MLA_NOPE], zkv], axis=-1).reshape(MLA_KV_RANK, -1).astype(BF16)
    wv = jnp.concatenate([ukv[..., MLA_NOPE:], zkv], axis=-1).reshape(MLA_KV_RANK, -1).astype(BF16)
    return wlat, wdil, wgate, wq, wqr, wk, wv


def kernel(x, p, positions, attn_norm, w_in, q_norm, w_uq, kv_norm, w_ukv, w_br_mla, w_br_dil, w_out, ffn_norm, dense_w_gate, dense_w_up, dense_w_down, router_w, router_b, moe_w_gate, moe_w_up, moe_w_down, ple_norm, ple_w_gate, ple_w_proj, final_norm):
    batch, seq, d = x.shape
    t = batch * seq
    depth = w_in.shape[0]
    xf = x.reshape(t, d)
    cos_t, sin_t = _rope_tables(positions)
    slopes = _alibi_slopes(DIL_HEADS)
    row = lambda v: v.reshape(1, -1)
    h = _norm(xf, attn_norm[0])
    for i in range(depth):
        wlat, wdil, wgate, wq, wqr, wk, wv = _layer_weights(i, w_in, w_uq, w_ukv)
        q, k, v = _mla_proj(h, wlat, wq, wqr, wk, wv, row(q_norm[i]), row(kv_norm[i]), cos_t, sin_t)
        o_mla = _mla_attn(q, k, v, batch, seq)
        qkv_d = [_dil_proj(h, wdil, seq, g) for g in range(DIL_GROUPS)]
        o_dil = _dil_attn(qkv_d, slopes, batch, seq)
        moe_layer = i % 2 == 1
        j = i // 2
        xf, h2, *routing = _merge(xf, h, o_mla, o_dil, wgate, w_br_mla[i].astype(BF16), w_br_dil[i].astype(BF16),
                                  w_out[i].astype(BF16), row(ffn_norm[i]),
                                  _router_params(router_w[j], router_b[j]) if moe_layer else None)
        moe = None
        if moe_layer:
            idx_l, wt_l = routing
            moe = _moe(h2, idx_l, moe_w_gate[j].astype(BF16), moe_w_up[j].astype(BF16),
                       moe_w_down[j].astype(BF16)) + (wt_l,)
        else:
            xf = _ffn(xf, h2, dense_w_gate[j].astype(BF16), dense_w_up[j].astype(BF16),
                      dense_w_down[j].astype(BF16))
        last = i == depth - 1
        g_next = final_norm if last else attn_norm[i + 1]
        *xnext, h = _ple(xf, p[i].reshape(t, -1), ple_w_gate[i].astype(BF16), ple_w_proj[i].astype(BF16),
                         row(ple_norm[i]), row(g_next), F32 if last else BF16, not last, moe)
        xf = xnext[0] if xnext else None
    return h.reshape(batch, seq, d)
```

```python
import functools

import numpy as np
import jax
import jax.numpy as jnp
from jax import lax
from jax.experimental import pallas as pl
from jax.experimental.pallas import tpu as pltpu

F32 = jnp.float32
BF16 = jnp.bfloat16

D_MODEL = 1024
P_DIM = 256
NORM_EPS = 1e-6
DEPTH = 2

MLA_HEADS = 8
MLA_Q_RANK = 384
MLA_KV_RANK = 256
MLA_NOPE = 64
MLA_ROPE = 32
MLA_V = 64
ROPE_THETA = 10000.0
MLA_SLAB = 128

DIL_CONFIGS = ((128, 1), (512, 4), (2048, 16))
DIL_GROUPS = 3
DIL_HPG = 8
DIL_HEADS = 24
DIL_HD = 64
DIL_WIDTH = DIL_HEADS * DIL_HD
DIL_SPAN = 128

D_FF = 3584
N_EXPERTS = 8
TOP_K = 2

LANES = 128
ROW_TILES = D_MODEL // LANES
NEG = -1e30
LOG2E = 1.4426950408889634
VMEM_LIMIT = 56 * 1024 * 1024

IN_OFF = tuple(int(o) for o in np.cumsum((0, MLA_Q_RANK, MLA_KV_RANK, MLA_ROPE, DIL_WIDTH, DIL_WIDTH,
                                          DIL_WIDTH, D_MODEL, D_MODEL)))


def _cparams(sem):
    return pltpu.CompilerParams(dimension_semantics=sem, vmem_limit_bytes=VMEM_LIMIT)


def _rms(x, g):
    return x * lax.rsqrt(jnp.mean(x * x, axis=-1, keepdims=True) + NORM_EPS) * g


def _sigmoid(x):
    return 1.0 / (1.0 + jnp.exp(-x))


def _alibi_slopes(n):
    def pow2(m):
        start = 2.0 ** (-8.0 / m)
        return [start ** (i + 1) for i in range(m)]
    if float(np.log2(n)).is_integer():
        s = pow2(n)
    else:
        c = 2 ** int(np.floor(np.log2(n)))
        s = pow2(c) + pow2(2 * c)[0::2][: n - c]
    return jnp.asarray(sorted(s, reverse=True), dtype=F32)


def _norm_kernel(x_ref, g_ref, o_ref):
    o_ref[...] = _rms(x_ref[...], g_ref[...]).astype(o_ref.dtype)


def _norm(x, g, tm=1024):
    t, d = x.shape
    return pl.pallas_call(
        _norm_kernel,
        grid=(t // tm,),
        in_specs=[pl.BlockSpec((tm, d), lambda i: (i, 0)), pl.BlockSpec((1, d), lambda i: (0, 0))],
        out_specs=pl.BlockSpec((tm, d), lambda i: (i, 0)),
        out_shape=jax.ShapeDtypeStruct((t, d), BF16),
        compiler_params=_cparams(("parallel",)),
        name="rms_norm",
    )(x, g.reshape(1, d))


def _rope_table_kernel(pos_ref, invf_ref, cos_ref, sin_ref):
    ang = pos_ref[...].astype(F32) * invf_ref[...]
    lane = lax.broadcasted_iota(jnp.int32, ang.shape, 1)
    rope_lane = (lane >= MLA_NOPE) & (lane < MLA_NOPE + MLA_ROPE)
    cos_ref[...] = jnp.where(lane < MLA_NOPE, 1.0, jnp.where(rope_lane, jnp.cos(ang), 0.0))
    sin_ref[...] = jnp.where(rope_lane, jnp.sin(ang), 0.0)


def _rope_tables(positions, tm=2048):
    t = positions.size
    half = MLA_ROPE // 2
    inv_freq = ROPE_THETA ** (-jnp.arange(half, dtype=F32) / half)
    invf = jnp.zeros((1, LANES), F32).at[0, MLA_NOPE:MLA_NOPE + MLA_ROPE].set(jnp.concatenate([inv_freq, inv_freq]))
    pos_b = jnp.broadcast_to(positions.reshape(t, 1), (t, LANES))
    spec = pl.BlockSpec((tm, LANES), lambda i: (i, 0))
    return pl.pallas_call(
        _rope_table_kernel,
        grid=(t // tm,),
        in_specs=[spec, pl.BlockSpec((1, LANES), lambda i: (0, 0))],
        out_specs=[spec, spec],
        out_shape=[jax.ShapeDtypeStruct((t, LANES), F32)] * 2,
        compiler_params=_cparams(("parallel",)),
        name="rope_tables",
    )(pos_b, invf)


def _mla_proj_kernel(h_ref, wlat_ref, wq_ref, wqr_ref, wk_ref, wv_ref, qn_ref, kvn_ref, cos_ref, sin_ref,
                     q_out, k_out, v_out):
    lat = jnp.dot(h_ref[...], wlat_ref[...], preferred_element_type=F32)
    cqn = _rms(lat[:, :MLA_Q_RANK], qn_ref[...]).astype(BF16)
    ckvn = _rms(lat[:, MLA_Q_RANK:MLA_Q_RANK + MLA_KV_RANK], kvn_ref[...]).astype(BF16)
    cos = cos_ref[...]
    sin = sin_ref[...]
    o = MLA_Q_RANK + MLA_KV_RANK
    k_rope = lat[:, o:o + LANES] * cos + lat[:, o + LANES:o + 2 * LANES] * sin
    qa = jnp.dot(cqn, wq_ref[...], preferred_element_type=F32)
    qb = jnp.dot(cqn, wqr_ref[...], preferred_element_type=F32)
    kk = jnp.dot(ckvn, wk_ref[...], preferred_element_type=F32)
    scale = (MLA_NOPE + MLA_ROPE) ** -0.5 * LOG2E
    vv = jnp.dot(ckvn, wv_ref[...], preferred_element_type=F32)
    ones_lane = lax.broadcasted_iota(jnp.int32, cos.shape, 1) >= MLA_V
    for hd in range(MLA_HEADS):
        sl = slice(hd * MLA_SLAB, (hd + 1) * MLA_SLAB)
        q_out[:, sl] = ((qa[:, sl] * cos + qb[:, sl] * sin) * scale).astype(BF16)
        k_out[:, sl] = (kk[:, sl] + k_rope).astype(BF16)
        v_out[:, sl] = jnp.where(ones_lane, 1.0, vv[:, sl]).astype(BF16)


def _mla_proj(h, wlat, wq, wqr, wk, wv, qn, kvn, cos_t, sin_t, tm=512):
    t = h.shape[0]
    row = lambda w: pl.BlockSpec((tm, w), lambda i: (i, 0))
    full = lambda a: pl.BlockSpec(a.shape, lambda i: (0, 0))
    hs = MLA_HEADS * MLA_SLAB
    return pl.pallas_call(
        _mla_proj_kernel,
        grid=(t // tm,),
        in_specs=[row(D_MODEL), full(wlat), full(wq), full(wqr), full(wk), full(wv), full(qn), full(kvn),
                  row(LANES), row(LANES)],
        out_specs=[row(hs), row(hs), row(hs)],
        out_shape=[jax.ShapeDtypeStruct((t, hs), BF16)] * 3,
        compiler_params=_cparams(("parallel",)),
        name="mla_proj",
    )(h, wlat, wq, wqr, wk, wv, qn, kvn, cos_t, sin_t)


MLA_ROW_SPLIT = 2


def _mla_attn_kernel(q_ref, k_ref, v_ref, o_ref, s_scr, *, tq):
    qi = pl.program_id(2)
    nh = q_ref.shape[1] // MLA_SLAB
    tr = tq // MLA_ROW_SPLIT

    def tile(n):
        def keys(c, rg):
            return (rg + 1) * tr if c == n else tq

        def pass1(hh, rg):
            rows = slice(rg * tr, (rg + 1) * tr)
            q = q_ref[rows, hh * MLA_SLAB:(hh + 1) * MLA_SLAB]
            mx = None
            for c in range(n + 1):
                kw = keys(c, rg)
                ks = k_ref[c * tq:c * tq + kw, hh * MLA_SLAB:(hh + 1) * MLA_SLAB]
                s = lax.dot_general(q, ks, (((1,), (1,)), ((), ())), preferred_element_type=F32)
                if c == n:
                    rq = lax.broadcasted_iota(jnp.int32, (tr, kw), 0)
                    ck = lax.broadcasted_iota(jnp.int32, (tr, kw), 1)
                    s = jnp.where(ck <= rq + rg * tr, s, NEG)
                s_scr[hh, c, rows, :kw] = s
                parts = [s[:, i * LANES:(i + 1) * LANES] for i in range(kw // LANES)]
                mx = functools.reduce(jnp.maximum, parts if mx is None else [mx] + parts)
            return jnp.max(mx, axis=-1, keepdims=True)

        def pass2(hh, rg, m):
            rows = slice(rg * tr, (rg + 1) * tr)
            acc = None
            for c in range(n + 1):
                kw = keys(c, rg)
                vs = v_ref[c * tq:c * tq + kw, hh * MLA_SLAB:(hh + 1) * MLA_SLAB]
                p = jnp.exp2((s_scr[hh, c, rows, :kw] - m).astype(BF16))
                pv = jnp.dot(p, vs, preferred_element_type=F32)
                acc = pv if acc is None else acc + pv
            o = acc / acc[:, MLA_V:MLA_V + 1]
            o_ref[rows, hh * MLA_V:(hh + 1) * MLA_V] = o[:, :MLA_V].astype(o_ref.dtype)

        chains = [(hh, rg) for hh in range(nh) for rg in range(MLA_ROW_SPLIT)]
        ms = [pass1(*chains[0])]
        for i, chain in enumerate(chains):
            if i + 1 < len(chains):
                ms.append(pass1(*chains[i + 1]))
            pass2(*chain, ms[i])

    for n in range(s_scr.shape[1]):
        pl.when(qi == n)(functools.partial(tile, n))


def _mla_attn(q, k, v, batch, seq, tq=512, nh=4):
    t = q.shape[0]
    nq = seq // tq
    return pl.pallas_call(
        functools.partial(_mla_attn_kernel, tq=tq),
        grid=(batch, MLA_HEADS // nh, nq),
        in_specs=[pl.BlockSpec((tq, nh * MLA_SLAB), lambda b, p, i: (b * nq + i, p)),
                  pl.BlockSpec((seq, nh * MLA_SLAB), lambda b, p, i: (b, p)),
                  pl.BlockSpec((seq, nh * MLA_SLAB), lambda b, p, i: (b, p))],
        out_specs=pl.BlockSpec((tq, nh * MLA_V), lambda b, p, i: (b * nq + i, p)),
        out_shape=jax.ShapeDtypeStruct((t, MLA_HEADS * MLA_V), BF16),
        scratch_shapes=[pltpu.VMEM((nh, nq, tq, tq), F32)],
        compiler_params=_cparams(("parallel", "parallel", "arbitrary")),
        name="mla_attn",
    )(q, k, v)


PHASE_STRIDE = 4


def _phase_of_slot(r):
    if r <= PHASE_STRIDE:
        return list(range(r))
    f2 = r // PHASE_STRIDE
    return [c1 + PHASE_STRIDE * c2 for c1 in range(PHASE_STRIDE) for c2 in range(f2)]


def _dil_proj_kernel(h_ref, w_ref, o_ref, acc_ref, tmp_ref, *, seq, r):
    nl = o_ref.shape[0]

    @pl.when(pl.program_id(0) == 0)
    def _():
        acc_ref[...] = jnp.zeros_like(acc_ref)

    res = jnp.dot(h_ref[...], w_ref[...], preferred_element_type=F32)
    f1 = min(r, PHASE_STRIDE)
    l1 = seq // f1
    dst = o_ref if r == f1 else tmp_ref
    for c1 in range(f1):
        for j in range(nl):
            src = acc_ref[j, pl.ds(c1, l1, stride=f1), :] if f1 > 1 else acc_ref[j]
            dst[j, c1 * l1:(c1 + 1) * l1, :] = src.astype(dst.dtype)
    if r > f1:
        f2 = r // f1
        l2 = l1 // f2
        for s in range(r):
            c1, c2 = divmod(s, f2)
            for j in range(nl):
                o_ref[j, s * l2:(s + 1) * l2, :] = (
                    tmp_ref[j, pl.ds(c1 * l1 + c2, l2, stride=f2), :].astype(o_ref.dtype))
    for j in range(nl):
        acc_ref[j] = res[:, j * LANES:(j + 1) * LANES]


def _dil_proj(h, w, seq, g, tn=512):
    t = h.shape[0]
    nb = t // seq
    nl = tn // LANES
    n_blocks = 3 * nb
    r = DIL_CONFIGS[g][1]

    def mm(s):
        b, tt = divmod(jnp.minimum(s, n_blocks - 1), 3)
        return tt, b

    def wr(s):
        b, tt = divmod(jnp.maximum(s - 1, 0), 3)
        return tt, b

    return pl.pallas_call(
        functools.partial(_dil_proj_kernel, seq=seq, r=r),
        grid=(n_blocks + 1,),
        in_specs=[pl.BlockSpec((seq, D_MODEL), lambda s: (mm(s)[1], 0)),
                  pl.BlockSpec((D_MODEL, tn), lambda s: (0, mm(s)[0] * DIL_GROUPS + g))],
        out_specs=pl.BlockSpec((nl, seq, LANES), lambda s: (wr(s)[0], wr(s)[1], 0)),
        out_shape=jax.ShapeDtypeStruct((3 * nl, t, LANES), BF16),
        scratch_shapes=[pltpu.VMEM((nl, seq, LANES), F32)] * 2,
        compiler_params=_cparams(("arbitrary",)),
        name="dil_proj",
    )(h, w)


def _dil_attn_kernel(slopes_ref, q0, q1, q2, k0, k1, k2, v0, v1, v2, o_ref, acc_s, m_s, l_s, *, seq):
    pair = pl.program_id(1)
    sp = DIL_SPAN
    head0 = lax.broadcasted_iota(jnp.int32, (sp, LANES), 1) < DIL_HD
    qi = lax.broadcasted_iota(jnp.int32, (sp, 2 * sp), 0)
    kj = lax.broadcasted_iota(jnp.int32, (sp, 2 * sp), 1)
    dist_w = qi + sp - kj
    valid_w = (dist_w >= 0) & (dist_w <= sp)
    dist_1 = dist_w[:, sp:]
    valid_1 = dist_1 >= 0
    qs, ks, vs = (q0, q1, q2), (k0, k1, k2), (v0, v1, v2)
    for g, (_, r) in enumerate(DIL_CONFIGS):
        ln = seq // r
        nb = ln // sp
        sl = [slopes_ref[g * DIL_HPG + 2 * pair + hh] * float(r) for hh in range(2)]
        bias_w = jnp.concatenate([jnp.where(valid_w, -s * dist_w.astype(F32), NEG) for s in sl], axis=0)
        bias_1 = jnp.concatenate([jnp.where(valid_1, -s * dist_1.astype(F32), NEG) for s in sl], axis=0)
        for slot_i, c in enumerate(_phase_of_slot(r)):
            for i in range(nb):
                row0 = slot_i * ln + i * sp
                qb = qs[g][0, row0:row0 + sp, :]
                zero = jnp.zeros_like(qb)
                q2h = jnp.concatenate([jnp.where(head0, qb, zero), jnp.where(head0, zero, qb)], axis=0)
                lo = row0 if i == 0 else row0 - sp
                kw = ks[g][0, lo:row0 + sp, :]
                vw = vs[g][0, lo:row0 + sp, :]
                s = lax.dot_general(q2h, kw, (((1,), (1,)), ((), ())), preferred_element_type=F32)
                s = s + (bias_1 if i == 0 else bias_w)
                m = jnp.max(s, axis=-1, keepdims=True)
                p = jnp.exp((s - m).astype(BF16))
                l = jnp.sum(p.astype(F32), axis=-1, keepdims=True)
                o2 = jnp.dot(p, vw, preferred_element_type=F32)
                dst = pl.ds(i * sp * r + c, sp, stride=r) if r > 1 else pl.ds(i * sp, sp)
                acc_s[g, dst, :] = jnp.where(head0, o2[:sp], o2[sp:])
                m_s[g, dst, :] = jnp.where(head0, m[:sp], m[sp:])
                l_s[g, dst, :] = jnp.where(head0, l[:sp], l[sp:])

    ch = 256

    def combine(i, _):
        rows = pl.ds(pl.multiple_of(i * ch, ch), ch)
        ms = [m_s[g, rows, :] for g in range(DIL_GROUPS)]
        mx = jnp.maximum(jnp.maximum(ms[0], ms[1]), ms[2])
        ws = [jnp.exp(m - mx) for m in ms]
        num = sum(ws[g] * acc_s[g, rows, :] for g in range(DIL_GROUPS))
        den = sum(ws[g] * l_s[g, rows, :] for g in range(DIL_GROUPS))
        o_ref[rows, :] = (num / den).astype(o_ref.dtype)
        return 0

    lax.fori_loop(0, seq // ch, combine, 0)


def _dil_attn(qkvs, slopes, batch, seq):
    t = qkvs[0].shape[1]
    pairs = DIL_HPG // 2

    def spec(tt):
        return pl.BlockSpec((1, seq, LANES), lambda b, p: (tt * pairs + p, b, 0))

    in_specs = [pl.BlockSpec(memory_space=pltpu.SMEM)] + [spec(tt) for tt in range(3) for _ in range(DIL_GROUPS)]
    return pl.pallas_call(
        functools.partial(_dil_attn_kernel, seq=seq),
        grid=(batch, pairs),
        in_specs=in_specs,
        out_specs=pl.BlockSpec((seq, LANES), lambda b, p: (b, p)),
        out_shape=jax.ShapeDtypeStruct((t, DIL_HPG * DIL_HD), BF16),
        scratch_shapes=[pltpu.VMEM((DIL_GROUPS, seq, LANES), F32)] * 3,
        compiler_params=_cparams(("parallel", "parallel")),
        name="dil_attn",
    )(slopes, *[qkvs[g] for _ in range(3) for g in range(DIL_GROUPS)])


def _to_token_tiles(dst_ref, val):
    n = val.shape[0]
    for s in range(ROW_TILES):
        dst_ref[pl.ds(s, n, stride=ROW_TILES), :] = val[:, s * LANES:(s + 1) * LANES]


def _from_token_tiles(src_ref, n):
    return jnp.concatenate([src_ref[pl.ds(s, n, stride=ROW_TILES), :] for s in range(ROW_TILES)], axis=1)


def _top2(logits):
    lane = lax.broadcasted_iota(jnp.int32, logits.shape, 1)
    m1 = jnp.max(logits, axis=-1, keepdims=True)
    i1 = jnp.min(jnp.where(logits == m1, lane, LANES), axis=-1, keepdims=True)
    rest = jnp.where(lane == i1, NEG, logits)
    m2 = jnp.max(rest, axis=-1, keepdims=True)
    i2 = jnp.min(jnp.where(rest == m2, lane, LANES), axis=-1, keepdims=True)
    e = jnp.exp(m2 - m1)
    w1 = 1.0 / (1.0 + e)
    return (jnp.where(lane == 0, i1, jnp.where(lane == 1, i2, 0)),
            jnp.where(lane == 0, w1, jnp.where(lane == 1, e * w1, 0.0)))


def _merge_kernel(x_ref, h_ref, om_ref, od_ref, wg_ref, wbm_ref, wbd_ref, wo_ref, g_ref, *rest, route):
    gates = jnp.dot(h_ref[...], wg_ref[...], preferred_element_type=F32)
    bm = jnp.dot(om_ref[...], wbm_ref[...], preferred_element_type=F32)
    bd = jnp.dot(od_ref[...], wbd_ref[...], preferred_element_type=F32)
    merged = _sigmoid(gates[:, :D_MODEL]) * bm + _sigmoid(gates[:, D_MODEL:]) * bd
    x1 = x_ref[...] + jnp.dot(merged.astype(BF16), wo_ref[...], preferred_element_type=F32)
    hn = _rms(x1, g_ref[...])
    if route:
        wr_ref, br_ref, x_out, h_out, idx_out, wt_out = rest
        _to_token_tiles(h_out, hn)
        hi = hn.astype(BF16)
        lo = (hn - hi.astype(F32)).astype(BF16)
        w = wr_ref[...]
        w_hi = w.astype(BF16)
        w_lo = (w - w_hi.astype(F32)).astype(BF16)
        both = jnp.dot(hi, jnp.concatenate([w_hi, w_lo], axis=1), preferred_element_type=F32)
        logits = (both[:, :LANES] + both[:, LANES:]
                  + jnp.dot(lo, w_hi, preferred_element_type=F32) + br_ref[...])
        idx_out[...], wt_out[...] = _top2(logits)
    else:
        x_out, h_out = rest
        h_out[...] = hn.astype(h_out.dtype)
    x_out[...] = x1


def _merge(x, h, o_mla, o_dil, wg, wbm, wbd, wo, g_next, router=None, tm=512):
    t = x.shape[0]
    row = lambda w: pl.BlockSpec((tm, w), lambda i: (i, 0))
    full = lambda a: pl.BlockSpec(a.shape, lambda i: (0, 0))
    in_specs = [row(D_MODEL), row(D_MODEL), row(o_mla.shape[1]), row(o_dil.shape[1]),
                full(wg), full(wbm), full(wbd), full(wo), full(g_next)]
    args = [x, h, o_mla, o_dil, wg, wbm, wbd, wo, g_next]
    out_specs = [row(D_MODEL)]
    out_shape = [jax.ShapeDtypeStruct((t, D_MODEL), F32)]
    if router is None:
        out_specs += [row(D_MODEL)]
        out_shape += [jax.ShapeDtypeStruct((t, D_MODEL), BF16)]
    else:
        in_specs += [full(router[0]), full(router[1])]
        args += list(router)
        out_specs += [pl.BlockSpec((tm * ROW_TILES, LANES), lambda i: (i, 0)), row(LANES), row(LANES)]
        out_shape += [jax.ShapeDtypeStruct((t * ROW_TILES, LANES), F32),
                      jax.ShapeDtypeStruct((t, LANES), jnp.int32), jax.ShapeDtypeStruct((t, LANES), F32)]
    return pl.pallas_call(
        functools.partial(_merge_kernel, route=router is not None),
        grid=(t // tm,),
        in_specs=in_specs,
        out_specs=out_specs,
        out_shape=out_shape,
        compiler_params=_cparams(("parallel",)),
        name="merge",
    )(*args)


def _ffn_kernel(x_ref, h_ref, wg_ref, wu_ref, wd_ref, p_ref, pwg_ref, pwp_ref, g_ref, gn_ref, o_ref, hn_ref):
    f = pl.program_id(1)

    @pl.when(f == 0)
    def _():
        o_ref[...] = x_ref[...]

    h = h_ref[...]
    a = jnp.dot(h, wg_ref[...], preferred_element_type=F32)
    u = jnp.dot(h, wu_ref[...], preferred_element_type=F32)
    o_ref[...] += jnp.dot((a * _sigmoid(a) * u).astype(BF16), wd_ref[...], preferred_element_type=F32)

    @pl.when(f == pl.num_programs(1) - 1)
    def _():
        _ple_body(o_ref[...], p_ref, pwg_ref, pwp_ref, g_ref, gn_ref, (o_ref, hn_ref))


def _ffn_ple(x, h, wg, wu, wd, p, pwg, pwp, g, g_next, h_dtype, tm=512, tf=1792):
    t = x.shape[0]
    nf = wg.shape[1] // tf
    row = lambda w: pl.BlockSpec((tm, w), lambda i, f: (i, 0))
    full = lambda a: pl.BlockSpec(a.shape, lambda i, f: (0, 0))
    return pl.pallas_call(
        _ffn_kernel,
        grid=(t // tm, nf),
        in_specs=[row(D_MODEL), row(D_MODEL),
                  pl.BlockSpec((D_MODEL, tf), lambda i, f: (0, f)),
                  pl.BlockSpec((D_MODEL, tf), lambda i, f: (0, f)),
                  pl.BlockSpec((tf, D_MODEL), lambda i, f: (f, 0)),
                  row(P_DIM), full(pwg), full(pwp), full(g), full(g_next)],
        out_specs=[row(D_MODEL), row(D_MODEL)],
        out_shape=[jax.ShapeDtypeStruct((t, D_MODEL), F32), jax.ShapeDtypeStruct((t, D_MODEL), h_dtype)],
        compiler_params=_cparams(("parallel", "arbitrary")),
        name="dense_ffn_ple",
    )(x, h, wg, wu, wd, p, pwg, pwp, g, g_next)


def _tile_at(ref, row8):
    return ref.at[pl.ds(pl.multiple_of(row8, ROW_TILES), ROW_TILES), :]


def _dispatch_kernel(pos_ref, fill_ref, h_ref, xs_hbm, zero_ref, sem, sem_fill, *, t, tmd, tm):
    i = pl.program_id(0)

    @pl.when(i == 0)
    def _():
        zero_ref[...] = jnp.zeros_like(zero_ref)
        for e in range(fill_ref.shape[0]):
            @pl.when(fill_ref[e] >= 0)
            def _():
                cp = pltpu.make_async_copy(
                    zero_ref, xs_hbm.at[pl.ds(pl.multiple_of(fill_ref[e], ROW_TILES), tm * ROW_TILES), :], sem_fill)
                cp.start()
                cp.wait()

    for k in range(TOP_K):
        for r in range(tmd):
            pltpu.make_async_copy(_tile_at(h_ref, r * ROW_TILES),
                                  _tile_at(xs_hbm, pos_ref[k * t + i * tmd + r]), sem.at[k]).start(priority=r % 2)
    for k in range(TOP_K):
        pltpu.make_async_copy(h_ref, h_ref, sem.at[k]).wait()


def _dispatch(h, pos8, fill8, n_rows, tm, tmd=512):
    t = h.shape[0] // ROW_TILES
    return pl.pallas_call(
        functools.partial(_dispatch_kernel, t=t, tmd=tmd, tm=tm),
        grid_spec=pltpu.PrefetchScalarGridSpec(
            num_scalar_prefetch=2,
            grid=(t // tmd,),
            in_specs=[pl.BlockSpec((tmd * ROW_TILES, LANES), lambda i, pos, fill: (i, 0))],
            out_specs=pl.BlockSpec(memory_space=pl.ANY),
            scratch_shapes=[pltpu.VMEM((tm * ROW_TILES, LANES), F32),
                            pltpu.SemaphoreType.DMA((TOP_K,)), pltpu.SemaphoreType.DMA(())]),
        out_shape=jax.ShapeDtypeStruct((n_rows * ROW_TILES, LANES), F32),
        compiler_params=_cparams(("arbitrary",)),
        name="moe_dispatch",
    )(pos8, fill8, h)


def _moe_ffn_kernel(te_ref, nv_ref, x_ref, wg_ref, wu_ref, wd_ref, o_ref, acc_ref, xb_ref, *, tm):
    i = pl.program_id(0)
    f = pl.program_id(1)
    valid = i < nv_ref[0]
    last = f == pl.num_programs(1) - 1

    @pl.when(valid & (f == 0))
    def _():
        xb_ref[...] = _from_token_tiles(x_ref, tm).astype(BF16)
        acc_ref[...] = jnp.zeros_like(acc_ref)

    @pl.when(valid)
    def _():
        h = xb_ref[...]
        a = jnp.dot(h, wg_ref[0], preferred_element_type=F32)
        u = jnp.dot(h, wu_ref[0], preferred_element_type=F32)
        acc_ref[...] += jnp.dot((a * _sigmoid(a) * u).astype(BF16), wd_ref[0], preferred_element_type=F32)

    @pl.when(valid & last)
    def _():
        _to_token_tiles(o_ref, acc_ref[...])

    @pl.when(jnp.logical_not(valid) & last)
    def _():
        o_ref[...] = jnp.zeros_like(o_ref)


def _moe_ffn(xs, tile_expert, n_valid, wg, wu, wd, tm, tf=1792):
    n_tiles = xs.shape[0] // (tm * ROW_TILES)
    nf = wg.shape[2] // tf

    def live(i, f, te, nv):
        ok = i < nv[0]
        return jnp.where(ok, i, nv[0] - 1), jnp.where(ok, f, nf - 1)

    return pl.pallas_call(
        functools.partial(_moe_ffn_kernel, tm=tm),
        grid_spec=pltpu.PrefetchScalarGridSpec(
            num_scalar_prefetch=2,
            grid=(n_tiles, nf),
            in_specs=[pl.BlockSpec((tm * ROW_TILES, LANES), lambda *a: (live(*a)[0], 0)),
                      pl.BlockSpec((1, D_MODEL, tf), lambda *a: (a[2][live(*a)[0]], 0, live(*a)[1])),
                      pl.BlockSpec((1, D_MODEL, tf), lambda *a: (a[2][live(*a)[0]], 0, live(*a)[1])),
                      pl.BlockSpec((1, tf, D_MODEL), lambda *a: (a[2][live(*a)[0]], live(*a)[1], 0))],
            out_specs=pl.BlockSpec((tm * ROW_TILES, LANES), lambda i, f, te, nv: (i, 0)),
            scratch_shapes=[pltpu.VMEM((tm, D_MODEL), F32), pltpu.VMEM((tm, D_MODEL), BF16)]),
        out_shape=jax.ShapeDtypeStruct(xs.shape, F32),
        compiler_params=_cparams(("arbitrary", "arbitrary")),
        name="moe_ffn",
    )(tile_expert, n_valid, xs, wg, wu, wd)


def _router_params(w_router, b_router):
    w_pad = jnp.zeros((D_MODEL, LANES), F32).at[:, :N_EXPERTS].set(w_router)
    b_pad = jnp.full((1, LANES), NEG, F32).at[0, :N_EXPERTS].set(b_router)
    return w_pad, b_pad


def _moe(h, idx_l, wg, wu, wd, tm=448):
    t = h.shape[0] // ROW_TILES
    expert = idx_l[:, :TOP_K].T.reshape(-1)
    onehot = (expert[:, None] == jnp.arange(N_EXPERTS)[None, :]).astype(jnp.int32)
    csum = jnp.cumsum(onehot, axis=0)
    counts = csum[-1]
    rank = jnp.sum((csum - onehot) * onehot, axis=1)
    tiles = (counts + tm - 1) // tm
    tile_end = jnp.cumsum(tiles)
    pad_off = (tile_end - tiles) * tm
    pos8 = (pad_off[expert] + rank) * ROW_TILES
    n_tiles = (TOP_K * t) // tm + N_EXPERTS
    n_valid = tile_end[-1:].astype(jnp.int32)
    tile_ids = jnp.minimum(jnp.arange(n_tiles), n_valid[0] - 1)
    tile_expert = jnp.sum(tile_end[None, :] <= tile_ids[:, None], axis=1).astype(jnp.int32)
    spare = n_valid[0] + jnp.arange(N_EXPERTS)
    fill8 = jnp.concatenate([jnp.where(tiles > 0, tile_end - 1, -1), jnp.where(spare < n_tiles, spare, -1)])
    fill8 = jnp.where(fill8 >= 0, fill8 * (tm * ROW_TILES), -1).astype(jnp.int32)
    xs = _dispatch(h, pos8, fill8, n_tiles * tm, tm)
    ys = _moe_ffn(xs, tile_expert, n_valid, wg, wu, wd, tm)
    return ys, pos8


def _ple_body(x, p_ref, wg_ref, wp_ref, g_ref, gn_ref, outs):
    gate = _sigmoid(jnp.dot(_rms(x, g_ref[...]).astype(BF16), wg_ref[...], preferred_element_type=F32))
    x2 = x + gate * jnp.dot(p_ref[...].astype(BF16), wp_ref[...], preferred_element_type=F32)
    if len(outs) == 2:
        outs[0][...] = x2
    outs[-1][...] = _rms(x2, gn_ref[...]).astype(outs[-1].dtype)


def _ple_moe_kernel(pos_ref, x_ref, p_ref, wt_ref, wg_ref, wp_ref, g_ref, gn_ref, y_hbm, *rest, t, tm):
    *outs, ybuf, sem = rest
    i = pl.program_id(0)
    slot = i % 2

    def gather(block, s):
        for k in range(TOP_K):
            for r in range(tm):
                pltpu.make_async_copy(_tile_at(y_hbm, pos_ref[k * t + block * tm + r]),
                                      _tile_at(ybuf.at[s, k], r * ROW_TILES), sem.at[s]).start(priority=r % 2)

    def wait(s):
        pltpu.make_async_copy(ybuf.at[s], ybuf.at[s], sem.at[s]).wait()

    last = pl.num_programs(0) - 1

    @pl.when(i == 0)
    def _():
        gather(0, 0)

    wait(slot)
    gather(jnp.minimum(i + 1, last), 1 - slot)
    wt = wt_ref[...]
    x = (x_ref[...] + wt[:, 0:1] * _from_token_tiles(ybuf.at[slot, 0], tm)
         + wt[:, 1:2] * _from_token_tiles(ybuf.at[slot, 1], tm))
    _ple_body(x, p_ref, wg_ref, wp_ref, g_ref, gn_ref, outs)

    @pl.when(i == last)
    def _():
        wait(1 - slot)


def _ple_moe(x, p, wg, wp, g, g_next, h_dtype, want_x, moe, tm=256):
    t = x.shape[0]
    n_out = 2 if want_x else 1
    out_shape = ([jax.ShapeDtypeStruct((t, D_MODEL), F32)] * (n_out - 1)
                 + [jax.ShapeDtypeStruct((t, D_MODEL), h_dtype)])
    ys, pos8, wt = moe
    row = lambda w: pl.BlockSpec((tm, w), lambda i, pos: (i, 0))
    full = lambda a: pl.BlockSpec(a.shape, lambda i, pos: (0, 0))
    return pl.pallas_call(
        functools.partial(_ple_moe_kernel, t=t, tm=tm),
        grid_spec=pltpu.PrefetchScalarGridSpec(
            num_scalar_prefetch=1,
            grid=(t // tm,),
            in_specs=[row(D_MODEL), row(P_DIM), row(LANES), full(wg), full(wp), full(g), full(g_next),
                      pl.BlockSpec(memory_space=pl.ANY)],
            out_specs=[row(D_MODEL)] * n_out,
            scratch_shapes=[pltpu.VMEM((2, TOP_K, tm * ROW_TILES, LANES), F32), pltpu.SemaphoreType.DMA((2,))]),
        out_shape=out_shape,
        compiler_params=_cparams(("arbitrary",)),
        name="ple_moe",
    )(pos8, x, p, wt, wg, wp, g, g_next, ys)


def _rot_cols(w):
    half = w.shape[-1] // 2
    return jnp.concatenate([-w[:, half:], w[:, :half]], axis=-1)


def _layer_weights(i, w_in, w_uq, w_ukv):
    wi = w_in[i]
    z = lambda n: jnp.zeros((wi.shape[0], n), F32)
    w_kr = wi[:, IN_OFF[2]:IN_OFF[3]]
    pad = MLA_SLAB - MLA_NOPE - MLA_ROPE
    wlat = jnp.concatenate([wi[:, :IN_OFF[2]], z(MLA_NOPE), w_kr, z(pad), z(MLA_NOPE), _rot_cols(w_kr), z(pad)],
                           axis=1).astype(BF16)
    qscale = jnp.concatenate([jnp.full((DIL_WIDTH,), DIL_HD ** -0.5, F32), jnp.ones((2 * DIL_WIDTH,), F32)])
    wdil = (wi[:, IN_OFF[3]:IN_OFF[6]] * qscale).astype(BF16)
    wgate = wi[:, IN_OFF[6]:].astype(BF16)
    uq = w_uq[i].reshape(MLA_Q_RANK, MLA_HEADS, MLA_NOPE + MLA_ROPE)
    zq = lambda n: jnp.zeros((MLA_Q_RANK, MLA_HEADS, n), F32)
    rope_rot = jnp.concatenate([-uq[..., MLA_NOPE + MLA_ROPE // 2:], uq[..., MLA_NOPE:MLA_NOPE + MLA_ROPE // 2]], -1)
    wq = jnp.concatenate([uq, zq(pad)], axis=-1).reshape(MLA_Q_RANK, -1).astype(BF16)
    wqr = jnp.concatenate([zq(MLA_NOPE), rope_rot, zq(pad)], axis=-1).reshape(MLA_Q_RANK, -1).astype(BF16)
    ukv = w_ukv[i].reshape(MLA_KV_RANK, MLA_HEADS, MLA_NOPE + MLA_V)
    zkv = jnp.zeros((MLA_KV_RANK, MLA_HEADS, MLA_SLAB - MLA_NOPE), F32)
    wk = jnp.concatenate([ukv[..., :MLA_NOPE], zkv], axis=-1).reshape(MLA_KV_RANK, -1).astype(BF16)
    wv = jnp.concatenate([ukv[..., MLA_NOPE:], zkv], axis=-1).reshape(MLA_KV_RANK, -1).astype(BF16)
    return wlat, wdil, wgate, wq, wqr, wk, wv


def kernel(x, p, positions, attn_norm, w_in, q_norm, w_uq, kv_norm, w_ukv, w_br_mla, w_br_dil, w_out, ffn_norm, dense_w_gate, dense_w_up, dense_w_down, router_w, router_b, moe_w_gate, moe_w_up, moe_w_down, ple_norm, ple_w_gate, ple_w_proj, final_norm):
    batch, seq, d = x.shape
    t = batch * seq
    depth = w_in.shape[0]
    xf = x.reshape(t, d)
    cos_t, sin_t = _rope_tables(positions)
    slopes = _alibi_slopes(DIL_HEADS)
    row = lambda v: v.reshape(1, -1)
    h = _norm(xf, attn_norm[0])
    for i in range(depth):
        wlat, wdil, wgate, wq, wqr, wk, wv = _layer_weights(i, w_in, w_uq, w_ukv)
        q, k, v = _mla_proj(h, wlat, wq, wqr, wk, wv, row(q_norm[i]), row(kv_norm[i]), cos_t, sin_t)
        o_mla = _mla_attn(q, k, v, batch, seq)
        qkv_d = [_dil_proj(h, wdil, seq, g) for g in range(DIL_GROUPS)]
        o_dil = _dil_attn(qkv_d, slopes, batch, seq)
        moe_layer = i % 2 == 1
        j = i // 2
        xf, h2, *routing = _merge(xf, h, o_mla, o_dil, wgate, w_br_mla[i].astype(BF16), w_br_dil[i].astype(BF16),
                                  w_out[i].astype(BF16), row(ffn_norm[i]),
                                  _router_params(router_w[j], router_b[j]) if moe_layer else None)
        last = i == depth - 1
        g_next = final_norm if last else attn_norm[i + 1]
        ple_args = (p[i].reshape(t, -1), ple_w_gate[i].astype(BF16), ple_w_proj[i].astype(BF16),
                    row(ple_norm[i]), row(g_next), F32 if last else BF16)
        if moe_layer:
            idx_l, wt_l = routing
            moe = _moe(h2, idx_l, moe_w_gate[j].astype(BF16), moe_w_up[j].astype(BF16),
                       moe_w_down[j].astype(BF16)) + (wt_l,)
            *xnext, h = _ple_moe(xf, *ple_args, not last, moe)
            xf = xnext[0] if xnext else None
        else:
            xf, h = _ffn_ple(xf, h2, dense_w_gate[j].astype(BF16), dense_w_up[j].astype(BF16),
                             dense_w_down[j].astype(BF16), *ple_args)
    return h.reshape(batch, seq, d)
```

```python
import functools

import numpy as np
import jax
import jax.numpy as jnp
from jax import lax
from jax.experimental import pallas as pl
from jax.experimental.pallas import tpu as pltpu

F32 = jnp.float32
BF16 = jnp.bfloat16

D_MODEL = 1024
P_DIM = 256
NORM_EPS = 1e-6
DEPTH = 2

MLA_HEADS = 8
MLA_Q_RANK = 384
MLA_KV_RANK = 256
MLA_NOPE = 64
MLA_ROPE = 32
MLA_V = 64
ROPE_THETA = 10000.0
MLA_SLAB = 128

DIL_CONFIGS = ((128, 1), (512, 4), (2048, 16))
DIL_GROUPS = 3
DIL_HPG = 8
DIL_HEADS = 24
DIL_HD = 64
DIL_WIDTH = DIL_HEADS * DIL_HD
DIL_SPAN = 128

D_FF = 3584
N_EXPERTS = 8
TOP_K = 2

LANES = 128
ROW_TILES = D_MODEL // LANES
NEG = -1e30
LOG2E = 1.4426950408889634
VMEM_LIMIT = 56 * 1024 * 1024

IN_OFF = tuple(int(o) for o in np.cumsum((0, MLA_Q_RANK, MLA_KV_RANK, MLA_ROPE, DIL_WIDTH, DIL_WIDTH,
                                          DIL_WIDTH, D_MODEL, D_MODEL)))


def _cparams(sem):
    return pltpu.CompilerParams(dimension_semantics=sem, vmem_limit_bytes=VMEM_LIMIT)


def _rms(x, g):
    return x * lax.rsqrt(jnp.mean(x * x, axis=-1, keepdims=True) + NORM_EPS) * g


def _sigmoid(x):
    return 1.0 / (1.0 + jnp.exp(-x))


def _alibi_slopes(n):
    def pow2(m):
        start = 2.0 ** (-8.0 / m)
        return [start ** (i + 1) for i in range(m)]
    if float(np.log2(n)).is_integer():
        s = pow2(n)
    else:
        c = 2 ** int(np.floor(np.log2(n)))
        s = pow2(c) + pow2(2 * c)[0::2][: n - c]
    return jnp.asarray(sorted(s, reverse=True), dtype=F32)


def _norm_kernel(x_ref, g_ref, o_ref):
    o_ref[...] = _rms(x_ref[...], g_ref[...]).astype(o_ref.dtype)


def _norm(x, g, tm=1024):
    t, d = x.shape
    return pl.pallas_call(
        _norm_kernel,
        grid=(t // tm,),
        in_specs=[pl.BlockSpec((tm, d), lambda i: (i, 0)), pl.BlockSpec((1, d), lambda i: (0, 0))],
        out_specs=pl.BlockSpec((tm, d), lambda i: (i, 0)),
        out_shape=jax.ShapeDtypeStruct((t, d), BF16),
        compiler_params=_cparams(("parallel",)),
        name="rms_norm",
    )(x, g.reshape(1, d))


def _rope_table_kernel(pos_ref, invf_ref, cos_ref, sin_ref):
    ang = pos_ref[...].astype(F32) * invf_ref[...]
    lane = lax.broadcasted_iota(jnp.int32, ang.shape, 1)
    rope_lane = (lane >= MLA_NOPE) & (lane < MLA_NOPE + MLA_ROPE)
    cos_ref[...] = jnp.where(lane < MLA_NOPE, 1.0, jnp.where(rope_lane, jnp.cos(ang), 0.0))
    sin_ref[...] = jnp.where(rope_lane, jnp.sin(ang), 0.0)


def _rope_tables(positions, tm=2048):
    t = positions.size
    half = MLA_ROPE // 2
    inv_freq = ROPE_THETA ** (-jnp.arange(half, dtype=F32) / half)
    invf = jnp.zeros((1, LANES), F32).at[0, MLA_NOPE:MLA_NOPE + MLA_ROPE].set(jnp.concatenate([inv_freq, inv_freq]))
    pos_b = jnp.broadcast_to(positions.reshape(t, 1), (t, LANES))
    spec = pl.BlockSpec((tm, LANES), lambda i: (i, 0))
    return pl.pallas_call(
        _rope_table_kernel,
        grid=(t // tm,),
        in_specs=[spec, pl.BlockSpec((1, LANES), lambda i: (0, 0))],
        out_specs=[spec, spec],
        out_shape=[jax.ShapeDtypeStruct((t, LANES), F32)] * 2,
        compiler_params=_cparams(("parallel",)),
        name="rope_tables",
    )(pos_b, invf)


def _mla_proj_kernel(h_ref, wlat_ref, wq_ref, wqr_ref, wk_ref, wv_ref, qn_ref, kvn_ref, cos_ref, sin_ref,
                     q_out, k_out, v_out):
    lat = jnp.dot(h_ref[...], wlat_ref[...], preferred_element_type=F32)
    cqn = _rms(lat[:, :MLA_Q_RANK], qn_ref[...]).astype(BF16)
    ckvn = _rms(lat[:, MLA_Q_RANK:MLA_Q_RANK + MLA_KV_RANK], kvn_ref[...]).astype(BF16)
    cos = cos_ref[...]
    sin = sin_ref[...]
    o = MLA_Q_RANK + MLA_KV_RANK
    k_rope = lat[:, o:o + LANES] * cos + lat[:, o + LANES:o + 2 * LANES] * sin
    qa = jnp.dot(cqn, wq_ref[...], preferred_element_type=F32)
    qb = jnp.dot(cqn, wqr_ref[...], preferred_element_type=F32)
    kk = jnp.dot(ckvn, wk_ref[...], preferred_element_type=F32)
    scale = (MLA_NOPE + MLA_ROPE) ** -0.5 * LOG2E
    vv = jnp.dot(ckvn, wv_ref[...], preferred_element_type=F32)
    ones_lane = lax.broadcasted_iota(jnp.int32, cos.shape, 1) >= MLA_V
    for hd in range(MLA_HEADS):
        sl = slice(hd * MLA_SLAB, (hd + 1) * MLA_SLAB)
        q_out[:, sl] = ((qa[:, sl] * cos + qb[:, sl] * sin) * scale).astype(BF16)
        k_out[:, sl] = (kk[:, sl] + k_rope).astype(BF16)
        v_out[:, sl] = jnp.where(ones_lane, 1.0, vv[:, sl]).astype(BF16)


def _mla_proj(h, wlat, wq, wqr, wk, wv, qn, kvn, cos_t, sin_t, tm=512):
    t = h.shape[0]
    row = lambda w: pl.BlockSpec((tm, w), lambda i: (i, 0))
    full = lambda a: pl.BlockSpec(a.shape, lambda i: (0, 0))
    hs = MLA_HEADS * MLA_SLAB
    return pl.pallas_call(
        _mla_proj_kernel,
        grid=(t // tm,),
        in_specs=[row(D_MODEL), full(wlat), full(wq), full(wqr), full(wk), full(wv), full(qn), full(kvn),
                  row(LANES), row(LANES)],
        out_specs=[row(hs), row(hs), row(hs)],
        out_shape=[jax.ShapeDtypeStruct((t, hs), BF16)] * 3,
        compiler_params=_cparams(("parallel",)),
        name="mla_proj",
    )(h, wlat, wq, wqr, wk, wv, qn, kvn, cos_t, sin_t)


MLA_ROW_SPLIT = 2


def _mla_attn_kernel(q_ref, k_ref, v_ref, o_ref, s_scr, *, tq):
    qi = pl.program_id(2)
    nh = q_ref.shape[1] // MLA_SLAB
    tr = tq // MLA_ROW_SPLIT

    def tile(n):
        def keys(c, rg):
            return (rg + 1) * tr if c == n else tq

        def pass1(hh, rg):
            rows = slice(rg * tr, (rg + 1) * tr)
            q = q_ref[rows, hh * MLA_SLAB:(hh + 1) * MLA_SLAB]
            mx = None
            for c in range(n + 1):
                kw = keys(c, rg)
                ks = k_ref[c * tq:c * tq + kw, hh * MLA_SLAB:(hh + 1) * MLA_SLAB]
                s = lax.dot_general(q, ks, (((1,), (1,)), ((), ())), preferred_element_type=F32)
                if c == n:
                    rq = lax.broadcasted_iota(jnp.int32, (tr, kw), 0)
                    ck = lax.broadcasted_iota(jnp.int32, (tr, kw), 1)
                    s = jnp.where(ck <= rq + rg * tr, s, NEG)
                s_scr[hh, c, rows, :kw] = s
                parts = [s[:, i * LANES:(i + 1) * LANES] for i in range(kw // LANES)]
                mx = functools.reduce(jnp.maximum, parts if mx is None else [mx] + parts)
            return jnp.max(mx, axis=-1, keepdims=True)

        def pass2(hh, rg, m):
            rows = slice(rg * tr, (rg + 1) * tr)
            acc = None
            for c in range(n + 1):
                kw = keys(c, rg)
                vs = v_ref[c * tq:c * tq + kw, hh * MLA_SLAB:(hh + 1) * MLA_SLAB]
                p = jnp.exp2((s_scr[hh, c, rows, :kw] - m).astype(BF16))
                pv = jnp.dot(p, vs, preferred_element_type=F32)
                acc = pv if acc is None else acc + pv
            o = acc / acc[:, MLA_V:MLA_V + 1]
            o_ref[rows, hh * MLA_V:(hh + 1) * MLA_V] = o[:, :MLA_V].astype(o_ref.dtype)

        chains = [(hh, rg) for hh in range(nh) for rg in range(MLA_ROW_SPLIT)]
        ms = [pass1(*chains[0])]
        for i, chain in enumerate(chains):
            if i + 1 < len(chains):
                ms.append(pass1(*chains[i + 1]))
            pass2(*chain, ms[i])

    for n in range(s_scr.shape[1]):
        pl.when(qi == n)(functools.partial(tile, n))


def _mla_attn(q, k, v, batch, seq, tq=512, nh=4):
    t = q.shape[0]
    nq = seq // tq
    return pl.pallas_call(
        functools.partial(_mla_attn_kernel, tq=tq),
        grid=(batch, MLA_HEADS // nh, nq),
        in_specs=[pl.BlockSpec((tq, nh * MLA_SLAB), lambda b, p, i: (b * nq + i, p)),
                  pl.BlockSpec((seq, nh * MLA_SLAB), lambda b, p, i: (b, p)),
                  pl.BlockSpec((seq, nh * MLA_SLAB), lambda b, p, i: (b, p))],
        out_specs=pl.BlockSpec((tq, nh * MLA_V), lambda b, p, i: (b * nq + i, p)),
        out_shape=jax.ShapeDtypeStruct((t, MLA_HEADS * MLA_V), BF16),
        scratch_shapes=[pltpu.VMEM((nh, nq, tq, tq), F32)],
        compiler_params=_cparams(("parallel", "parallel", "arbitrary")),
        name="mla_attn",
    )(q, k, v)


PHASE_STRIDE = 4


def _phase_of_slot(r):
    if r <= PHASE_STRIDE:
        return list(range(r))
    f2 = r // PHASE_STRIDE
    return [c1 + PHASE_STRIDE * c2 for c1 in range(PHASE_STRIDE) for c2 in range(f2)]


def _dil_proj_kernel(h_ref, w_ref, o_ref, acc_ref, tmp_ref, *, seq, r):
    nl = o_ref.shape[0]

    @pl.when(pl.program_id(0) == 0)
    def _():
        acc_ref[...] = jnp.zeros_like(acc_ref)

    res = jnp.dot(h_ref[...], w_ref[...], preferred_element_type=F32)
    f1 = min(r, PHASE_STRIDE)
    l1 = seq // f1
    dst = o_ref if r == f1 else tmp_ref
    for c1 in range(f1):
        for j in range(nl):
            src = acc_ref[j, pl.ds(c1, l1, stride=f1), :] if f1 > 1 else acc_ref[j]
            dst[j, c1 * l1:(c1 + 1) * l1, :] = src.astype(dst.dtype)
    if r > f1:
        f2 = r // f1
        l2 = l1 // f2
        for s in range(r):
            c1, c2 = divmod(s, f2)
            for j in range(nl):
                o_ref[j, s * l2:(s + 1) * l2, :] = (
                    tmp_ref[j, pl.ds(c1 * l1 + c2, l2, stride=f2), :].astype(o_ref.dtype))
    for j in range(nl):
        acc_ref[j] = res[:, j * LANES:(j + 1) * LANES]


def _dil_proj(h, w, seq, g, tn=512):
    t = h.shape[0]
    nb = t // seq
    nl = tn // LANES
    n_blocks = 3 * nb
    r = DIL_CONFIGS[g][1]

    def mm(s):
        b, tt = divmod(jnp.minimum(s, n_blocks - 1), 3)
        return tt, b

    def wr(s):
        b, tt = divmod(jnp.maximum(s - 1, 0), 3)
        return tt, b

    return pl.pallas_call(
        functools.partial(_dil_proj_kernel, seq=seq, r=r),
        grid=(n_blocks + 1,),
        in_specs=[pl.BlockSpec((seq, D_MODEL), lambda s: (mm(s)[1], 0)),
                  pl.BlockSpec((D_MODEL, tn), lambda s: (0, mm(s)[0] * DIL_GROUPS + g))],
        out_specs=pl.BlockSpec((nl, seq, LANES), lambda s: (wr(s)[0], wr(s)[1], 0)),
        out_shape=jax.ShapeDtypeStruct((3 * nl, t, LANES), BF16),
        scratch_shapes=[pltpu.VMEM((nl, seq, LANES), F32)] * 2,
        compiler_params=_cparams(("arbitrary",)),
        name="dil_proj",
    )(h, w)


def _dil_attn_kernel(slopes_ref, q0, q1, q2, k0, k1, k2, v0, v1, v2, o_ref, acc_s, m_s, l_s, *, seq):
    pair = pl.program_id(1)
    sp = DIL_SPAN
    head0 = lax.broadcasted_iota(jnp.int32, (sp, LANES), 1) < DIL_HD
    qi = lax.broadcasted_iota(jnp.int32, (sp, 2 * sp), 0)
    kj = lax.broadcasted_iota(jnp.int32, (sp, 2 * sp), 1)
    dist_w = qi + sp - kj
    valid_w = (dist_w >= 0) & (dist_w <= sp)
    dist_1 = dist_w[:, sp:]
    valid_1 = dist_1 >= 0
    qs, ks, vs = (q0, q1, q2), (k0, k1, k2), (v0, v1, v2)
    for g, (_, r) in enumerate(DIL_CONFIGS):
        ln = seq // r
        nb = ln // sp
        sl = [slopes_ref[g * DIL_HPG + 2 * pair + hh] * float(r) for hh in range(2)]
        bias_w = jnp.concatenate([jnp.where(valid_w, -s * dist_w.astype(F32), NEG) for s in sl], axis=0)
        bias_1 = jnp.concatenate([jnp.where(valid_1, -s * dist_1.astype(F32), NEG) for s in sl], axis=0)
        for slot_i, c in enumerate(_phase_of_slot(r)):
            for i in range(nb):
                row0 = slot_i * ln + i * sp
                qb = qs[g][0, row0:row0 + sp, :]
                zero = jnp.zeros_like(qb)
                q2h = jnp.concatenate([jnp.where(head0, qb, zero), jnp.where(head0, zero, qb)], axis=0)
                lo = row0 if i == 0 else row0 - sp
                kw = ks[g][0, lo:row0 + sp, :]
                vw = vs[g][0, lo:row0 + sp, :]
                s = lax.dot_general(q2h, kw, (((1,), (1,)), ((), ())), preferred_element_type=F32)
                s = s + (bias_1 if i == 0 else bias_w)
                m = jnp.max(s, axis=-1, keepdims=True)
                p = jnp.exp((s - m).astype(BF16))
                l = jnp.sum(p.astype(F32), axis=-1, keepdims=True)
                o2 = jnp.dot(p, vw, preferred_element_type=F32)
                dst = pl.ds(i * sp * r + c, sp, stride=r) if r > 1 else pl.ds(i * sp, sp)
                acc_s[g, dst, :] = jnp.where(head0, o2[:sp], o2[sp:])
                m_s[g, dst, :] = jnp.where(head0, m[:sp], m[sp:])
                l_s[g, dst, :] = jnp.where(head0, l[:sp], l[sp:])

    ch = 256

    def combine(i, _):
        rows = pl.ds(pl.multiple_of(i * ch, ch), ch)
        ms = [m_s[g, rows, :] for g in range(DIL_GROUPS)]
        mx = jnp.maximum(jnp.maximum(ms[0], ms[1]), ms[2])
        ws = [jnp.exp(m - mx) for m in ms]
        num = sum(ws[g] * acc_s[g, rows, :] for g in range(DIL_GROUPS))
        den = sum(ws[g] * l_s[g, rows, :] for g in range(DIL_GROUPS))
        o_ref[rows, :] = (num / den).astype(o_ref.dtype)
        return 0

    lax.fori_loop(0, seq // ch, combine, 0)


def _dil_attn(qkvs, slopes, batch, seq):
    t = qkvs[0].shape[1]
    pairs = DIL_HPG // 2

    def spec(tt):
        return pl.BlockSpec((1, seq, LANES), lambda b, p: (tt * pairs + p, b, 0))

    in_specs = [pl.BlockSpec(memory_space=pltpu.SMEM)] + [spec(tt) for tt in range(3) for _ in range(DIL_GROUPS)]
    return pl.pallas_call(
        functools.partial(_dil_attn_kernel, seq=seq),
        grid=(batch, pairs),
        in_specs=in_specs,
        out_specs=pl.BlockSpec((seq, LANES), lambda b, p: (b, p)),
        out_shape=jax.ShapeDtypeStruct((t, DIL_HPG * DIL_HD), BF16),
        scratch_shapes=[pltpu.VMEM((DIL_GROUPS, seq, LANES), F32)] * 3,
        compiler_params=_cparams(("parallel", "parallel")),
        name="dil_attn",
    )(slopes, *[qkvs[g] for _ in range(3) for g in range(DIL_GROUPS)])


def _to_token_tiles(dst_ref, val):
    n = val.shape[0]
    for s in range(ROW_TILES):
        dst_ref[pl.ds(s, n, stride=ROW_TILES), :] = val[:, s * LANES:(s + 1) * LANES]


def _from_token_tiles(src_ref, n):
    return jnp.concatenate([src_ref[pl.ds(s, n, stride=ROW_TILES), :] for s in range(ROW_TILES)], axis=1)


def _top2(logits):
    lane = lax.broadcasted_iota(jnp.int32, logits.shape, 1)
    m1 = jnp.max(logits, axis=-1, keepdims=True)
    i1 = jnp.min(jnp.where(logits == m1, lane, LANES), axis=-1, keepdims=True)
    rest = jnp.where(lane == i1, NEG, logits)
    m2 = jnp.max(rest, axis=-1, keepdims=True)
    i2 = jnp.min(jnp.where(rest == m2, lane, LANES), axis=-1, keepdims=True)
    e = jnp.exp(m2 - m1)
    w1 = 1.0 / (1.0 + e)
    return (jnp.where(lane == 0, i1, jnp.where(lane == 1, i2, 0)),
            jnp.where(lane == 0, w1, jnp.where(lane == 1, e * w1, 0.0)))


def _merge_kernel(x_ref, h_ref, om_ref, od_ref, wg_ref, wbm_ref, wbd_ref, wo_ref, g_ref, *rest, route):
    gates = jnp.dot(h_ref[...], wg_ref[...], preferred_element_type=F32)
    bm = jnp.dot(om_ref[...], wbm_ref[...], preferred_element_type=F32)
    bd = jnp.dot(od_ref[...], wbd_ref[...], preferred_element_type=F32)
    merged = _sigmoid(gates[:, :D_MODEL]) * bm + _sigmoid(gates[:, D_MODEL:]) * bd
    x1 = x_ref[...] + jnp.dot(merged.astype(BF16), wo_ref[...], preferred_element_type=F32)
    hn = _rms(x1, g_ref[...])
    if route:
        wr_ref, br_ref, x_out, h_out, idx_out, wt_out = rest
        _to_token_tiles(h_out, hn)
        hi = hn.astype(BF16)
        lo = (hn - hi.astype(F32)).astype(BF16)
        w = wr_ref[...]
        w_hi = w.astype(BF16)
        w_lo = (w - w_hi.astype(F32)).astype(BF16)
        both = jnp.dot(hi, jnp.concatenate([w_hi, w_lo], axis=1), preferred_element_type=F32)
        logits = (both[:, :LANES] + both[:, LANES:]
                  + jnp.dot(lo, w_hi, preferred_element_type=F32) + br_ref[...])
        idx_out[...], wt_out[...] = _top2(logits)
    else:
        x_out, h_out = rest
        h_out[...] = hn.astype(h_out.dtype)
    x_out[...] = x1


def _merge(x, h, o_mla, o_dil, wg, wbm, wbd, wo, g_next, router=None, tm=512):
    t = x.shape[0]
    row = lambda w: pl.BlockSpec((tm, w), lambda i: (i, 0))
    full = lambda a: pl.BlockSpec(a.shape, lambda i: (0, 0))
    in_specs = [row(D_MODEL), row(D_MODEL), row(o_mla.shape[1]), row(o_dil.shape[1]),
                full(wg), full(wbm), full(wbd), full(wo), full(g_next)]
    args = [x, h, o_mla, o_dil, wg, wbm, wbd, wo, g_next]
    out_specs = [row(D_MODEL)]
    out_shape = [jax.ShapeDtypeStruct((t, D_MODEL), F32)]
    if router is None:
        out_specs += [row(D_MODEL)]
        out_shape += [jax.ShapeDtypeStruct((t, D_MODEL), BF16)]
    else:
        in_specs += [full(router[0]), full(router[1])]
        args += list(router)
        out_specs += [pl.BlockSpec((tm * ROW_TILES, LANES), lambda i: (i, 0)), row(LANES), row(LANES)]
        out_shape += [jax.ShapeDtypeStruct((t * ROW_TILES, LANES), F32),
                      jax.ShapeDtypeStruct((t, LANES), jnp.int32), jax.ShapeDtypeStruct((t, LANES), F32)]
    return pl.pallas_call(
        functools.partial(_merge_kernel, route=router is not None),
        grid=(t // tm,),
        in_specs=in_specs,
        out_specs=out_specs,
        out_shape=out_shape,
        compiler_params=_cparams(("parallel",)),
        name="merge",
    )(*args)


def _ffn_kernel(x_ref, h_ref, wg_ref, wu_ref, wd_ref, p_ref, pwg_ref, pwp_ref, g_ref, gn_ref, o_ref, hn_ref):
    f = pl.program_id(1)

    @pl.when(f == 0)
    def _():
        o_ref[...] = x_ref[...]

    h = h_ref[...]
    a = jnp.dot(h, wg_ref[...], preferred_element_type=F32)
    u = jnp.dot(h, wu_ref[...], preferred_element_type=F32)
    o_ref[...] += jnp.dot((a * _sigmoid(a) * u).astype(BF16), wd_ref[...], preferred_element_type=F32)

    @pl.when(f == pl.num_programs(1) - 1)
    def _():
        _ple_body(o_ref[...], p_ref, pwg_ref, pwp_ref, g_ref, gn_ref, (o_ref, hn_ref))


def _ffn_ple(x, h, wg, wu, wd, p, pwg, pwp, g, g_next, h_dtype, tm=512, tf=1792):
    t = x.shape[0]
    nf = wg.shape[1] // tf
    row = lambda w: pl.BlockSpec((tm, w), lambda i, f: (i, 0))
    full = lambda a: pl.BlockSpec(a.shape, lambda i, f: (0, 0))
    return pl.pallas_call(
        _ffn_kernel,
        grid=(t // tm, nf),
        in_specs=[row(D_MODEL), row(D_MODEL),
                  pl.BlockSpec((D_MODEL, tf), lambda i, f: (0, f)),
                  pl.BlockSpec((D_MODEL, tf), lambda i, f: (0, f)),
                  pl.BlockSpec((tf, D_MODEL), lambda i, f: (f, 0)),
                  row(P_DIM), full(pwg), full(pwp), full(g), full(g_next)],
        out_specs=[row(D_MODEL), row(D_MODEL)],
        out_shape=[jax.ShapeDtypeStruct((t, D_MODEL), F32), jax.ShapeDtypeStruct((t, D_MODEL), h_dtype)],
        compiler_params=_cparams(("parallel", "arbitrary")),
        name="dense_ffn_ple",
    )(x, h, wg, wu, wd, p, pwg, pwp, g, g_next)


def _tile_at(ref, row8):
    return ref.at[pl.ds(pl.multiple_of(row8, ROW_TILES), ROW_TILES), :]


def _dispatch_kernel(pos_ref, fill_ref, h_hbm, xs_hbm, zero_ref, hbuf, sem_in, sem, sem_fill, *, t, tmd, tm):
    i = pl.program_id(0)
    n = pl.num_programs(0)
    rows = tmd * ROW_TILES

    def load(block, slot):
        return pltpu.make_async_copy(h_hbm.at[pl.ds(pl.multiple_of(block * rows, rows), rows), :],
                                     hbuf.at[slot], sem_in.at[slot])

    def wait_rows(parity):
        for _ in range(TOP_K):
            pltpu.make_async_copy(hbuf.at[0], hbuf.at[0], sem.at[parity]).wait()

    @pl.when(i == 0)
    def _():
        load(0, 0).start()
        load(1, 1).start()

    @pl.when(i == 0)
    def _():
        zero_ref[...] = jnp.zeros_like(zero_ref)
        for e in range(fill_ref.shape[0]):
            @pl.when(fill_ref[e] >= 0)
            def _():
                cp = pltpu.make_async_copy(
                    zero_ref, xs_hbm.at[pl.ds(pl.multiple_of(fill_ref[e], ROW_TILES), tm * ROW_TILES), :], sem_fill)
                cp.start()
                cp.wait()

    slot = i % 3
    par = i % 2
    load(i, slot).wait()
    h_ref = hbuf.at[slot]
    for k in range(TOP_K):
        for r in range(tmd):
            pltpu.make_async_copy(_tile_at(h_ref, r * ROW_TILES),
                                  _tile_at(xs_hbm, pos_ref[k * t + i * tmd + r]), sem.at[par]).start(priority=r % 2)

    @pl.when(i > 0)
    def _():
        wait_rows(1 - par)

    @pl.when(i + 2 < n)
    def _():
        load(i + 2, (i + 2) % 3).start()

    @pl.when(i == n - 1)
    def _():
        wait_rows(par)


def _dispatch(h, pos8, fill8, n_rows, tm, tmd=512):
    t = h.shape[0] // ROW_TILES
    return pl.pallas_call(
        functools.partial(_dispatch_kernel, t=t, tmd=tmd, tm=tm),
        grid_spec=pltpu.PrefetchScalarGridSpec(
            num_scalar_prefetch=2,
            grid=(t // tmd,),
            in_specs=[pl.BlockSpec(memory_space=pl.ANY)],
            out_specs=pl.BlockSpec(memory_space=pl.ANY),
            scratch_shapes=[pltpu.VMEM((tm * ROW_TILES, LANES), F32),
                            pltpu.VMEM((3, tmd * ROW_TILES, LANES), F32),
                            pltpu.SemaphoreType.DMA((3,)), pltpu.SemaphoreType.DMA((2,)),
                            pltpu.SemaphoreType.DMA(())]),
        out_shape=jax.ShapeDtypeStruct((n_rows * ROW_TILES, LANES), F32),
        compiler_params=_cparams(("arbitrary",)),
        name="moe_dispatch",
    )(pos8, fill8, h)


def _moe_ffn_kernel(te_ref, nv_ref, x_ref, wg_ref, wu_ref, wd_ref, o_ref, acc_ref, xb_ref, *, tm):
    i = pl.program_id(0)
    f = pl.program_id(1)
    valid = i < nv_ref[0]
    last = f == pl.num_programs(1) - 1

    @pl.when(valid & (f == 0))
    def _():
        xb_ref[...] = _from_token_tiles(x_ref, tm).astype(BF16)
        acc_ref[...] = jnp.zeros_like(acc_ref)

    @pl.when(valid)
    def _():
        h = xb_ref[...]
        a = jnp.dot(h, wg_ref[0], preferred_element_type=F32)
        u = jnp.dot(h, wu_ref[0], preferred_element_type=F32)
        acc_ref[...] += jnp.dot((a * _sigmoid(a) * u).astype(BF16), wd_ref[0], preferred_element_type=F32)

    @pl.when(valid & last)
    def _():
        _to_token_tiles(o_ref, acc_ref[...])

    @pl.when(jnp.logical_not(valid) & last)
    def _():
        o_ref[...] = jnp.zeros_like(o_ref)


def _moe_ffn(xs, tile_expert, n_valid, wg, wu, wd, tm, tf=1792):
    n_tiles = xs.shape[0] // (tm * ROW_TILES)
    nf = wg.shape[2] // tf

    def live(i, f, te, nv):
        ok = i < nv[0]
        return jnp.where(ok, i, nv[0] - 1), jnp.where(ok, f, nf - 1)

    return pl.pallas_call(
        functools.partial(_moe_ffn_kernel, tm=tm),
        grid_spec=pltpu.PrefetchScalarGridSpec(
            num_scalar_prefetch=2,
            grid=(n_tiles, nf),
            in_specs=[pl.BlockSpec((tm * ROW_TILES, LANES), lambda *a: (live(*a)[0], 0)),
                      pl.BlockSpec((1, D_MODEL, tf), lambda *a: (a[2][live(*a)[0]], 0, live(*a)[1])),
                      pl.BlockSpec((1, D_MODEL, tf), lambda *a: (a[2][live(*a)[0]], 0, live(*a)[1])),
                      pl.BlockSpec((1, tf, D_MODEL), lambda *a: (a[2][live(*a)[0]], live(*a)[1], 0))],
            out_specs=pl.BlockSpec((tm * ROW_TILES, LANES), lambda i, f, te, nv: (i, 0)),
            scratch_shapes=[pltpu.VMEM((tm, D_MODEL), F32), pltpu.VMEM((tm, D_MODEL), BF16)]),
        out_shape=jax.ShapeDtypeStruct(xs.shape, F32),
        compiler_params=_cparams(("arbitrary", "arbitrary")),
        name="moe_ffn",
    )(tile_expert, n_valid, xs, wg, wu, wd)


def _router_params(w_router, b_router):
    w_pad = jnp.zeros((D_MODEL, LANES), F32).at[:, :N_EXPERTS].set(w_router)
    b_pad = jnp.full((1, LANES), NEG, F32).at[0, :N_EXPERTS].set(b_router)
    return w_pad, b_pad


def _moe(h, idx_l, wg, wu, wd, tm=448):
    t = h.shape[0] // ROW_TILES
    expert = idx_l[:, :TOP_K].T.reshape(-1)
    onehot = (expert[:, None] == jnp.arange(N_EXPERTS)[None, :]).astype(jnp.int32)
    csum = jnp.cumsum(onehot, axis=0)
    counts = csum[-1]
    rank = jnp.sum((csum - onehot) * onehot, axis=1)
    tiles = (counts + tm - 1) // tm
    tile_end = jnp.cumsum(tiles)
    pad_off = (tile_end - tiles) * tm
    pos8 = (pad_off[expert] + rank) * ROW_TILES
    n_tiles = (TOP_K * t) // tm + N_EXPERTS
    n_valid = tile_end[-1:].astype(jnp.int32)
    tile_ids = jnp.minimum(jnp.arange(n_tiles), n_valid[0] - 1)
    tile_expert = jnp.sum(tile_end[None, :] <= tile_ids[:, None], axis=1).astype(jnp.int32)
    spare = n_valid[0] + jnp.arange(N_EXPERTS)
    fill8 = jnp.concatenate([jnp.where(tiles > 0, tile_end - 1, -1), jnp.where(spare < n_tiles, spare, -1)])
    fill8 = jnp.where(fill8 >= 0, fill8 * (tm * ROW_TILES), -1).astype(jnp.int32)
    xs = _dispatch(h, pos8, fill8, n_tiles * tm, tm)
    ys = _moe_ffn(xs, tile_expert, n_valid, wg, wu, wd, tm)
    return ys, pos8


def _ple_body(x, p_ref, wg_ref, wp_ref, g_ref, gn_ref, outs):
    gate = _sigmoid(jnp.dot(_rms(x, g_ref[...]).astype(BF16), wg_ref[...], preferred_element_type=F32))
    x2 = x + gate * jnp.dot(p_ref[...].astype(BF16), wp_ref[...], preferred_element_type=F32)
    if len(outs) == 2:
        outs[0][...] = x2
    outs[-1][...] = _rms(x2, gn_ref[...]).astype(outs[-1].dtype)


def _ple_moe_kernel(pos_ref, x_ref, p_ref, wt_ref, wg_ref, wp_ref, g_ref, gn_ref, y_hbm, *rest, t, tm):
    *outs, ybuf, sem = rest
    i = pl.program_id(0)
    slot = i % 2

    def gather(block, s):
        for k in range(TOP_K):
            for r in range(tm):
                pltpu.make_async_copy(_tile_at(y_hbm, pos_ref[k * t + block * tm + r]),
                                      _tile_at(ybuf.at[s, k], r * ROW_TILES), sem.at[s]).start(priority=r % 2)

    def wait(s):
        pltpu.make_async_copy(ybuf.at[s], ybuf.at[s], sem.at[s]).wait()

    last = pl.num_programs(0) - 1

    @pl.when(i == 0)
    def _():
        gather(0, 0)

    wait(slot)
    gather(jnp.minimum(i + 1, last), 1 - slot)
    wt = wt_ref[...]
    x = (x_ref[...] + wt[:, 0:1] * _from_token_tiles(ybuf.at[slot, 0], tm)
         + wt[:, 1:2] * _from_token_tiles(ybuf.at[slot, 1], tm))
    _ple_body(x, p_ref, wg_ref, wp_ref, g_ref, gn_ref, outs)

    @pl.when(i == last)
    def _():
        wait(1 - slot)


def _ple_moe(x, p, wg, wp, g, g_next, h_dtype, want_x, moe, tm=256):
    t = x.shape[0]
    n_out = 2 if want_x else 1
    out_shape = ([jax.ShapeDtypeStruct((t, D_MODEL), F32)] * (n_out - 1)
                 + [jax.ShapeDtypeStruct((t, D_MODEL), h_dtype)])
    ys, pos8, wt = moe
    row = lambda w: pl.BlockSpec((tm, w), lambda i, pos: (i, 0))
    full = lambda a: pl.BlockSpec(a.shape, lambda i, pos: (0, 0))
    return pl.pallas_call(
        functools.partial(_ple_moe_kernel, t=t, tm=tm),
        grid_spec=pltpu.PrefetchScalarGridSpec(
            num_scalar_prefetch=1,
            grid=(t // tm,),
            in_specs=[row(D_MODEL), row(P_DIM), row(LANES), full(wg), full(wp), full(g), full(g_next),
                      pl.BlockSpec(memory_space=pl.ANY)],
            out_specs=[row(D_MODEL)] * n_out,
            scratch_shapes=[pltpu.VMEM((2, TOP_K, tm * ROW_TILES, LANES), F32), pltpu.SemaphoreType.DMA((2,))]),
        out_shape=out_shape,
        compiler_params=_cparams(("arbitrary",)),
        name="ple_moe",
    )(pos8, x, p, wt, wg, wp, g, g_next, ys)


def _rot_cols(w):
    half = w.shape[-1] // 2
    return jnp.concatenate([-w[:, half:], w[:, :half]], axis=-1)


def _layer_weights(i, w_in, w_uq, w_ukv):
    wi = w_in[i]
    z = lambda n: jnp.zeros((wi.shape[0], n), F32)
    w_kr = wi[:, IN_OFF[2]:IN_OFF[3]]
    pad = MLA_SLAB - MLA_NOPE - MLA_ROPE
    wlat = jnp.concatenate([wi[:, :IN_OFF[2]], z(MLA_NOPE), w_kr, z(pad), z(MLA_NOPE), _rot_cols(w_kr), z(pad)],
                           axis=1).astype(BF16)
    qscale = jnp.concatenate([jnp.full((DIL_WIDTH,), DIL_HD ** -0.5, F32), jnp.ones((2 * DIL_WIDTH,), F32)])
    wdil = (wi[:, IN_OFF[3]:IN_OFF[6]] * qscale).astype(BF16)
    wgate = wi[:, IN_OFF[6]:].astype(BF16)
    uq = w_uq[i].reshape(MLA_Q_RANK, MLA_HEADS, MLA_NOPE + MLA_ROPE)
    zq = lambda n: jnp.zeros((MLA_Q_RANK, MLA_HEADS, n), F32)
    rope_rot = jnp.concatenate([-uq[..., MLA_NOPE + MLA_ROPE // 2:], uq[..., MLA_NOPE:MLA_NOPE + MLA_ROPE // 2]], -1)
    wq = jnp.concatenate([uq, zq(pad)], axis=-1).reshape(MLA_Q_RANK, -1).astype(BF16)
    wqr = jnp.concatenate([zq(MLA_NOPE), rope_rot, zq(pad)], axis=-1).reshape(MLA_Q_RANK, -1).astype(BF16)
    ukv = w_ukv[i].reshape(MLA_KV_RANK, MLA_HEADS, MLA_NOPE + MLA_V)
    zkv = jnp.zeros((MLA_KV_RANK, MLA_HEADS, MLA_SLAB - MLA_NOPE), F32)
    wk = jnp.concatenate([ukv[..., :MLA_NOPE], zkv], axis=-1).reshape(MLA_KV_RANK, -1).astype(BF16)
    wv = jnp.concatenate([ukv[..., MLA_NOPE:], zkv], axis=-1).reshape(MLA_KV_RANK, -1).astype(BF16)
    return wlat, wdil, wgate, wq, wqr, wk, wv


def kernel(x, p, positions, attn_norm, w_in, q_norm, w_uq, kv_norm, w_ukv, w_br_mla, w_br_dil, w_out, ffn_norm, dense_w_gate, dense_w_up, dense_w_down, router_w, router_b, moe_w_gate, moe_w_up, moe_w_down, ple_norm, ple_w_gate, ple_w_proj, final_norm):
    batch, seq, d = x.shape
    t = batch * seq
    depth = w_in.shape[0]
    xf = x.reshape(t, d)
    cos_t, sin_t = _rope_tables(positions)
    slopes = _alibi_slopes(DIL_HEADS)
    row = lambda v: v.reshape(1, -1)
    h = _norm(xf, attn_norm[0])
    for i in range(depth):
        wlat, wdil, wgate, wq, wqr, wk, wv = _layer_weights(i, w_in, w_uq, w_ukv)
        q, k, v = _mla_proj(h, wlat, wq, wqr, wk, wv, row(q_norm[i]), row(kv_norm[i]), cos_t, sin_t)
        o_mla = _mla_attn(q, k, v, batch, seq)
        qkv_d = [_dil_proj(h, wdil, seq, g) for g in range(DIL_GROUPS)]
        o_dil = _dil_attn(qkv_d, slopes, batch, seq)
        moe_layer = i % 2 == 1
        j = i // 2
        xf, h2, *routing = _merge(xf, h, o_mla, o_dil, wgate, w_br_mla[i].astype(BF16), w_br_dil[i].astype(BF16),
                                  w_out[i].astype(BF16), row(ffn_norm[i]),
                                  _router_params(router_w[j], router_b[j]) if moe_layer else None)
        last = i == depth - 1
        g_next = final_norm if last else attn_norm[i + 1]
        ple_args = (p[i].reshape(t, -1), ple_w_gate[i].astype(BF16), ple_w_proj[i].astype(BF16),
                    row(ple_norm[i]), row(g_next), F32 if last else BF16)
        if moe_layer:
            idx_l, wt_l = routing
            moe = _moe(h2, idx_l, moe_w_gate[j].astype(BF16), moe_w_up[j].astype(BF16),
                       moe_w_down[j].astype(BF16)) + (wt_l,)
            *xnext, h = _ple_moe(xf, *ple_args, not last, moe)
            xf = xnext[0] if xnext else None
        else:
            xf, h = _ffn_ple(xf, h2, dense_w_gate[j].astype(BF16), dense_w_up[j].astype(BF16),
                             dense_w_down[j].astype(BF16), *ple_args)
    return h.reshape(batch, seq, d)
```
